```python
import jax, jax.numpy as jnp
from jax import lax
import numpy as np

D_MODEL = 1024
BATCH = 8
SEQ = 8192
DEPTH = 1
DEC_BATCH = 4
DEC_SEQ = 8192
PAST_LEN = 128

GRID_W = 64
POOL_WINDOWS = (2, 4, 8, 16)
N_POOL_GROUPS = 4
D_POOL = D_MODEL // 2
POOL_GC = D_POOL // N_POOL_GROUPS
HEAD_DIM = 128
N_Q_HEADS = (D_MODEL // 2) // HEAD_DIM
N_KV_HEADS = 2
Q_PER_KV = N_Q_HEADS // N_KV_HEADS
D_ATTN = N_Q_HEADS * HEAD_DIM
D_KV = N_KV_HEADS * HEAD_DIM
ROPE_AXIS_DIM = HEAD_DIM // 2
ROPE_HALF = ROPE_AXIS_DIM // 2
ROPE_THETA = 10000.0
Q_BLOCK = 128
N_BRANCH = 2
D_IN = D_POOL + D_ATTN + 2 * D_KV + N_BRANCH * D_MODEL
N_GROUPS = 4
EXPERTS_PER_GROUP = 8
N_EXPERTS = N_GROUPS * EXPERTS_PER_GROUP
TOP_K_IN_GROUP = 2
D_EXPERT = D_MODEL // 4
EPS = 1e-6

kernel_name = "hybrid_pool_gqa_hmoe_encoder"


def _rmsnorm(x, g):
    xf = x.astype(jnp.float32)
    y = xf * lax.rsqrt(jnp.mean(xf * xf, axis=-1, keepdims=True) + EPS) * g.astype(jnp.float32)
    return y.astype(x.dtype)


def _axial_rope_tables(T):
    rows = T // GRID_W
    row = jnp.repeat(jnp.arange(rows, dtype=jnp.float32), GRID_W)
    col = jnp.tile(jnp.arange(GRID_W, dtype=jnp.float32), rows)
    inv_freq = ROPE_THETA ** (-jnp.arange(0, ROPE_AXIS_DIM, 2, dtype=jnp.float32) / ROPE_AXIS_DIM)
    ang_r = row[:, None] * inv_freq[None, :]
    ang_c = col[:, None] * inv_freq[None, :]
    ang = jnp.concatenate([ang_r, ang_r, ang_c, ang_c], axis=-1)
    return jnp.cos(ang), jnp.sin(ang)


def _apply_rope(x, cos, sin):
    xr = x.reshape(x.shape[:-1] + (2, 2, ROPE_HALF))
    rot = jnp.stack([-xr[..., 1, :], xr[..., 0, :]], axis=-2).reshape(x.shape)
    return x * cos[None, :, None, :] + rot * sin[None, :, None, :]


def _pool_mixer(u, w_mix, scale):
    B, T, _ = u.shape
    uf = u.astype(jnp.float32)
    S = jnp.concatenate([jnp.zeros((B, 1, D_POOL), jnp.float32), jnp.cumsum(uf, axis=1)], axis=1)
    t = jnp.arange(T)
    outs = []
    for gi, w in enumerate(POOL_WINDOWS):
        sl = slice(gi * POOL_GC, (gi + 1) * POOL_GC)
        lo = jnp.clip(t - w // 2, 0, T)
        hi = jnp.clip(t - w // 2 + w, 0, T)
        cnt = (hi - lo).astype(jnp.float32)
        Sg = S[..., sl]
        mean = (jnp.take(Sg, hi, axis=1) - jnp.take(Sg, lo, axis=1)) / cnt[None, :, None]
        outs.append(mean - uf[..., sl])
    p = jnp.stack(outs, axis=2)
    p = jnp.einsum('btgc,gcd->btgd', p, w_mix.astype(jnp.float32)).reshape(B, T, D_POOL)
    return (p * scale.astype(jnp.float32)).astype(u.dtype)


def _block_attention(q, k, v):
    B, T = q.shape[0], q.shape[1]
    nblk = T // Q_BLOCK
    sm_scale = 1.0 / np.sqrt(HEAD_DIM)
    qb = q.reshape(B, nblk, Q_BLOCK, N_KV_HEADS, Q_PER_KV, HEAD_DIM).transpose(1, 0, 2, 3, 4, 5)

    def one(qblk):
        s = jnp.einsum('bqkrd,bskd->bkrqs', qblk, k).astype(jnp.float32) * sm_scale
        p = jax.nn.softmax(s, axis=-1)
        return jnp.einsum('bkrqs,bskd->bqkrd', p.astype(v.dtype), v)

    o = lax.map(one, qb)
    return o.transpose(1, 0, 2, 3, 4, 5).reshape(B, T, D_ATTN)


def _hier_moe(h, w_rg, b_rg, w_re, b_re, w_g, w_u, w_d):
    B, T, D = h.shape
    hf = h.reshape(B * T, D)
    lg = (hf @ w_rg).astype(jnp.float32) + b_rg.astype(jnp.float32)
    pg = jax.nn.softmax(lg, axis=-1)
    gsel = jnp.argmax(lg, axis=-1)
    pgsel = jnp.take_along_axis(pg, gsel[:, None], axis=1)[:, 0]
    le = ((hf @ w_re).astype(jnp.float32) + b_re.astype(jnp.float32)).reshape(-1, N_GROUPS, EXPERTS_PER_GROUP)
    le_sel = jnp.take_along_axis(le, gsel[:, None, None], axis=1)[:, 0]
    topv, topi = lax.top_k(le_sel, TOP_K_IN_GROUP)
    w2 = jax.nn.softmax(topv, axis=-1) * pgsel[:, None]
    eid = gsel[:, None] * EXPERTS_PER_GROUP + topi
    comb = jnp.sum(jax.nn.one_hot(eid, N_EXPERTS, dtype=jnp.float32) * w2[..., None], axis=1)
    y = jnp.zeros((B * T, D), jnp.float32)
    for e in range(N_EXPERTS):
        he = jax.nn.silu(hf @ w_g[e]) * (hf @ w_u[e])
        y = y + comb[:, e:e + 1] * (he @ w_d[e]).astype(jnp.float32)
    return y.astype(h.dtype).reshape(B, T, D)


def _trunk(x, c, w_ada, b_ada, norm1_g, norm2_g, w_in, q_norm_g, k_norm_g, w_pool_mix, pool_scale,
           w_pool_proj, w_attn_proj, w_o, w_router_group, b_router_group, w_router_expert,
           b_router_expert, w_exp_gate, w_exp_up, w_exp_down, final_g):
    B, T, _ = x.shape
    cos, sin = _axial_rope_tables(T)
    for l in range(DEPTH):
        mod = (jax.nn.silu(c) @ w_ada[l] + b_ada[l])
        sh1, sc1, g1, sh2, sc2, g2 = jnp.split(mod, 6, axis=-1)
        h = _rmsnorm(x, norm1_g[l]) * (1.0 + sc1[:, None, :]) + sh1[:, None, :]
        z = h @ w_in[l]
        o1 = D_POOL
        o2 = o1 + D_ATTN
        o3 = o2 + D_KV
        o4 = o3 + D_KV
        o5 = o4 + D_MODEL
        u = z[..., :o1]
        q = z[..., o1:o2].reshape(B, T, N_Q_HEADS, HEAD_DIM)
        k = z[..., o2:o3].reshape(B, T, N_KV_HEADS, HEAD_DIM)
        v = z[..., o3:o4].reshape(B, T, N_KV_HEADS, HEAD_DIM)
        gate_pool = jax.nn.sigmoid(z[..., o4:o5])
        gate_attn = jax.nn.sigmoid(z[..., o5:])
        q = _apply_rope(_rmsnorm(q, q_norm_g[l]).astype(jnp.float32), cos, sin)
        k = _apply_rope(_rmsnorm(k, k_norm_g[l]).astype(jnp.float32), cos, sin)
        attn = _block_attention(q, k, v.astype(jnp.float32)).astype(x.dtype)
        pool_out = _pool_mixer(u, w_pool_mix[l], pool_scale[l]) @ w_pool_proj[l]
        attn_out = attn @ w_attn_proj[l]
        mixed = (gate_pool * pool_out + gate_attn * attn_out) @ w_o[l]
        x = x + g1[:, None, :] * mixed
        h2 = _rmsnorm(x, norm2_g[l]) * (1.0 + sc2[:, None, :]) + sh2[:, None, :]
        x = x + g2[:, None, :] * _hier_moe(h2, w_router_group[l], b_router_group[l], w_router_expert[l],
                                            b_router_expert[l], w_exp_gate[l], w_exp_up[l], w_exp_down[l])
    return _rmsnorm(x, final_g)


def setup_inputs(seed: int = 0) -> dict:
    key = jax.random.key(seed)
    ks = jax.random.split(key, 24)
    f32 = jnp.float32
    D = D_MODEL
    n = lambda k, s, sc: jax.random.normal(k, s, f32) * sc
    return {
        "x_prompt": n(ks[0], (BATCH, SEQ, D), 1.0),
        "x_sample": n(ks[1], (DEC_BATCH, DEC_SEQ, D), 1.0),
        "c_prompt": n(ks[2], (BATCH, D), 1.0),
        "c_sample": n(ks[3], (DEC_BATCH, D), 1.0),
        "w_ada": n(ks[4], (DEPTH, D, 6 * D), 0.5 * D ** -0.5),
        "b_ada": n(ks[5], (DEPTH, 6 * D), 0.02),
        "norm1_g": 1.0 + n(ks[6], (DEPTH, D), 0.02),
        "norm2_g": 1.0 + n(ks[7], (DEPTH, D), 0.02),
        "w_in": n(ks[8], (DEPTH, D, D_IN), D ** -0.5),
        "q_norm_g": 1.0 + n(ks[9], (DEPTH, HEAD_DIM), 0.02),
        "k_norm_g": 1.0 + n(ks[10], (DEPTH, HEAD_DIM), 0.02),
        "w_pool_mix": n(ks[11], (DEPTH, N_POOL_GROUPS, POOL_GC, POOL_GC), POOL_GC ** -0.5),
        "pool_scale": 1.0 + n(ks[12], (DEPTH, D_POOL), 0.02),
        "w_pool_proj": n(ks[13], (DEPTH, D_POOL, D), D_POOL ** -0.5),
        "w_attn_proj": n(ks[14], (DEPTH, D_ATTN, D), D_ATTN ** -0.5),
        "w_o": n(ks[15], (DEPTH, D, D), D ** -0.5),
        "w_router_group": n(ks[16], (DEPTH, D, N_GROUPS), D ** -0.5),
        "b_router_group": n(ks[17], (DEPTH, N_GROUPS), 0.01),
        "w_router_expert": n(ks[18], (DEPTH, D, N_EXPERTS), D ** -0.5),
        "b_router_expert": n(ks[19], (DEPTH, N_EXPERTS), 0.01),
        "w_exp_gate": n(ks[20], (DEPTH, N_EXPERTS, D, D_EXPERT), D ** -0.5),
        "w_exp_up": n(ks[21], (DEPTH, N_EXPERTS, D, D_EXPERT), D ** -0.5),
        "w_exp_down": n(ks[22], (DEPTH, N_EXPERTS, D_EXPERT, D), D_EXPERT ** -0.5),
        "final_g": 1.0 + n(ks[23], (D,), 0.02),
    }


def reference(x_prompt, x_sample, c_prompt, c_sample, w_ada, b_ada, norm1_g, norm2_g, w_in, q_norm_g,
              k_norm_g, w_pool_mix, pool_scale, w_pool_proj, w_attn_proj, w_o, w_router_group,
              b_router_group, w_router_expert, b_router_expert, w_exp_gate, w_exp_up, w_exp_down, final_g):
    y_prompt = _trunk(x_prompt, c_prompt, w_ada, b_ada, norm1_g, norm2_g, w_in, q_norm_g, k_norm_g,
                      w_pool_mix, pool_scale, w_pool_proj, w_attn_proj, w_o, w_router_group,
                      b_router_group, w_router_expert, b_router_expert, w_exp_gate, w_exp_up,
                      w_exp_down, final_g)
    y_sample = _trunk(x_sample, c_sample, w_ada, b_ada, norm1_g, norm2_g, w_in, q_norm_g, k_norm_g,
                      w_pool_mix, pool_scale, w_pool_proj, w_attn_proj, w_o, w_router_group,
                      b_router_group, w_router_expert, b_router_expert, w_exp_gate, w_exp_up,
                      w_exp_down, final_g)
    return (y_prompt, y_sample)
```

```python
import functools

import numpy as np
import jax
import jax.numpy as jnp
from jax import lax
from jax.experimental import pallas as pl
from jax.experimental.pallas import tpu as pltpu
from jax.experimental.pallas import tpu_sc as plsc

F32 = jnp.float32
BF16 = jnp.bfloat16
I32 = jnp.int32

D_MODEL = 1024
GRID_W = 64
POOL_WINDOWS = (2, 4, 8, 16)
POOL_GC = 128
D_POOL = 512
HEAD_DIM = 128
N_Q_HEADS = 4
N_KV_HEADS = 2
D_ATTN = 512
D_KV = 256
ROPE_THETA = 10000.0
N_GROUPS = 4
EXPERTS_PER_GROUP = 8
N_EXPERTS = 32
D_EXPERT = 256
EPS = 1e-6
D_QKVU = D_POOL + D_ATTN + 2 * D_KV
POOL_HALO = 8
LANES = 128
D_PACK = D_MODEL // 2

V7X_VMEM_LIMIT = 56 * 1024 * 1024
SC_WINDOW = 64


def _const_spec(shape):
    nd = len(shape)
    return pl.BlockSpec(shape, lambda *_: (0,) * nd, pipeline_mode=pl.Buffered(1))


def _sigmoid(x):
    return 1.0 / (1.0 + jnp.exp(-x))


def _pack_bf16_pair(lo, hi):
    lo_bits = lax.bitcast_convert_type(lo.astype(BF16).astype(F32), jnp.uint32) >> 16
    hi_bits = lax.bitcast_convert_type(hi.astype(BF16).astype(F32), jnp.uint32) & jnp.uint32(0xFFFF0000)
    return lax.bitcast_convert_type(lo_bits | hi_bits, I32)


def _unpack_bf16_pair(packed):
    u = lax.bitcast_convert_type(packed, jnp.uint32)
    lo = lax.bitcast_convert_type(u << 16, F32)
    hi = lax.bitcast_convert_type(u & jnp.uint32(0xFFFF0000), F32)
    return lo, hi


def _mod_kernel(c_ref, w_ref, b_ref, o_ref):
    c = c_ref[...]
    s = c * _sigmoid(c)
    o_ref[...] = jnp.dot(s, w_ref[...], preferred_element_type=F32,
                         precision=lax.Precision.HIGHEST) + b_ref[...]


def _modulation(c, w_ada, b_ada):
    B = c.shape[0]
    n_out = w_ada.shape[1]
    bn = D_MODEL
    return pl.pallas_call(
        _mod_kernel,
        grid=(n_out // bn,),
        in_specs=[pl.BlockSpec((B, D_MODEL), lambda j: (0, 0)),
                  pl.BlockSpec((D_MODEL, bn), lambda j: (0, j)),
                  pl.BlockSpec((1, bn), lambda j: (0, j))],
        out_specs=pl.BlockSpec((B, bn), lambda j: (0, j)),
        out_shape=jax.ShapeDtypeStruct((B, n_out), F32),
        name="modulation",
    )(c, w_ada, b_ada.reshape(1, n_out))


def _modulated_norm(x, gain, shift, scale):
    ms = jnp.mean(x * x, axis=-1, keepdims=True)
    return x * lax.rsqrt(ms + EPS) * gain * (1.0 + scale) + shift


def _stage_a_kernel(x_ref, mod_ref, n1g_ref, w_ref, qg_ref, kg_ref, cos_ref, sin_ref,
                    u_ref, qt_ref, k_ref, vt_ref):
    h = _modulated_norm(x_ref[...], n1g_ref[...], mod_ref[0:1, :], mod_ref[1:2, :])
    z = jnp.dot(h.astype(BF16), w_ref[...], preferred_element_type=F32)
    u_ref[...] = z[:, :D_POOL]
    cos = cos_ref[...]
    sin_signed = sin_ref[...]
    lane = lax.broadcasted_iota(I32, (1, HEAD_DIM), 1)
    first_half = (lane % 64) < 32

    def norm_rope(xh, g):
        ms = jnp.mean(xh * xh, axis=-1, keepdims=True)
        xn = xh * lax.rsqrt(ms + EPS) * g
        rot = jnp.where(first_half, pltpu.roll(xn, 96, 1), pltpu.roll(xn, 32, 1))
        return xn * cos + rot * sin_signed

    qg = qg_ref[...] * (1.0 / np.sqrt(HEAD_DIM))
    for hq in range(N_Q_HEADS):
        o = D_POOL + hq * HEAD_DIM
        qt_ref[hq] = norm_rope(z[:, o:o + HEAD_DIM], qg).T.astype(BF16)
    for hk in range(N_KV_HEADS):
        o = D_POOL + D_ATTN + hk * HEAD_DIM
        k_ref[:, hk * HEAD_DIM:(hk + 1) * HEAD_DIM] = norm_rope(z[:, o:o + HEAD_DIM], kg_ref[...]).astype(BF16)
        o = D_POOL + D_ATTN + D_KV + hk * HEAD_DIM
        vt_ref[hk] = z[:, o:o + HEAD_DIM].T.astype(BF16)


def _stage_a(x, mod, n1g, w_qkvu, qg, kg, cos, sin_signed, tm):
    B, T, _ = x.shape
    return pl.pallas_call(
        _stage_a_kernel,
        grid=(B, T // tm),
        in_specs=[
            pl.BlockSpec((None, tm, D_MODEL), lambda b, i: (b, i, 0)),
            pl.BlockSpec((None, 6, D_MODEL), lambda b, i: (b, 0, 0)),
            _const_spec((1, D_MODEL)),
            _const_spec((D_MODEL, D_QKVU)),
            _const_spec((1, HEAD_DIM)),
            _const_spec((1, HEAD_DIM)),
            pl.BlockSpec((tm, HEAD_DIM), lambda b, i: (i, 0)),
            pl.BlockSpec((tm, HEAD_DIM), lambda b, i: (i, 0)),
        ],
        out_specs=[
            pl.BlockSpec((None, tm, D_POOL), lambda b, i: (b, i, 0)),
            pl.BlockSpec((None, N_Q_HEADS, HEAD_DIM, tm), lambda b, i: (b, 0, 0, i)),
            pl.BlockSpec((None, tm, D_KV), lambda b, i: (b, i, 0)),
            pl.BlockSpec((None, N_KV_HEADS, HEAD_DIM, tm), lambda b, i: (b, 0, 0, i)),
        ],
        out_shape=[
            jax.ShapeDtypeStruct((B, T, D_POOL), F32),
            jax.ShapeDtypeStruct((B, N_Q_HEADS, HEAD_DIM, T), BF16),
            jax.ShapeDtypeStruct((B, T, D_KV), BF16),
            jax.ShapeDtypeStruct((B, N_KV_HEADS, HEAD_DIM, T), BF16),
        ],
        compiler_params=pltpu.CompilerParams(
            dimension_semantics=("parallel", "parallel"), vmem_limit_bytes=V7X_VMEM_LIMIT),
        name="stage_a",
    )(x, mod, n1g, w_qkvu, qg, kg, cos, sin_signed)


def _attn_kernel(qt_ref, k_ref, vt_ref, o_ref, *, tk):
    T = k_ref.shape[0]
    tq = qt_ref.shape[2]
    qt = jnp.concatenate([qt_ref[0], qt_ref[1]], axis=1)

    def body(s, carry):
        m, l, acc = carry
        off = pl.multiple_of(s * tk, tk)
        st = jnp.dot(k_ref[pl.ds(off, tk), :], qt, preferred_element_type=F32)
        m_new = jnp.maximum(m, jnp.max(st, axis=0, keepdims=True))
        alpha = jnp.exp(m - m_new)
        p = jnp.exp(st - m_new)
        l = alpha * l + jnp.sum(p, axis=0, keepdims=True)
        pv = jnp.dot(vt_ref[:, pl.ds(off, tk)], p.astype(BF16), preferred_element_type=F32)
        return m_new, l, alpha * acc + pv

    m0 = jnp.full((1, 2 * tq), -jnp.inf, F32)
    l0 = jnp.zeros((1, 2 * tq), F32)
    acc0 = jnp.zeros((HEAD_DIM, 2 * tq), F32)
    _, l, acc = lax.fori_loop(0, T // tk, body, (m0, l0, acc0))
    o = acc * (1.0 / l)
    o_ref[:, :HEAD_DIM] = o[:, :tq].T.astype(BF16)
    o_ref[:, HEAD_DIM:] = o[:, tq:].T.astype(BF16)


def _attention(qt, k, vt, tq, tk):
    B, _, _, T = qt.shape
    group = N_Q_HEADS // N_KV_HEADS
    return pl.pallas_call(
        functools.partial(_attn_kernel, tk=tk),
        grid=(B, N_KV_HEADS, T // tq),
        in_specs=[
            pl.BlockSpec((None, group, HEAD_DIM, tq), lambda b, j, i: (b, j, 0, i)),
            pl.BlockSpec((None, T, HEAD_DIM), lambda b, j, i: (b, 0, j)),
            pl.BlockSpec((None, None, HEAD_DIM, T), lambda b, j, i: (b, j, 0, 0)),
        ],
        out_specs=pl.BlockSpec((None, tq, group * HEAD_DIM), lambda b, j, i: (b, i, j)),
        out_shape=jax.ShapeDtypeStruct((B, T, D_ATTN), BF16),
        compiler_params=pltpu.CompilerParams(
            dimension_semantics=("parallel", "parallel", "parallel"),
            vmem_limit_bytes=V7X_VMEM_LIMIT),
        name="attention",
    )(qt, k, vt)


def _route(logits):
    shape = logits.shape
    lane = lax.broadcasted_iota(I32, shape, 1).astype(F32)
    neg = jnp.float32(-jnp.inf)
    far = jnp.float32(LANES)

    def top(vals):
        v = jnp.max(vals, axis=1, keepdims=True)
        i = jnp.min(jnp.where(vals == v, lane, far), axis=1, keepdims=True)
        return v, i

    is_group = lane < N_GROUPS
    lg = jnp.where(is_group, logits, neg)
    gmax, gsel = top(lg)
    denom = jnp.sum(jnp.where(is_group, jnp.exp(lg - gmax), 0.0), axis=1, keepdims=True)
    p_group = 1.0 / denom
    base = N_GROUPS + EXPERTS_PER_GROUP * gsel
    le = jnp.where((lane >= base) & (lane < base + EXPERTS_PER_GROUP), logits, neg)
    v1, i1 = top(le)
    v2, i2 = top(jnp.where(lane == i1, neg, le))
    e21 = jnp.exp(v2 - v1)
    w1 = p_group / (1.0 + e21)
    w2 = w1 * e21
    eid = jnp.where(lane == 0, i1 - N_GROUPS, jnp.where(lane == 1, i2 - N_GROUPS, 0.0)).astype(I32)
    rw = jnp.where(lane == 0, w1, jnp.where(lane == 1, w2, 0.0))
    return eid, rw


def _stage_c_kernel(x_ref, mod_ref, n1g_ref, n2g_ref, u_ref, up_ref, un_ref, attn_ref,
                    wgate_ref, wmix_ref, ps_ref, wpp_ref, wap_ref, wo_ref, wr_ref, br_ref,
                    x1_ref, h2_ref, eid_ref, rw_ref, ue_ref, *, seq_len):
    i = pl.program_id(1)
    tm = x_ref.shape[0]
    x = x_ref[...]
    h = _modulated_norm(x, n1g_ref[...], mod_ref[0:1, :], mod_ref[1:2, :])
    gates = _sigmoid(jnp.dot(h.astype(BF16), wgate_ref[...], preferred_element_type=F32))

    ue_ref[0:POOL_HALO, :] = up_ref[...] * (i > 0).astype(F32)
    ue_ref[POOL_HALO:POOL_HALO + tm, :] = u_ref[...]
    ue_ref[POOL_HALO + tm:, :] = un_ref[...] * (i < pl.num_programs(1) - 1).astype(F32)
    t = i * tm + lax.broadcasted_iota(I32, (tm, 1), 0)
    mixed_groups = []
    for gi, w in enumerate(POOL_WINDOWS):
        half = w // 2
        cols = slice(gi * POOL_GC, (gi + 1) * POOL_GC)
        wsum = ue_ref[POOL_HALO - half:POOL_HALO - half + tm, cols]
        for d in range(1, w):
            wsum = wsum + ue_ref[POOL_HALO - half + d:POOL_HALO - half + d + tm, cols]
        cnt = (jnp.minimum(t + half, seq_len) - jnp.maximum(t - half, 0)).astype(F32)
        p = wsum / cnt - ue_ref[POOL_HALO:POOL_HALO + tm, cols]
        pm = jnp.dot(p.astype(BF16), wmix_ref[gi], preferred_element_type=F32)
        mixed_groups.append(pm * ps_ref[:, cols])
    pool = jnp.concatenate(mixed_groups, axis=1)
    pool_out = jnp.dot(pool.astype(BF16), wpp_ref[...], preferred_element_type=F32)
    attn_out = jnp.dot(attn_ref[...], wap_ref[...], preferred_element_type=F32)
    merged = gates[:, :D_MODEL] * pool_out + gates[:, D_MODEL:] * attn_out
    mixed = jnp.dot(merged.astype(BF16), wo_ref[...], preferred_element_type=F32)
    x1 = x + mod_ref[2:3, :] * mixed
    x1_ref[...] = x1

    h2 = _modulated_norm(x1, n2g_ref[...], mod_ref[3:4, :], mod_ref[4:5, :])
    h2_ref[...] = _pack_bf16_pair(h2[:, :D_PACK], h2[:, D_PACK:])
    logits = jnp.dot(h2, wr_ref[...], preferred_element_type=F32,
                     precision=lax.Precision.HIGHEST) + br_ref[...]
    eid, rw = _route(logits)
    eid_ref[...] = eid
    rw_ref[...] = rw


def _stage_c(x, mod, n1g, n2g, u, attn, w_gate, w_mix, pool_scale, w_pp, w_ap, w_o, w_r, b_r, tm):
    B, T, _ = x.shape
    hb = tm // POOL_HALO
    n_halo_blocks = T // POOL_HALO
    tok = lambda b, i: (b, i, 0)
    flat = lambda b, i: (b * (T // tm) + i, 0)
    return pl.pallas_call(
        functools.partial(_stage_c_kernel, seq_len=T),
        grid=(B, T // tm),
        in_specs=[
            pl.BlockSpec((None, tm, D_MODEL), tok),
            pl.BlockSpec((None, 6, D_MODEL), lambda b, i: (b, 0, 0)),
            _const_spec((1, D_MODEL)),
            _const_spec((1, D_MODEL)),
            pl.BlockSpec((None, tm, D_POOL), tok),
            pl.BlockSpec((None, POOL_HALO, D_POOL), lambda b, i: (b, jnp.maximum(i * hb - 1, 0), 0)),
            pl.BlockSpec((None, POOL_HALO, D_POOL),
                         lambda b, i: (b, jnp.minimum((i + 1) * hb, n_halo_blocks - 1), 0)),
            pl.BlockSpec((None, tm, D_ATTN), tok),
            _const_spec((D_MODEL, 2 * D_MODEL)),
            _const_spec((len(POOL_WINDOWS), POOL_GC, POOL_GC)),
            _const_spec((1, D_POOL)),
            _const_spec((D_POOL, D_MODEL)),
            _const_spec((D_ATTN, D_MODEL)),
            _const_spec((D_MODEL, D_MODEL)),
            _const_spec((D_MODEL, LANES)),
            _const_spec((1, LANES)),
        ],
        out_specs=[
            pl.BlockSpec((None, tm, D_MODEL), tok),
            pl.BlockSpec((tm, D_PACK), flat),
            pl.BlockSpec((tm, LANES), flat),
            pl.BlockSpec((tm, LANES), flat),
        ],
        out_shape=[
            jax.ShapeDtypeStruct((B, T, D_MODEL), F32),
            jax.ShapeDtypeStruct((B * T, D_PACK), I32),
            jax.ShapeDtypeStruct((B * T, LANES), I32),
            jax.ShapeDtypeStruct((B * T, LANES), F32),
        ],
        scratch_shapes=[pltpu.VMEM((tm + 2 * POOL_HALO, D_POOL), F32)],
        compiler_params=pltpu.CompilerParams(
            dimension_semantics=("parallel", "parallel"), vmem_limit_bytes=V7X_VMEM_LIMIT),
        name="stage_c",
    )(x, mod, n1g, n2g, u, u, u, attn, w_gate, w_mix, pool_scale, w_pp, w_ap, w_o, w_r, b_r)


def _sc_workers():
    info = plsc.get_sparse_core_info()
    return info.num_cores, info.num_subcores


def _sc_scatter_rows(rows, idx0, idx1, n_out):
    n, d = rows.shape
    nc, ns = _sc_workers()
    per_worker = n // (nc * ns)
    n_win = per_worker // SC_WINDOW
    mesh = plsc.VectorSubcoreMesh(core_axis_name="c", subcore_axis_name="s")

    @functools.partial(
        pl.kernel, mesh=mesh,
        out_type=jax.ShapeDtypeStruct((n_out, d), rows.dtype),
        scratch_types=[pltpu.VMEM((SC_WINDOW,), I32), pltpu.VMEM((SC_WINDOW,), I32),
                       pltpu.VMEM((SC_WINDOW, d), rows.dtype)],
        name="sc_scatter_rows",
    )
    def scatter(rows_hbm, idx0_hbm, idx1_hbm, out_hbm, i0_v, i1_v, rows_v):
        wid = lax.axis_index("s") * nc + lax.axis_index("c")

        @pl.loop(0, n_win)
        def _(w):
            base = wid * per_worker + w * SC_WINDOW
            pltpu.sync_copy(rows_hbm.at[pl.ds(base, SC_WINDOW)], rows_v)
            pltpu.sync_copy(idx0_hbm.at[pl.ds(base, SC_WINDOW)], i0_v)
            pltpu.sync_copy(idx1_hbm.at[pl.ds(base, SC_WINDOW)], i1_v)
            pltpu.sync_copy(rows_v, out_hbm.at[i0_v])
            pltpu.sync_copy(rows_v, out_hbm.at[i1_v])

    return scatter(rows, idx0, idx1)


def _sc_gather_rows(table, idx):
    n = idx.shape[0]
    d = table.shape[1]
    nc, ns = _sc_workers()
    per_worker = n // (nc * ns)
    n_win = per_worker // SC_WINDOW
    mesh = plsc.VectorSubcoreMesh(core_axis_name="c", subcore_axis_name="s")

    @functools.partial(
        pl.kernel, mesh=mesh,
        out_type=jax.ShapeDtypeStruct((n, d), table.dtype),
        scratch_types=[pltpu.VMEM((SC_WINDOW,), I32), pltpu.VMEM((SC_WINDOW, d), table.dtype)],
        name="sc_gather_rows",
    )
    def gather(table_hbm, idx_hbm, out_hbm, i_v, rows_v):
        wid = lax.axis_index("s") * nc + lax.axis_index("c")

        @pl.loop(0, n_win)
        def _(w):
            base = wid * per_worker + w * SC_WINDOW
            pltpu.sync_copy(idx_hbm.at[pl.ds(base, SC_WINDOW)], i_v)
            pltpu.sync_copy(table_hbm.at[i_v], rows_v)
            pltpu.sync_copy(rows_v, out_hbm.at[pl.ds(base, SC_WINDOW)])

    return gather(table, idx)


def _moe_kernel(te_ref, tv_ref, x_ref, wg_ref, wu_ref, wd_ref, o_ref):
    valid = tv_ref[pl.program_id(0)]
    tm = x_ref.shape[0]

    @pl.when(valid > 0)
    def _():
        live = lax.broadcasted_iota(I32, (tm, 1), 0) < valid
        lo, hi = _unpack_bf16_pair(jnp.where(live, x_ref[...], 0))
        x = jnp.concatenate([lo, hi], axis=1).astype(BF16)
        a = jnp.dot(x, wg_ref[...], preferred_element_type=F32)
        b = jnp.dot(x, wu_ref[...], preferred_element_type=F32)
        hmid = (a * _sigmoid(a) * b).astype(BF16)
        y = jnp.dot(hmid, wd_ref[...], preferred_element_type=F32)
        o_ref[...] = _pack_bf16_pair(y[:, :D_PACK], y[:, D_PACK:])

    @pl.when(valid == 0)
    def _():
        o_ref[...] = jnp.zeros_like(o_ref)


def _moe(xs, tile_expert, tile_valid, w_g, w_u, w_d, tm):
    n_tiles = xs.shape[0] // tm
    grid_spec = pltpu.PrefetchScalarGridSpec(
        num_scalar_prefetch=2,
        grid=(n_tiles,),
        in_specs=[
            pl.BlockSpec((tm, D_PACK), lambda i, te, tv: (i, 0)),
            pl.BlockSpec((None, D_MODEL, D_EXPERT), lambda i, te, tv: (te[i], 0, 0)),
            pl.BlockSpec((None, D_MODEL, D_EXPERT), lambda i, te, tv: (te[i], 0, 0)),
            pl.BlockSpec((None, D_EXPERT, D_MODEL), lambda i, te, tv: (te[i], 0, 0)),
        ],
        out_specs=pl.BlockSpec((tm, D_PACK), lambda i, te, tv: (i, 0)),
    )
    return pl.pallas_call(
        _moe_kernel,
        grid_spec=grid_spec,
        out_shape=jax.ShapeDtypeStruct(xs.shape, I32),
        compiler_params=pltpu.CompilerParams(
            dimension_semantics=("arbitrary",), vmem_limit_bytes=V7X_VMEM_LIMIT),
        name="experts",
    )(tile_expert, tile_valid, xs, w_g, w_u, w_d)


def _final_kernel(x1_ref, y_ref, rw_ref, mod_ref, fg_ref, o_ref):
    y0lo, y0hi = _unpack_bf16_pair(y_ref[:, :D_PACK])
    y1lo, y1hi = _unpack_bf16_pair(y_ref[:, D_PACK:])
    w0 = rw_ref[:, 0:1]
    w1 = rw_ref[:, 1:2]
    moe = jnp.concatenate([w0 * y0lo + w1 * y1lo, w0 * y0hi + w1 * y1hi], axis=1)
    x2 = x1_ref[...] + mod_ref[5:6, :] * moe
    ms = jnp.mean(x2 * x2, axis=-1, keepdims=True)
    o_ref[...] = x2 * lax.rsqrt(ms + EPS) * fg_ref[...]


def _final(x1, yb, rw, mod, final_g, tm):
    B, T, _ = x1.shape
    flat = lambda b, i: (b * (T // tm) + i, 0)
    return pl.pallas_call(
        _final_kernel,
        grid=(B, T // tm),
        in_specs=[
            pl.BlockSpec((None, tm, D_MODEL), lambda b, i: (b, i, 0)),
            pl.BlockSpec((tm, 2 * D_PACK), flat),
            pl.BlockSpec((tm, LANES), flat),
            pl.BlockSpec((None, 6, D_MODEL), lambda b, i: (b, 0, 0)),
            _const_spec((1, D_MODEL)),
        ],
        out_specs=pl.BlockSpec((None, tm, D_MODEL), lambda b, i: (b, i, 0)),
        out_shape=jax.ShapeDtypeStruct((B, T, D_MODEL), F32),
        compiler_params=pltpu.CompilerParams(
            dimension_semantics=("parallel", "parallel"), vmem_limit_bytes=V7X_VMEM_LIMIT),
        name="final_combine",
    )(x1, yb, rw, mod, final_g)


def _rope_tables(T):
    rows = T // GRID_W
    row = jnp.repeat(jnp.arange(rows, dtype=F32), GRID_W)
    col = jnp.tile(jnp.arange(GRID_W, dtype=F32), rows)
    inv_freq = ROPE_THETA ** (-jnp.arange(0, 64, 2, dtype=F32) / 64)
    ang_r = row[:, None] * inv_freq[None, :]
    ang_c = col[:, None] * inv_freq[None, :]
    ang = jnp.concatenate([ang_r, ang_r, ang_c, ang_c], axis=-1)
    sign = jnp.where((jnp.arange(HEAD_DIM) % 64) < 32, -1.0, 1.0).astype(F32)
    return jnp.cos(ang), jnp.sin(ang) * sign[None, :]


def _dispatch_plan(eid, tm, n_tiles):
    e = eid[:, :2].reshape(-1)
    onehot = (e[:, None] == jnp.arange(N_EXPERTS, dtype=I32)[None, :]).astype(I32)
    csum = jnp.cumsum(onehot, axis=0)
    rank = jnp.sum(csum * onehot, axis=1) - 1
    counts = csum[-1]
    tiles_per_e = (counts + tm - 1) // tm
    tile_end = jnp.cumsum(tiles_per_e)
    tile_start = tile_end - tiles_per_e
    pos = tile_start[e] * tm + rank
    tile_id = jnp.arange(n_tiles, dtype=I32)
    te = jnp.minimum(jnp.searchsorted(tile_end, tile_id, side="right").astype(I32), N_EXPERTS - 1)
    live = jnp.clip(counts[te] - (tile_id - tile_start[te]) * tm, 0, tm)
    tv = jnp.where(tile_id < tile_end[-1], live, 0).astype(I32)
    return pos.astype(I32).reshape(-1, 2), te, tv


def _pick_tile(T, want):
    t = min(T, want)
    assert T % t == 0
    return t


def _trunk(x, c, p):
    B, T, _ = x.shape
    N = B * T
    tm = _pick_tile(T, 512)
    tq = _pick_tile(T, 512)
    tk = _pick_tile(T, 256)
    tme = _pick_tile(N, 256)
    cos, sin_signed = _rope_tables(T)
    mod = _modulation(c, p["w_ada"], p["b_ada"]).reshape(B, 6, D_MODEL)
    u, qt, k, vt = _stage_a(x, mod, p["n1g"], p["w_qkvu"], p["qg"], p["kg"], cos, sin_signed, tm)
    attn = _attention(qt, k, vt, tq, tk)
    x1, h2p, eid, rw = _stage_c(x, mod, p["n1g"], p["n2g"], u, attn, p["w_gate"], p["w_mix"],
                                p["pool_scale"], p["w_pp"], p["w_ap"], p["w_o"], p["w_r"], p["b_r"], tm)
    n_tiles = 2 * N // tme + N_EXPERTS
    pos, te, tv = _dispatch_plan(eid, tme, n_tiles)
    xs = _sc_scatter_rows(h2p, pos[:, 0], pos[:, 1], n_tiles * tme)
    ys = _moe(xs, te, tv, p["w_eg"], p["w_eu"], p["w_ed"], tme)
    yb = _sc_gather_rows(ys, pos.reshape(-1)).reshape(N, 2 * D_PACK)
    return _final(x1, yb, rw, mod, p["final_g"], tm)


def kernel(x_prompt, x_sample, c_prompt, c_sample, w_ada, b_ada, norm1_g, norm2_g, w_in, q_norm_g,
           k_norm_g, w_pool_mix, pool_scale, w_pool_proj, w_attn_proj, w_o, w_router_group,
           b_router_group, w_router_expert, b_router_expert, w_exp_gate, w_exp_up, w_exp_down, final_g):
    assert w_ada.shape[0] == 1, "single-layer block"
    n_r = N_GROUPS + N_EXPERTS
    w_r = jnp.concatenate([w_router_group[0], w_router_expert[0],
                           jnp.zeros((D_MODEL, LANES - n_r), F32)], axis=1)
    b_r = jnp.concatenate([b_router_group[0], b_router_expert[0], jnp.zeros((LANES - n_r,), F32)])
    p = dict(
        w_ada=w_ada[0], b_ada=b_ada[0],
        n1g=norm1_g[0].reshape(1, D_MODEL), n2g=norm2_g[0].reshape(1, D_MODEL),
        w_qkvu=w_in[0][:, :D_QKVU].astype(BF16), w_gate=w_in[0][:, D_QKVU:].astype(BF16),
        qg=q_norm_g[0].reshape(1, HEAD_DIM), kg=k_norm_g[0].reshape(1, HEAD_DIM),
        w_mix=w_pool_mix[0].astype(BF16), pool_scale=pool_scale[0].reshape(1, D_POOL),
        w_pp=w_pool_proj[0].astype(BF16), w_ap=w_attn_proj[0].astype(BF16), w_o=w_o[0].astype(BF16),
        w_r=w_r, b_r=b_r.reshape(1, LANES),
        w_eg=w_exp_gate[0].astype(BF16), w_eu=w_exp_up[0].astype(BF16), w_ed=w_exp_down[0].astype(BF16),
        final_g=final_g.reshape(1, D_MODEL),
    )
    return _trunk(x_prompt, c_prompt, p), _trunk(x_sample, c_sample, p)
```

```python
import functools

import numpy as np
import jax
import jax.numpy as jnp
from jax import lax
from jax.experimental import pallas as pl
from jax.experimental.pallas import tpu as pltpu
from jax.experimental.pallas import tpu_sc as plsc

F32 = jnp.float32
BF16 = jnp.bfloat16
I32 = jnp.int32

D_MODEL = 1024
GRID_W = 64
POOL_WINDOWS = (2, 4, 8, 16)
POOL_GC = 128
D_POOL = 512
HEAD_DIM = 128
N_Q_HEADS = 4
N_KV_HEADS = 2
D_ATTN = 512
D_KV = 256
ROPE_THETA = 10000.0
N_GROUPS = 4
EXPERTS_PER_GROUP = 8
N_EXPERTS = 32
D_EXPERT = 256
EPS = 1e-6
LOG2E = 1.4426950408889634
EXP2_SAFE_SPAN = 100.0
D_QKVU = D_POOL + D_ATTN + 2 * D_KV
POOL_HALO = 8
LANES = 128
D_PACK = D_MODEL // 2

V7X_VMEM_LIMIT = 56 * 1024 * 1024
SC_WINDOW = 64


def _const_spec(shape):
    nd = len(shape)
    return pl.BlockSpec(shape, lambda *_: (0,) * nd, pipeline_mode=pl.Buffered(1))


def _sigmoid(x):
    return 1.0 / (1.0 + jnp.exp(-x))


def _pack_bf16_pair(lo, hi):
    lo_bits = lax.bitcast_convert_type(lo.astype(BF16).astype(F32), jnp.uint32) >> 16
    hi_bits = lax.bitcast_convert_type(hi.astype(BF16).astype(F32), jnp.uint32) & jnp.uint32(0xFFFF0000)
    return lax.bitcast_convert_type(lo_bits | hi_bits, I32)


def _unpack_bf16_pair(packed):
    u = lax.bitcast_convert_type(packed, jnp.uint32)
    lo = lax.bitcast_convert_type(u << 16, F32)
    hi = lax.bitcast_convert_type(u & jnp.uint32(0xFFFF0000), F32)
    return lo, hi


def _mod_kernel(c_ref, w_ref, b_ref, o_ref):
    c = c_ref[...]
    s = c * _sigmoid(c)
    o_ref[...] = jnp.dot(s, w_ref[...], preferred_element_type=F32,
                         precision=lax.Precision.HIGHEST) + b_ref[...]


def _modulation(c, w_ada, b_ada):
    B = c.shape[0]
    n_out = w_ada.shape[1]
    bn = D_MODEL
    return pl.pallas_call(
        _mod_kernel,
        grid=(n_out // bn,),
        in_specs=[pl.BlockSpec((B, D_MODEL), lambda j: (0, 0)),
                  pl.BlockSpec((D_MODEL, bn), lambda j: (0, j)),
                  pl.BlockSpec((1, bn), lambda j: (0, j))],
        out_specs=pl.BlockSpec((B, bn), lambda j: (0, j)),
        out_shape=jax.ShapeDtypeStruct((B, n_out), F32),
        name="modulation",
    )(c, w_ada, b_ada.reshape(1, n_out))


def _modulated_norm(x, gain, shift, scale):
    ms = jnp.mean(x * x, axis=-1, keepdims=True)
    return x * lax.rsqrt(ms + EPS) * gain * (1.0 + scale) + shift


def _stage_a_kernel(x_ref, mod_ref, n1g_ref, w_ref, qg_ref, kg_ref, cos_ref, sin_ref,
                    u_ref, qt_ref, qb_ref, k_ref, vt_ref):
    h = _modulated_norm(x_ref[...], n1g_ref[...], mod_ref[0:1, :], mod_ref[1:2, :])
    z = jnp.dot(h.astype(BF16), w_ref[...], preferred_element_type=F32)
    u_ref[...] = z[:, :D_POOL]
    cos = cos_ref[...]
    sin_signed = sin_ref[...]
    lane = lax.broadcasted_iota(I32, (1, HEAD_DIM), 1)
    first_half = (lane % 64) < 32

    def norm_rope(xh, g):
        ms = jnp.mean(xh * xh, axis=-1, keepdims=True)
        xn = xh * lax.rsqrt(ms + EPS) * g
        rot = jnp.where(first_half, pltpu.roll(xn, 96, 1), pltpu.roll(xn, 32, 1))
        return xn * cos + rot * sin_signed

    qg = qg_ref[...] * (LOG2E / np.sqrt(HEAD_DIM))
    k_norm_max = jnp.max(jnp.abs(kg_ref[...]), axis=-1, keepdims=True) * (np.sqrt(HEAD_DIM) * 1.01)
    for hq in range(N_Q_HEADS):
        o = D_POOL + hq * HEAD_DIM
        q_t = norm_rope(z[:, o:o + HEAD_DIM], qg).T.astype(BF16)
        qt_ref[hq] = q_t
        q_f = q_t.astype(F32)
        qb_ref[hq] = jnp.sqrt(jnp.sum(q_f * q_f, axis=0, keepdims=True)) * k_norm_max
    for hk in range(N_KV_HEADS):
        o = D_POOL + D_ATTN + hk * HEAD_DIM
        k_ref[:, hk * HEAD_DIM:(hk + 1) * HEAD_DIM] = norm_rope(z[:, o:o + HEAD_DIM], kg_ref[...]).astype(BF16)
        o = D_POOL + D_ATTN + D_KV + hk * HEAD_DIM
        vt_ref[hk] = z[:, o:o + HEAD_DIM].T.astype(BF16)


def _stage_a(x, mod, n1g, w_qkvu, qg, kg, cos, sin_signed, tm):
    B, T, _ = x.shape
    return pl.pallas_call(
        _stage_a_kernel,
        grid=(B, T // tm),
        in_specs=[
            pl.BlockSpec((None, tm, D_MODEL), lambda b, i: (b, i, 0)),
            pl.BlockSpec((None, 6, D_MODEL), lambda b, i: (b, 0, 0)),
            _const_spec((1, D_MODEL)),
            _const_spec((D_MODEL, D_QKVU)),
            _const_spec((1, HEAD_DIM)),
            _const_spec((1, HEAD_DIM)),
            pl.BlockSpec((tm, HEAD_DIM), lambda b, i: (i, 0)),
            pl.BlockSpec((tm, HEAD_DIM), lambda b, i: (i, 0)),
        ],
        out_specs=[
            pl.BlockSpec((None, tm, D_POOL), lambda b, i: (b, i, 0)),
            pl.BlockSpec((None, N_Q_HEADS, HEAD_DIM, tm), lambda b, i: (b, 0, 0, i)),
            pl.BlockSpec((None, N_Q_HEADS, 1, tm), lambda b, i: (b, 0, 0, i)),
            pl.BlockSpec((None, tm, D_KV), lambda b, i: (b, i, 0)),
            pl.BlockSpec((None, N_KV_HEADS, HEAD_DIM, tm), lambda b, i: (b, 0, 0, i)),
        ],
        out_shape=[
            jax.ShapeDtypeStruct((B, T, D_POOL), F32),
            jax.ShapeDtypeStruct((B, N_Q_HEADS, HEAD_DIM, T), BF16),
            jax.ShapeDtypeStruct((B, N_Q_HEADS, 1, T), F32),
            jax.ShapeDtypeStruct((B, T, D_KV), BF16),
            jax.ShapeDtypeStruct((B, N_KV_HEADS, HEAD_DIM, T), BF16),
        ],
        compiler_params=pltpu.CompilerParams(
            dimension_semantics=("parallel", "parallel"), vmem_limit_bytes=V7X_VMEM_LIMIT),
        name="stage_a",
    )(x, mod, n1g, w_qkvu, qg, kg, cos, sin_signed)


def _write_attn_output(o_ref, acc, l, tq):
    o = acc * (1.0 / l)
    o_ref[:, :HEAD_DIM] = o[:, :tq].T.astype(BF16)
    o_ref[:, HEAD_DIM:] = o[:, tq:].T.astype(BF16)


def _attn_bounded_kernel(qt_ref, qb_ref, k_ref, vt_ref, o_ref, st_ref, p_ref, acc_ref, l_ref, *, tk):
    T = k_ref.shape[0]
    tq = qt_ref.shape[2]
    nq = 2 * tq
    nk = T // tk
    qt = jnp.concatenate([qt_ref[0], qt_ref[1]], axis=1)
    qb = jnp.concatenate([qb_ref[0], qb_ref[1]], axis=1)

    def scores(j, slot):
        off = pl.multiple_of(j * tk, tk)
        st_ref[slot] = jnp.dot(k_ref[pl.ds(off, tk), :], qt, preferred_element_type=F32)

    def weights(slot):
        p = jnp.exp2(st_ref[slot] - qb)
        l_ref[...] += jnp.sum(p.reshape(tk // 8, 8, nq), axis=0)
        p_ref[slot] = p.astype(BF16)

    def values(j, slot):
        off = pl.multiple_of(j * tk, tk)
        acc_ref[...] += jnp.dot(vt_ref[:, pl.ds(off, tk)], p_ref[slot], preferred_element_type=F32)

    acc_ref[...] = jnp.zeros_like(acc_ref)
    l_ref[...] = jnp.zeros_like(l_ref)
    scores(0, 0)
    weights(0)
    scores(1, 1)

    @pl.loop(0, (nk - 2) // 2)
    def _(g):
        a = 2 * g + 1
        values(a - 1, 0)
        weights(1)
        scores(a + 1, 0)
        values(a, 1)
        weights(0)
        scores(a + 2, 1)

    values(nk - 2, 0)
    weights(1)
    values(nk - 1, 1)
    _write_attn_output(o_ref, acc_ref[...], jnp.sum(l_ref[...], axis=0, keepdims=True), tq)


def _attn_online_kernel(qt_ref, qb_ref, k_ref, vt_ref, o_ref, *, tk):
    del qb_ref
    T = k_ref.shape[0]
    tq = qt_ref.shape[2]
    nq = 2 * tq
    qt = jnp.concatenate([qt_ref[0], qt_ref[1]], axis=1)

    def body(s, carry):
        m, l, acc = carry
        off = pl.multiple_of(s * tk, tk)
        st = jnp.dot(k_ref[pl.ds(off, tk), :], qt, preferred_element_type=F32)
        m_new = jnp.maximum(m, jnp.max(st, axis=0, keepdims=True))
        alpha = jnp.exp2(m - m_new)
        p = jnp.exp2(st - m_new)
        l = alpha * l + jnp.sum(p, axis=0, keepdims=True)
        pv = jnp.dot(vt_ref[:, pl.ds(off, tk)], p.astype(BF16), preferred_element_type=F32)
        return m_new, l, alpha * acc + pv

    init = (jnp.full((1, nq), -jnp.inf, F32), jnp.zeros((1, nq), F32), jnp.zeros((HEAD_DIM, nq), F32))
    _, l, acc = lax.fori_loop(0, T // tk, body, init)
    _write_attn_output(o_ref, acc, l, tq)


def _attention(qt, qb, k, vt, tq, tk, running_max):
    B, _, _, T = qt.shape
    group = N_Q_HEADS // N_KV_HEADS
    nq = group * tq
    if running_max:
        body, scratch, name = functools.partial(_attn_online_kernel, tk=tk), [], "attention_online_max"
    else:
        assert (T // tk) % 2 == 0 and T // tk >= 2
        body = functools.partial(_attn_bounded_kernel, tk=tk)
        scratch = [pltpu.VMEM((2, tk, nq), F32), pltpu.VMEM((2, tk, nq), BF16),
                   pltpu.VMEM((HEAD_DIM, nq), F32), pltpu.VMEM((8, nq), F32)]
        name = "attention_bounded"
    return pl.pallas_call(
        body,
        grid=(B, N_KV_HEADS, T // tq),
        in_specs=[
            pl.BlockSpec((None, group, HEAD_DIM, tq), lambda b, j, i: (b, j, 0, i)),
            pl.BlockSpec((None, group, 1, tq), lambda b, j, i: (b, j, 0, i)),
            pl.BlockSpec((None, T, HEAD_DIM), lambda b, j, i: (b, 0, j)),
            pl.BlockSpec((None, None, HEAD_DIM, T), lambda b, j, i: (b, j, 0, 0)),
        ],
        out_specs=pl.BlockSpec((None, tq, group * HEAD_DIM), lambda b, j, i: (b, i, j)),
        out_shape=jax.ShapeDtypeStruct((B, T, D_ATTN), BF16),
        scratch_shapes=scratch,
        compiler_params=pltpu.CompilerParams(
            dimension_semantics=("parallel", "parallel", "parallel"),
            vmem_limit_bytes=V7X_VMEM_LIMIT),
        name=name,
    )(qt, qb, k, vt)


def _route(logits):
    shape = logits.shape
    lane = lax.broadcasted_iota(I32, shape, 1).astype(F32)
    neg = jnp.float32(-jnp.inf)
    far = jnp.float32(LANES)

    def top(vals):
        v = jnp.max(vals, axis=1, keepdims=True)
        i = jnp.min(jnp.where(vals == v, lane, far), axis=1, keepdims=True)
        return v, i

    is_group = lane < N_GROUPS
    lg = jnp.where(is_group, logits, neg)
    gmax, gsel = top(lg)
    denom = jnp.sum(jnp.where(is_group, jnp.exp(lg - gmax), 0.0), axis=1, keepdims=True)
    p_group = 1.0 / denom
    base = N_GROUPS + EXPERTS_PER_GROUP * gsel
    le = jnp.where((lane >= base) & (lane < base + EXPERTS_PER_GROUP), logits, neg)
    v1, i1 = top(le)
    v2, i2 = top(jnp.where(lane == i1, neg, le))
    e21 = jnp.exp(v2 - v1)
    w1 = p_group / (1.0 + e21)
    w2 = w1 * e21
    eid = jnp.where(lane == 0, i1 - N_GROUPS, jnp.where(lane == 1, i2 - N_GROUPS, 0.0)).astype(I32)
    rw = jnp.where(lane == 0, w1, jnp.where(lane == 1, w2, 0.0))
    return eid, rw


def _stage_c_kernel(x_ref, mod_ref, n1g_ref, n2g_ref, u_ref, up_ref, un_ref, attn_ref,
                    wgate_ref, wmix_ref, ps_ref, wpp_ref, wap_ref, wo_ref, wr_ref, br_ref,
                    x1_ref, h2_ref, eid_ref, rw_ref, ue_ref, *, seq_len):
    i = pl.program_id(1)
    tm = x_ref.shape[0]
    x = x_ref[...]
    h = _modulated_norm(x, n1g_ref[...], mod_ref[0:1, :], mod_ref[1:2, :])
    gates = _sigmoid(jnp.dot(h.astype(BF16), wgate_ref[...], preferred_element_type=F32))

    ue_ref[0:POOL_HALO, :] = up_ref[...] * (i > 0).astype(F32)
    ue_ref[POOL_HALO:POOL_HALO + tm, :] = u_ref[...]
    ue_ref[POOL_HALO + tm:, :] = un_ref[...] * (i < pl.num_programs(1) - 1).astype(F32)
    t = i * tm + lax.broadcasted_iota(I32, (tm, 1), 0)
    mixed_groups = []
    for gi, w in enumerate(POOL_WINDOWS):
        half = w // 2
        cols = slice(gi * POOL_GC, (gi + 1) * POOL_GC)
        wsum = ue_ref[POOL_HALO - half:POOL_HALO - half + tm, cols]
        for d in range(1, w):
            wsum = wsum + ue_ref[POOL_HALO - half + d:POOL_HALO - half + d + tm, cols]
        cnt = (jnp.minimum(t + half, seq_len) - jnp.maximum(t - half, 0)).astype(F32)
        p = wsum / cnt - ue_ref[POOL_HALO:POOL_HALO + tm, cols]
        pm = jnp.dot(p.astype(BF16), wmix_ref[gi], preferred_element_type=F32)
        mixed_groups.append(pm * ps_ref[:, cols])
    pool = jnp.concatenate(mixed_groups, axis=1)
    pool_out = jnp.dot(pool.astype(BF16), wpp_ref[...], preferred_element_type=F32)
    attn_out = jnp.dot(attn_ref[...], wap_ref[...], preferred_element_type=F32)
    merged = gates[:, :D_MODEL] * pool_out + gates[:, D_MODEL:] * attn_out
    mixed = jnp.dot(merged.astype(BF16), wo_ref[...], preferred_element_type=F32)
    x1 = x + mod_ref[2:3, :] * mixed
    x1_ref[...] = x1

    h2 = _modulated_norm(x1, n2g_ref[...], mod_ref[3:4, :], mod_ref[4:5, :])
    h2_ref[...] = _pack_bf16_pair(h2[:, :D_PACK], h2[:, D_PACK:])
    logits = jnp.dot(h2, wr_ref[...], preferred_element_type=F32,
                     precision=lax.Precision.HIGHEST) + br_ref[...]
    eid, rw = _route(logits)
    eid_ref[...] = eid
    rw_ref[...] = rw


def _stage_c(x, mod, n1g, n2g, u, attn, w_gate, w_mix, pool_scale, w_pp, w_ap, w_o, w_r, b_r, tm):
    B, T, _ = x.shape
    hb = tm // POOL_HALO
    n_halo_blocks = T // POOL_HALO
    tok = lambda b, i: (b, i, 0)
    flat = lambda b, i: (b * (T // tm) + i, 0)
    return pl.pallas_call(
        functools.partial(_stage_c_kernel, seq_len=T),
        grid=(B, T // tm),
        in_specs=[
            pl.BlockSpec((None, tm, D_MODEL), tok),
            pl.BlockSpec((None, 6, D_MODEL), lambda b, i: (b, 0, 0)),
            _const_spec((1, D_MODEL)),
            _const_spec((1, D_MODEL)),
            pl.BlockSpec((None, tm, D_POOL), tok),
            pl.BlockSpec((None, POOL_HALO, D_POOL), lambda b, i: (b, jnp.maximum(i * hb - 1, 0), 0)),
            pl.BlockSpec((None, POOL_HALO, D_POOL),
                         lambda b, i: (b, jnp.minimum((i + 1) * hb, n_halo_blocks - 1), 0)),
            pl.BlockSpec((None, tm, D_ATTN), tok),
            _const_spec((D_MODEL, 2 * D_MODEL)),
            _const_spec((len(POOL_WINDOWS), POOL_GC, POOL_GC)),
            _const_spec((1, D_POOL)),
            _const_spec((D_POOL, D_MODEL)),
            _const_spec((D_ATTN, D_MODEL)),
            _const_spec((D_MODEL, D_MODEL)),
            _const_spec((D_MODEL, LANES)),
            _const_spec((1, LANES)),
        ],
        out_specs=[
            pl.BlockSpec((None, tm, D_MODEL), tok),
            pl.BlockSpec((tm, D_PACK), flat),
            pl.BlockSpec((tm, LANES), flat),
            pl.BlockSpec((tm, LANES), flat),
        ],
        out_shape=[
            jax.ShapeDtypeStruct((B, T, D_MODEL), F32),
            jax.ShapeDtypeStruct((B * T, D_PACK), I32),
            jax.ShapeDtypeStruct((B * T, LANES), I32),
            jax.ShapeDtypeStruct((B * T, LANES), F32),
        ],
        scratch_shapes=[pltpu.VMEM((tm + 2 * POOL_HALO, D_POOL), F32)],
        compiler_params=pltpu.CompilerParams(
            dimension_semantics=("parallel", "parallel"), vmem_limit_bytes=V7X_VMEM_LIMIT),
        name="stage_c",
    )(x, mod, n1g, n2g, u, u, u, attn, w_gate, w_mix, pool_scale, w_pp, w_ap, w_o, w_r, b_r)


def _sc_workers():
    info = plsc.get_sparse_core_info()
    return info.num_cores, info.num_subcores


def _sc_scatter_rows(rows, idx0, idx1, n_out):
    n, d = rows.shape
    nc, ns = _sc_workers()
    per_worker = n // (nc * ns)
    n_win = per_worker // SC_WINDOW
    mesh = plsc.VectorSubcoreMesh(core_axis_name="c", subcore_axis_name="s")

    @functools.partial(
        pl.kernel, mesh=mesh,
        out_type=jax.ShapeDtypeStruct((n_out, d), rows.dtype),
        scratch_types=[pltpu.VMEM((SC_WINDOW,), I32), pltpu.VMEM((SC_WINDOW,), I32),
                       pltpu.VMEM((SC_WINDOW, d), rows.dtype)],
        name="sc_scatter_rows",
    )
    def scatter(rows_hbm, idx0_hbm, idx1_hbm, out_hbm, i0_v, i1_v, rows_v):
        wid = lax.axis_index("s") * nc + lax.axis_index("c")

        @pl.loop(0, n_win)
        def _(w):
            base = wid * per_worker + w * SC_WINDOW
            pltpu.sync_copy(rows_hbm.at[pl.ds(base, SC_WINDOW)], rows_v)
            pltpu.sync_copy(idx0_hbm.at[pl.ds(base, SC_WINDOW)], i0_v)
            pltpu.sync_copy(idx1_hbm.at[pl.ds(base, SC_WINDOW)], i1_v)
            pltpu.sync_copy(rows_v, out_hbm.at[i0_v])
            pltpu.sync_copy(rows_v, out_hbm.at[i1_v])

    return scatter(rows, idx0, idx1)


def _sc_gather_rows(table, idx):
    n = idx.shape[0]
    d = table.shape[1]
    nc, ns = _sc_workers()
    per_worker = n // (nc * ns)
    n_win = per_worker // SC_WINDOW
    mesh = plsc.VectorSubcoreMesh(core_axis_name="c", subcore_axis_name="s")

    @functools.partial(
        pl.kernel, mesh=mesh,
        out_type=jax.ShapeDtypeStruct((n, d), table.dtype),
        scratch_types=[pltpu.VMEM((SC_WINDOW,), I32), pltpu.VMEM((SC_WINDOW, d), table.dtype)],
        name="sc_gather_rows",
    )
    def gather(table_hbm, idx_hbm, out_hbm, i_v, rows_v):
        wid = lax.axis_index("s") * nc + lax.axis_index("c")

        @pl.loop(0, n_win)
        def _(w):
            base = wid * per_worker + w * SC_WINDOW
            pltpu.sync_copy(idx_hbm.at[pl.ds(base, SC_WINDOW)], i_v)
            pltpu.sync_copy(table_hbm.at[i_v], rows_v)
            pltpu.sync_copy(rows_v, out_hbm.at[pl.ds(base, SC_WINDOW)])

    return gather(table, idx)


def _moe_kernel(te_ref, tv_ref, x_ref, wg_ref, wu_ref, wd_ref, o_ref):
    valid = tv_ref[pl.program_id(0)]
    tm = x_ref.shape[0]

    @pl.when(valid > 0)
    def _():
        live = lax.broadcasted_iota(I32, (tm, 1), 0) < valid
        lo, hi = _unpack_bf16_pair(jnp.where(live, x_ref[...], 0))
        x = jnp.concatenate([lo, hi], axis=1).astype(BF16)
        a = jnp.dot(x, wg_ref[...], preferred_element_type=F32)
        b = jnp.dot(x, wu_ref[...], preferred_element_type=F32)
        hmid = (a * _sigmoid(a) * b).astype(BF16)
        y = jnp.dot(hmid, wd_ref[...], preferred_element_type=F32)
        o_ref[...] = _pack_bf16_pair(y[:, :D_PACK], y[:, D_PACK:])

    @pl.when(valid == 0)
    def _():
        o_ref[...] = jnp.zeros_like(o_ref)


def _moe(xs, tile_expert, tile_valid, w_g, w_u, w_d, tm):
    n_tiles = xs.shape[0] // tm
    grid_spec = pltpu.PrefetchScalarGridSpec(
        num_scalar_prefetch=2,
        grid=(n_tiles,),
        in_specs=[
            pl.BlockSpec((tm, D_PACK), lambda i, te, tv: (i, 0)),
            pl.BlockSpec((None, D_MODEL, D_EXPERT), lambda i, te, tv: (te[i], 0, 0)),
            pl.BlockSpec((None, D_MODEL, D_EXPERT), lambda i, te, tv: (te[i], 0, 0)),
            pl.BlockSpec((None, D_EXPERT, D_MODEL), lambda i, te, tv: (te[i], 0, 0)),
        ],
        out_specs=pl.BlockSpec((tm, D_PACK), lambda i, te, tv: (i, 0)),
    )
    return pl.pallas_call(
        _moe_kernel,
        grid_spec=grid_spec,
        out_shape=jax.ShapeDtypeStruct(xs.shape, I32),
        compiler_params=pltpu.CompilerParams(
            dimension_semantics=("arbitrary",), vmem_limit_bytes=V7X_VMEM_LIMIT),
        name="experts",
    )(tile_expert, tile_valid, xs, w_g, w_u, w_d)


def _final_kernel(x1_ref, y0_ref, y1_ref, rw_ref, mod_ref, fg_ref, o_ref):
    y0lo, y0hi = _unpack_bf16_pair(y0_ref[...])
    y1lo, y1hi = _unpack_bf16_pair(y1_ref[...])
    w0 = rw_ref[:, 0:1]
    w1 = rw_ref[:, 1:2]
    moe = jnp.concatenate([w0 * y0lo + w1 * y1lo, w0 * y0hi + w1 * y1hi], axis=1)
    x2 = x1_ref[...] + mod_ref[5:6, :] * moe
    ms = jnp.mean(x2 * x2, axis=-1, keepdims=True)
    o_ref[...] = x2 * lax.rsqrt(ms + EPS) * fg_ref[...]


def _final(x1, yb, rw, mod, final_g, tm):
    B, T, _ = x1.shape
    n_blocks = B * T // tm
    flat = lambda b, i: (b * (T // tm) + i, 0)
    return pl.pallas_call(
        _final_kernel,
        grid=(B, T // tm),
        in_specs=[
            pl.BlockSpec((None, tm, D_MODEL), lambda b, i: (b, i, 0)),
            pl.BlockSpec((tm, D_PACK), flat),
            pl.BlockSpec((tm, D_PACK), lambda b, i: (n_blocks + b * (T // tm) + i, 0)),
            pl.BlockSpec((tm, LANES), flat),
            pl.BlockSpec((None, 6, D_MODEL), lambda b, i: (b, 0, 0)),
            _const_spec((1, D_MODEL)),
        ],
        out_specs=pl.BlockSpec((None, tm, D_MODEL), lambda b, i: (b, i, 0)),
        out_shape=jax.ShapeDtypeStruct((B, T, D_MODEL), F32),
        compiler_params=pltpu.CompilerParams(
            dimension_semantics=("parallel", "parallel"), vmem_limit_bytes=V7X_VMEM_LIMIT),
        name="final_combine",
    )(x1, yb, yb, rw, mod, final_g)


def _rope_tables(T):
    rows = T // GRID_W
    row = jnp.repeat(jnp.arange(rows, dtype=F32), GRID_W)
    col = jnp.tile(jnp.arange(GRID_W, dtype=F32), rows)
    inv_freq = ROPE_THETA ** (-jnp.arange(0, 64, 2, dtype=F32) / 64)
    ang_r = row[:, None] * inv_freq[None, :]
    ang_c = col[:, None] * inv_freq[None, :]
    ang = jnp.concatenate([ang_r, ang_r, ang_c, ang_c], axis=-1)
    sign = jnp.where((jnp.arange(HEAD_DIM) % 64) < 32, -1.0, 1.0).astype(F32)
    return jnp.cos(ang), jnp.sin(ang) * sign[None, :]


def _dispatch_plan(eid, tm, n_tiles):
    e = eid[:, :2].reshape(-1)
    onehot = (e[:, None] == jnp.arange(N_EXPERTS, dtype=I32)[None, :]).astype(I32)
    csum = jnp.cumsum(onehot, axis=0)
    rank = jnp.sum(csum * onehot, axis=1) - 1
    counts = csum[-1]
    tiles_per_e = (counts + tm - 1) // tm
    tile_end = jnp.cumsum(tiles_per_e)
    tile_start = tile_end - tiles_per_e
    pos = tile_start[e] * tm + rank
    tile_id = jnp.arange(n_tiles, dtype=I32)
    te = jnp.minimum(jnp.searchsorted(tile_end, tile_id, side="right").astype(I32), N_EXPERTS - 1)
    live = jnp.clip(counts[te] - (tile_id - tile_start[te]) * tm, 0, tm)
    tv = jnp.where(tile_id < tile_end[-1], live, 0).astype(I32)
    return pos.astype(I32).reshape(-1, 2), te, tv


def _pick_tile(T, want):
    t = min(T, want)
    assert T % t == 0
    return t


def _trunk(x, c, p):
    B, T, _ = x.shape
    N = B * T
    tm = _pick_tile(T, 512)
    tq = _pick_tile(T, 512)
    tk = _pick_tile(T, 256)
    tme = _pick_tile(N, 256)
    cos, sin_signed = _rope_tables(T)
    mod = _modulation(c, p["w_ada"], p["b_ada"]).reshape(B, 6, D_MODEL)
    u, qt, qb, k, vt = _stage_a(x, mod, p["n1g"], p["w_qkvu"], p["qg"], p["kg"], cos, sin_signed, tm)
    attn = lax.cond(
        p["bounded_softmax_ok"],
        functools.partial(_attention, tq=tq, tk=tk, running_max=False),
        functools.partial(_attention, tq=tq, tk=tk, running_max=True),
        qt, qb, k, vt)
    x1, h2p, eid, rw = _stage_c(x, mod, p["n1g"], p["n2g"], u, attn, p["w_gate"], p["w_mix"],
                                p["pool_scale"], p["w_pp"], p["w_ap"], p["w_o"], p["w_r"], p["b_r"], tm)
    n_tiles = 2 * N // tme + N_EXPERTS
    pos, te, tv = _dispatch_plan(eid, tme, n_tiles)
    xs = _sc_scatter_rows(h2p, pos[:, 0], pos[:, 1], n_tiles * tme)
    ys = _moe(xs, te, tv, p["w_eg"], p["w_eu"], p["w_ed"], tme)
    yb = _sc_gather_rows(ys, jnp.concatenate([pos[:, 0], pos[:, 1]]))
    return _final(x1, yb, rw, mod, p["final_g"], tm)


def kernel(x_prompt, x_sample, c_prompt, c_sample, w_ada, b_ada, norm1_g, norm2_g, w_in, q_norm_g,
           k_norm_g, w_pool_mix, pool_scale, w_pool_proj, w_attn_proj, w_o, w_router_group,
           b_router_group, w_router_expert, b_router_expert, w_exp_gate, w_exp_up, w_exp_down, final_g):
    assert w_ada.shape[0] == 1, "single-layer block"
    n_r = N_GROUPS + N_EXPERTS
    w_r = jnp.concatenate([w_router_group[0], w_router_expert[0],
                           jnp.zeros((D_MODEL, LANES - n_r), F32)], axis=1)
    b_r = jnp.concatenate([b_router_group[0], b_router_expert[0], jnp.zeros((LANES - n_r,), F32)])
    score_bound = (1.01 * LOG2E * np.sqrt(HEAD_DIM)) * jnp.max(jnp.abs(q_norm_g[0])) * jnp.max(jnp.abs(k_norm_g[0]))
    p = dict(
        bounded_softmax_ok=2.0 * score_bound <= EXP2_SAFE_SPAN,
        w_ada=w_ada[0], b_ada=b_ada[0],
        n1g=norm1_g[0].reshape(1, D_MODEL), n2g=norm2_g[0].reshape(1, D_MODEL),
        w_qkvu=w_in[0][:, :D_QKVU].astype(BF16), w_gate=w_in[0][:, D_QKVU:].astype(BF16),
        qg=q_norm_g[0].reshape(1, HEAD_DIM), kg=k_norm_g[0].reshape(1, HEAD_DIM),
        w_mix=w_pool_mix[0].astype(BF16), pool_scale=pool_scale[0].reshape(1, D_POOL),
        w_pp=w_pool_proj[0].astype(BF16), w_ap=w_attn_proj[0].astype(BF16), w_o=w_o[0].astype(BF16),
        w_r=w_r, b_r=b_r.reshape(1, LANES),
        w_eg=w_exp_gate[0].astype(BF16), w_eu=w_exp_up[0].astype(BF16), w_ed=w_exp_down[0].astype(BF16),
        final_g=final_g.reshape(1, D_MODEL),
    )
    return _trunk(x_prompt, c_prompt, p), _trunk(x_sample, c_sample, p)
```

```python
import functools

import numpy as np
import jax
import jax.numpy as jnp
from jax import lax
from jax.experimental import pallas as pl
from jax.experimental.pallas import tpu as pltpu
from jax.experimental.pallas import tpu_sc as plsc

F32 = jnp.float32
BF16 = jnp.bfloat16
I32 = jnp.int32

D_MODEL = 1024
GRID_W = 64
POOL_WINDOWS = (2, 4, 8, 16)
POOL_GC = 128
D_POOL = 512
HEAD_DIM = 128
N_Q_HEADS = 4
N_KV_HEADS = 2
D_ATTN = 512
D_KV = 256
ROPE_THETA = 10000.0
N_GROUPS = 4
EXPERTS_PER_GROUP = 8
N_EXPERTS = 32
D_EXPERT = 256
EPS = 1e-6
LOG2E = 1.4426950408889634
EXP2_SAFE_SPAN = 100.0
D_QKVU = D_POOL + D_ATTN + 2 * D_KV
POOL_HALO = 16
STAGE_ROWS = 256
LANES = 128
D_PACK = D_MODEL // 2

V7X_VMEM_LIMIT = 56 * 1024 * 1024
SC_WINDOW = 64


def _const_spec(shape):
    nd = len(shape)
    return pl.BlockSpec(shape, lambda *_: (0,) * nd, pipeline_mode=pl.Buffered(1))


def _sigmoid(x):
    return 0.5 * jnp.tanh(0.5 * x) + 0.5


def _pack_bf16_pair(lo, hi):
    lo_bits = lax.bitcast_convert_type(lo.astype(BF16).astype(F32), jnp.uint32) >> 16
    hi_bits = lax.bitcast_convert_type(hi.astype(BF16).astype(F32), jnp.uint32) & jnp.uint32(0xFFFF0000)
    return lax.bitcast_convert_type(lo_bits | hi_bits, I32)


def _unpack_bf16_pair(packed):
    u = lax.bitcast_convert_type(packed, jnp.uint32)
    lo = lax.bitcast_convert_type(u << 16, F32)
    hi = lax.bitcast_convert_type(u & jnp.uint32(0xFFFF0000), F32)
    return lo, hi


def _mod_kernel(c_ref, w_ref, b_ref, o_ref):
    c = c_ref[...]
    s = c * _sigmoid(c)
    o_ref[...] = jnp.dot(s, w_ref[...], preferred_element_type=F32,
                         precision=lax.Precision.HIGHEST) + b_ref[...]


def _modulation(c, w_ada, b_ada):
    B = c.shape[0]
    n_out = w_ada.shape[1]
    bn = D_MODEL
    return pl.pallas_call(
        _mod_kernel,
        grid=(n_out // bn,),
        in_specs=[pl.BlockSpec((B, D_MODEL), lambda j: (0, 0)),
                  pl.BlockSpec((D_MODEL, bn), lambda j: (0, j)),
                  pl.BlockSpec((1, bn), lambda j: (0, j))],
        out_specs=pl.BlockSpec((B, bn), lambda j: (0, j)),
        out_shape=jax.ShapeDtypeStruct((B, n_out), F32),
        name="modulation",
    )(c, w_ada, b_ada.reshape(1, n_out))


def _scaled_norm(x, gain, shift):
    ms = jnp.mean(x * x, axis=-1, keepdims=True)
    return x * lax.rsqrt(ms + EPS) * gain + shift


def _stage_a_kernel(x_ref, mod_ref, n1g_ref, w_ref, qg_ref, kg_ref, cos_ref, sin_ref,
                    u_ref, qt_ref, qb_ref, k_ref, vt_ref):
    tm = x_ref.shape[0]
    gain1 = n1g_ref[...] * (1.0 + mod_ref[1:2, :])
    shift1 = mod_ref[0:1, :]
    lane = lax.broadcasted_iota(I32, (1, HEAD_DIM), 1)
    first_half = (lane % 64) < 32
    qg = qg_ref[...] * (LOG2E / np.sqrt(HEAD_DIM))
    k_norm_max = jnp.max(jnp.abs(kg_ref[...]), axis=-1, keepdims=True) * (np.sqrt(HEAD_DIM) * 1.01)

    for r0 in range(0, tm, STAGE_ROWS):
        rows = slice(r0, r0 + min(STAGE_ROWS, tm))
        h = _scaled_norm(x_ref[rows, :], gain1, shift1)
        z = jnp.dot(h.astype(BF16), w_ref[...], preferred_element_type=F32)
        u_ref[rows, :] = z[:, :D_POOL].astype(BF16)
        cos = cos_ref[rows, :]
        sin_signed = sin_ref[rows, :]

        def norm_rope(xh, g):
            ms = jnp.mean(xh * xh, axis=-1, keepdims=True)
            xn = xh * lax.rsqrt(ms + EPS) * g
            rot = jnp.where(first_half, pltpu.roll(xn, 96, 1), pltpu.roll(xn, 32, 1))
            return xn * cos + rot * sin_signed

        for hq in range(N_Q_HEADS):
            o = D_POOL + hq * HEAD_DIM
            q_t = norm_rope(z[:, o:o + HEAD_DIM], qg).T.astype(BF16)
            qt_ref[hq, :, rows] = q_t
            q_f = q_t.astype(F32)
            qb_ref[hq, :, rows] = jnp.sqrt(jnp.sum(q_f * q_f, axis=0, keepdims=True)) * k_norm_max
        for hk in range(N_KV_HEADS):
            o = D_POOL + D_ATTN + hk * HEAD_DIM
            k_ref[rows, hk * HEAD_DIM:(hk + 1) * HEAD_DIM] = norm_rope(z[:, o:o + HEAD_DIM], kg_ref[...]).astype(BF16)
            o = D_POOL + D_ATTN + D_KV + hk * HEAD_DIM
            vt_ref[hk, :, rows] = z[:, o:o + HEAD_DIM].T.astype(BF16)


def _stage_a(x, mod, n1g, w_qkvu, qg, kg, cos, sin_signed, tm):
    B, T, _ = x.shape
    return pl.pallas_call(
        _stage_a_kernel,
        grid=(B, T // tm),
        in_specs=[
            pl.BlockSpec((None, tm, D_MODEL), lambda b, i: (b, i, 0)),
            pl.BlockSpec((None, 6, D_MODEL), lambda b, i: (b, 0, 0)),
            _const_spec((1, D_MODEL)),
            _const_spec((D_MODEL, D_QKVU)),
            _const_spec((1, HEAD_DIM)),
            _const_spec((1, HEAD_DIM)),
            pl.BlockSpec((tm, HEAD_DIM), lambda b, i: (i, 0)),
            pl.BlockSpec((tm, HEAD_DIM), lambda b, i: (i, 0)),
        ],
        out_specs=[
            pl.BlockSpec((None, tm, D_POOL), lambda b, i: (b, i, 0)),
            pl.BlockSpec((None, N_Q_HEADS, HEAD_DIM, tm), lambda b, i: (b, 0, 0, i)),
            pl.BlockSpec((None, N_Q_HEADS, 1, tm), lambda b, i: (b, 0, 0, i)),
            pl.BlockSpec((None, tm, D_KV), lambda b, i: (b, i, 0)),
            pl.BlockSpec((None, N_KV_HEADS, HEAD_DIM, tm), lambda b, i: (b, 0, 0, i)),
        ],
        out_shape=[
            jax.ShapeDtypeStruct((B, T, D_POOL), BF16),
            jax.ShapeDtypeStruct((B, N_Q_HEADS, HEAD_DIM, T), BF16),
            jax.ShapeDtypeStruct((B, N_Q_HEADS, 1, T), F32),
            jax.ShapeDtypeStruct((B, T, D_KV), BF16),
            jax.ShapeDtypeStruct((B, N_KV_HEADS, HEAD_DIM, T), BF16),
        ],
        compiler_params=pltpu.CompilerParams(
            dimension_semantics=("parallel", "parallel"), vmem_limit_bytes=V7X_VMEM_LIMIT),
        name="stage_a",
    )(x, mod, n1g, w_qkvu, qg, kg, cos, sin_signed)


def _write_attn_output(o_ref, acc, l, tq):
    o = acc * (1.0 / l)
    o_ref[:, :HEAD_DIM] = o[:, :tq].T.astype(BF16)
    o_ref[:, HEAD_DIM:] = o[:, tq:].T.astype(BF16)


def _attn_bounded_kernel(qt_ref, qb_ref, k_ref, vt_ref, o_ref, st_ref, p_ref, acc_ref, l_ref, *, tk):
    T = k_ref.shape[0]
    tq = qt_ref.shape[2]
    nq = 2 * tq
    nk = T // tk
    qt = jnp.concatenate([qt_ref[0], qt_ref[1]], axis=1)
    qb = jnp.concatenate([qb_ref[0], qb_ref[1]], axis=1)

    def scores(j, slot):
        off = pl.multiple_of(j * tk, tk)
        st_ref[slot] = jnp.dot(k_ref[pl.ds(off, tk), :], qt, preferred_element_type=F32)

    def weights(slot):
        p = jnp.exp2(st_ref[slot] - qb)
        l_ref[...] += jnp.sum(p.reshape(tk // 8, 8, nq), axis=0)
        p_ref[slot] = p.astype(BF16)

    def values(j, slot):
        off = pl.multiple_of(j * tk, tk)
        acc_ref[...] += jnp.dot(vt_ref[:, pl.ds(off, tk)], p_ref[slot], preferred_element_type=F32)

    acc_ref[...] = jnp.zeros_like(acc_ref)
    l_ref[...] = jnp.zeros_like(l_ref)
    scores(0, 0)
    weights(0)
    scores(1, 1)

    unroll = 6 if (nk - 2) % 6 == 0 else 2

    @pl.loop(0, (nk - 2) // unroll)
    def _(g):
        for d in range(0, unroll, 2):
            a = unroll * g + d + 1
            values(a - 1, 0)
            weights(1)
            scores(a + 1, 0)
            values(a, 1)
            weights(0)
            scores(a + 2, 1)

    values(nk - 2, 0)
    weights(1)
    values(nk - 1, 1)
    _write_attn_output(o_ref, acc_ref[...], jnp.sum(l_ref[...], axis=0, keepdims=True), tq)


def _attn_online_kernel(qt_ref, qb_ref, k_ref, vt_ref, o_ref, *, tk):
    del qb_ref
    T = k_ref.shape[0]
    tq = qt_ref.shape[2]
    nq = 2 * tq
    qt = jnp.concatenate([qt_ref[0], qt_ref[1]], axis=1)

    def body(s, carry):
        m, l, acc = carry
        off = pl.multiple_of(s * tk, tk)
        st = jnp.dot(k_ref[pl.ds(off, tk), :], qt, preferred_element_type=F32)
        m_new = jnp.maximum(m, jnp.max(st, axis=0, keepdims=True))
        alpha = jnp.exp2(m - m_new)
        p = jnp.exp2(st - m_new)
        l = alpha * l + jnp.sum(p, axis=0, keepdims=True)
        pv = jnp.dot(vt_ref[:, pl.ds(off, tk)], p.astype(BF16), preferred_element_type=F32)
        return m_new, l, alpha * acc + pv

    init = (jnp.full((1, nq), -jnp.inf, F32), jnp.zeros((1, nq), F32), jnp.zeros((HEAD_DIM, nq), F32))
    _, l, acc = lax.fori_loop(0, T // tk, body, init)
    _write_attn_output(o_ref, acc, l, tq)


def _attention(qt, qb, k, vt, tq, tk, running_max):
    B, _, _, T = qt.shape
    group = N_Q_HEADS // N_KV_HEADS
    nq = group * tq
    if running_max:
        body, scratch, name = functools.partial(_attn_online_kernel, tk=tk), [], "attention_online_max"
    else:
        assert (T // tk) % 2 == 0 and T // tk >= 2
        body = functools.partial(_attn_bounded_kernel, tk=tk)
        scratch = [pltpu.VMEM((2, tk, nq), F32), pltpu.VMEM((2, tk, nq), BF16),
                   pltpu.VMEM((HEAD_DIM, nq), F32), pltpu.VMEM((8, nq), F32)]
        name = "attention_bounded"
    return pl.pallas_call(
        body,
        grid=(B, N_KV_HEADS, T // tq),
        in_specs=[
            pl.BlockSpec((None, group, HEAD_DIM, tq), lambda b, j, i: (b, j, 0, i)),
            pl.BlockSpec((None, group, 1, tq), lambda b, j, i: (b, j, 0, i)),
            pl.BlockSpec((None, T, HEAD_DIM), lambda b, j, i: (b, 0, j)),
            pl.BlockSpec((None, None, HEAD_DIM, T), lambda b, j, i: (b, j, 0, 0)),
        ],
        out_specs=pl.BlockSpec((None, tq, group * HEAD_DIM), lambda b, j, i: (b, i, j)),
        out_shape=jax.ShapeDtypeStruct((B, T, D_ATTN), BF16),
        scratch_shapes=scratch,
        compiler_params=pltpu.CompilerParams(
            dimension_semantics=("parallel", "parallel", "parallel"),
            vmem_limit_bytes=V7X_VMEM_LIMIT),
        name=name,
    )(qt, qb, k, vt)


def _route(logits):
    shape = logits.shape
    lane = lax.broadcasted_iota(I32, shape, 1).astype(F32)
    neg = jnp.float32(-jnp.inf)
    far = jnp.float32(LANES)

    def top(vals):
        v = jnp.max(vals, axis=1, keepdims=True)
        i = jnp.min(jnp.where(vals == v, lane, far), axis=1, keepdims=True)
        return v, i

    is_group = lane < N_GROUPS
    lg = jnp.where(is_group, logits, neg)
    gmax, gsel = top(lg)
    denom = jnp.sum(jnp.where(is_group, jnp.exp(lg - gmax), 0.0), axis=1, keepdims=True)
    p_group = 1.0 / denom
    base = N_GROUPS + EXPERTS_PER_GROUP * gsel
    le = jnp.where((lane >= base) & (lane < base + EXPERTS_PER_GROUP), logits, neg)
    v1, i1 = top(le)
    v2, i2 = top(jnp.where(lane == i1, neg, le))
    e21 = jnp.exp(v2 - v1)
    w1 = p_group / (1.0 + e21)
    w2 = w1 * e21
    eid = jnp.where(lane == 0, i1 - N_GROUPS, jnp.where(lane == 1, i2 - N_GROUPS, 0.0)).astype(I32)
    rw = jnp.where(lane == 0, w1, jnp.where(lane == 1, w2, 0.0))
    return eid, rw


def _pool_deviation(ue_ref, lvl_ref, pool_ref, tile_start, tm, seq_len):
    n = tm + 2 * POOL_HALO
    t = tile_start + lax.broadcasted_iota(I32, (tm, 1), 0)
    for gi, w in enumerate(POOL_WINDOWS):
        half = w // 2
        cols = slice(gi * POOL_GC, (gi + 1) * POOL_GC)
        level = ue_ref[0:n, cols] + ue_ref[1:n + 1, cols]
        span, buf = 2, 0
        lvl_ref[buf, 0:n, :] = level
        while span < w:
            level = lvl_ref[buf, 0:n, :] + lvl_ref[buf, span:span + n, :]
            span, buf = 2 * span, 1 - buf
            lvl_ref[buf, 0:n, :] = level
        wsum = lvl_ref[buf, POOL_HALO - half:POOL_HALO - half + tm, :]
        cnt = (jnp.minimum(t + half, seq_len) - jnp.maximum(t - half, 0)).astype(F32)
        p = wsum * (1.0 / cnt) - ue_ref[POOL_HALO:POOL_HALO + tm, cols]
        pool_ref[:, cols] = p.astype(BF16)


def _stage_c_kernel(x_ref, mod_ref, n1g_ref, n2g_ref, u_ref, up_ref, un_ref, attn_ref,
                    wgate_ref, wmix_ref, ps_ref, wpp_ref, wap_ref, wo_ref, wr_ref, br_ref,
                    x1_ref, h2_ref, eid_ref, rw_ref, ue_ref, lvl_ref, pool_ref, *, seq_len):
    i = pl.program_id(1)
    tm = x_ref.shape[0]
    gain1 = n1g_ref[...] * (1.0 + mod_ref[1:2, :])
    shift1 = mod_ref[0:1, :]
    gain2 = n2g_ref[...] * (1.0 + mod_ref[4:5, :])
    shift2 = mod_ref[3:4, :]
    res_gate = mod_ref[2:3, :]

    ue_ref[0:POOL_HALO, :] = up_ref[...].astype(F32) * (i > 0).astype(F32)
    ue_ref[POOL_HALO:POOL_HALO + tm, :] = u_ref[...].astype(F32)
    ue_ref[POOL_HALO + tm:2 * POOL_HALO + tm, :] = un_ref[...].astype(F32) * (i < pl.num_programs(1) - 1).astype(F32)
    ue_ref[2 * POOL_HALO + tm:, :] = jnp.zeros((POOL_HALO, D_POOL), F32)
    lvl_ref[:, 2 * POOL_HALO + tm:, :] = jnp.zeros((2, POOL_HALO, POOL_GC), F32)
    _pool_deviation(ue_ref, lvl_ref, pool_ref, i * tm, tm, seq_len)

    for r0 in range(0, tm, STAGE_ROWS):
        rows = slice(r0, r0 + min(STAGE_ROWS, tm))
        x = x_ref[rows, :]
        h = _scaled_norm(x, gain1, shift1)
        gates = _sigmoid(jnp.dot(h.astype(BF16), wgate_ref[...], preferred_element_type=F32))
        mixed_groups = []
        for gi in range(len(POOL_WINDOWS)):
            cols = slice(gi * POOL_GC, (gi + 1) * POOL_GC)
            pm = jnp.dot(pool_ref[rows, cols], wmix_ref[gi], preferred_element_type=F32)
            mixed_groups.append(pm * ps_ref[:, cols])
        pool = jnp.concatenate(mixed_groups, axis=1)
        pool_out = jnp.dot(pool.astype(BF16), wpp_ref[...], preferred_element_type=F32)
        attn_out = jnp.dot(attn_ref[rows, :], wap_ref[...], preferred_element_type=F32)
        merged = gates[:, :D_MODEL] * pool_out + gates[:, D_MODEL:] * attn_out
        mixed = jnp.dot(merged.astype(BF16), wo_ref[...], preferred_element_type=F32)
        x1 = x + res_gate * mixed
        x1_ref[rows, :] = x1

        h2 = _scaled_norm(x1, gain2, shift2)
        h2_ref[rows, :] = _pack_bf16_pair(h2[:, :D_PACK], h2[:, D_PACK:])
        h_hi = h2.astype(BF16)
        h_lo = (h2 - h_hi.astype(F32)).astype(BF16)
        parts = (jnp.dot(h_hi, wr_ref[...], preferred_element_type=F32)
                 + jnp.dot(h_lo, wr_ref[...], preferred_element_type=F32))
        logits = parts[:, :LANES] + parts[:, LANES:] + br_ref[...]
        eid, rw = _route(logits)
        eid_ref[rows, :] = eid
        rw_ref[rows, :] = rw


def _stage_c(x, mod, n1g, n2g, u, attn, w_gate, w_mix, pool_scale, w_pp, w_ap, w_o, w_r, b_r, tm):
    B, T, _ = x.shape
    hb = tm // POOL_HALO
    n_halo_blocks = T // POOL_HALO
    tok = lambda b, i: (b, i, 0)
    flat = lambda b, i: (b * (T // tm) + i, 0)
    return pl.pallas_call(
        functools.partial(_stage_c_kernel, seq_len=T),
        grid=(B, T // tm),
        in_specs=[
            pl.BlockSpec((None, tm, D_MODEL), tok),
            pl.BlockSpec((None, 6, D_MODEL), lambda b, i: (b, 0, 0)),
            _const_spec((1, D_MODEL)),
            _const_spec((1, D_MODEL)),
            pl.BlockSpec((None, tm, D_POOL), tok),
            pl.BlockSpec((None, POOL_HALO, D_POOL), lambda b, i: (b, jnp.maximum(i * hb - 1, 0), 0)),
            pl.BlockSpec((None, POOL_HALO, D_POOL),
                         lambda b, i: (b, jnp.minimum((i + 1) * hb, n_halo_blocks - 1), 0)),
            pl.BlockSpec((None, tm, D_ATTN), tok),
            _const_spec((D_MODEL, 2 * D_MODEL)),
            _const_spec((len(POOL_WINDOWS), POOL_GC, POOL_GC)),
            _const_spec((1, D_POOL)),
            _const_spec((D_POOL, D_MODEL)),
            _const_spec((D_ATTN, D_MODEL)),
            _const_spec((D_MODEL, D_MODEL)),
            _const_spec((D_MODEL, 2 * LANES)),
            _const_spec((1, LANES)),
        ],
        out_specs=[
            pl.BlockSpec((None, tm, D_MODEL), tok),
            pl.BlockSpec((tm, D_PACK), flat),
            pl.BlockSpec((tm, LANES), flat),
            pl.BlockSpec((tm, LANES), flat),
        ],
        out_shape=[
            jax.ShapeDtypeStruct((B, T, D_MODEL), F32),
            jax.ShapeDtypeStruct((B * T, D_PACK), I32),
            jax.ShapeDtypeStruct((B * T, LANES), I32),
            jax.ShapeDtypeStruct((B * T, LANES), F32),
        ],
        scratch_shapes=[pltpu.VMEM((tm + 3 * POOL_HALO, D_POOL), F32),
                        pltpu.VMEM((2, tm + 3 * POOL_HALO, POOL_GC), F32),
                        pltpu.VMEM((tm, D_POOL), BF16)],
        compiler_params=pltpu.CompilerParams(
            dimension_semantics=("parallel", "parallel"), vmem_limit_bytes=V7X_VMEM_LIMIT),
        name="stage_c",
    )(x, mod, n1g, n2g, u, u, u, attn, w_gate, w_mix, pool_scale, w_pp, w_ap, w_o, w_r, b_r)


def _sc_workers():
    info = plsc.get_sparse_core_info()
    return info.num_cores, info.num_subcores


def _sc_scatter_rows(rows, idx0, idx1, n_out):
    n, d = rows.shape
    nc, ns = _sc_workers()
    per_worker = n // (nc * ns)
    n_win = per_worker // SC_WINDOW
    mesh = plsc.VectorSubcoreMesh(core_axis_name="c", subcore_axis_name="s")

    @functools.partial(
        pl.kernel, mesh=mesh,
        out_type=jax.ShapeDtypeStruct((n_out, d), rows.dtype),
        scratch_types=[pltpu.VMEM((SC_WINDOW,), I32), pltpu.VMEM((SC_WINDOW,), I32),
                       pltpu.VMEM((SC_WINDOW, d), rows.dtype)],
        name="sc_scatter_rows",
    )
    def scatter(rows_hbm, idx0_hbm, idx1_hbm, out_hbm, i0_v, i1_v, rows_v):
        wid = lax.axis_index("s") * nc + lax.axis_index("c")

        @pl.loop(0, n_win)
        def _(w):
            base = wid * per_worker + w * SC_WINDOW
            pltpu.sync_copy(rows_hbm.at[pl.ds(base, SC_WINDOW)], rows_v)
            pltpu.sync_copy(idx0_hbm.at[pl.ds(base, SC_WINDOW)], i0_v)
            pltpu.sync_copy(idx1_hbm.at[pl.ds(base, SC_WINDOW)], i1_v)
            pltpu.sync_copy(rows_v, out_hbm.at[i0_v])
            pltpu.sync_copy(rows_v, out_hbm.at[i1_v])

    return scatter(rows, idx0, idx1)


def _sc_gather_rows(table, idx):
    n = idx.shape[0]
    d = table.shape[1]
    nc, ns = _sc_workers()
    per_worker = n // (nc * ns)
    n_win = per_worker // SC_WINDOW
    mesh = plsc.VectorSubcoreMesh(core_axis_name="c", subcore_axis_name="s")

    @functools.partial(
        pl.kernel, mesh=mesh,
        out_type=jax.ShapeDtypeStruct((n, d), table.dtype),
        scratch_types=[pltpu.VMEM((SC_WINDOW,), I32), pltpu.VMEM((SC_WINDOW, d), table.dtype)],
        name="sc_gather_rows",
    )
    def gather(table_hbm, idx_hbm, out_hbm, i_v, rows_v):
        wid = lax.axis_index("s") * nc + lax.axis_index("c")

        @pl.loop(0, n_win)
        def _(w):
            base = wid * per_worker + w * SC_WINDOW
            pltpu.sync_copy(idx_hbm.at[pl.ds(base, SC_WINDOW)], i_v)
            pltpu.sync_copy(table_hbm.at[i_v], rows_v)
            pltpu.sync_copy(rows_v, out_hbm.at[pl.ds(base, SC_WINDOW)])

    return gather(table, idx)


def _moe_kernel(te_ref, tv_ref, x_ref, wg_ref, wu_ref, wd_ref, o_ref, wg_bf, wu_bf, wd_bf):
    i = pl.program_id(0)
    valid = tv_ref[i]
    tm = x_ref.shape[0]

    @pl.when((i == 0) | (te_ref[i] != te_ref[jnp.maximum(i - 1, 0)]))
    def _():
        wg_bf[...] = wg_ref[...].astype(BF16)
        wu_bf[...] = wu_ref[...].astype(BF16)
        wd_bf[...] = wd_ref[...].astype(BF16)

    @pl.when(valid > 0)
    def _():
        live = lax.broadcasted_iota(I32, (tm, 1), 0) < valid
        lo, hi = _unpack_bf16_pair(jnp.where(live, x_ref[...], 0))
        x = jnp.concatenate([lo, hi], axis=1).astype(BF16)
        a = jnp.dot(x, wg_bf[...], preferred_element_type=F32)
        b = jnp.dot(x, wu_bf[...], preferred_element_type=F32)
        hmid = (a * _sigmoid(a) * b).astype(BF16)
        y = jnp.dot(hmid, wd_bf[...], preferred_element_type=F32)
        o_ref[...] = _pack_bf16_pair(y[:, :D_PACK], y[:, D_PACK:])

    @pl.when(valid == 0)
    def _():
        o_ref[...] = jnp.zeros_like(o_ref)


def _moe(xs, tile_expert, tile_valid, w_g, w_u, w_d, tm):
    n_tiles = xs.shape[0] // tm
    grid_spec = pltpu.PrefetchScalarGridSpec(
        num_scalar_prefetch=2,
        grid=(n_tiles,),
        in_specs=[
            pl.BlockSpec((tm, D_PACK), lambda i, te, tv: (i, 0)),
            pl.BlockSpec((None, D_MODEL, D_EXPERT), lambda i, te, tv: (te[i], 0, 0)),
            pl.BlockSpec((None, D_MODEL, D_EXPERT), lambda i, te, tv: (te[i], 0, 0)),
            pl.BlockSpec((None, D_EXPERT, D_MODEL), lambda i, te, tv: (te[i], 0, 0)),
        ],
        out_specs=pl.BlockSpec((tm, D_PACK), lambda i, te, tv: (i, 0)),
        scratch_shapes=[pltpu.VMEM((D_MODEL, D_EXPERT), BF16), pltpu.VMEM((D_MODEL, D_EXPERT), BF16),
                        pltpu.VMEM((D_EXPERT, D_MODEL), BF16)],
    )
    return pl.pallas_call(
        _moe_kernel,
        grid_spec=grid_spec,
        out_shape=jax.ShapeDtypeStruct(xs.shape, I32),
        compiler_params=pltpu.CompilerParams(
            dimension_semantics=("arbitrary",), vmem_limit_bytes=V7X_VMEM_LIMIT),
        name="experts",
    )(tile_expert, tile_valid, xs, w_g, w_u, w_d)


def _final_kernel(x1_ref, y0_ref, y1_ref, rw_ref, mod_ref, fg_ref, o_ref):
    y0lo, y0hi = _unpack_bf16_pair(y0_ref[...])
    y1lo, y1hi = _unpack_bf16_pair(y1_ref[...])
    w0 = rw_ref[:, 0:1]
    w1 = rw_ref[:, 1:2]
    moe = jnp.concatenate([w0 * y0lo + w1 * y1lo, w0 * y0hi + w1 * y1hi], axis=1)
    x2 = x1_ref[...] + mod_ref[5:6, :] * moe
    ms = jnp.mean(x2 * x2, axis=-1, keepdims=True)
    o_ref[...] = x2 * lax.rsqrt(ms + EPS) * fg_ref[...]


def _final(x1, yb, rw, mod, final_g, tm):
    B, T, _ = x1.shape
    n_blocks = B * T // tm
    flat = lambda b, i: (b * (T // tm) + i, 0)
    return pl.pallas_call(
        _final_kernel,
        grid=(B, T // tm),
        in_specs=[
            pl.BlockSpec((None, tm, D_MODEL), lambda b, i: (b, i, 0)),
            pl.BlockSpec((tm, D_PACK), flat),
            pl.BlockSpec((tm, D_PACK), lambda b, i: (n_blocks + b * (T // tm) + i, 0)),
            pl.BlockSpec((tm, LANES), flat),
            pl.BlockSpec((None, 6, D_MODEL), lambda b, i: (b, 0, 0)),
            _const_spec((1, D_MODEL)),
        ],
        out_specs=pl.BlockSpec((None, tm, D_MODEL), lambda b, i: (b, i, 0)),
        out_shape=jax.ShapeDtypeStruct((B, T, D_MODEL), F32),
        compiler_params=pltpu.CompilerParams(
            dimension_semantics=("parallel", "parallel"), vmem_limit_bytes=V7X_VMEM_LIMIT),
        name="final_combine",
    )(x1, yb, yb, rw, mod, final_g)


def _rope_tables(T):
    rows = T // GRID_W
    row = jnp.repeat(jnp.arange(rows, dtype=F32), GRID_W)
    col = jnp.tile(jnp.arange(GRID_W, dtype=F32), rows)
    inv_freq = ROPE_THETA ** (-jnp.arange(0, 64, 2, dtype=F32) / 64)
    ang_r = row[:, None] * inv_freq[None, :]
    ang_c = col[:, None] * inv_freq[None, :]
    ang = jnp.concatenate([ang_r, ang_r, ang_c, ang_c], axis=-1)
    sign = jnp.where((jnp.arange(HEAD_DIM) % 64) < 32, -1.0, 1.0).astype(F32)
    return jnp.cos(ang), jnp.sin(ang) * sign[None, :]


def _dispatch_plan(eid, tm, n_tiles):
    e = eid[:, :2].reshape(-1)
    onehot = (e[:, None] == jnp.arange(N_EXPERTS, dtype=I32)[None, :]).astype(I32)
    csum = jnp.cumsum(onehot, axis=0)
    rank = jnp.sum(csum * onehot, axis=1) - 1
    counts = csum[-1]
    tiles_per_e = (counts + tm - 1) // tm
    tile_end = jnp.cumsum(tiles_per_e)
    tile_start = tile_end - tiles_per_e
    pos = tile_start[e] * tm + rank
    tile_id = jnp.arange(n_tiles, dtype=I32)
    te = jnp.minimum(jnp.searchsorted(tile_end, tile_id, side="right").astype(I32), N_EXPERTS - 1)
    live = jnp.clip(counts[te] - (tile_id - tile_start[te]) * tm, 0, tm)
    tv = jnp.where(tile_id < tile_end[-1], live, 0).astype(I32)
    return pos.astype(I32).reshape(-1, 2), te, tv


def _pick_tile(T, want):
    t = min(T, want)
    assert T % t == 0
    return t


def _trunk(x, c, p):
    B, T, _ = x.shape
    N = B * T
    tm = _pick_tile(T, 512)
    tq = _pick_tile(T, 512)
    tk = _pick_tile(T, 256)
    tme = _pick_tile(N, 256)
    cos, sin_signed = _rope_tables(T)
    mod = _modulation(c, p["w_ada"], p["b_ada"]).reshape(B, 6, D_MODEL)
    u, qt, qb, k, vt = _stage_a(x, mod, p["n1g"], p["w_qkvu"], p["qg"], p["kg"], cos, sin_signed, tm)
    attn = lax.cond(
        p["bounded_softmax_ok"],
        functools.partial(_attention, tq=tq, tk=tk, running_max=False),
        functools.partial(_attention, tq=tq, tk=tk, running_max=True),
        qt, qb, k, vt)
    x1, h2p, eid, rw = _stage_c(x, mod, p["n1g"], p["n2g"], u, attn, p["w_gate"], p["w_mix"],
                                p["pool_scale"], p["w_pp"], p["w_ap"], p["w_o"], p["w_r"], p["b_r"], tm)
    n_tiles = 2 * N // tme + N_EXPERTS
    pos, te, tv = _dispatch_plan(eid, tme, n_tiles)
    xs = _sc_scatter_rows(h2p, pos[:, 0], pos[:, 1], n_tiles * tme)
    ys = _moe(xs, te, tv, p["w_eg"], p["w_eu"], p["w_ed"], tme)
    yb = _sc_gather_rows(ys, jnp.concatenate([pos[:, 0], pos[:, 1]]))
    return _final(x1, yb, rw, mod, p["final_g"], tm)


def kernel(x_prompt, x_sample, c_prompt, c_sample, w_ada, b_ada, norm1_g, norm2_g, w_in, q_norm_g,
           k_norm_g, w_pool_mix, pool_scale, w_pool_proj, w_attn_proj, w_o, w_router_group,
           b_router_group, w_router_expert, b_router_expert, w_exp_gate, w_exp_up, w_exp_down, final_g):
    assert w_ada.shape[0] == 1, "single-layer block"
    n_r = N_GROUPS + N_EXPERTS
    w_r = jnp.concatenate([w_router_group[0], w_router_expert[0],
                           jnp.zeros((D_MODEL, LANES - n_r), F32)], axis=1)
    w_r_hi = w_r.astype(BF16)
    b_r = jnp.concatenate([b_router_group[0], b_router_expert[0], jnp.zeros((LANES - n_r,), F32)])
    score_bound = (1.01 * LOG2E * np.sqrt(HEAD_DIM)) * jnp.max(jnp.abs(q_norm_g[0])) * jnp.max(jnp.abs(k_norm_g[0]))
    p = dict(
        bounded_softmax_ok=2.0 * score_bound <= EXP2_SAFE_SPAN,
        w_ada=w_ada[0], b_ada=b_ada[0],
        n1g=norm1_g[0].reshape(1, D_MODEL), n2g=norm2_g[0].reshape(1, D_MODEL),
        w_qkvu=w_in[0][:, :D_QKVU].astype(BF16), w_gate=w_in[0][:, D_QKVU:].astype(BF16),
        qg=q_norm_g[0].reshape(1, HEAD_DIM), kg=k_norm_g[0].reshape(1, HEAD_DIM),
        w_mix=w_pool_mix[0].astype(BF16), pool_scale=pool_scale[0].reshape(1, D_POOL),
        w_pp=w_pool_proj[0].astype(BF16), w_ap=w_attn_proj[0].astype(BF16), w_o=w_o[0].astype(BF16),
        w_r=jnp.concatenate([w_r_hi, (w_r - w_r_hi.astype(F32)).astype(BF16)], axis=1), b_r=b_r.reshape(1, LANES),
        w_eg=w_exp_gate[0], w_eu=w_exp_up[0], w_ed=w_exp_down[0],
        final_g=final_g.reshape(1, D_MODEL),
    )
    return _trunk(x_prompt, c_prompt, p), _trunk(x_sample, c_sample, p)
```

```python
import functools

import numpy as np
import jax
import jax.numpy as jnp
from jax import lax
from jax.experimental import pallas as pl
from jax.experimental.pallas import tpu as pltpu
from jax.experimental.pallas import tpu_sc as plsc

F32 = jnp.float32
BF16 = jnp.bfloat16
I32 = jnp.int32

D_MODEL = 1024
GRID_W = 64
POOL_WINDOWS = (2, 4, 8, 16)
POOL_GC = 128
D_POOL = 512
HEAD_DIM = 128
N_Q_HEADS = 4
N_KV_HEADS = 2
D_ATTN = 512
D_KV = 256
ROPE_THETA = 10000.0
N_GROUPS = 4
EXPERTS_PER_GROUP = 8
N_EXPERTS = 32
D_EXPERT = 256
EPS = 1e-6
LOG2E = 1.4426950408889634
EXP2_SAFE_SPAN = 100.0
D_QKVU = D_POOL + D_ATTN + 2 * D_KV
POOL_HALO = 16
STAGE_ROWS = 256
META_ROWS = 8
LANES = 128
D_PACK = D_MODEL // 2

V7X_VMEM_LIMIT = 56 * 1024 * 1024
SC_WINDOW = 64


def _const_spec(shape):
    nd = len(shape)
    return pl.BlockSpec(shape, lambda *_: (0,) * nd, pipeline_mode=pl.Buffered(1))


def _sigmoid(x):
    return 0.5 * jnp.tanh(0.5 * x) + 0.5


def _pack_bf16_pair(lo, hi):
    lo_bits = lax.bitcast_convert_type(lo.astype(BF16).astype(F32), jnp.uint32) >> 16
    hi_bits = lax.bitcast_convert_type(hi.astype(BF16).astype(F32), jnp.uint32) & jnp.uint32(0xFFFF0000)
    return lax.bitcast_convert_type(lo_bits | hi_bits, I32)


def _unpack_bf16_pair(packed):
    u = lax.bitcast_convert_type(packed, jnp.uint32)
    lo = lax.bitcast_convert_type(u << 16, F32)
    hi = lax.bitcast_convert_type(u & jnp.uint32(0xFFFF0000), F32)
    return lo, hi


def _mod_kernel(c_ref, w_ref, b_ref, o_ref):
    c = c_ref[...]
    s = c * _sigmoid(c)
    o_ref[...] = jnp.dot(s, w_ref[...], preferred_element_type=F32,
                         precision=lax.Precision.HIGHEST) + b_ref[...]


def _modulation(c, w_ada, b_ada):
    B = c.shape[0]
    n_out = w_ada.shape[1]
    bn = D_MODEL
    return pl.pallas_call(
        _mod_kernel,
        grid=(n_out // bn,),
        in_specs=[pl.BlockSpec((B, D_MODEL), lambda j: (0, 0)),
                  pl.BlockSpec((D_MODEL, bn), lambda j: (0, j)),
                  pl.BlockSpec((1, bn), lambda j: (0, j))],
        out_specs=pl.BlockSpec((B, bn), lambda j: (0, j)),
        out_shape=jax.ShapeDtypeStruct((B, n_out), F32),
        name="modulation",
    )(c, w_ada, b_ada.reshape(1, n_out))


def _scaled_norm(x, gain, shift):
    ms = jnp.mean(x * x, axis=-1, keepdims=True)
    return x * lax.rsqrt(ms + EPS) * gain + shift


def _stage_a_kernel(x_ref, mod_ref, n1g_ref, w_ref, qg_ref, kg_ref, cos_ref, sin_ref,
                    u_ref, qt_ref, qb_ref, k_ref, vt_ref):
    tm = x_ref.shape[0]
    gain1 = n1g_ref[...] * (1.0 + mod_ref[1:2, :])
    shift1 = mod_ref[0:1, :]
    lane = lax.broadcasted_iota(I32, (1, HEAD_DIM), 1)
    first_half = (lane % 64) < 32
    qg = qg_ref[...] * (LOG2E / np.sqrt(HEAD_DIM))
    k_norm_max = jnp.max(jnp.abs(kg_ref[...]), axis=-1, keepdims=True) * (np.sqrt(HEAD_DIM) * 1.01)

    for r0 in range(0, tm, STAGE_ROWS):
        rows = slice(r0, r0 + min(STAGE_ROWS, tm))
        h = _scaled_norm(x_ref[rows, :], gain1, shift1)
        z = jnp.dot(h.astype(BF16), w_ref[...], preferred_element_type=F32)
        u_ref[rows, :] = z[:, :D_POOL].astype(BF16)
        cos = cos_ref[rows, :]
        sin_signed = sin_ref[rows, :]

        def norm_rope(xh, g):
            ms = jnp.mean(xh * xh, axis=-1, keepdims=True)
            xn = xh * lax.rsqrt(ms + EPS) * g
            rot = jnp.where(first_half, pltpu.roll(xn, 96, 1), pltpu.roll(xn, 32, 1))
            return xn * cos + rot * sin_signed

        for hq in range(N_Q_HEADS):
            o = D_POOL + hq * HEAD_DIM
            q_t = norm_rope(z[:, o:o + HEAD_DIM], qg).T.astype(BF16)
            qt_ref[hq, :, rows] = q_t
            q_f = q_t.astype(F32)
            qb_ref[hq, :, rows] = jnp.sqrt(jnp.sum(q_f * q_f, axis=0, keepdims=True)) * k_norm_max
        for hk in range(N_KV_HEADS):
            o = D_POOL + D_ATTN + hk * HEAD_DIM
            k_ref[rows, hk * HEAD_DIM:(hk + 1) * HEAD_DIM] = norm_rope(z[:, o:o + HEAD_DIM], kg_ref[...]).astype(BF16)
            o = D_POOL + D_ATTN + D_KV + hk * HEAD_DIM
            vt_ref[hk, :, rows] = z[:, o:o + HEAD_DIM].T.astype(BF16)


def _stage_a(x, mod, n1g, w_qkvu, qg, kg, cos, sin_signed, tm):
    B, T, _ = x.shape
    return pl.pallas_call(
        _stage_a_kernel,
        grid=(B, T // tm),
        in_specs=[
            pl.BlockSpec((None, tm, D_MODEL), lambda b, i: (b, i, 0)),
            pl.BlockSpec((None, 6, D_MODEL), lambda b, i: (b, 0, 0)),
            _const_spec((1, D_MODEL)),
            _const_spec((D_MODEL, D_QKVU)),
            _const_spec((1, HEAD_DIM)),
            _const_spec((1, HEAD_DIM)),
            pl.BlockSpec((tm, HEAD_DIM), lambda b, i: (i, 0)),
            pl.BlockSpec((tm, HEAD_DIM), lambda b, i: (i, 0)),
        ],
        out_specs=[
            pl.BlockSpec((None, tm, D_POOL), lambda b, i: (b, i, 0)),
            pl.BlockSpec((None, N_Q_HEADS, HEAD_DIM, tm), lambda b, i: (b, 0, 0, i)),
            pl.BlockSpec((None, N_Q_HEADS, 1, tm), lambda b, i: (b, 0, 0, i)),
            pl.BlockSpec((None, tm, D_KV), lambda b, i: (b, i, 0)),
            pl.BlockSpec((None, N_KV_HEADS, HEAD_DIM, tm), lambda b, i: (b, 0, 0, i)),
        ],
        out_shape=[
            jax.ShapeDtypeStruct((B, T, D_POOL), BF16),
            jax.ShapeDtypeStruct((B, N_Q_HEADS, HEAD_DIM, T), BF16),
            jax.ShapeDtypeStruct((B, N_Q_HEADS, 1, T), F32),
            jax.ShapeDtypeStruct((B, T, D_KV), BF16),
            jax.ShapeDtypeStruct((B, N_KV_HEADS, HEAD_DIM, T), BF16),
        ],
        compiler_params=pltpu.CompilerParams(
            dimension_semantics=("parallel", "parallel"), vmem_limit_bytes=V7X_VMEM_LIMIT),
        name="stage_a",
    )(x, mod, n1g, w_qkvu, qg, kg, cos, sin_signed)


def _write_attn_output(o_ref, acc, l, tq):
    o = acc * (1.0 / l)
    o_ref[:, :HEAD_DIM] = o[:, :tq].T.astype(BF16)
    o_ref[:, HEAD_DIM:] = o[:, tq:].T.astype(BF16)


def _attn_bounded_kernel(qt_ref, qb_ref, k_ref, vt_ref, o_ref, st_ref, p_ref, acc_ref, l_ref, *, tk):
    T = k_ref.shape[0]
    tq = qt_ref.shape[2]
    nq = 2 * tq
    nk = T // tk
    qt = jnp.concatenate([qt_ref[0], qt_ref[1]], axis=1)
    qb = jnp.concatenate([qb_ref[0], qb_ref[1]], axis=1)

    def scores(j, slot):
        off = pl.multiple_of(j * tk, tk)
        st_ref[slot] = jnp.dot(k_ref[pl.ds(off, tk), :], qt, preferred_element_type=F32)

    def weights(slot):
        p = jnp.exp2(st_ref[slot] - qb)
        l_ref[...] += jnp.sum(p.reshape(tk // 8, 8, nq), axis=0)
        p_ref[slot] = p.astype(BF16)

    def values(j, slot):
        off = pl.multiple_of(j * tk, tk)
        acc_ref[...] += jnp.dot(vt_ref[:, pl.ds(off, tk)], p_ref[slot], preferred_element_type=F32)

    acc_ref[...] = jnp.zeros_like(acc_ref)
    l_ref[...] = jnp.zeros_like(l_ref)
    scores(0, 0)
    weights(0)
    scores(1, 1)

    unroll = 6 if (nk - 2) % 6 == 0 else 2

    @pl.loop(0, (nk - 2) // unroll)
    def _(g):
        for d in range(0, unroll, 2):
            a = unroll * g + d + 1
            values(a - 1, 0)
            weights(1)
            scores(a + 1, 0)
            values(a, 1)
            weights(0)
            scores(a + 2, 1)

    values(nk - 2, 0)
    weights(1)
    values(nk - 1, 1)
    _write_attn_output(o_ref, acc_ref[...], jnp.sum(l_ref[...], axis=0, keepdims=True), tq)


def _attn_online_kernel(qt_ref, qb_ref, k_ref, vt_ref, o_ref, *, tk):
    del qb_ref
    T = k_ref.shape[0]
    tq = qt_ref.shape[2]
    nq = 2 * tq
    qt = jnp.concatenate([qt_ref[0], qt_ref[1]], axis=1)

    def body(s, carry):
        m, l, acc = carry
        off = pl.multiple_of(s * tk, tk)
        st = jnp.dot(k_ref[pl.ds(off, tk), :], qt, preferred_element_type=F32)
        m_new = jnp.maximum(m, jnp.max(st, axis=0, keepdims=True))
        alpha = jnp.exp2(m - m_new)
        p = jnp.exp2(st - m_new)
        l = alpha * l + jnp.sum(p, axis=0, keepdims=True)
        pv = jnp.dot(vt_ref[:, pl.ds(off, tk)], p.astype(BF16), preferred_element_type=F32)
        return m_new, l, alpha * acc + pv

    init = (jnp.full((1, nq), -jnp.inf, F32), jnp.zeros((1, nq), F32), jnp.zeros((HEAD_DIM, nq), F32))
    _, l, acc = lax.fori_loop(0, T // tk, body, init)
    _write_attn_output(o_ref, acc, l, tq)


def _attention(qt, qb, k, vt, tq, tk, running_max):
    B, _, _, T = qt.shape
    group = N_Q_HEADS // N_KV_HEADS
    nq = group * tq
    if running_max:
        body, scratch, name = functools.partial(_attn_online_kernel, tk=tk), [], "attention_online_max"
    else:
        assert (T // tk) % 2 == 0 and T // tk >= 2
        body = functools.partial(_attn_bounded_kernel, tk=tk)
        scratch = [pltpu.VMEM((2, tk, nq), F32), pltpu.VMEM((2, tk, nq), BF16),
                   pltpu.VMEM((HEAD_DIM, nq), F32), pltpu.VMEM((8, nq), F32)]
        name = "attention_bounded"
    return pl.pallas_call(
        body,
        grid=(B, N_KV_HEADS, T // tq),
        in_specs=[
            pl.BlockSpec((None, group, HEAD_DIM, tq), lambda b, j, i: (b, j, 0, i)),
            pl.BlockSpec((None, group, 1, tq), lambda b, j, i: (b, j, 0, i)),
            pl.BlockSpec((None, T, HEAD_DIM), lambda b, j, i: (b, 0, j)),
            pl.BlockSpec((None, None, HEAD_DIM, T), lambda b, j, i: (b, j, 0, 0)),
        ],
        out_specs=pl.BlockSpec((None, tq, group * HEAD_DIM), lambda b, j, i: (b, i, j)),
        out_shape=jax.ShapeDtypeStruct((B, T, D_ATTN), BF16),
        scratch_shapes=scratch,
        compiler_params=pltpu.CompilerParams(
            dimension_semantics=("parallel", "parallel", "parallel"),
            vmem_limit_bytes=V7X_VMEM_LIMIT),
        name=name,
    )(qt, qb, k, vt)


def _route(logits):
    shape = logits.shape
    lane = lax.broadcasted_iota(I32, shape, 1).astype(F32)
    neg = jnp.float32(-jnp.inf)
    far = jnp.float32(LANES)

    def top(vals):
        v = jnp.max(vals, axis=1, keepdims=True)
        i = jnp.min(jnp.where(vals == v, lane, far), axis=1, keepdims=True)
        return v, i

    is_group = lane < N_GROUPS
    lg = jnp.where(is_group, logits, neg)
    gmax, gsel = top(lg)
    denom = jnp.sum(jnp.where(is_group, jnp.exp(lg - gmax), 0.0), axis=1, keepdims=True)
    p_group = 1.0 / denom
    base = N_GROUPS + EXPERTS_PER_GROUP * gsel
    le = jnp.where((lane >= base) & (lane < base + EXPERTS_PER_GROUP), logits, neg)
    v1, i1 = top(le)
    v2, i2 = top(jnp.where(lane == i1, neg, le))
    e21 = jnp.exp(v2 - v1)
    w1 = p_group / (1.0 + e21)
    w2 = w1 * e21
    rw = jnp.where(lane == 0, w1, jnp.where(lane == 1, w2, 0.0))
    return i1 - N_GROUPS, i2 - N_GROUPS, rw


def _pool_deviation(ue_ref, lvl_ref, pool_ref, tile_start, tm, seq_len):
    n = tm + 2 * POOL_HALO
    t = tile_start + lax.broadcasted_iota(I32, (tm, 1), 0)
    for gi, w in enumerate(POOL_WINDOWS):
        half = w // 2
        cols = slice(gi * POOL_GC, (gi + 1) * POOL_GC)
        level = ue_ref[0:n, cols] + ue_ref[1:n + 1, cols]
        span, buf = 2, 0
        lvl_ref[buf, 0:n, :] = level
        while span < w:
            level = lvl_ref[buf, 0:n, :] + lvl_ref[buf, span:span + n, :]
            span, buf = 2 * span, 1 - buf
            lvl_ref[buf, 0:n, :] = level
        wsum = lvl_ref[buf, POOL_HALO - half:POOL_HALO - half + tm, :]
        cnt = (jnp.minimum(t + half, seq_len) - jnp.maximum(t - half, 0)).astype(F32)
        p = wsum * (1.0 / cnt) - ue_ref[POOL_HALO:POOL_HALO + tm, cols]
        pool_ref[:, cols] = p.astype(BF16)


def _stage_c_kernel(x_ref, mod_ref, n1g_ref, n2g_ref, u_ref, up_ref, un_ref, attn_ref,
                    wgate_ref, wmix_ref, ps_ref, wpp_ref, wap_ref, wo_ref, wr_ref, br_ref, ltri_ref,
                    x1_ref, h2_ref, meta_ref, rw_ref, count_ref, ue_ref, lvl_ref, pool_ref, run_ref, *, seq_len):
    i = pl.program_id(1)
    tm = x_ref.shape[0]

    @pl.when((pl.program_id(0) == 0) & (i == 0))
    def _():
        run_ref[...] = jnp.zeros_like(run_ref)

    gain1 = n1g_ref[...] * (1.0 + mod_ref[1:2, :])
    shift1 = mod_ref[0:1, :]
    gain2 = n2g_ref[...] * (1.0 + mod_ref[4:5, :])
    shift2 = mod_ref[3:4, :]
    res_gate = mod_ref[2:3, :]

    ue_ref[0:POOL_HALO, :] = up_ref[...].astype(F32) * (i > 0).astype(F32)
    ue_ref[POOL_HALO:POOL_HALO + tm, :] = u_ref[...].astype(F32)
    ue_ref[POOL_HALO + tm:2 * POOL_HALO + tm, :] = un_ref[...].astype(F32) * (i < pl.num_programs(1) - 1).astype(F32)
    ue_ref[2 * POOL_HALO + tm:, :] = jnp.zeros((POOL_HALO, D_POOL), F32)
    lvl_ref[:, 2 * POOL_HALO + tm:, :] = jnp.zeros((2, POOL_HALO, POOL_GC), F32)
    _pool_deviation(ue_ref, lvl_ref, pool_ref, i * tm, tm, seq_len)

    for r0 in range(0, tm, STAGE_ROWS):
        rows = slice(r0, r0 + min(STAGE_ROWS, tm))
        x = x_ref[rows, :]
        h = _scaled_norm(x, gain1, shift1)
        gates = _sigmoid(jnp.dot(h.astype(BF16), wgate_ref[...], preferred_element_type=F32))
        mixed_groups = []
        for gi in range(len(POOL_WINDOWS)):
            cols = slice(gi * POOL_GC, (gi + 1) * POOL_GC)
            pm = jnp.dot(pool_ref[rows, cols], wmix_ref[gi], preferred_element_type=F32)
            mixed_groups.append(pm * ps_ref[:, cols])
        pool = jnp.concatenate(mixed_groups, axis=1)
        pool_out = jnp.dot(pool.astype(BF16), wpp_ref[...], preferred_element_type=F32)
        attn_out = jnp.dot(attn_ref[rows, :], wap_ref[...], preferred_element_type=F32)
        merged = gates[:, :D_MODEL] * pool_out + gates[:, D_MODEL:] * attn_out
        mixed = jnp.dot(merged.astype(BF16), wo_ref[...], preferred_element_type=F32)
        x1 = x + res_gate * mixed
        x1_ref[rows, :] = x1

        h2 = _scaled_norm(x1, gain2, shift2)
        h2_ref[rows, :] = _pack_bf16_pair(h2[:, :D_PACK], h2[:, D_PACK:])
        h_hi = h2.astype(BF16)
        h_lo = (h2 - h_hi.astype(F32)).astype(BF16)
        parts = (jnp.dot(h_hi, wr_ref[...], preferred_element_type=F32)
                 + jnp.dot(h_lo, wr_ref[...], preferred_element_type=F32))
        logits = parts[:, :LANES] + parts[:, LANES:] + br_ref[...]
        e1, e2, rw = _route(logits)
        rw_ref[rows, :] = rw

        lane = lax.broadcasted_iota(I32, logits.shape, 1).astype(F32)
        running = run_ref[...]
        ranks = []
        for e in (e1, e2):
            hit = lane == e
            prefix = jnp.dot(ltri_ref[...], jnp.where(hit, 1.0, 0.0).astype(BF16), preferred_element_type=F32)
            ranks.append(jnp.sum(jnp.where(hit, running + prefix, 0.0), axis=1, keepdims=True) - 1.0)
            running = running + prefix[-1:, :]
        run_ref[...] = running
        meta = jnp.where(lane == 0, e1, jnp.where(lane == 1, e2,
                         jnp.where(lane == 2, ranks[0], jnp.where(lane == 3, ranks[1], 0.0))))
        meta_ref[:, rows] = meta.T[0:META_ROWS, :].astype(I32)
    count_ref[...] = jnp.broadcast_to(run_ref[...], count_ref.shape)


def _stage_c(x, mod, n1g, n2g, u, attn, w_gate, w_mix, pool_scale, w_pp, w_ap, w_o, w_r, b_r, tm):
    B, T, _ = x.shape
    hb = tm // POOL_HALO
    n_halo_blocks = T // POOL_HALO
    tok = lambda b, i: (b, i, 0)
    flat = lambda b, i: (b * (T // tm) + i, 0)
    group_rows = min(STAGE_ROWS, tm)
    ltri = jnp.tri(group_rows, dtype=BF16)
    return pl.pallas_call(
        functools.partial(_stage_c_kernel, seq_len=T),
        grid=(B, T // tm),
        in_specs=[
            pl.BlockSpec((None, tm, D_MODEL), tok),
            pl.BlockSpec((None, 6, D_MODEL), lambda b, i: (b, 0, 0)),
            _const_spec((1, D_MODEL)),
            _const_spec((1, D_MODEL)),
            pl.BlockSpec((None, tm, D_POOL), tok),
            pl.BlockSpec((None, POOL_HALO, D_POOL), lambda b, i: (b, jnp.maximum(i * hb - 1, 0), 0)),
            pl.BlockSpec((None, POOL_HALO, D_POOL),
                         lambda b, i: (b, jnp.minimum((i + 1) * hb, n_halo_blocks - 1), 0)),
            pl.BlockSpec((None, tm, D_ATTN), tok),
            _const_spec((D_MODEL, 2 * D_MODEL)),
            _const_spec((len(POOL_WINDOWS), POOL_GC, POOL_GC)),
            _const_spec((1, D_POOL)),
            _const_spec((D_POOL, D_MODEL)),
            _const_spec((D_ATTN, D_MODEL)),
            _const_spec((D_MODEL, D_MODEL)),
            _const_spec((D_MODEL, 2 * LANES)),
            _const_spec((1, LANES)),
            _const_spec((group_rows, group_rows)),
        ],
        out_specs=[
            pl.BlockSpec((None, tm, D_MODEL), tok),
            pl.BlockSpec((tm, D_PACK), flat),
            pl.BlockSpec((META_ROWS, tm), lambda b, i: (0, b * (T // tm) + i)),
            pl.BlockSpec((tm, LANES), flat),
            pl.BlockSpec((META_ROWS, LANES), lambda b, i: (0, 0)),
        ],
        out_shape=[
            jax.ShapeDtypeStruct((B, T, D_MODEL), F32),
            jax.ShapeDtypeStruct((B * T, D_PACK), I32),
            jax.ShapeDtypeStruct((META_ROWS, B * T), I32),
            jax.ShapeDtypeStruct((B * T, LANES), F32),
            jax.ShapeDtypeStruct((META_ROWS, LANES), F32),
        ],
        scratch_shapes=[pltpu.VMEM((tm + 3 * POOL_HALO, D_POOL), F32),
                        pltpu.VMEM((2, tm + 3 * POOL_HALO, POOL_GC), F32),
                        pltpu.VMEM((tm, D_POOL), BF16),
                        pltpu.VMEM((1, LANES), F32)],
        compiler_params=pltpu.CompilerParams(
            dimension_semantics=("arbitrary", "arbitrary"), vmem_limit_bytes=V7X_VMEM_LIMIT),
        name="stage_c",
    )(x, mod, n1g, n2g, u, u, u, attn, w_gate, w_mix, pool_scale, w_pp, w_ap, w_o, w_r, b_r, ltri)


def _sc_workers():
    info = plsc.get_sparse_core_info()
    return info.num_cores, info.num_subcores


def _sc_scatter_rows(rows, idx0, idx1, n_out):
    n, d = rows.shape
    nc, ns = _sc_workers()
    per_worker = n // (nc * ns)
    n_win = per_worker // SC_WINDOW
    mesh = plsc.VectorSubcoreMesh(core_axis_name="c", subcore_axis_name="s")

    @functools.partial(
        pl.kernel, mesh=mesh,
        out_type=jax.ShapeDtypeStruct((n_out, d), rows.dtype),
        scratch_types=[pltpu.VMEM((SC_WINDOW,), I32), pltpu.VMEM((SC_WINDOW,), I32),
                       pltpu.VMEM((SC_WINDOW, d), rows.dtype)],
        name="sc_scatter_rows",
    )
    def scatter(rows_hbm, idx0_hbm, idx1_hbm, out_hbm, i0_v, i1_v, rows_v):
        wid = lax.axis_index("s") * nc + lax.axis_index("c")

        @pl.loop(0, n_win)
        def _(w):
            base = wid * per_worker + w * SC_WINDOW
            pltpu.sync_copy(rows_hbm.at[pl.ds(base, SC_WINDOW)], rows_v)
            pltpu.sync_copy(idx0_hbm.at[pl.ds(base, SC_WINDOW)], i0_v)
            pltpu.sync_copy(idx1_hbm.at[pl.ds(base, SC_WINDOW)], i1_v)
            pltpu.sync_copy(rows_v, out_hbm.at[i0_v])
            pltpu.sync_copy(rows_v, out_hbm.at[i1_v])

    return scatter(rows, idx0, idx1)


def _sc_gather_rows(table, idx):
    n = idx.shape[0]
    d = table.shape[1]
    nc, ns = _sc_workers()
    per_worker = n // (nc * ns)
    n_win = per_worker // SC_WINDOW
    mesh = plsc.VectorSubcoreMesh(core_axis_name="c", subcore_axis_name="s")

    @functools.partial(
        pl.kernel, mesh=mesh,
        out_type=jax.ShapeDtypeStruct((n, d), table.dtype),
        scratch_types=[pltpu.VMEM((SC_WINDOW,), I32), pltpu.VMEM((SC_WINDOW, d), table.dtype)],
        name="sc_gather_rows",
    )
    def gather(table_hbm, idx_hbm, out_hbm, i_v, rows_v):
        wid = lax.axis_index("s") * nc + lax.axis_index("c")

        @pl.loop(0, n_win)
        def _(w):
            base = wid * per_worker + w * SC_WINDOW
            pltpu.sync_copy(idx_hbm.at[pl.ds(base, SC_WINDOW)], i_v)
            pltpu.sync_copy(table_hbm.at[i_v], rows_v)
            pltpu.sync_copy(rows_v, out_hbm.at[pl.ds(base, SC_WINDOW)])

    return gather(table, idx)


def _moe_kernel(te_ref, tv_ref, x_ref, wg_ref, wu_ref, wd_ref, o_ref, wg_bf, wu_bf, wd_bf):
    i = pl.program_id(0)
    valid = tv_ref[i]
    tm = x_ref.shape[0]

    @pl.when((i == 0) | (te_ref[i] != te_ref[jnp.maximum(i - 1, 0)]))
    def _():
        wg_bf[...] = wg_ref[...].astype(BF16)
        wu_bf[...] = wu_ref[...].astype(BF16)
        wd_bf[...] = wd_ref[...].astype(BF16)

    group = min(STAGE_ROWS, tm)
    for r0 in range(0, tm, group):
        rows = slice(r0, r0 + group)

        @pl.when(valid > r0)
        def _():
            live = lax.broadcasted_iota(I32, (group, 1), 0) < valid - r0
            lo, hi = _unpack_bf16_pair(jnp.where(live, x_ref[rows, :], 0))
            x = jnp.concatenate([lo, hi], axis=1).astype(BF16)
            a = jnp.dot(x, wg_bf[...], preferred_element_type=F32)
            b = jnp.dot(x, wu_bf[...], preferred_element_type=F32)
            hmid = (a * _sigmoid(a) * b).astype(BF16)
            y = jnp.dot(hmid, wd_bf[...], preferred_element_type=F32)
            o_ref[rows, :] = _pack_bf16_pair(y[:, :D_PACK], y[:, D_PACK:])

        @pl.when(valid <= r0)
        def _():
            o_ref[rows, :] = jnp.zeros((group, D_PACK), I32)


def _moe(xs, tile_expert, tile_valid, w_g, w_u, w_d, tm):
    n_tiles = xs.shape[0] // tm
    grid_spec = pltpu.PrefetchScalarGridSpec(
        num_scalar_prefetch=2,
        grid=(n_tiles,),
        in_specs=[
            pl.BlockSpec((tm, D_PACK), lambda i, te, tv: (i, 0)),
            pl.BlockSpec((None, D_MODEL, D_EXPERT), lambda i, te, tv: (te[i], 0, 0)),
            pl.BlockSpec((None, D_MODEL, D_EXPERT), lambda i, te, tv: (te[i], 0, 0)),
            pl.BlockSpec((None, D_EXPERT, D_MODEL), lambda i, te, tv: (te[i], 0, 0)),
        ],
        out_specs=pl.BlockSpec((tm, D_PACK), lambda i, te, tv: (i, 0)),
        scratch_shapes=[pltpu.VMEM((D_MODEL, D_EXPERT), BF16), pltpu.VMEM((D_MODEL, D_EXPERT), BF16),
                        pltpu.VMEM((D_EXPERT, D_MODEL), BF16)],
    )
    return pl.pallas_call(
        _moe_kernel,
        grid_spec=grid_spec,
        out_shape=jax.ShapeDtypeStruct(xs.shape, I32),
        compiler_params=pltpu.CompilerParams(
            dimension_semantics=("arbitrary",), vmem_limit_bytes=V7X_VMEM_LIMIT),
        name="experts",
    )(tile_expert, tile_valid, xs, w_g, w_u, w_d)


def _final_kernel(x1_ref, y0_ref, y1_ref, rw_ref, mod_ref, fg_ref, o_ref):
    y0lo, y0hi = _unpack_bf16_pair(y0_ref[...])
    y1lo, y1hi = _unpack_bf16_pair(y1_ref[...])
    w0 = rw_ref[:, 0:1]
    w1 = rw_ref[:, 1:2]
    moe = jnp.concatenate([w0 * y0lo + w1 * y1lo, w0 * y0hi + w1 * y1hi], axis=1)
    x2 = x1_ref[...] + mod_ref[5:6, :] * moe
    ms = jnp.mean(x2 * x2, axis=-1, keepdims=True)
    o_ref[...] = x2 * lax.rsqrt(ms + EPS) * fg_ref[...]


def _final(x1, yb, rw, mod, final_g, tm):
    B, T, _ = x1.shape
    n_blocks = B * T // tm
    flat = lambda b, i: (b * (T // tm) + i, 0)
    return pl.pallas_call(
        _final_kernel,
        grid=(B, T // tm),
        in_specs=[
            pl.BlockSpec((None, tm, D_MODEL), lambda b, i: (b, i, 0)),
            pl.BlockSpec((tm, D_PACK), flat),
            pl.BlockSpec((tm, D_PACK), lambda b, i: (n_blocks + b * (T // tm) + i, 0)),
            pl.BlockSpec((tm, LANES), flat),
            pl.BlockSpec((None, 6, D_MODEL), lambda b, i: (b, 0, 0)),
            _const_spec((1, D_MODEL)),
        ],
        out_specs=pl.BlockSpec((None, tm, D_MODEL), lambda b, i: (b, i, 0)),
        out_shape=jax.ShapeDtypeStruct((B, T, D_MODEL), F32),
        compiler_params=pltpu.CompilerParams(
            dimension_semantics=("parallel", "parallel"), vmem_limit_bytes=V7X_VMEM_LIMIT),
        name="final_combine",
    )(x1, yb, yb, rw, mod, final_g)


def _rope_tables(T):
    rows = T // GRID_W
    row = jnp.repeat(jnp.arange(rows, dtype=F32), GRID_W)
    col = jnp.tile(jnp.arange(GRID_W, dtype=F32), rows)
    inv_freq = ROPE_THETA ** (-jnp.arange(0, 64, 2, dtype=F32) / 64)
    ang_r = row[:, None] * inv_freq[None, :]
    ang_c = col[:, None] * inv_freq[None, :]
    ang = jnp.concatenate([ang_r, ang_r, ang_c, ang_c], axis=-1)
    sign = jnp.where((jnp.arange(HEAD_DIM) % 64) < 32, -1.0, 1.0).astype(F32)
    return jnp.cos(ang), jnp.sin(ang) * sign[None, :]


def _dispatch_plan(counts, tm, n_tiles):
    tiles_per_e = (counts + tm - 1) // tm
    tile_end = jnp.cumsum(tiles_per_e)
    tile_start = tile_end - tiles_per_e
    tile_id = jnp.arange(n_tiles, dtype=I32)
    te = jnp.minimum(jnp.searchsorted(tile_end, tile_id, side="right").astype(I32), N_EXPERTS - 1)
    live = jnp.clip(counts[te] - (tile_id - tile_start[te]) * tm, 0, tm)
    tv = jnp.where(tile_id < tile_end[-1], live, 0).astype(I32)
    return tile_start.astype(I32), te, tv


def _slot_kernel(start_ref, meta_ref, pos_ref, *, tile_rows):
    e = meta_ref[0:2, :]
    first_tile = jnp.zeros_like(e)
    for ex in range(N_EXPERTS):
        first_tile = jnp.where(e == ex, start_ref[ex], first_tile)
    pos_ref[...] = first_tile * tile_rows + meta_ref[2:4, :]


def _slots(meta, tile_start, tile_rows):
    n = meta.shape[1]
    tn = min(n, 8192)
    grid_spec = pltpu.PrefetchScalarGridSpec(
        num_scalar_prefetch=1,
        grid=(n // tn,),
        in_specs=[pl.BlockSpec((META_ROWS, tn), lambda i, start: (0, i))],
        out_specs=pl.BlockSpec((2, tn), lambda i, start: (0, i)),
    )
    return pl.pallas_call(
        functools.partial(_slot_kernel, tile_rows=tile_rows),
        grid_spec=grid_spec,
        out_shape=jax.ShapeDtypeStruct((2, n), I32),
        name="dispatch_slots",
    )(tile_start, meta)


def _pick_tile(T, want):
    t = min(T, want)
    assert T % t == 0
    return t


def _trunk(x, c, p):
    B, T, _ = x.shape
    N = B * T
    tm = _pick_tile(T, 512)
    tq = _pick_tile(T, 512)
    tk = _pick_tile(T, 256)
    tme = _pick_tile(N, 512)
    cos, sin_signed = _rope_tables(T)
    mod = _modulation(c, p["w_ada"], p["b_ada"]).reshape(B, 6, D_MODEL)
    u, qt, qb, k, vt = _stage_a(x, mod, p["n1g"], p["w_qkvu"], p["qg"], p["kg"], cos, sin_signed, tm)
    attn = lax.cond(
        p["bounded_softmax_ok"],
        functools.partial(_attention, tq=tq, tk=tk, running_max=False),
        functools.partial(_attention, tq=tq, tk=tk, running_max=True),
        qt, qb, k, vt)
    x1, h2p, meta, rw, counts = _stage_c(x, mod, p["n1g"], p["n2g"], u, attn, p["w_gate"], p["w_mix"],
                                         p["pool_scale"], p["w_pp"], p["w_ap"], p["w_o"], p["w_r"], p["b_r"], tm)
    n_tiles = 2 * N // tme + N_EXPERTS
    tile_start, te, tv = _dispatch_plan(counts[0, :N_EXPERTS].astype(I32), tme, n_tiles)
    pos = _slots(meta, tile_start, tme)
    xs = _sc_scatter_rows(h2p, pos[0], pos[1], n_tiles * tme)
    ys = _moe(xs, te, tv, p["w_eg"], p["w_eu"], p["w_ed"], tme)
    yb = _sc_gather_rows(ys, pos.reshape(-1))
    return _final(x1, yb, rw, mod, p["final_g"], tm)


def kernel(x_prompt, x_sample, c_prompt, c_sample, w_ada, b_ada, norm1_g, norm2_g, w_in, q_norm_g,
           k_norm_g, w_pool_mix, pool_scale, w_pool_proj, w_attn_proj, w_o, w_router_group,
           b_router_group, w_router_expert, b_router_expert, w_exp_gate, w_exp_up, w_exp_down, final_g):
    assert w_ada.shape[0] == 1, "single-layer block"
    n_r = N_GROUPS + N_EXPERTS
    w_r = jnp.concatenate([w_router_group[0], w_router_expert[0],
                           jnp.zeros((D_MODEL, LANES - n_r), F32)], axis=1)
    w_r_hi = w_r.astype(BF16)
    b_r = jnp.concatenate([b_router_group[0], b_router_expert[0], jnp.zeros((LANES - n_r,), F32)])
    score_bound = (1.01 * LOG2E * np.sqrt(HEAD_DIM)) * jnp.max(jnp.abs(q_norm_g[0])) * jnp.max(jnp.abs(k_norm_g[0]))
    p = dict(
        bounded_softmax_ok=2.0 * score_bound <= EXP2_SAFE_SPAN,
        w_ada=w_ada[0], b_ada=b_ada[0],
        n1g=norm1_g[0].reshape(1, D_MODEL), n2g=norm2_g[0].reshape(1, D_MODEL),
        w_qkvu=w_in[0][:, :D_QKVU].astype(BF16), w_gate=w_in[0][:, D_QKVU:].astype(BF16),
        qg=q_norm_g[0].reshape(1, HEAD_DIM), kg=k_norm_g[0].reshape(1, HEAD_DIM),
        w_mix=w_pool_mix[0].astype(BF16), pool_scale=pool_scale[0].reshape(1, D_POOL),
        w_pp=w_pool_proj[0].astype(BF16), w_ap=w_attn_proj[0].astype(BF16), w_o=w_o[0].astype(BF16),
        w_r=jnp.concatenate([w_r_hi, (w_r - w_r_hi.astype(F32)).astype(BF16)], axis=1), b_r=b_r.reshape(1, LANES),
        w_eg=w_exp_gate[0], w_eu=w_exp_up[0], w_ed=w_exp_down[0],
        final_g=final_g.reshape(1, D_MODEL),
    )
    return _trunk(x_prompt, c_prompt, p), _trunk(x_sample, c_sample, p)
```

```python
import functools

import numpy as np
import jax
import jax.numpy as jnp
from jax import lax
from jax.experimental import pallas as pl
from jax.experimental.pallas import tpu as pltpu
from jax.experimental.pallas import tpu_sc as plsc

F32 = jnp.float32
BF16 = jnp.bfloat16
I32 = jnp.int32

D_MODEL = 1024
GRID_W = 64
POOL_WINDOWS = (2, 4, 8, 16)
POOL_GC = 128
D_POOL = 512
HEAD_DIM = 128
N_Q_HEADS = 4
N_KV_HEADS = 2
D_ATTN = 512
D_KV = 256
ROPE_THETA = 10000.0
N_GROUPS = 4
EXPERTS_PER_GROUP = 8
N_EXPERTS = 32
D_EXPERT = 256
EPS = 1e-6
LOG2E = 1.4426950408889634
EXP2_SAFE_SPAN = 100.0
D_QKVU = D_POOL + D_ATTN + 2 * D_KV
POOL_HALO = 16
STAGE_ROWS = 256
MIXER_ROWS = 512
ATTN_TILES_PER_TRIP = (10, 6, 2)
META_ROWS = 8
LANES = 128
D_PACK = D_MODEL // 2

V7X_VMEM_LIMIT = 56 * 1024 * 1024
SC_WINDOW = 64


def _const_spec(shape):
    nd = len(shape)
    return pl.BlockSpec(shape, lambda *_: (0,) * nd, pipeline_mode=pl.Buffered(1))


def _sigmoid(x):
    return 0.5 * jnp.tanh(0.5 * x) + 0.5


def _pack_bf16_pair(lo, hi):
    lo_bits = lax.bitcast_convert_type(lo.astype(BF16).astype(F32), jnp.uint32) >> 16
    hi_bits = lax.bitcast_convert_type(hi.astype(BF16).astype(F32), jnp.uint32) & jnp.uint32(0xFFFF0000)
    return lax.bitcast_convert_type(lo_bits | hi_bits, I32)


def _unpack_bf16_pair(packed):
    u = lax.bitcast_convert_type(packed, jnp.uint32)
    lo = lax.bitcast_convert_type(u << 16, F32)
    hi = lax.bitcast_convert_type(u & jnp.uint32(0xFFFF0000), F32)
    return lo, hi


def _mod_kernel(c_ref, w_ref, b_ref, o_ref):
    c = c_ref[...]
    s = c * _sigmoid(c)
    o_ref[...] = jnp.dot(s, w_ref[...], preferred_element_type=F32,
                         precision=lax.Precision.HIGHEST) + b_ref[...]


def _modulation(c, w_ada, b_ada):
    B = c.shape[0]
    n_out = w_ada.shape[1]
    bn = D_MODEL
    return pl.pallas_call(
        _mod_kernel,
        grid=(n_out // bn,),
        in_specs=[pl.BlockSpec((B, D_MODEL), lambda j: (0, 0)),
                  pl.BlockSpec((D_MODEL, bn), lambda j: (0, j)),
                  pl.BlockSpec((1, bn), lambda j: (0, j))],
        out_specs=pl.BlockSpec((B, bn), lambda j: (0, j)),
        out_shape=jax.ShapeDtypeStruct((B, n_out), F32),
        name="modulation",
    )(c, w_ada, b_ada.reshape(1, n_out))


def _scaled_norm(x, gain, shift):
    ms = jnp.mean(x * x, axis=-1, keepdims=True)
    return x * lax.rsqrt(ms + EPS) * gain + shift


def _stage_a_kernel(x_ref, mod_ref, n1g_ref, w_ref, qg_ref, kg_ref, cos_ref, sin_ref,
                    u_ref, qt_ref, qb_ref, k_ref, vt_ref):
    tm = x_ref.shape[0]
    gain1 = n1g_ref[...] * (1.0 + mod_ref[1:2, :])
    shift1 = mod_ref[0:1, :]
    lane = lax.broadcasted_iota(I32, (1, HEAD_DIM), 1)
    first_half = (lane % 64) < 32
    qg = qg_ref[...] * (LOG2E / np.sqrt(HEAD_DIM))
    k_norm_max = jnp.max(jnp.abs(kg_ref[...]), axis=-1, keepdims=True) * (np.sqrt(HEAD_DIM) * 1.01)

    for r0 in range(0, tm, STAGE_ROWS):
        rows = slice(r0, r0 + min(STAGE_ROWS, tm))
        h = _scaled_norm(x_ref[rows, :], gain1, shift1)
        z = jnp.dot(h.astype(BF16), w_ref[...], preferred_element_type=F32)
        u_ref[rows, :] = z[:, :D_POOL].astype(BF16)
        cos = cos_ref[rows, :]
        sin_signed = sin_ref[rows, :]

        def norm_rope(xh, g):
            ms = jnp.mean(xh * xh, axis=-1, keepdims=True)
            xn = xh * lax.rsqrt(ms + EPS) * g
            rot = jnp.where(first_half, pltpu.roll(xn, 96, 1), pltpu.roll(xn, 32, 1))
            return xn * cos + rot * sin_signed

        for hq in range(N_Q_HEADS):
            o = D_POOL + hq * HEAD_DIM
            q_t = norm_rope(z[:, o:o + HEAD_DIM], qg).T.astype(BF16)
            qt_ref[hq, :, rows] = q_t
            q_f = q_t.astype(F32)
            qb_ref[hq, :, rows] = jnp.sqrt(jnp.sum(q_f * q_f, axis=0, keepdims=True)) * k_norm_max
        for hk in range(N_KV_HEADS):
            o = D_POOL + D_ATTN + hk * HEAD_DIM
            k_ref[rows, hk * HEAD_DIM:(hk + 1) * HEAD_DIM] = norm_rope(z[:, o:o + HEAD_DIM], kg_ref[...]).astype(BF16)
            o = D_POOL + D_ATTN + D_KV + hk * HEAD_DIM
            vt_ref[hk, :, rows] = z[:, o:o + HEAD_DIM].T.astype(BF16)


def _stage_a(x, mod, n1g, w_qkvu, qg, kg, cos, sin_signed, tm):
    B, T, _ = x.shape
    return pl.pallas_call(
        _stage_a_kernel,
        grid=(B, T // tm),
        in_specs=[
            pl.BlockSpec((None, tm, D_MODEL), lambda b, i: (b, i, 0)),
            pl.BlockSpec((None, 6, D_MODEL), lambda b, i: (b, 0, 0)),
            _const_spec((1, D_MODEL)),
            _const_spec((D_MODEL, D_QKVU)),
            _const_spec((1, HEAD_DIM)),
            _const_spec((1, HEAD_DIM)),
            pl.BlockSpec((tm, HEAD_DIM), lambda b, i: (i, 0)),
            pl.BlockSpec((tm, HEAD_DIM), lambda b, i: (i, 0)),
        ],
        out_specs=[
            pl.BlockSpec((None, tm, D_POOL), lambda b, i: (b, i, 0)),
            pl.BlockSpec((None, N_Q_HEADS, HEAD_DIM, tm), lambda b, i: (b, 0, 0, i)),
            pl.BlockSpec((None, N_Q_HEADS, 1, tm), lambda b, i: (b, 0, 0, i)),
            pl.BlockSpec((None, tm, D_KV), lambda b, i: (b, i, 0)),
            pl.BlockSpec((None, N_KV_HEADS, HEAD_DIM, tm), lambda b, i: (b, 0, 0, i)),
        ],
        out_shape=[
            jax.ShapeDtypeStruct((B, T, D_POOL), BF16),
            jax.ShapeDtypeStruct((B, N_Q_HEADS, HEAD_DIM, T), BF16),
            jax.ShapeDtypeStruct((B, N_Q_HEADS, 1, T), F32),
            jax.ShapeDtypeStruct((B, T, D_KV), BF16),
            jax.ShapeDtypeStruct((B, N_KV_HEADS, HEAD_DIM, T), BF16),
        ],
        compiler_params=pltpu.CompilerParams(
            dimension_semantics=("parallel", "parallel"), vmem_limit_bytes=V7X_VMEM_LIMIT),
        name="stage_a",
    )(x, mod, n1g, w_qkvu, qg, kg, cos, sin_signed)


def _write_attn_output(o_ref, acc, l, tq):
    o = acc * (1.0 / l)
    o_ref[:, :HEAD_DIM] = o[:, :tq].T.astype(BF16)
    o_ref[:, HEAD_DIM:] = o[:, tq:].T.astype(BF16)


def _attn_bounded_kernel(qt_ref, qb_ref, k_ref, vt_ref, o_ref, st_ref, p_ref, acc_ref, l_ref, *, tk):
    T = k_ref.shape[0]
    tq = qt_ref.shape[2]
    nq = 2 * tq
    nk = T // tk
    qt = jnp.concatenate([qt_ref[0], qt_ref[1]], axis=1)
    qb = jnp.concatenate([qb_ref[0], qb_ref[1]], axis=1)

    def scores(j, slot):
        off = pl.multiple_of(j * tk, tk)
        st_ref[slot] = jnp.dot(k_ref[pl.ds(off, tk), :], qt, preferred_element_type=F32)

    def weights(slot):
        p = jnp.exp2(st_ref[slot] - qb)
        l_ref[...] += jnp.sum(p.reshape(tk // 8, 8, nq), axis=0)
        p_ref[slot] = p.astype(BF16)

    def values(j, slot):
        off = pl.multiple_of(j * tk, tk)
        acc_ref[...] += jnp.dot(vt_ref[:, pl.ds(off, tk)], p_ref[slot], preferred_element_type=F32)

    acc_ref[...] = jnp.zeros_like(acc_ref)
    l_ref[...] = jnp.zeros_like(l_ref)
    scores(0, 0)
    weights(0)
    scores(1, 1)

    unroll = next(u for u in ATTN_TILES_PER_TRIP if (nk - 2) % u == 0)

    @pl.loop(0, (nk - 2) // unroll)
    def _(g):
        for d in range(0, unroll, 2):
            a = unroll * g + d + 1
            values(a - 1, 0)
            weights(1)
            scores(a + 1, 0)
            values(a, 1)
            weights(0)
            scores(a + 2, 1)

    values(nk - 2, 0)
    weights(1)
    values(nk - 1, 1)
    _write_attn_output(o_ref, acc_ref[...], jnp.sum(l_ref[...], axis=0, keepdims=True), tq)


def _attn_online_kernel(qt_ref, qb_ref, k_ref, vt_ref, o_ref, *, tk):
    del qb_ref
    T = k_ref.shape[0]
    tq = qt_ref.shape[2]
    nq = 2 * tq
    qt = jnp.concatenate([qt_ref[0], qt_ref[1]], axis=1)

    def body(s, carry):
        m, l, acc = carry
        off = pl.multiple_of(s * tk, tk)
        st = jnp.dot(k_ref[pl.ds(off, tk), :], qt, preferred_element_type=F32)
        m_new = jnp.maximum(m, jnp.max(st, axis=0, keepdims=True))
        alpha = jnp.exp2(m - m_new)
        p = jnp.exp2(st - m_new)
        l = alpha * l + jnp.sum(p, axis=0, keepdims=True)
        pv = jnp.dot(vt_ref[:, pl.ds(off, tk)], p.astype(BF16), preferred_element_type=F32)
        return m_new, l, alpha * acc + pv

    init = (jnp.full((1, nq), -jnp.inf, F32), jnp.zeros((1, nq), F32), jnp.zeros((HEAD_DIM, nq), F32))
    _, l, acc = lax.fori_loop(0, T // tk, body, init)
    _write_attn_output(o_ref, acc, l, tq)


def _attention(qt, qb, k, vt, tq, tk, running_max):
    B, _, _, T = qt.shape
    group = N_Q_HEADS // N_KV_HEADS
    nq = group * tq
    if running_max:
        body, scratch, name = functools.partial(_attn_online_kernel, tk=tk), [], "attention_online_max"
    else:
        assert (T // tk) % 2 == 0 and T // tk >= 2
        body = functools.partial(_attn_bounded_kernel, tk=tk)
        scratch = [pltpu.VMEM((2, tk, nq), F32), pltpu.VMEM((2, tk, nq), BF16),
                   pltpu.VMEM((HEAD_DIM, nq), F32), pltpu.VMEM((8, nq), F32)]
        name = "attention_bounded"
    return pl.pallas_call(
        body,
        grid=(B, N_KV_HEADS, T // tq),
        in_specs=[
            pl.BlockSpec((None, group, HEAD_DIM, tq), lambda b, j, i: (b, j, 0, i)),
            pl.BlockSpec((None, group, 1, tq), lambda b, j, i: (b, j, 0, i)),
            pl.BlockSpec((None, T, HEAD_DIM), lambda b, j, i: (b, 0, j)),
            pl.BlockSpec((None, None, HEAD_DIM, T), lambda b, j, i: (b, j, 0, 0)),
        ],
        out_specs=pl.BlockSpec((None, tq, group * HEAD_DIM), lambda b, j, i: (b, i, j)),
        out_shape=jax.ShapeDtypeStruct((B, T, D_ATTN), BF16),
        scratch_shapes=scratch,
        compiler_params=pltpu.CompilerParams(
            dimension_semantics=("parallel", "parallel", "parallel"),
            vmem_limit_bytes=V7X_VMEM_LIMIT),
        name=name,
    )(qt, qb, k, vt)


def _route(logits):
    shape = logits.shape
    lane = lax.broadcasted_iota(I32, shape, 1).astype(F32)
    neg = jnp.float32(-jnp.inf)
    far = jnp.float32(LANES)

    def top(vals):
        v = jnp.max(vals, axis=1, keepdims=True)
        i = jnp.min(jnp.where(vals == v, lane, far), axis=1, keepdims=True)
        return v, i

    is_group = lane < N_GROUPS
    lg = jnp.where(is_group, logits, neg)
    gmax, gsel = top(lg)
    denom = jnp.sum(jnp.where(is_group, jnp.exp(lg - gmax), 0.0), axis=1, keepdims=True)
    p_group = 1.0 / denom
    base = N_GROUPS + EXPERTS_PER_GROUP * gsel
    le = jnp.where((lane >= base) & (lane < base + EXPERTS_PER_GROUP), logits, neg)
    v1, i1 = top(le)
    v2, i2 = top(jnp.where(lane == i1, neg, le))
    e21 = jnp.exp(v2 - v1)
    w1 = p_group / (1.0 + e21)
    w2 = w1 * e21
    rw = jnp.where(lane == 0, w1, jnp.where(lane == 1, w2, 0.0))
    return i1 - N_GROUPS, i2 - N_GROUPS, rw


def _pool_deviation(ue_ref, lvl_ref, pool_ref, tile_start, tm, seq_len):
    n = tm + 2 * POOL_HALO
    t = tile_start + lax.broadcasted_iota(I32, (tm, 1), 0)
    for gi, w in enumerate(POOL_WINDOWS):
        half = w // 2
        cols = slice(gi * POOL_GC, (gi + 1) * POOL_GC)
        level = ue_ref[0:n, cols] + ue_ref[1:n + 1, cols]
        span, buf = 2, 0
        lvl_ref[buf, 0:n, :] = level
        while span < w:
            level = lvl_ref[buf, 0:n, :] + lvl_ref[buf, span:span + n, :]
            span, buf = 2 * span, 1 - buf
            lvl_ref[buf, 0:n, :] = level
        wsum = lvl_ref[buf, POOL_HALO - half:POOL_HALO - half + tm, :]
        cnt = (jnp.minimum(t + half, seq_len) - jnp.maximum(t - half, 0)).astype(F32)
        p = wsum * (1.0 / cnt) - ue_ref[POOL_HALO:POOL_HALO + tm, cols]
        pool_ref[:, cols] = p.astype(BF16)


def _stage_c_kernel(x_ref, mod_ref, n1g_ref, n2g_ref, u_ref, up_ref, un_ref, attn_ref,
                    wgate_ref, wmix_ref, ps_ref, wpp_ref, wap_ref, wo_ref, wr_ref, br_ref, ltri_ref,
                    x1_ref, h2_ref, meta_ref, rw_ref, count_ref, ue_ref, lvl_ref, pool_ref, run_ref, *, seq_len):
    i = pl.program_id(1)
    tm = x_ref.shape[0]

    @pl.when((pl.program_id(0) == 0) & (i == 0))
    def _():
        run_ref[...] = jnp.zeros_like(run_ref)

    gain1 = n1g_ref[...] * (1.0 + mod_ref[1:2, :])
    shift1 = mod_ref[0:1, :]
    gain2 = n2g_ref[...] * (1.0 + mod_ref[4:5, :])
    shift2 = mod_ref[3:4, :]
    res_gate = mod_ref[2:3, :]

    ue_ref[0:POOL_HALO, :] = up_ref[...].astype(F32) * (i > 0).astype(F32)
    ue_ref[POOL_HALO:POOL_HALO + tm, :] = u_ref[...].astype(F32)
    ue_ref[POOL_HALO + tm:2 * POOL_HALO + tm, :] = un_ref[...].astype(F32) * (i < pl.num_programs(1) - 1).astype(F32)
    ue_ref[2 * POOL_HALO + tm:, :] = jnp.zeros((POOL_HALO, D_POOL), F32)
    lvl_ref[:, 2 * POOL_HALO + tm:, :] = jnp.zeros((2, POOL_HALO, POOL_GC), F32)
    _pool_deviation(ue_ref, lvl_ref, pool_ref, i * tm, tm, seq_len)

    for r0 in range(0, tm, MIXER_ROWS):
        rows = slice(r0, r0 + min(MIXER_ROWS, tm))
        x = x_ref[rows, :]
        h = _scaled_norm(x, gain1, shift1)
        gates = _sigmoid(jnp.dot(h.astype(BF16), wgate_ref[...], preferred_element_type=F32))
        mixed_groups = []
        for gi in range(len(POOL_WINDOWS)):
            cols = slice(gi * POOL_GC, (gi + 1) * POOL_GC)
            pm = jnp.dot(pool_ref[rows, cols], wmix_ref[gi], preferred_element_type=F32)
            mixed_groups.append(pm * ps_ref[:, cols])
        pool = jnp.concatenate(mixed_groups, axis=1)
        pool_out = jnp.dot(pool.astype(BF16), wpp_ref[...], preferred_element_type=F32)
        attn_out = jnp.dot(attn_ref[rows, :], wap_ref[...], preferred_element_type=F32)
        merged = gates[:, :D_MODEL] * pool_out + gates[:, D_MODEL:] * attn_out
        mixed = jnp.dot(merged.astype(BF16), wo_ref[...], preferred_element_type=F32)
        x1 = x + res_gate * mixed
        x1_ref[rows, :] = x1

        h2 = _scaled_norm(x1, gain2, shift2)
        h2_ref[rows, :] = _pack_bf16_pair(h2[:, :D_PACK], h2[:, D_PACK:])
        h_hi = h2.astype(BF16)
        h_lo = (h2 - h_hi.astype(F32)).astype(BF16)
        parts = (jnp.dot(h_hi, wr_ref[...], preferred_element_type=F32)
                 + jnp.dot(h_lo, wr_ref[...], preferred_element_type=F32))
        logits = parts[:, :LANES] + parts[:, LANES:] + br_ref[...]
        e1, e2, rw = _route(logits)
        rw_ref[rows, :] = rw

        lane = lax.broadcasted_iota(I32, logits.shape, 1).astype(F32)
        hit1 = lane == e1
        hit2 = lane == e2
        taken = jnp.where(hit1 | hit2, 1.0, 0.0).astype(BF16)
        before = run_ref[...] + jnp.dot(ltri_ref[...], taken, preferred_element_type=F32)
        rank1 = jnp.sum(jnp.where(hit1, before, 0.0), axis=1, keepdims=True)
        rank2 = jnp.sum(jnp.where(hit2, before, 0.0), axis=1, keepdims=True)
        run_ref[...] = before[-1:, :] + taken[-1:, :].astype(F32)
        meta = jnp.where(lane == 0, e1, jnp.where(lane == 1, e2,
                         jnp.where(lane == 2, rank1, jnp.where(lane == 3, rank2, 0.0))))
        meta_ref[:, rows] = meta.T[0:META_ROWS, :].astype(I32)
    count_ref[...] = jnp.broadcast_to(run_ref[...], count_ref.shape)


def _stage_c(x, mod, n1g, n2g, u, attn, w_gate, w_mix, pool_scale, w_pp, w_ap, w_o, w_r, b_r, tm):
    B, T, _ = x.shape
    hb = tm // POOL_HALO
    n_halo_blocks = T // POOL_HALO
    tok = lambda b, i: (b, i, 0)
    flat = lambda b, i: (b * (T // tm) + i, 0)
    group_rows = min(MIXER_ROWS, tm)
    ltri = jnp.tri(group_rows, k=-1, dtype=BF16)
    return pl.pallas_call(
        functools.partial(_stage_c_kernel, seq_len=T),
        grid=(B, T // tm),
        in_specs=[
            pl.BlockSpec((None, tm, D_MODEL), tok),
            pl.BlockSpec((None, 6, D_MODEL), lambda b, i: (b, 0, 0)),
            _const_spec((1, D_MODEL)),
            _const_spec((1, D_MODEL)),
            pl.BlockSpec((None, tm, D_POOL), tok),
            pl.BlockSpec((None, POOL_HALO, D_POOL), lambda b, i: (b, jnp.maximum(i * hb - 1, 0), 0)),
            pl.BlockSpec((None, POOL_HALO, D_POOL),
                         lambda b, i: (b, jnp.minimum((i + 1) * hb, n_halo_blocks - 1), 0)),
            pl.BlockSpec((None, tm, D_ATTN), tok),
            _const_spec((D_MODEL, 2 * D_MODEL)),
            _const_spec((len(POOL_WINDOWS), POOL_GC, POOL_GC)),
            _const_spec((1, D_POOL)),
            _const_spec((D_POOL, D_MODEL)),
            _const_spec((D_ATTN, D_MODEL)),
            _const_spec((D_MODEL, D_MODEL)),
            _const_spec((D_MODEL, 2 * LANES)),
            _const_spec((1, LANES)),
            _const_spec((group_rows, group_rows)),
        ],
        out_specs=[
            pl.BlockSpec((None, tm, D_MODEL), tok),
            pl.BlockSpec((tm, D_PACK), flat),
            pl.BlockSpec((META_ROWS, tm), lambda b, i: (0, b * (T // tm) + i)),
            pl.BlockSpec((tm, LANES), flat),
            pl.BlockSpec((META_ROWS, LANES), lambda b, i: (0, 0)),
        ],
        out_shape=[
            jax.ShapeDtypeStruct((B, T, D_MODEL), F32),
            jax.ShapeDtypeStruct((B * T, D_PACK), I32),
            jax.ShapeDtypeStruct((META_ROWS, B * T), I32),
            jax.ShapeDtypeStruct((B * T, LANES), F32),
            jax.ShapeDtypeStruct((META_ROWS, LANES), F32),
        ],
        scratch_shapes=[pltpu.VMEM((tm + 3 * POOL_HALO, D_POOL), F32),
                        pltpu.VMEM((2, tm + 3 * POOL_HALO, POOL_GC), F32),
                        pltpu.VMEM((tm, D_POOL), BF16),
                        pltpu.VMEM((1, LANES), F32)],
        compiler_params=pltpu.CompilerParams(
            dimension_semantics=("arbitrary", "arbitrary"), vmem_limit_bytes=V7X_VMEM_LIMIT),
        name="stage_c",
    )(x, mod, n1g, n2g, u, u, u, attn, w_gate, w_mix, pool_scale, w_pp, w_ap, w_o, w_r, b_r, ltri)


def _sc_workers():
    info = plsc.get_sparse_core_info()
    return info.num_cores, info.num_subcores


def _sc_scatter_rows(rows, idx0, idx1, n_out):
    n, d = rows.shape
    nc, ns = _sc_workers()
    per_worker = n // (nc * ns)
    n_win = per_worker // SC_WINDOW
    mesh = plsc.VectorSubcoreMesh(core_axis_name="c", subcore_axis_name="s")

    @functools.partial(
        pl.kernel, mesh=mesh,
        out_type=jax.ShapeDtypeStruct((n_out, d), rows.dtype),
        scratch_types=[pltpu.VMEM((SC_WINDOW,), I32), pltpu.VMEM((SC_WINDOW,), I32),
                       pltpu.VMEM((SC_WINDOW, d), rows.dtype)],
        name="sc_scatter_rows",
    )
    def scatter(rows_hbm, idx0_hbm, idx1_hbm, out_hbm, i0_v, i1_v, rows_v):
        wid = lax.axis_index("s") * nc + lax.axis_index("c")

        @pl.loop(0, n_win)
        def _(w):
            base = wid * per_worker + w * SC_WINDOW
            pltpu.sync_copy(rows_hbm.at[pl.ds(base, SC_WINDOW)], rows_v)
            pltpu.sync_copy(idx0_hbm.at[pl.ds(base, SC_WINDOW)], i0_v)
            pltpu.sync_copy(idx1_hbm.at[pl.ds(base, SC_WINDOW)], i1_v)
            pltpu.sync_copy(rows_v, out_hbm.at[i0_v])
            pltpu.sync_copy(rows_v, out_hbm.at[i1_v])

    return scatter(rows, idx0, idx1)


def _sc_gather_rows(table, idx):
    n = idx.shape[0]
    d = table.shape[1]
    nc, ns = _sc_workers()
    per_worker = n // (nc * ns)
    n_win = per_worker // SC_WINDOW
    mesh = plsc.VectorSubcoreMesh(core_axis_name="c", subcore_axis_name="s")

    @functools.partial(
        pl.kernel, mesh=mesh,
        out_type=jax.ShapeDtypeStruct((n, d), table.dtype),
        scratch_types=[pltpu.VMEM((SC_WINDOW,), I32), pltpu.VMEM((SC_WINDOW, d), table.dtype)],
        name="sc_gather_rows",
    )
    def gather(table_hbm, idx_hbm, out_hbm, i_v, rows_v):
        wid = lax.axis_index("s") * nc + lax.axis_index("c")

        @pl.loop(0, n_win)
        def _(w):
            base = wid * per_worker + w * SC_WINDOW
            pltpu.sync_copy(idx_hbm.at[pl.ds(base, SC_WINDOW)], i_v)
            pltpu.sync_copy(table_hbm.at[i_v], rows_v)
            pltpu.sync_copy(rows_v, out_hbm.at[pl.ds(base, SC_WINDOW)])

    return gather(table, idx)


def _moe_kernel(te_ref, tv_ref, x_ref, wg_ref, wu_ref, wd_ref, o_ref, wgu_bf, wd_bf):
    i = pl.program_id(0)
    valid = tv_ref[i]
    tm = x_ref.shape[0]

    @pl.when((i == 0) | (te_ref[i] != te_ref[jnp.maximum(i - 1, 0)]))
    def _():
        wgu_bf[:, :D_EXPERT] = wg_ref[...].astype(BF16)
        wgu_bf[:, D_EXPERT:] = wu_ref[...].astype(BF16)
        wd_bf[...] = wd_ref[...].astype(BF16)

    group = min(STAGE_ROWS, tm)
    n_groups = tm // group

    def expert_rows(r0):
        rows = slice(r0, r0 + group)
        live = lax.broadcasted_iota(I32, (group, 1), 0) < valid - r0
        lo, hi = _unpack_bf16_pair(jnp.where(live, x_ref[rows, :], 0))
        x = jnp.concatenate([lo, hi], axis=1).astype(BF16)
        gu = jnp.dot(x, wgu_bf[...], preferred_element_type=F32)
        a = gu[:, :D_EXPERT]
        hmid = (a * _sigmoid(a) * gu[:, D_EXPERT:]).astype(BF16)
        y = jnp.dot(hmid, wd_bf[...], preferred_element_type=F32)
        o_ref[rows, :] = _pack_bf16_pair(y[:, :D_PACK], y[:, D_PACK:])

    for n_live in range(n_groups + 1):
        @pl.when((valid > (n_live - 1) * group) & (valid <= n_live * group))
        def _():
            for g in range(n_live):
                expert_rows(g * group)
            if n_live < n_groups:
                o_ref[n_live * group:, :] = jnp.zeros((tm - n_live * group, D_PACK), I32)


def _moe(xs, tile_expert, tile_valid, w_g, w_u, w_d, tm):
    n_tiles = xs.shape[0] // tm
    grid_spec = pltpu.PrefetchScalarGridSpec(
        num_scalar_prefetch=2,
        grid=(n_tiles,),
        in_specs=[
            pl.BlockSpec((tm, D_PACK), lambda i, te, tv: (i, 0)),
            pl.BlockSpec((None, D_MODEL, D_EXPERT), lambda i, te, tv: (te[i], 0, 0)),
            pl.BlockSpec((None, D_MODEL, D_EXPERT), lambda i, te, tv: (te[i], 0, 0)),
            pl.BlockSpec((None, D_EXPERT, D_MODEL), lambda i, te, tv: (te[i], 0, 0)),
        ],
        out_specs=pl.BlockSpec((tm, D_PACK), lambda i, te, tv: (i, 0)),
        scratch_shapes=[pltpu.VMEM((D_MODEL, 2 * D_EXPERT), BF16), pltpu.VMEM((D_EXPERT, D_MODEL), BF16)],
    )
    return pl.pallas_call(
        _moe_kernel,
        grid_spec=grid_spec,
        out_shape=jax.ShapeDtypeStruct(xs.shape, I32),
        compiler_params=pltpu.CompilerParams(
            dimension_semantics=("arbitrary",), vmem_limit_bytes=V7X_VMEM_LIMIT),
        name="experts",
    )(tile_expert, tile_valid, xs, w_g, w_u, w_d)


def _final_kernel(x1_ref, y0_ref, y1_ref, rw_ref, mod_ref, fg_ref, o_ref):
    y0lo, y0hi = _unpack_bf16_pair(y0_ref[...])
    y1lo, y1hi = _unpack_bf16_pair(y1_ref[...])
    w0 = rw_ref[:, 0:1]
    w1 = rw_ref[:, 1:2]
    moe = jnp.concatenate([w0 * y0lo + w1 * y1lo, w0 * y0hi + w1 * y1hi], axis=1)
    x2 = x1_ref[...] + mod_ref[5:6, :] * moe
    ms = jnp.mean(x2 * x2, axis=-1, keepdims=True)
    o_ref[...] = x2 * lax.rsqrt(ms + EPS) * fg_ref[...]


def _final(x1, yb, rw, mod, final_g, tm):
    B, T, _ = x1.shape
    n_blocks = B * T // tm
    flat = lambda b, i: (b * (T // tm) + i, 0)
    return pl.pallas_call(
        _final_kernel,
        grid=(B, T // tm),
        in_specs=[
            pl.BlockSpec((None, tm, D_MODEL), lambda b, i: (b, i, 0)),
            pl.BlockSpec((tm, D_PACK), flat),
            pl.BlockSpec((tm, D_PACK), lambda b, i: (n_blocks + b * (T // tm) + i, 0)),
            pl.BlockSpec((tm, LANES), flat),
            pl.BlockSpec((None, 6, D_MODEL), lambda b, i: (b, 0, 0)),
            _const_spec((1, D_MODEL)),
        ],
        out_specs=pl.BlockSpec((None, tm, D_MODEL), lambda b, i: (b, i, 0)),
        out_shape=jax.ShapeDtypeStruct((B, T, D_MODEL), F32),
        compiler_params=pltpu.CompilerParams(
            dimension_semantics=("parallel", "parallel"), vmem_limit_bytes=V7X_VMEM_LIMIT),
        name="final_combine",
    )(x1, yb, yb, rw, mod, final_g)


def _rope_tables(T):
    rows = T // GRID_W
    row = jnp.repeat(jnp.arange(rows, dtype=F32), GRID_W)
    col = jnp.tile(jnp.arange(GRID_W, dtype=F32), rows)
    inv_freq = ROPE_THETA ** (-jnp.arange(0, 64, 2, dtype=F32) / 64)
    ang_r = row[:, None] * inv_freq[None, :]
    ang_c = col[:, None] * inv_freq[None, :]
    ang = jnp.concatenate([ang_r, ang_r, ang_c, ang_c], axis=-1)
    sign = jnp.where((jnp.arange(HEAD_DIM) % 64) < 32, -1.0, 1.0).astype(F32)
    return jnp.cos(ang), jnp.sin(ang) * sign[None, :]


def _dispatch_plan(counts, tm, n_tiles):
    tiles_per_e = (counts + tm - 1) // tm
    tile_end = jnp.cumsum(tiles_per_e)
    tile_start = tile_end - tiles_per_e
    tile_id = jnp.arange(n_tiles, dtype=I32)
    te = jnp.minimum(jnp.sum((tile_id[:, None] >= tile_end[None, :]).astype(I32), axis=1), N_EXPERTS - 1)
    live = jnp.clip(counts[te] - (tile_id - tile_start[te]) * tm, 0, tm)
    tv = jnp.where(tile_id < tile_end[-1], live, 0).astype(I32)
    return tile_start.astype(I32), te, tv


def _slot_kernel(start_ref, meta_ref, pos_ref, *, tile_rows):
    e = meta_ref[0:2, :]
    first_tile = jnp.zeros_like(e)
    for ex in range(N_EXPERTS):
        first_tile = jnp.where(e == ex, start_ref[ex], first_tile)
    pos_ref[...] = first_tile * tile_rows + meta_ref[2:4, :]


def _slots(meta, tile_start, tile_rows):
    n = meta.shape[1]
    tn = min(n, 8192)
    grid_spec = pltpu.PrefetchScalarGridSpec(
        num_scalar_prefetch=1,
        grid=(n // tn,),
        in_specs=[pl.BlockSpec((META_ROWS, tn), lambda i, start: (0, i))],
        out_specs=pl.BlockSpec((2, tn), lambda i, start: (0, i)),
    )
    return pl.pallas_call(
        functools.partial(_slot_kernel, tile_rows=tile_rows),
        grid_spec=grid_spec,
        out_shape=jax.ShapeDtypeStruct((2, n), I32),
        name="dispatch_slots",
    )(tile_start, meta)


def _pick_tile(T, want):
    t = min(T, want)
    assert T % t == 0
    return t


def _trunk(x, c, p):
    B, T, _ = x.shape
    N = B * T
    tm = _pick_tile(T, 512)
    tq = _pick_tile(T, 512)
    tk = _pick_tile(T, 256)
    tme = _pick_tile(N, 512)
    cos, sin_signed = _rope_tables(T)
    mod = _modulation(c, p["w_ada"], p["b_ada"]).reshape(B, 6, D_MODEL)
    u, qt, qb, k, vt = _stage_a(x, mod, p["n1g"], p["w_qkvu"], p["qg"], p["kg"], cos, sin_signed, tm)
    attn = lax.cond(
        p["bounded_softmax_ok"],
        functools.partial(_attention, tq=tq, tk=tk, running_max=False),
        functools.partial(_attention, tq=tq, tk=tk, running_max=True),
        qt, qb, k, vt)
    x1, h2p, meta, rw, counts = _stage_c(x, mod, p["n1g"], p["n2g"], u, attn, p["w_gate"], p["w_mix"],
                                         p["pool_scale"], p["w_pp"], p["w_ap"], p["w_o"], p["w_r"], p["b_r"], tm)
    n_tiles = 2 * N // tme + N_EXPERTS
    tile_start, te, tv = _dispatch_plan(counts[0, :N_EXPERTS].astype(I32), tme, n_tiles)
    pos = _slots(meta, tile_start, tme)
    xs = _sc_scatter_rows(h2p, pos[0], pos[1], n_tiles * tme)
    ys = _moe(xs, te, tv, p["w_eg"], p["w_eu"], p["w_ed"], tme)
    yb = _sc_gather_rows(ys, pos.reshape(-1))
    return _final(x1, yb, rw, mod, p["final_g"], _pick_tile(T, 1024))


def kernel(x_prompt, x_sample, c_prompt, c_sample, w_ada, b_ada, norm1_g, norm2_g, w_in, q_norm_g,
           k_norm_g, w_pool_mix, pool_scale, w_pool_proj, w_attn_proj, w_o, w_router_group,
           b_router_group, w_router_expert, b_router_expert, w_exp_gate, w_exp_up, w_exp_down, final_g):
    assert w_ada.shape[0] == 1, "single-layer block"
    n_r = N_GROUPS + N_EXPERTS
    w_r = jnp.concatenate([w_router_group[0], w_router_expert[0],
                           jnp.zeros((D_MODEL, LANES - n_r), F32)], axis=1)
    w_r_hi = w_r.astype(BF16)
    b_r = jnp.concatenate([b_router_group[0], b_router_expert[0], jnp.zeros((LANES - n_r,), F32)])
    score_bound = (1.01 * LOG2E * np.sqrt(HEAD_DIM)) * jnp.max(jnp.abs(q_norm_g[0])) * jnp.max(jnp.abs(k_norm_g[0]))
    p = dict(
        bounded_softmax_ok=2.0 * score_bound <= EXP2_SAFE_SPAN,
        w_ada=w_ada[0], b_ada=b_ada[0],
        n1g=norm1_g[0].reshape(1, D_MODEL), n2g=norm2_g[0].reshape(1, D_MODEL),
        w_qkvu=w_in[0][:, :D_QKVU].astype(BF16), w_gate=w_in[0][:, D_QKVU:].astype(BF16),
        qg=q_norm_g[0].reshape(1, HEAD_DIM), kg=k_norm_g[0].reshape(1, HEAD_DIM),
        w_mix=w_pool_mix[0].astype(BF16), pool_scale=pool_scale[0].reshape(1, D_POOL),
        w_pp=w_pool_proj[0].astype(BF16), w_ap=w_attn_proj[0].astype(BF16), w_o=w_o[0].astype(BF16),
        w_r=jnp.concatenate([w_r_hi, (w_r - w_r_hi.astype(F32)).astype(BF16)], axis=1), b_r=b_r.reshape(1, LANES),
        w_eg=w_exp_gate[0], w_eu=w_exp_up[0], w_ed=w_exp_down[0],
        final_g=final_g.reshape(1, D_MODEL),
    )
    return _trunk(x_prompt, c_prompt, p), _trunk(x_sample, c_sample, p)
```

```python
import functools

import numpy as np
import jax
import jax.numpy as jnp
from jax import lax
from jax.experimental import pallas as pl
from jax.experimental.pallas import tpu as pltpu
from jax.experimental.pallas import tpu_sc as plsc

F32 = jnp.float32
BF16 = jnp.bfloat16
I32 = jnp.int32

D_MODEL = 1024
GRID_W = 64
POOL_WINDOWS = (2, 4, 8, 16)
POOL_GC = 128
D_POOL = 512
HEAD_DIM = 128
N_Q_HEADS = 4
N_KV_HEADS = 2
D_ATTN = 512
D_KV = 256
ROPE_THETA = 10000.0
N_GROUPS = 4
EXPERTS_PER_GROUP = 8
N_EXPERTS = 32
D_EXPERT = 256
EPS = 1e-6
LOG2E = 1.4426950408889634
EXP2_SAFE_SPAN = 100.0
D_QKVU = D_POOL + D_ATTN + 2 * D_KV
POOL_HALO = 16
STAGE_ROWS = 256
MIXER_ROWS = 512
EXPERT_RING = 3
ATTN_TILES_PER_TRIP = (10, 6, 2)
META_ROWS = 8
LANES = 128
D_PACK = D_MODEL // 2

V7X_VMEM_LIMIT = 56 * 1024 * 1024
SC_WINDOW = 64


def _const_spec(shape):
    nd = len(shape)
    return pl.BlockSpec(shape, lambda *_: (0,) * nd, pipeline_mode=pl.Buffered(1))


def _sigmoid(x):
    return 0.5 * jnp.tanh(0.5 * x) + 0.5


def _pack_bf16_pair(lo, hi):
    lo_bits = lax.bitcast_convert_type(lo.astype(BF16).astype(F32), jnp.uint32) >> 16
    hi_bits = lax.bitcast_convert_type(hi.astype(BF16).astype(F32), jnp.uint32) & jnp.uint32(0xFFFF0000)
    return lax.bitcast_convert_type(lo_bits | hi_bits, I32)


def _unpack_bf16_pair(packed):
    u = lax.bitcast_convert_type(packed, jnp.uint32)
    lo = lax.bitcast_convert_type(u << 16, F32)
    hi = lax.bitcast_convert_type(u & jnp.uint32(0xFFFF0000), F32)
    return lo, hi


def _mod_kernel(c_ref, w_ref, b_ref, o_ref):
    c = c_ref[...]
    s = c * _sigmoid(c)
    o_ref[...] = jnp.dot(s, w_ref[...], preferred_element_type=F32,
                         precision=lax.Precision.HIGHEST) + b_ref[...]


def _modulation(c, w_ada, b_ada):
    B = c.shape[0]
    n_out = w_ada.shape[1]
    bn = D_MODEL
    return pl.pallas_call(
        _mod_kernel,
        grid=(n_out // bn,),
        in_specs=[pl.BlockSpec((B, D_MODEL), lambda j: (0, 0)),
                  pl.BlockSpec((D_MODEL, bn), lambda j: (0, j)),
                  pl.BlockSpec((1, bn), lambda j: (0, j))],
        out_specs=pl.BlockSpec((B, bn), lambda j: (0, j)),
        out_shape=jax.ShapeDtypeStruct((B, n_out), F32),
        name="modulation",
    )(c, w_ada, b_ada.reshape(1, n_out))


def _scaled_norm(x, gain, shift):
    ms = jnp.mean(x * x, axis=-1, keepdims=True)
    return x * lax.rsqrt(ms + EPS) * gain + shift


def _stage_a_kernel(x_ref, mod_ref, n1g_ref, w_ref, qg_ref, kg_ref, cos_ref, sin_ref,
                    u_ref, qt_ref, qb_ref, k_ref, vt_ref):
    tm = x_ref.shape[0]
    gain1 = n1g_ref[...] * (1.0 + mod_ref[1:2, :])
    shift1 = mod_ref[0:1, :]
    lane = lax.broadcasted_iota(I32, (1, HEAD_DIM), 1)
    first_half = (lane % 64) < 32
    qg = qg_ref[...] * (LOG2E / np.sqrt(HEAD_DIM))
    k_norm_max = jnp.max(jnp.abs(kg_ref[...]), axis=-1, keepdims=True) * (np.sqrt(HEAD_DIM) * 1.01)

    for r0 in range(0, tm, STAGE_ROWS):
        rows = slice(r0, r0 + min(STAGE_ROWS, tm))
        h = _scaled_norm(x_ref[rows, :], gain1, shift1)
        z = jnp.dot(h.astype(BF16), w_ref[...], preferred_element_type=F32)
        u_ref[rows, :] = z[:, :D_POOL].astype(BF16)
        cos = cos_ref[rows, :]
        sin_signed = sin_ref[rows, :]

        def norm_rope(xh, g):
            ms = jnp.mean(xh * xh, axis=-1, keepdims=True)
            xn = xh * lax.rsqrt(ms + EPS) * g
            rot = jnp.where(first_half, pltpu.roll(xn, 96, 1), pltpu.roll(xn, 32, 1))
            return xn * cos + rot * sin_signed

        for hq in range(N_Q_HEADS):
            o = D_POOL + hq * HEAD_DIM
            q_t = norm_rope(z[:, o:o + HEAD_DIM], qg).T.astype(BF16)
            qt_ref[hq, :, rows] = q_t
            q_f = q_t.astype(F32)
            qb_ref[hq, :, rows] = jnp.sqrt(jnp.sum(q_f * q_f, axis=0, keepdims=True)) * k_norm_max
        for hk in range(N_KV_HEADS):
            o = D_POOL + D_ATTN + hk * HEAD_DIM
            k_ref[rows, hk * HEAD_DIM:(hk + 1) * HEAD_DIM] = norm_rope(z[:, o:o + HEAD_DIM], kg_ref[...]).astype(BF16)
            o = D_POOL + D_ATTN + D_KV + hk * HEAD_DIM
            vt_ref[hk, :, rows] = z[:, o:o + HEAD_DIM].T.astype(BF16)


def _stage_a(x, mod, n1g, w_qkvu, qg, kg, cos, sin_signed, tm):
    B, T, _ = x.shape
    return pl.pallas_call(
        _stage_a_kernel,
        grid=(B, T // tm),
        in_specs=[
            pl.BlockSpec((None, tm, D_MODEL), lambda b, i: (b, i, 0)),
            pl.BlockSpec((None, 6, D_MODEL), lambda b, i: (b, 0, 0)),
            _const_spec((1, D_MODEL)),
            _const_spec((D_MODEL, D_QKVU)),
            _const_spec((1, HEAD_DIM)),
            _const_spec((1, HEAD_DIM)),
            pl.BlockSpec((tm, HEAD_DIM), lambda b, i: (i, 0)),
            pl.BlockSpec((tm, HEAD_DIM), lambda b, i: (i, 0)),
        ],
        out_specs=[
            pl.BlockSpec((None, tm, D_POOL), lambda b, i: (b, i, 0)),
            pl.BlockSpec((None, N_Q_HEADS, HEAD_DIM, tm), lambda b, i: (b, 0, 0, i)),
            pl.BlockSpec((None, N_Q_HEADS, 1, tm), lambda b, i: (b, 0, 0, i)),
            pl.BlockSpec((None, tm, D_KV), lambda b, i: (b, i, 0)),
            pl.BlockSpec((None, N_KV_HEADS, HEAD_DIM, tm), lambda b, i: (b, 0, 0, i)),
        ],
        out_shape=[
            jax.ShapeDtypeStruct((B, T, D_POOL), BF16),
            jax.ShapeDtypeStruct((B, N_Q_HEADS, HEAD_DIM, T), BF16),
            jax.ShapeDtypeStruct((B, N_Q_HEADS, 1, T), F32),
            jax.ShapeDtypeStruct((B, T, D_KV), BF16),
            jax.ShapeDtypeStruct((B, N_KV_HEADS, HEAD_DIM, T), BF16),
        ],
        compiler_params=pltpu.CompilerParams(
            dimension_semantics=("parallel", "parallel"), vmem_limit_bytes=V7X_VMEM_LIMIT),
        name="stage_a",
    )(x, mod, n1g, w_qkvu, qg, kg, cos, sin_signed)


def _write_attn_output(o_ref, acc, l, tq):
    o = acc * (1.0 / l)
    o_ref[:, :HEAD_DIM] = o[:, :tq].T.astype(BF16)
    o_ref[:, HEAD_DIM:] = o[:, tq:].T.astype(BF16)


def _attn_bounded_kernel(qt_ref, qb_ref, k_ref, vt_ref, o_ref, st_ref, p_ref, acc_ref, l_ref, *, tk):
    T = k_ref.shape[0]
    tq = qt_ref.shape[2]
    nq = 2 * tq
    nk = T // tk
    qt = jnp.concatenate([qt_ref[0], qt_ref[1]], axis=1)
    qb = jnp.concatenate([qb_ref[0], qb_ref[1]], axis=1)

    def scores(j, slot):
        off = pl.multiple_of(j * tk, tk)
        st_ref[slot] = jnp.dot(k_ref[pl.ds(off, tk), :], qt, preferred_element_type=F32)

    def weights(slot):
        p = jnp.exp2(st_ref[slot] - qb)
        l_ref[...] += jnp.sum(p.reshape(tk // 8, 8, nq), axis=0)
        p_ref[slot] = p.astype(BF16)

    def values(j, slot):
        off = pl.multiple_of(j * tk, tk)
        acc_ref[...] += jnp.dot(vt_ref[:, pl.ds(off, tk)], p_ref[slot], preferred_element_type=F32)

    acc_ref[...] = jnp.zeros_like(acc_ref)
    l_ref[...] = jnp.zeros_like(l_ref)
    scores(0, 0)
    weights(0)
    scores(1, 1)

    unroll = next(u for u in ATTN_TILES_PER_TRIP if (nk - 2) % u == 0)

    @pl.loop(0, (nk - 2) // unroll)
    def _(g):
        for d in range(0, unroll, 2):
            a = unroll * g + d + 1
            values(a - 1, 0)
            weights(1)
            scores(a + 1, 0)
            values(a, 1)
            weights(0)
            scores(a + 2, 1)

    values(nk - 2, 0)
    weights(1)
    values(nk - 1, 1)
    _write_attn_output(o_ref, acc_ref[...], jnp.sum(l_ref[...], axis=0, keepdims=True), tq)


def _attn_online_kernel(qt_ref, qb_ref, k_ref, vt_ref, o_ref, *, tk):
    del qb_ref
    T = k_ref.shape[0]
    tq = qt_ref.shape[2]
    nq = 2 * tq
    qt = jnp.concatenate([qt_ref[0], qt_ref[1]], axis=1)

    def body(s, carry):
        m, l, acc = carry
        off = pl.multiple_of(s * tk, tk)
        st = jnp.dot(k_ref[pl.ds(off, tk), :], qt, preferred_element_type=F32)
        m_new = jnp.maximum(m, jnp.max(st, axis=0, keepdims=True))
        alpha = jnp.exp2(m - m_new)
        p = jnp.exp2(st - m_new)
        l = alpha * l + jnp.sum(p, axis=0, keepdims=True)
        pv = jnp.dot(vt_ref[:, pl.ds(off, tk)], p.astype(BF16), preferred_element_type=F32)
        return m_new, l, alpha * acc + pv

    init = (jnp.full((1, nq), -jnp.inf, F32), jnp.zeros((1, nq), F32), jnp.zeros((HEAD_DIM, nq), F32))
    _, l, acc = lax.fori_loop(0, T // tk, body, init)
    _write_attn_output(o_ref, acc, l, tq)


def _attention(qt, qb, k, vt, tq, tk, running_max):
    B, _, _, T = qt.shape
    group = N_Q_HEADS // N_KV_HEADS
    nq = group * tq
    if running_max:
        body, scratch, name = functools.partial(_attn_online_kernel, tk=tk), [], "attention_online_max"
    else:
        assert (T // tk) % 2 == 0 and T // tk >= 2
        body = functools.partial(_attn_bounded_kernel, tk=tk)
        scratch = [pltpu.VMEM((2, tk, nq), F32), pltpu.VMEM((2, tk, nq), BF16),
                   pltpu.VMEM((HEAD_DIM, nq), F32), pltpu.VMEM((8, nq), F32)]
        name = "attention_bounded"
    return pl.pallas_call(
        body,
        grid=(B, N_KV_HEADS, T // tq),
        in_specs=[
            pl.BlockSpec((None, group, HEAD_DIM, tq), lambda b, j, i: (b, j, 0, i)),
            pl.BlockSpec((None, group, 1, tq), lambda b, j, i: (b, j, 0, i)),
            pl.BlockSpec((None, T, HEAD_DIM), lambda b, j, i: (b, 0, j)),
            pl.BlockSpec((None, None, HEAD_DIM, T), lambda b, j, i: (b, j, 0, 0)),
        ],
        out_specs=pl.BlockSpec((None, tq, group * HEAD_DIM), lambda b, j, i: (b, i, j)),
        out_shape=jax.ShapeDtypeStruct((B, T, D_ATTN), BF16),
        scratch_shapes=scratch,
        compiler_params=pltpu.CompilerParams(
            dimension_semantics=("parallel", "parallel", "parallel"),
            vmem_limit_bytes=V7X_VMEM_LIMIT),
        name=name,
    )(qt, qb, k, vt)


def _route(logits):
    shape = logits.shape
    lane = lax.broadcasted_iota(I32, shape, 1).astype(F32)
    neg = jnp.float32(-jnp.inf)
    far = jnp.float32(LANES)

    def top(vals):
        v = jnp.max(vals, axis=1, keepdims=True)
        i = jnp.min(jnp.where(vals == v, lane, far), axis=1, keepdims=True)
        return v, i

    is_group = lane < N_GROUPS
    lg = jnp.where(is_group, logits, neg)
    gmax, gsel = top(lg)
    denom = jnp.sum(jnp.where(is_group, jnp.exp(lg - gmax), 0.0), axis=1, keepdims=True)
    p_group = 1.0 / denom
    base = N_GROUPS + EXPERTS_PER_GROUP * gsel
    le = jnp.where((lane >= base) & (lane < base + EXPERTS_PER_GROUP), logits, neg)
    v1, i1 = top(le)
    v2, i2 = top(jnp.where(lane == i1, neg, le))
    e21 = jnp.exp(v2 - v1)
    w1 = p_group / (1.0 + e21)
    w2 = w1 * e21
    rw = jnp.where(lane == 0, w1, jnp.where(lane == 1, w2, 0.0))
    return i1 - N_GROUPS, i2 - N_GROUPS, rw


def _pool_deviation(ue_ref, lvl_ref, pool_ref, tile_start, tm, seq_len):
    n = tm + 2 * POOL_HALO
    t = tile_start + lax.broadcasted_iota(I32, (tm, 1), 0)
    for gi, w in enumerate(POOL_WINDOWS):
        half = w // 2
        cols = slice(gi * POOL_GC, (gi + 1) * POOL_GC)
        level = ue_ref[0:n, cols] + ue_ref[1:n + 1, cols]
        span, buf = 2, 0
        lvl_ref[buf, 0:n, :] = level
        while span < w:
            level = lvl_ref[buf, 0:n, :] + lvl_ref[buf, span:span + n, :]
            span, buf = 2 * span, 1 - buf
            lvl_ref[buf, 0:n, :] = level
        wsum = lvl_ref[buf, POOL_HALO - half:POOL_HALO - half + tm, :]
        cnt = (jnp.minimum(t + half, seq_len) - jnp.maximum(t - half, 0)).astype(F32)
        p = wsum * (1.0 / cnt) - ue_ref[POOL_HALO:POOL_HALO + tm, cols]
        pool_ref[:, cols] = p.astype(BF16)


def _stage_c_kernel(x_ref, mod_ref, n1g_ref, n2g_ref, u_ref, up_ref, un_ref, attn_ref,
                    wgate_ref, wmix_ref, ps_ref, wpp_ref, wap_ref, wo_ref, wr_ref, br_ref, ltri_ref,
                    x1_ref, h2_ref, meta_ref, rw_ref, count_ref, ue_ref, lvl_ref, pool_ref, run_ref, *, seq_len):
    i = pl.program_id(1)
    tm = x_ref.shape[0]

    @pl.when((pl.program_id(0) == 0) & (i == 0))
    def _():
        run_ref[...] = jnp.zeros_like(run_ref)

    gain1 = n1g_ref[...] * (1.0 + mod_ref[1:2, :])
    shift1 = mod_ref[0:1, :]
    gain2 = n2g_ref[...] * (1.0 + mod_ref[4:5, :])
    shift2 = mod_ref[3:4, :]
    res_gate = mod_ref[2:3, :]

    ue_ref[0:POOL_HALO, :] = up_ref[...].astype(F32) * (i > 0).astype(F32)
    ue_ref[POOL_HALO:POOL_HALO + tm, :] = u_ref[...].astype(F32)
    ue_ref[POOL_HALO + tm:2 * POOL_HALO + tm, :] = un_ref[...].astype(F32) * (i < pl.num_programs(1) - 1).astype(F32)
    ue_ref[2 * POOL_HALO + tm:, :] = jnp.zeros((POOL_HALO, D_POOL), F32)
    lvl_ref[:, 2 * POOL_HALO + tm:, :] = jnp.zeros((2, POOL_HALO, POOL_GC), F32)
    _pool_deviation(ue_ref, lvl_ref, pool_ref, i * tm, tm, seq_len)

    for r0 in range(0, tm, MIXER_ROWS):
        rows = slice(r0, r0 + min(MIXER_ROWS, tm))
        x = x_ref[rows, :]
        h = _scaled_norm(x, gain1, shift1)
        gates = _sigmoid(jnp.dot(h.astype(BF16), wgate_ref[...], preferred_element_type=F32))
        mixed_groups = []
        for gi in range(len(POOL_WINDOWS)):
            cols = slice(gi * POOL_GC, (gi + 1) * POOL_GC)
            pm = jnp.dot(pool_ref[rows, cols], wmix_ref[gi], preferred_element_type=F32)
            mixed_groups.append(pm * ps_ref[:, cols])
        pool = jnp.concatenate(mixed_groups, axis=1)
        pool_out = jnp.dot(pool.astype(BF16), wpp_ref[...], preferred_element_type=F32)
        attn_out = jnp.dot(attn_ref[rows, :], wap_ref[...], preferred_element_type=F32)
        merged = gates[:, :D_MODEL] * pool_out + gates[:, D_MODEL:] * attn_out
        mixed = jnp.dot(merged.astype(BF16), wo_ref[...], preferred_element_type=F32)
        x1 = x + res_gate * mixed
        x1_ref[rows, :] = x1

        h2 = _scaled_norm(x1, gain2, shift2)
        h2_ref[rows, :] = _pack_bf16_pair(h2[:, :D_PACK], h2[:, D_PACK:])
        h_hi = h2.astype(BF16)
        h_lo = (h2 - h_hi.astype(F32)).astype(BF16)
        parts = (jnp.dot(h_hi, wr_ref[...], preferred_element_type=F32)
                 + jnp.dot(h_lo, wr_ref[...], preferred_element_type=F32))
        logits = parts[:, :LANES] + parts[:, LANES:] + br_ref[...]
        e1, e2, rw = _route(logits)
        rw_ref[rows, :] = rw

        lane = lax.broadcasted_iota(I32, logits.shape, 1).astype(F32)
        hit1 = lane == e1
        hit2 = lane == e2
        taken = jnp.where(hit1 | hit2, 1.0, 0.0).astype(BF16)
        before = run_ref[...] + jnp.dot(ltri_ref[...], taken, preferred_element_type=F32)
        rank1 = jnp.sum(jnp.where(hit1, before, 0.0), axis=1, keepdims=True)
        rank2 = jnp.sum(jnp.where(hit2, before, 0.0), axis=1, keepdims=True)
        run_ref[...] = before[-1:, :] + taken[-1:, :].astype(F32)
        meta = jnp.where(lane == 0, e1, jnp.where(lane == 1, e2,
                         jnp.where(lane == 2, rank1, jnp.where(lane == 3, rank2, 0.0))))
        meta_ref[:, rows] = meta.T[0:META_ROWS, :].astype(I32)
    count_ref[...] = jnp.broadcast_to(run_ref[...], count_ref.shape)


def _stage_c(x, mod, n1g, n2g, u, attn, w_gate, w_mix, pool_scale, w_pp, w_ap, w_o, w_r, b_r, tm):
    B, T, _ = x.shape
    hb = tm // POOL_HALO
    n_halo_blocks = T // POOL_HALO
    tok = lambda b, i: (b, i, 0)
    flat = lambda b, i: (b * (T // tm) + i, 0)
    group_rows = min(MIXER_ROWS, tm)
    ltri = jnp.tri(group_rows, k=-1, dtype=BF16)
    return pl.pallas_call(
        functools.partial(_stage_c_kernel, seq_len=T),
        grid=(B, T // tm),
        in_specs=[
            pl.BlockSpec((None, tm, D_MODEL), tok),
            pl.BlockSpec((None, 6, D_MODEL), lambda b, i: (b, 0, 0)),
            _const_spec((1, D_MODEL)),
            _const_spec((1, D_MODEL)),
            pl.BlockSpec((None, tm, D_POOL), tok),
            pl.BlockSpec((None, POOL_HALO, D_POOL), lambda b, i: (b, jnp.maximum(i * hb - 1, 0), 0)),
            pl.BlockSpec((None, POOL_HALO, D_POOL),
                         lambda b, i: (b, jnp.minimum((i + 1) * hb, n_halo_blocks - 1), 0)),
            pl.BlockSpec((None, tm, D_ATTN), tok),
            _const_spec((D_MODEL, 2 * D_MODEL)),
            _const_spec((len(POOL_WINDOWS), POOL_GC, POOL_GC)),
            _const_spec((1, D_POOL)),
            _const_spec((D_POOL, D_MODEL)),
            _const_spec((D_ATTN, D_MODEL)),
            _const_spec((D_MODEL, D_MODEL)),
            _const_spec((D_MODEL, 2 * LANES)),
            _const_spec((1, LANES)),
            _const_spec((group_rows, group_rows)),
        ],
        out_specs=[
            pl.BlockSpec((None, tm, D_MODEL), tok),
            pl.BlockSpec((tm, D_PACK), flat),
            pl.BlockSpec((META_ROWS, tm), lambda b, i: (0, b * (T // tm) + i)),
            pl.BlockSpec((tm, LANES), flat),
            pl.BlockSpec((META_ROWS, LANES), lambda b, i: (0, 0)),
        ],
        out_shape=[
            jax.ShapeDtypeStruct((B, T, D_MODEL), F32),
            jax.ShapeDtypeStruct((B * T, D_PACK), I32),
            jax.ShapeDtypeStruct((META_ROWS, B * T), I32),
            jax.ShapeDtypeStruct((B * T, LANES), F32),
            jax.ShapeDtypeStruct((META_ROWS, LANES), F32),
        ],
        scratch_shapes=[pltpu.VMEM((tm + 3 * POOL_HALO, D_POOL), F32),
                        pltpu.VMEM((2, tm + 3 * POOL_HALO, POOL_GC), F32),
                        pltpu.VMEM((tm, D_POOL), BF16),
                        pltpu.VMEM((1, LANES), F32)],
        compiler_params=pltpu.CompilerParams(
            dimension_semantics=("arbitrary", "arbitrary"), vmem_limit_bytes=V7X_VMEM_LIMIT),
        name="stage_c",
    )(x, mod, n1g, n2g, u, u, u, attn, w_gate, w_mix, pool_scale, w_pp, w_ap, w_o, w_r, b_r, ltri)


def _sc_workers():
    info = plsc.get_sparse_core_info()
    return info.num_cores, info.num_subcores


def _sc_scatter_rows(rows, idx0, idx1, n_out):
    n, d = rows.shape
    nc, ns = _sc_workers()
    per_worker = n // (nc * ns)
    n_win = per_worker // SC_WINDOW
    mesh = plsc.VectorSubcoreMesh(core_axis_name="c", subcore_axis_name="s")

    @functools.partial(
        pl.kernel, mesh=mesh,
        out_type=jax.ShapeDtypeStruct((n_out, d), rows.dtype),
        scratch_types=[pltpu.VMEM((SC_WINDOW,), I32), pltpu.VMEM((SC_WINDOW,), I32),
                       pltpu.VMEM((SC_WINDOW, d), rows.dtype)],
        name="sc_scatter_rows",
    )
    def scatter(rows_hbm, idx0_hbm, idx1_hbm, out_hbm, i0_v, i1_v, rows_v):
        wid = lax.axis_index("s") * nc + lax.axis_index("c")

        @pl.loop(0, n_win)
        def _(w):
            base = wid * per_worker + w * SC_WINDOW
            pltpu.sync_copy(rows_hbm.at[pl.ds(base, SC_WINDOW)], rows_v)
            pltpu.sync_copy(idx0_hbm.at[pl.ds(base, SC_WINDOW)], i0_v)
            pltpu.sync_copy(idx1_hbm.at[pl.ds(base, SC_WINDOW)], i1_v)
            pltpu.sync_copy(rows_v, out_hbm.at[i0_v])
            pltpu.sync_copy(rows_v, out_hbm.at[i1_v])

    return scatter(rows, idx0, idx1)


def _sc_gather_rows(table, idx):
    n = idx.shape[0]
    d = table.shape[1]
    nc, ns = _sc_workers()
    per_worker = n // (nc * ns)
    n_win = per_worker // SC_WINDOW
    mesh = plsc.VectorSubcoreMesh(core_axis_name="c", subcore_axis_name="s")

    @functools.partial(
        pl.kernel, mesh=mesh,
        out_type=jax.ShapeDtypeStruct((n, d), table.dtype),
        scratch_types=[pltpu.VMEM((SC_WINDOW,), I32), pltpu.VMEM((SC_WINDOW, d), table.dtype)],
        name="sc_gather_rows",
    )
    def gather(table_hbm, idx_hbm, out_hbm, i_v, rows_v):
        wid = lax.axis_index("s") * nc + lax.axis_index("c")

        @pl.loop(0, n_win)
        def _(w):
            base = wid * per_worker + w * SC_WINDOW
            pltpu.sync_copy(idx_hbm.at[pl.ds(base, SC_WINDOW)], i_v)
            pltpu.sync_copy(table_hbm.at[i_v], rows_v)
            pltpu.sync_copy(rows_v, out_hbm.at[pl.ds(base, SC_WINDOW)])

    return gather(table, idx)


def _moe_kernel(te_ref, tv_ref, nl_ref, x_hbm, wgu_ref, wd_ref, o_ref, xbuf, sem):
    del te_ref
    i = pl.program_id(0)
    n_live_tiles = nl_ref[0]
    valid = tv_ref[i]
    tm = o_ref.shape[0]
    slot = lax.rem(i, EXPERT_RING)

    def fetch(step):
        s = lax.rem(step, EXPERT_RING)
        rows = pl.ds(pl.multiple_of(step * tm, tm), tm)
        return pltpu.make_async_copy(x_hbm.at[rows, :], xbuf.at[s], sem.at[s])

    @pl.when((i == 0) & (n_live_tiles > 0))
    def _():
        fetch(0).start()

    @pl.when((i == 0) & (n_live_tiles > 1))
    def _():
        fetch(1).start()

    @pl.when(i + 2 < n_live_tiles)
    def _():
        fetch(i + 2).start()

    @pl.when(i < n_live_tiles)
    def _():
        fetch(i).wait()

    group = min(STAGE_ROWS, tm)
    n_groups = tm // group

    def expert_rows(r0):
        rows = slice(r0, r0 + group)
        live = lax.broadcasted_iota(I32, (group, 1), 0) < valid - r0
        lo, hi = _unpack_bf16_pair(jnp.where(live, xbuf[slot, rows, :], 0))
        x = jnp.concatenate([lo, hi], axis=1).astype(BF16)
        gu = jnp.dot(x, wgu_ref[...], preferred_element_type=F32)
        a = gu[:, :D_EXPERT]
        hmid = (a * _sigmoid(a) * gu[:, D_EXPERT:]).astype(BF16)
        y = jnp.dot(hmid, wd_ref[...], preferred_element_type=F32)
        o_ref[rows, :] = _pack_bf16_pair(y[:, :D_PACK], y[:, D_PACK:])

    for n_live in range(n_groups + 1):
        @pl.when((valid > (n_live - 1) * group) & (valid <= n_live * group))
        def _():
            for g in range(n_live):
                expert_rows(g * group)
            if n_live < n_groups:
                o_ref[n_live * group:, :] = jnp.zeros((tm - n_live * group, D_PACK), I32)


def _moe(xs, tile_expert, tile_valid, n_live_tiles, w_gu, w_d, tm):
    n_tiles = xs.shape[0] // tm
    grid_spec = pltpu.PrefetchScalarGridSpec(
        num_scalar_prefetch=3,
        grid=(n_tiles,),
        in_specs=[
            pl.BlockSpec(memory_space=pl.ANY),
            pl.BlockSpec((None, D_MODEL, 2 * D_EXPERT), lambda i, te, tv, nl: (te[i], 0, 0)),
            pl.BlockSpec((None, D_EXPERT, D_MODEL), lambda i, te, tv, nl: (te[i], 0, 0)),
        ],
        out_specs=pl.BlockSpec((tm, D_PACK), lambda i, te, tv, nl: (jnp.minimum(i, nl[0]), 0)),
        scratch_shapes=[pltpu.VMEM((EXPERT_RING, tm, D_PACK), I32), pltpu.SemaphoreType.DMA((EXPERT_RING,))],
    )
    return pl.pallas_call(
        _moe_kernel,
        grid_spec=grid_spec,
        out_shape=jax.ShapeDtypeStruct(xs.shape, I32),
        compiler_params=pltpu.CompilerParams(
            dimension_semantics=("arbitrary",), vmem_limit_bytes=V7X_VMEM_LIMIT),
        name="experts",
    )(tile_expert, tile_valid, n_live_tiles, xs, w_gu, w_d)


def _final_kernel(x1_ref, y0_ref, y1_ref, rw_ref, mod_ref, fg_ref, o_ref):
    y0lo, y0hi = _unpack_bf16_pair(y0_ref[...])
    y1lo, y1hi = _unpack_bf16_pair(y1_ref[...])
    w0 = rw_ref[:, 0:1]
    w1 = rw_ref[:, 1:2]
    moe = jnp.concatenate([w0 * y0lo + w1 * y1lo, w0 * y0hi + w1 * y1hi], axis=1)
    x2 = x1_ref[...] + mod_ref[5:6, :] * moe
    ms = jnp.mean(x2 * x2, axis=-1, keepdims=True)
    o_ref[...] = x2 * lax.rsqrt(ms + EPS) * fg_ref[...]


def _final(x1, yb, rw, mod, final_g, tm):
    B, T, _ = x1.shape
    n_blocks = B * T // tm
    flat = lambda b, i: (b * (T // tm) + i, 0)
    return pl.pallas_call(
        _final_kernel,
        grid=(B, T // tm),
        in_specs=[
            pl.BlockSpec((None, tm, D_MODEL), lambda b, i: (b, i, 0)),
            pl.BlockSpec((tm, D_PACK), flat),
            pl.BlockSpec((tm, D_PACK), lambda b, i: (n_blocks + b * (T // tm) + i, 0)),
            pl.BlockSpec((tm, LANES), flat),
            pl.BlockSpec((None, 6, D_MODEL), lambda b, i: (b, 0, 0)),
            _const_spec((1, D_MODEL)),
        ],
        out_specs=pl.BlockSpec((None, tm, D_MODEL), lambda b, i: (b, i, 0)),
        out_shape=jax.ShapeDtypeStruct((B, T, D_MODEL), F32),
        compiler_params=pltpu.CompilerParams(
            dimension_semantics=("parallel", "parallel"), vmem_limit_bytes=V7X_VMEM_LIMIT),
        name="final_combine",
    )(x1, yb, yb, rw, mod, final_g)


def _rope_tables(T):
    rows = T // GRID_W
    row = jnp.repeat(jnp.arange(rows, dtype=F32), GRID_W)
    col = jnp.tile(jnp.arange(GRID_W, dtype=F32), rows)
    inv_freq = ROPE_THETA ** (-jnp.arange(0, 64, 2, dtype=F32) / 64)
    ang_r = row[:, None] * inv_freq[None, :]
    ang_c = col[:, None] * inv_freq[None, :]
    ang = jnp.concatenate([ang_r, ang_r, ang_c, ang_c], axis=-1)
    sign = jnp.where((jnp.arange(HEAD_DIM) % 64) < 32, -1.0, 1.0).astype(F32)
    return jnp.cos(ang), jnp.sin(ang) * sign[None, :]


def _dispatch_plan(counts, tm, n_tiles):
    tiles_per_e = (counts + tm - 1) // tm
    tile_end = jnp.cumsum(tiles_per_e)
    tile_start = tile_end - tiles_per_e
    tile_id = jnp.arange(n_tiles, dtype=I32)
    te = jnp.minimum(jnp.sum((tile_id[:, None] >= tile_end[None, :]).astype(I32), axis=1), N_EXPERTS - 1)
    live = jnp.clip(counts[te] - (tile_id - tile_start[te]) * tm, 0, tm)
    tv = jnp.where(tile_id < tile_end[-1], live, 0).astype(I32)
    return tile_start.astype(I32), te, tv, tile_end[-1:].astype(I32)


def _slot_kernel(start_ref, meta_ref, pos_ref, *, tile_rows):
    e = meta_ref[0:2, :]
    first_tile = jnp.zeros_like(e)
    for ex in range(N_EXPERTS):
        first_tile = jnp.where(e == ex, start_ref[ex], first_tile)
    pos_ref[...] = first_tile * tile_rows + meta_ref[2:4, :]


def _slots(meta, tile_start, tile_rows):
    n = meta.shape[1]
    tn = min(n, 8192)
    grid_spec = pltpu.PrefetchScalarGridSpec(
        num_scalar_prefetch=1,
        grid=(n // tn,),
        in_specs=[pl.BlockSpec((META_ROWS, tn), lambda i, start: (0, i))],
        out_specs=pl.BlockSpec((2, tn), lambda i, start: (0, i)),
    )
    return pl.pallas_call(
        functools.partial(_slot_kernel, tile_rows=tile_rows),
        grid_spec=grid_spec,
        out_shape=jax.ShapeDtypeStruct((2, n), I32),
        name="dispatch_slots",
    )(tile_start, meta)


def _pick_tile(T, want):
    t = min(T, want)
    assert T % t == 0
    return t


def _trunk(x, c, p):
    B, T, _ = x.shape
    N = B * T
    tm = _pick_tile(T, 512)
    tq = _pick_tile(T, 512)
    tk = _pick_tile(T, 256)
    tme = _pick_tile(N, 512)
    cos, sin_signed = _rope_tables(T)
    mod = _modulation(c, p["w_ada"], p["b_ada"]).reshape(B, 6, D_MODEL)
    u, qt, qb, k, vt = _stage_a(x, mod, p["n1g"], p["w_qkvu"], p["qg"], p["kg"], cos, sin_signed, tm)
    attn = lax.cond(
        p["bounded_softmax_ok"],
        functools.partial(_attention, tq=tq, tk=tk, running_max=False),
        functools.partial(_attention, tq=tq, tk=tk, running_max=True),
        qt, qb, k, vt)
    x1, h2p, meta, rw, counts = _stage_c(x, mod, p["n1g"], p["n2g"], u, attn, p["w_gate"], p["w_mix"],
                                         p["pool_scale"], p["w_pp"], p["w_ap"], p["w_o"], p["w_r"], p["b_r"], tm)
    n_tiles = 2 * N // tme + N_EXPERTS
    tile_start, te, tv, n_live_tiles = _dispatch_plan(counts[0, :N_EXPERTS].astype(I32), tme, n_tiles)
    pos = _slots(meta, tile_start, tme)
    xs = _sc_scatter_rows(h2p, pos[0], pos[1], n_tiles * tme)
    ys = _moe(xs, te, tv, n_live_tiles, p["w_egu"], p["w_ed"], tme)
    yb = _sc_gather_rows(ys, pos.reshape(-1))
    return _final(x1, yb, rw, mod, p["final_g"], _pick_tile(T, 1024))


def kernel(x_prompt, x_sample, c_prompt, c_sample, w_ada, b_ada, norm1_g, norm2_g, w_in, q_norm_g,
           k_norm_g, w_pool_mix, pool_scale, w_pool_proj, w_attn_proj, w_o, w_router_group,
           b_router_group, w_router_expert, b_router_expert, w_exp_gate, w_exp_up, w_exp_down, final_g):
    assert w_ada.shape[0] == 1, "single-layer block"
    n_r = N_GROUPS + N_EXPERTS
    w_r = jnp.concatenate([w_router_group[0], w_router_expert[0],
                           jnp.zeros((D_MODEL, LANES - n_r), F32)], axis=1)
    w_r_hi = w_r.astype(BF16)
    b_r = jnp.concatenate([b_router_group[0], b_router_expert[0], jnp.zeros((LANES - n_r,), F32)])
    score_bound = (1.01 * LOG2E * np.sqrt(HEAD_DIM)) * jnp.max(jnp.abs(q_norm_g[0])) * jnp.max(jnp.abs(k_norm_g[0]))
    p = dict(
        bounded_softmax_ok=2.0 * score_bound <= EXP2_SAFE_SPAN,
        w_ada=w_ada[0], b_ada=b_ada[0],
        n1g=norm1_g[0].reshape(1, D_MODEL), n2g=norm2_g[0].reshape(1, D_MODEL),
        w_qkvu=w_in[0][:, :D_QKVU].astype(BF16), w_gate=w_in[0][:, D_QKVU:].astype(BF16),
        qg=q_norm_g[0].reshape(1, HEAD_DIM), kg=k_norm_g[0].reshape(1, HEAD_DIM),
        w_mix=w_pool_mix[0].astype(BF16), pool_scale=pool_scale[0].reshape(1, D_POOL),
        w_pp=w_pool_proj[0].astype(BF16), w_ap=w_attn_proj[0].astype(BF16), w_o=w_o[0].astype(BF16),
        w_r=jnp.concatenate([w_r_hi, (w_r - w_r_hi.astype(F32)).astype(BF16)], axis=1), b_r=b_r.reshape(1, LANES),
        w_egu=jnp.concatenate([w_exp_gate[0], w_exp_up[0]], axis=-1).astype(BF16), w_ed=w_exp_down[0].astype(BF16),
        final_g=final_g.reshape(1, D_MODEL),
    )
    return _trunk(x_prompt, c_prompt, p), _trunk(x_sample, c_sample, p)
```

```python
import functools

import numpy as np
import jax
import jax.numpy as jnp
from jax import lax
from jax.experimental import pallas as pl
from jax.experimental.pallas import tpu as pltpu
from jax.experimental.pallas import tpu_sc as plsc

F32 = jnp.float32
BF16 = jnp.bfloat16
I32 = jnp.int32

D_MODEL = 1024
GRID_W = 64
POOL_WINDOWS = (2, 4, 8, 16)
POOL_GC = 128
D_POOL = 512
HEAD_DIM = 128
N_Q_HEADS = 4
N_KV_HEADS = 2
D_ATTN = 512
D_KV = 256
ROPE_THETA = 10000.0
N_GROUPS = 4
EXPERTS_PER_GROUP = 8
N_EXPERTS = 32
D_EXPERT = 256
EPS = 1e-6
LOG2E = 1.4426950408889634
EXP2_SAFE_SPAN = 100.0
D_QKVU = D_POOL + D_ATTN + 2 * D_KV
POOL_HALO = 16
STAGE_ROWS = 256
MIXER_ROWS = 512
RING_SLOTS = 3
ATTN_TILES_PER_TRIP = (10, 6, 2)
ROUTE_ROWS = 40
META_ROWS = 8
LANES = 128
D_PACK = D_MODEL // 2

V7X_VMEM_LIMIT = 56 * 1024 * 1024
SC_WINDOW = 64


def _const_spec(shape):
    nd = len(shape)
    return pl.BlockSpec(shape, lambda *_: (0,) * nd, pipeline_mode=pl.Buffered(1))


def _sigmoid(x):
    return 0.5 * jnp.tanh(0.5 * x) + 0.5


def _pack_bf16_pair(lo, hi):
    lo_bits = lax.bitcast_convert_type(lo.astype(BF16).astype(F32), jnp.uint32) >> 16
    hi_bits = lax.bitcast_convert_type(hi.astype(BF16).astype(F32), jnp.uint32) & jnp.uint32(0xFFFF0000)
    return lax.bitcast_convert_type(lo_bits | hi_bits, I32)


def _unpack_bf16_pair(packed):
    u = lax.bitcast_convert_type(packed, jnp.uint32)
    lo = lax.bitcast_convert_type(u << 16, F32)
    hi = lax.bitcast_convert_type(u & jnp.uint32(0xFFFF0000), F32)
    return lo, hi


def _input_ring(x_hbm, buf, sem, step, n_steps, rows):
    def fetch(s):
        slot = lax.rem(s, RING_SLOTS)
        src = x_hbm.at[pl.ds(pl.multiple_of(s * rows, rows), rows), :]
        return pltpu.make_async_copy(src, buf.at[slot], sem.at[slot])

    @pl.when((step == 0) & (n_steps > 0))
    def _():
        fetch(0).start()

    @pl.when((step == 0) & (n_steps > 1))
    def _():
        fetch(1).start()

    @pl.when(step + 2 < n_steps)
    def _():
        fetch(step + 2).start()

    @pl.when(step < n_steps)
    def _():
        fetch(step).wait()

    return lax.rem(step, RING_SLOTS)


def _mod_kernel(c_ref, w_ref, b_ref, o_ref):
    c = c_ref[...]
    s = c * _sigmoid(c)
    o_ref[...] = jnp.dot(s, w_ref[...], preferred_element_type=F32,
                         precision=lax.Precision.HIGHEST) + b_ref[...]


def _modulation(c, w_ada, b_ada):
    B = c.shape[0]
    n_out = w_ada.shape[1]
    bn = D_MODEL
    return pl.pallas_call(
        _mod_kernel,
        grid=(n_out // bn,),
        in_specs=[pl.BlockSpec((B, D_MODEL), lambda j: (0, 0)),
                  pl.BlockSpec((D_MODEL, bn), lambda j: (0, j)),
                  pl.BlockSpec((1, bn), lambda j: (0, j))],
        out_specs=pl.BlockSpec((B, bn), lambda j: (0, j)),
        out_shape=jax.ShapeDtypeStruct((B, n_out), F32),
        name="modulation",
    )(c, w_ada, b_ada.reshape(1, n_out))


def _scaled_norm(x, gain, shift):
    ms = jnp.mean(x * x, axis=-1, keepdims=True)
    return x * lax.rsqrt(ms + EPS) * gain + shift


def _stage_a_kernel(x_hbm, mod_ref, n1g_ref, w_ref, qg_ref, kg_ref, cos_ref, sin_ref,
                    u_ref, qt_ref, qb_ref, k_ref, vt_ref, xbuf, sem):
    tm = u_ref.shape[0]
    n_steps = pl.num_programs(0) * pl.num_programs(1)
    slot = _input_ring(x_hbm, xbuf, sem, pl.program_id(0) * pl.num_programs(1) + pl.program_id(1), n_steps, tm)
    gain1 = n1g_ref[...] * (1.0 + mod_ref[1:2, :])
    shift1 = mod_ref[0:1, :]
    lane = lax.broadcasted_iota(I32, (1, HEAD_DIM), 1)
    first_half = (lane % 64) < 32
    qg = qg_ref[...] * (LOG2E / np.sqrt(HEAD_DIM))
    k_norm_max = jnp.max(jnp.abs(kg_ref[...]), axis=-1, keepdims=True) * (np.sqrt(HEAD_DIM) * 1.01)

    for r0 in range(0, tm, STAGE_ROWS):
        rows = slice(r0, r0 + min(STAGE_ROWS, tm))
        h = _scaled_norm(xbuf[slot, rows, :], gain1, shift1)
        z = jnp.dot(h.astype(BF16), w_ref[...], preferred_element_type=F32)
        u_ref[rows, :] = z[:, :D_POOL].astype(BF16)
        cos = cos_ref[rows, :]
        sin_signed = sin_ref[rows, :]

        def norm_rope(xh, g):
            ms = jnp.mean(xh * xh, axis=-1, keepdims=True)
            xn = xh * lax.rsqrt(ms + EPS) * g
            rot = jnp.where(first_half, pltpu.roll(xn, 96, 1), pltpu.roll(xn, 32, 1))
            return xn * cos + rot * sin_signed

        for hq in range(N_Q_HEADS):
            o = D_POOL + hq * HEAD_DIM
            q_t = norm_rope(z[:, o:o + HEAD_DIM], qg).T.astype(BF16)
            qt_ref[hq, :, rows] = q_t
            q_f = q_t.astype(F32)
            qb_ref[hq, :, rows] = jnp.sqrt(jnp.sum(q_f * q_f, axis=0, keepdims=True)) * k_norm_max
        for hk in range(N_KV_HEADS):
            o = D_POOL + D_ATTN + hk * HEAD_DIM
            k_ref[rows, hk * HEAD_DIM:(hk + 1) * HEAD_DIM] = norm_rope(z[:, o:o + HEAD_DIM], kg_ref[...]).astype(BF16)
            o = D_POOL + D_ATTN + D_KV + hk * HEAD_DIM
            vt_ref[hk, :, rows] = z[:, o:o + HEAD_DIM].T.astype(BF16)


def _stage_a(x, mod, n1g, w_qkvu, qg, kg, cos, sin_signed, tm):
    B, T, _ = x.shape
    return pl.pallas_call(
        _stage_a_kernel,
        grid=(B, T // tm),
        in_specs=[
            pl.BlockSpec(memory_space=pl.ANY),
            pl.BlockSpec((None, 6, D_MODEL), lambda b, i: (b, 0, 0)),
            _const_spec((1, D_MODEL)),
            _const_spec((D_MODEL, D_QKVU)),
            _const_spec((1, HEAD_DIM)),
            _const_spec((1, HEAD_DIM)),
            pl.BlockSpec((tm, HEAD_DIM), lambda b, i: (i, 0)),
            pl.BlockSpec((tm, HEAD_DIM), lambda b, i: (i, 0)),
        ],
        out_specs=[
            pl.BlockSpec((None, tm, D_POOL), lambda b, i: (b, i, 0)),
            pl.BlockSpec((None, N_Q_HEADS, HEAD_DIM, tm), lambda b, i: (b, 0, 0, i)),
            pl.BlockSpec((None, N_Q_HEADS, 1, tm), lambda b, i: (b, 0, 0, i)),
            pl.BlockSpec((None, tm, D_KV), lambda b, i: (b, i, 0)),
            pl.BlockSpec((None, N_KV_HEADS, HEAD_DIM, tm), lambda b, i: (b, 0, 0, i)),
        ],
        out_shape=[
            jax.ShapeDtypeStruct((B, T, D_POOL), BF16),
            jax.ShapeDtypeStruct((B, N_Q_HEADS, HEAD_DIM, T), BF16),
            jax.ShapeDtypeStruct((B, N_Q_HEADS, 1, T), F32),
            jax.ShapeDtypeStruct((B, T, D_KV), BF16),
            jax.ShapeDtypeStruct((B, N_KV_HEADS, HEAD_DIM, T), BF16),
        ],
        scratch_shapes=[pltpu.VMEM((RING_SLOTS, tm, D_MODEL), F32), pltpu.SemaphoreType.DMA((RING_SLOTS,))],
        compiler_params=pltpu.CompilerParams(
            dimension_semantics=("arbitrary", "arbitrary"), vmem_limit_bytes=V7X_VMEM_LIMIT),
        name="stage_a",
    )(x.reshape(B * T, D_MODEL), mod, n1g, w_qkvu, qg, kg, cos, sin_signed)


def _write_attn_output(o_ref, acc, l, tq):
    o = acc * (1.0 / l)
    o_ref[:, :HEAD_DIM] = o[:, :tq].T.astype(BF16)
    o_ref[:, HEAD_DIM:] = o[:, tq:].T.astype(BF16)


def _attn_bounded_kernel(qt_ref, qb_ref, k_ref, vt_ref, o_ref, st_ref, p_ref, acc_ref, l_ref, *, tk):
    T = k_ref.shape[0]
    tq = qt_ref.shape[2]
    nq = 2 * tq
    nk = T // tk
    qt = jnp.concatenate([qt_ref[0], qt_ref[1]], axis=1)
    qb = jnp.concatenate([qb_ref[0], qb_ref[1]], axis=1)

    def scores(j, slot):
        off = pl.multiple_of(j * tk, tk)
        st_ref[slot] = jnp.dot(k_ref[pl.ds(off, tk), :], qt, preferred_element_type=F32)

    def weights(slot):
        p = jnp.exp2(st_ref[slot] - qb)
        l_ref[...] += jnp.sum(p.reshape(tk // 8, 8, nq), axis=0)
        p_ref[slot] = p.astype(BF16)

    def values(j, slot):
        off = pl.multiple_of(j * tk, tk)
        acc_ref[...] += jnp.dot(vt_ref[:, pl.ds(off, tk)], p_ref[slot], preferred_element_type=F32)

    acc_ref[...] = jnp.zeros_like(acc_ref)
    l_ref[...] = jnp.zeros_like(l_ref)
    scores(0, 0)
    weights(0)
    scores(1, 1)

    unroll = next(u for u in ATTN_TILES_PER_TRIP if (nk - 2) % u == 0)

    @pl.loop(0, (nk - 2) // unroll)
    def _(g):
        for d in range(0, unroll, 2):
            a = unroll * g + d + 1
            values(a - 1, 0)
            weights(1)
            scores(a + 1, 0)
            values(a, 1)
            weights(0)
            scores(a + 2, 1)

    values(nk - 2, 0)
    weights(1)
    values(nk - 1, 1)
    _write_attn_output(o_ref, acc_ref[...], jnp.sum(l_ref[...], axis=0, keepdims=True), tq)


def _attn_online_kernel(qt_ref, qb_ref, k_ref, vt_ref, o_ref, *, tk):
    del qb_ref
    T = k_ref.shape[0]
    tq = qt_ref.shape[2]
    nq = 2 * tq
    qt = jnp.concatenate([qt_ref[0], qt_ref[1]], axis=1)

    def body(s, carry):
        m, l, acc = carry
        off = pl.multiple_of(s * tk, tk)
        st = jnp.dot(k_ref[pl.ds(off, tk), :], qt, preferred_element_type=F32)
        m_new = jnp.maximum(m, jnp.max(st, axis=0, keepdims=True))
        alpha = jnp.exp2(m - m_new)
        p = jnp.exp2(st - m_new)
        l = alpha * l + jnp.sum(p, axis=0, keepdims=True)
        pv = jnp.dot(vt_ref[:, pl.ds(off, tk)], p.astype(BF16), preferred_element_type=F32)
        return m_new, l, alpha * acc + pv

    init = (jnp.full((1, nq), -jnp.inf, F32), jnp.zeros((1, nq), F32), jnp.zeros((HEAD_DIM, nq), F32))
    _, l, acc = lax.fori_loop(0, T // tk, body, init)
    _write_attn_output(o_ref, acc, l, tq)


def _attention(qt, qb, k, vt, tq, tk, running_max):
    B, _, _, T = qt.shape
    group = N_Q_HEADS // N_KV_HEADS
    nq = group * tq
    if running_max:
        body, scratch, name = functools.partial(_attn_online_kernel, tk=tk), [], "attention_online_max"
    else:
        assert (T // tk) % 2 == 0 and T // tk >= 2
        body = functools.partial(_attn_bounded_kernel, tk=tk)
        scratch = [pltpu.VMEM((2, tk, nq), F32), pltpu.VMEM((2, tk, nq), BF16),
                   pltpu.VMEM((HEAD_DIM, nq), F32), pltpu.VMEM((8, nq), F32)]
        name = "attention_bounded"
    return pl.pallas_call(
        body,
        grid=(B, N_KV_HEADS, T // tq),
        in_specs=[
            pl.BlockSpec((None, group, HEAD_DIM, tq), lambda b, j, i: (b, j, 0, i)),
            pl.BlockSpec((None, group, 1, tq), lambda b, j, i: (b, j, 0, i)),
            pl.BlockSpec((None, T, HEAD_DIM), lambda b, j, i: (b, 0, j)),
            pl.BlockSpec((None, None, HEAD_DIM, T), lambda b, j, i: (b, j, 0, 0)),
        ],
        out_specs=pl.BlockSpec((None, tq, group * HEAD_DIM), lambda b, j, i: (b, i, j)),
        out_shape=jax.ShapeDtypeStruct((B, T, D_ATTN), BF16),
        scratch_shapes=scratch,
        compiler_params=pltpu.CompilerParams(
            dimension_semantics=("parallel", "parallel", "parallel"),
            vmem_limit_bytes=V7X_VMEM_LIMIT),
        name=name,
    )(qt, qb, k, vt)


def _route(logits_t):
    neg = jnp.float32(-jnp.inf)
    n = logits_t.shape[1]

    def top(vals, idx, far):
        v = jnp.max(vals, axis=0, keepdims=True)
        i = jnp.min(jnp.where(vals == v, idx, far), axis=0, keepdims=True)
        return v, i

    gidx = lax.broadcasted_iota(I32, (ROUTE_ROWS - N_EXPERTS, n), 0).astype(F32)
    grp = jnp.where(gidx < N_GROUPS, logits_t[N_EXPERTS:, :], neg)
    gmax, gsel = top(grp, gidx, jnp.float32(ROUTE_ROWS))
    p_group = 1.0 / jnp.sum(jnp.exp(grp - gmax), axis=0, keepdims=True)
    eidx = lax.broadcasted_iota(I32, (N_EXPERTS, n), 0).astype(F32)
    first = EXPERTS_PER_GROUP * gsel
    le = jnp.where((eidx >= first) & (eidx < first + EXPERTS_PER_GROUP), logits_t[:N_EXPERTS, :], neg)
    v1, e1 = top(le, eidx, jnp.float32(N_EXPERTS))
    v2, e2 = top(jnp.where(eidx == e1, neg, le), eidx, jnp.float32(N_EXPERTS))
    e21 = jnp.exp(v2 - v1)
    w1 = p_group / (1.0 + e21)
    return e1, e2, w1, w1 * e21


def _pool_deviation(ue_ref, lvl_ref, pool_ref, tile_start, tm, seq_len):
    n = tm + 2 * POOL_HALO
    t = tile_start + lax.broadcasted_iota(I32, (tm, 1), 0)
    for gi, w in enumerate(POOL_WINDOWS):
        half = w // 2
        cols = slice(gi * POOL_GC, (gi + 1) * POOL_GC)
        level = ue_ref[0:n, cols] + ue_ref[1:n + 1, cols]
        span, buf = 2, 0
        lvl_ref[buf, 0:n, :] = level
        while span < w:
            level = lvl_ref[buf, 0:n, :] + lvl_ref[buf, span:span + n, :]
            span, buf = 2 * span, 1 - buf
            lvl_ref[buf, 0:n, :] = level
        wsum = lvl_ref[buf, POOL_HALO - half:POOL_HALO - half + tm, :]
        cnt = (jnp.minimum(t + half, seq_len) - jnp.maximum(t - half, 0)).astype(F32)
        p = wsum * (1.0 / cnt) - ue_ref[POOL_HALO:POOL_HALO + tm, cols]
        pool_ref[:, cols] = p.astype(BF16)


def _stage_c_kernel(x_ref, mod_ref, n1g_ref, n2g_ref, u_ref, up_ref, un_ref, attn_ref,
                    wgate_ref, wmix_ref, ps_ref, wpp_ref, wap_ref, wo_ref, wr_ref, br_ref, utri_ref,
                    x1_ref, h2_ref, meta_ref, rw_ref, count_ref, ue_ref, lvl_ref, pool_ref, run_ref, *, seq_len):
    i = pl.program_id(1)
    tm = x_ref.shape[0]

    @pl.when((pl.program_id(0) == 0) & (i == 0))
    def _():
        run_ref[...] = jnp.zeros_like(run_ref)

    gain1 = n1g_ref[...] * (1.0 + mod_ref[1:2, :])
    shift1 = mod_ref[0:1, :]
    gain2 = n2g_ref[...] * (1.0 + mod_ref[4:5, :])
    shift2 = mod_ref[3:4, :]
    res_gate = mod_ref[2:3, :]

    ue_ref[0:POOL_HALO, :] = up_ref[...].astype(F32) * (i > 0).astype(F32)
    ue_ref[POOL_HALO:POOL_HALO + tm, :] = u_ref[...].astype(F32)
    ue_ref[POOL_HALO + tm:2 * POOL_HALO + tm, :] = un_ref[...].astype(F32) * (i < pl.num_programs(1) - 1).astype(F32)
    ue_ref[2 * POOL_HALO + tm:, :] = jnp.zeros((POOL_HALO, D_POOL), F32)
    lvl_ref[:, 2 * POOL_HALO + tm:, :] = jnp.zeros((2, POOL_HALO, POOL_GC), F32)
    _pool_deviation(ue_ref, lvl_ref, pool_ref, i * tm, tm, seq_len)

    for r0 in range(0, tm, MIXER_ROWS):
        rows = slice(r0, r0 + min(MIXER_ROWS, tm))
        x = x_ref[rows, :]
        h = _scaled_norm(x, gain1, shift1)
        gates = _sigmoid(jnp.dot(h.astype(BF16), wgate_ref[...], preferred_element_type=F32))
        mixed_groups = []
        for gi in range(len(POOL_WINDOWS)):
            cols = slice(gi * POOL_GC, (gi + 1) * POOL_GC)
            pm = jnp.dot(pool_ref[rows, cols], wmix_ref[gi], preferred_element_type=F32)
            mixed_groups.append(pm * ps_ref[:, cols])
        pool = jnp.concatenate(mixed_groups, axis=1)
        pool_out = jnp.dot(pool.astype(BF16), wpp_ref[...], preferred_element_type=F32)
        attn_out = jnp.dot(attn_ref[rows, :], wap_ref[...], preferred_element_type=F32)
        merged = gates[:, :D_MODEL] * pool_out + gates[:, D_MODEL:] * attn_out
        mixed = jnp.dot(merged.astype(BF16), wo_ref[...], preferred_element_type=F32)
        x1 = x + res_gate * mixed
        x1_ref[rows, :] = x1

        h2 = _scaled_norm(x1, gain2, shift2)
        h2_ref[rows, :] = _pack_bf16_pair(h2[:, :D_PACK], h2[:, D_PACK:])
        h_hi = h2.astype(BF16)
        h_lo = (h2 - h_hi.astype(F32)).astype(BF16)
        parts = (jnp.dot(h_hi, wr_ref[...], preferred_element_type=F32)
                 + jnp.dot(h_lo, wr_ref[...], preferred_element_type=F32))
        logits = parts[:, :LANES] + parts[:, LANES:] + br_ref[...]
        e1, e2, w1, w2 = _route(logits.T[0:ROUTE_ROWS, :])
        n = logits.shape[0]

        eidx = lax.broadcasted_iota(I32, (N_EXPERTS, n), 0).astype(F32)
        hit1 = eidx == e1
        hit2 = eidx == e2
        taken = jnp.where(hit1 | hit2, 1.0, 0.0)
        before = run_ref[...] + jnp.dot(taken.astype(BF16), utri_ref[...], preferred_element_type=F32)
        rank1 = jnp.sum(jnp.where(hit1, before, 0.0), axis=0, keepdims=True)
        rank2 = jnp.sum(jnp.where(hit2, before, 0.0), axis=0, keepdims=True)
        run_ref[...] = run_ref[...] + jnp.sum(taken, axis=1, keepdims=True)
        ridx = lax.broadcasted_iota(I32, (META_ROWS, n), 0)
        table = jnp.where(ridx == 0, e1, jnp.where(ridx == 1, e2, jnp.where(ridx == 2, rank1,
                          jnp.where(ridx == 3, rank2, 0.0))))
        meta_ref[:, rows] = table.astype(I32)
        weights_t = jnp.where(ridx == 0, w1, jnp.where(ridx == 1, w2, 0.0))
        rw_ref[rows, :] = jnp.concatenate([weights_t, jnp.zeros((LANES - META_ROWS, n), F32)], axis=0).T
    count_ref[...] = jnp.broadcast_to(run_ref[...], count_ref.shape)


def _stage_c(x, mod, n1g, n2g, u, attn, w_gate, w_mix, pool_scale, w_pp, w_ap, w_o, w_r, b_r, tm):
    B, T, _ = x.shape
    hb = tm // POOL_HALO
    n_halo_blocks = T // POOL_HALO
    tok = lambda b, i: (b, i, 0)
    flat = lambda b, i: (b * (T // tm) + i, 0)
    group_rows = min(MIXER_ROWS, tm)
    utri = jnp.triu(jnp.ones((group_rows, group_rows), BF16), k=1)
    return pl.pallas_call(
        functools.partial(_stage_c_kernel, seq_len=T),
        grid=(B, T // tm),
        in_specs=[
            pl.BlockSpec((None, tm, D_MODEL), tok),
            pl.BlockSpec((None, 6, D_MODEL), lambda b, i: (b, 0, 0)),
            _const_spec((1, D_MODEL)),
            _const_spec((1, D_MODEL)),
            pl.BlockSpec((None, tm, D_POOL), tok),
            pl.BlockSpec((None, POOL_HALO, D_POOL), lambda b, i: (b, jnp.maximum(i * hb - 1, 0), 0)),
            pl.BlockSpec((None, POOL_HALO, D_POOL),
                         lambda b, i: (b, jnp.minimum((i + 1) * hb, n_halo_blocks - 1), 0)),
            pl.BlockSpec((None, tm, D_ATTN), tok),
            _const_spec((D_MODEL, 2 * D_MODEL)),
            _const_spec((len(POOL_WINDOWS), POOL_GC, POOL_GC)),
            _const_spec((1, D_POOL)),
            _const_spec((D_POOL, D_MODEL)),
            _const_spec((D_ATTN, D_MODEL)),
            _const_spec((D_MODEL, D_MODEL)),
            _const_spec((D_MODEL, 2 * LANES)),
            _const_spec((1, LANES)),
            _const_spec((group_rows, group_rows)),
        ],
        out_specs=[
            pl.BlockSpec((None, tm, D_MODEL), tok),
            pl.BlockSpec((tm, D_PACK), flat),
            pl.BlockSpec((META_ROWS, tm), lambda b, i: (0, b * (T // tm) + i)),
            pl.BlockSpec((tm, LANES), flat),
            pl.BlockSpec((N_EXPERTS, LANES), lambda b, i: (0, 0)),
        ],
        out_shape=[
            jax.ShapeDtypeStruct((B, T, D_MODEL), F32),
            jax.ShapeDtypeStruct((B * T, D_PACK), I32),
            jax.ShapeDtypeStruct((META_ROWS, B * T), I32),
            jax.ShapeDtypeStruct((B * T, LANES), F32),
            jax.ShapeDtypeStruct((N_EXPERTS, LANES), F32),
        ],
        scratch_shapes=[pltpu.VMEM((tm + 3 * POOL_HALO, D_POOL), F32),
                        pltpu.VMEM((2, tm + 3 * POOL_HALO, POOL_GC), F32),
                        pltpu.VMEM((tm, D_POOL), BF16),
                        pltpu.VMEM((N_EXPERTS, 1), F32)],
        compiler_params=pltpu.CompilerParams(
            dimension_semantics=("arbitrary", "arbitrary"), vmem_limit_bytes=V7X_VMEM_LIMIT),
        name="stage_c",
    )(x, mod, n1g, n2g, u, u, u, attn, w_gate, w_mix, pool_scale, w_pp, w_ap, w_o, w_r, b_r, utri)


def _sc_workers():
    info = plsc.get_sparse_core_info()
    return info.num_cores, info.num_subcores


def _sc_scatter_rows(rows, idx0, idx1, n_out):
    n, d = rows.shape
    nc, ns = _sc_workers()
    per_worker = n // (nc * ns)
    n_win = per_worker // SC_WINDOW
    mesh = plsc.VectorSubcoreMesh(core_axis_name="c", subcore_axis_name="s")

    @functools.partial(
        pl.kernel, mesh=mesh,
        out_type=jax.ShapeDtypeStruct((n_out, d), rows.dtype),
        scratch_types=[pltpu.VMEM((SC_WINDOW,), I32), pltpu.VMEM((SC_WINDOW,), I32),
                       pltpu.VMEM((SC_WINDOW, d), rows.dtype)],
        name="sc_scatter_rows",
    )
    def scatter(rows_hbm, idx0_hbm, idx1_hbm, out_hbm, i0_v, i1_v, rows_v):
        wid = lax.axis_index("s") * nc + lax.axis_index("c")

        @pl.loop(0, n_win)
        def _(w):
            base = wid * per_worker + w * SC_WINDOW
            pltpu.sync_copy(rows_hbm.at[pl.ds(base, SC_WINDOW)], rows_v)
            pltpu.sync_copy(idx0_hbm.at[pl.ds(base, SC_WINDOW)], i0_v)
            pltpu.sync_copy(idx1_hbm.at[pl.ds(base, SC_WINDOW)], i1_v)
            pltpu.sync_copy(rows_v, out_hbm.at[i0_v])
            pltpu.sync_copy(rows_v, out_hbm.at[i1_v])

    return scatter(rows, idx0, idx1)


def _sc_gather_rows(table, idx):
    n = idx.shape[0]
    d = table.shape[1]
    nc, ns = _sc_workers()
    per_worker = n // (nc * ns)
    n_win = per_worker // SC_WINDOW
    mesh = plsc.VectorSubcoreMesh(core_axis_name="c", subcore_axis_name="s")

    @functools.partial(
        pl.kernel, mesh=mesh,
        out_type=jax.ShapeDtypeStruct((n, d), table.dtype),
        scratch_types=[pltpu.VMEM((SC_WINDOW,), I32), pltpu.VMEM((SC_WINDOW, d), table.dtype)],
        name="sc_gather_rows",
    )
    def gather(table_hbm, idx_hbm, out_hbm, i_v, rows_v):
        wid = lax.axis_index("s") * nc + lax.axis_index("c")

        @pl.loop(0, n_win)
        def _(w):
            base = wid * per_worker + w * SC_WINDOW
            pltpu.sync_copy(idx_hbm.at[pl.ds(base, SC_WINDOW)], i_v)
            pltpu.sync_copy(table_hbm.at[i_v], rows_v)
            pltpu.sync_copy(rows_v, out_hbm.at[pl.ds(base, SC_WINDOW)])

    return gather(table, idx)


def _moe_kernel(te_ref, tv_ref, nl_ref, x_hbm, wgu_ref, wd_ref, o_ref, xbuf, sem):
    del te_ref
    i = pl.program_id(0)
    n_live_tiles = nl_ref[0]
    valid = tv_ref[i]
    tm = o_ref.shape[0]
    slot = _input_ring(x_hbm, xbuf, sem, i, n_live_tiles, tm)

    group = min(STAGE_ROWS, tm)
    n_groups = tm // group

    def expert_rows(r0):
        rows = slice(r0, r0 + group)
        live = lax.broadcasted_iota(I32, (group, 1), 0) < valid - r0
        lo, hi = _unpack_bf16_pair(jnp.where(live, xbuf[slot, rows, :], 0))
        x = jnp.concatenate([lo, hi], axis=1).astype(BF16)
        gu = jnp.dot(x, wgu_ref[...], preferred_element_type=F32)
        a = gu[:, :D_EXPERT]
        hmid = (a * _sigmoid(a) * gu[:, D_EXPERT:]).astype(BF16)
        y = jnp.dot(hmid, wd_ref[...], preferred_element_type=F32)
        o_ref[rows, :] = _pack_bf16_pair(y[:, :D_PACK], y[:, D_PACK:])

    for n_live in range(n_groups + 1):
        @pl.when((valid > (n_live - 1) * group) & (valid <= n_live * group))
        def _():
            for g in range(n_live):
                expert_rows(g * group)
            if n_live < n_groups:
                o_ref[n_live * group:, :] = jnp.zeros((tm - n_live * group, D_PACK), I32)


def _moe(xs, tile_expert, tile_valid, n_live_tiles, w_gu, w_d, tm):
    n_tiles = xs.shape[0] // tm
    grid_spec = pltpu.PrefetchScalarGridSpec(
        num_scalar_prefetch=3,
        grid=(n_tiles,),
        in_specs=[
            pl.BlockSpec(memory_space=pl.ANY),
            pl.BlockSpec((None, D_MODEL, 2 * D_EXPERT), lambda i, te, tv, nl: (te[i], 0, 0)),
            pl.BlockSpec((None, D_EXPERT, D_MODEL), lambda i, te, tv, nl: (te[i], 0, 0)),
        ],
        out_specs=pl.BlockSpec((tm, D_PACK), lambda i, te, tv, nl: (jnp.minimum(i, nl[0]), 0)),
        scratch_shapes=[pltpu.VMEM((RING_SLOTS, tm, D_PACK), I32), pltpu.SemaphoreType.DMA((RING_SLOTS,))],
    )
    return pl.pallas_call(
        _moe_kernel,
        grid_spec=grid_spec,
        out_shape=jax.ShapeDtypeStruct(xs.shape, I32),
        compiler_params=pltpu.CompilerParams(
            dimension_semantics=("arbitrary",), vmem_limit_bytes=V7X_VMEM_LIMIT),
        name="experts",
    )(tile_expert, tile_valid, n_live_tiles, xs, w_gu, w_d)


def _final_kernel(x1_ref, y0_ref, y1_ref, rw_ref, mod_ref, fg_ref, o_ref):
    y0lo, y0hi = _unpack_bf16_pair(y0_ref[...])
    y1lo, y1hi = _unpack_bf16_pair(y1_ref[...])
    w0 = rw_ref[:, 0:1]
    w1 = rw_ref[:, 1:2]
    moe = jnp.concatenate([w0 * y0lo + w1 * y1lo, w0 * y0hi + w1 * y1hi], axis=1)
    x2 = x1_ref[...] + mod_ref[5:6, :] * moe
    ms = jnp.mean(x2 * x2, axis=-1, keepdims=True)
    o_ref[...] = x2 * lax.rsqrt(ms + EPS) * fg_ref[...]


def _final(x1, yb, rw, mod, final_g, tm):
    B, T, _ = x1.shape
    n_blocks = B * T // tm
    flat = lambda b, i: (b * (T // tm) + i, 0)
    return pl.pallas_call(
        _final_kernel,
        grid=(B, T // tm),
        in_specs=[
            pl.BlockSpec((None, tm, D_MODEL), lambda b, i: (b, i, 0)),
            pl.BlockSpec((tm, D_PACK), flat),
            pl.BlockSpec((tm, D_PACK), lambda b, i: (n_blocks + b * (T // tm) + i, 0)),
            pl.BlockSpec((tm, LANES), flat),
            pl.BlockSpec((None, 6, D_MODEL), lambda b, i: (b, 0, 0)),
            _const_spec((1, D_MODEL)),
        ],
        out_specs=pl.BlockSpec((None, tm, D_MODEL), lambda b, i: (b, i, 0)),
        out_shape=jax.ShapeDtypeStruct((B, T, D_MODEL), F32),
        compiler_params=pltpu.CompilerParams(
            dimension_semantics=("parallel", "parallel"), vmem_limit_bytes=V7X_VMEM_LIMIT),
        name="final_combine",
    )(x1, yb, yb, rw, mod, final_g)


def _rope_tables(T):
    rows = T // GRID_W
    row = jnp.repeat(jnp.arange(rows, dtype=F32), GRID_W)
    col = jnp.tile(jnp.arange(GRID_W, dtype=F32), rows)
    inv_freq = ROPE_THETA ** (-jnp.arange(0, 64, 2, dtype=F32) / 64)
    ang_r = row[:, None] * inv_freq[None, :]
    ang_c = col[:, None] * inv_freq[None, :]
    ang = jnp.concatenate([ang_r, ang_r, ang_c, ang_c], axis=-1)
    sign = jnp.where((jnp.arange(HEAD_DIM) % 64) < 32, -1.0, 1.0).astype(F32)
    return jnp.cos(ang), jnp.sin(ang) * sign[None, :]


def _dispatch_plan(counts, tm, n_tiles):
    tiles_per_e = (counts + tm - 1) // tm
    tile_end = jnp.cumsum(tiles_per_e)
    tile_start = tile_end - tiles_per_e
    tile_id = jnp.arange(n_tiles, dtype=I32)
    te = jnp.minimum(jnp.sum((tile_id[:, None] >= tile_end[None, :]).astype(I32), axis=1), N_EXPERTS - 1)
    live = jnp.clip(counts[te] - (tile_id - tile_start[te]) * tm, 0, tm)
    tv = jnp.where(tile_id < tile_end[-1], live, 0).astype(I32)
    return tile_start.astype(I32), te, tv, tile_end[-1:].astype(I32)


def _slot_kernel(start_ref, meta_ref, pos_ref, *, tile_rows):
    e = meta_ref[0:2, :]
    first_tile = jnp.zeros_like(e)
    for ex in range(N_EXPERTS):
        first_tile = jnp.where(e == ex, start_ref[ex], first_tile)
    pos_ref[...] = first_tile * tile_rows + meta_ref[2:4, :]


def _slots(meta, tile_start, tile_rows):
    n = meta.shape[1]
    tn = min(n, 8192)
    grid_spec = pltpu.PrefetchScalarGridSpec(
        num_scalar_prefetch=1,
        grid=(n // tn,),
        in_specs=[pl.BlockSpec((META_ROWS, tn), lambda i, start: (0, i))],
        out_specs=pl.BlockSpec((2, tn), lambda i, start: (0, i)),
    )
    return pl.pallas_call(
        functools.partial(_slot_kernel, tile_rows=tile_rows),
        grid_spec=grid_spec,
        out_shape=jax.ShapeDtypeStruct((2, n), I32),
        name="dispatch_slots",
    )(tile_start, meta)


def _pick_tile(T, want):
    t = min(T, want)
    assert T % t == 0
    return t


def _trunk(x, c, p):
    B, T, _ = x.shape
    N = B * T
    tm = _pick_tile(T, 512)
    tq = _pick_tile(T, 512)
    tk = _pick_tile(T, 256)
    tme = _pick_tile(N, 512)
    cos, sin_signed = _rope_tables(T)
    mod = _modulation(c, p["w_ada"], p["b_ada"]).reshape(B, 6, D_MODEL)
    u, qt, qb, k, vt = _stage_a(x, mod, p["n1g"], p["w_qkvu"], p["qg"], p["kg"], cos, sin_signed, tm)
    attn = lax.cond(
        p["bounded_softmax_ok"],
        functools.partial(_attention, tq=tq, tk=tk, running_max=False),
        functools.partial(_attention, tq=tq, tk=tk, running_max=True),
        qt, qb, k, vt)
    x1, h2p, meta, rw, counts = _stage_c(x, mod, p["n1g"], p["n2g"], u, attn, p["w_gate"], p["w_mix"],
                                         p["pool_scale"], p["w_pp"], p["w_ap"], p["w_o"], p["w_r"], p["b_r"], tm)
    n_tiles = 2 * N // tme + N_EXPERTS
    tile_start, te, tv, n_live_tiles = _dispatch_plan(counts[:, 0].astype(I32), tme, n_tiles)
    pos = _slots(meta, tile_start, tme)
    xs = _sc_scatter_rows(h2p, pos[0], pos[1], n_tiles * tme)
    ys = _moe(xs, te, tv, n_live_tiles, p["w_egu"], p["w_ed"], tme)
    yb = _sc_gather_rows(ys, pos.reshape(-1))
    return _final(x1, yb, rw, mod, p["final_g"], _pick_tile(T, 1024))


def kernel(x_prompt, x_sample, c_prompt, c_sample, w_ada, b_ada, norm1_g, norm2_g, w_in, q_norm_g,
           k_norm_g, w_pool_mix, pool_scale, w_pool_proj, w_attn_proj, w_o, w_router_group,
           b_router_group, w_router_expert, b_router_expert, w_exp_gate, w_exp_up, w_exp_down, final_g):
    assert w_ada.shape[0] == 1, "single-layer block"
    n_r = N_GROUPS + N_EXPERTS
    w_r = jnp.concatenate([w_router_expert[0], w_router_group[0],
                           jnp.zeros((D_MODEL, LANES - n_r), F32)], axis=1)
    w_r_hi = w_r.astype(BF16)
    b_r = jnp.concatenate([b_router_expert[0], b_router_group[0], jnp.zeros((LANES - n_r,), F32)])
    score_bound = (1.01 * LOG2E * np.sqrt(HEAD_DIM)) * jnp.max(jnp.abs(q_norm_g[0])) * jnp.max(jnp.abs(k_norm_g[0]))
    p = dict(
        bounded_softmax_ok=2.0 * score_bound <= EXP2_SAFE_SPAN,
        w_ada=w_ada[0], b_ada=b_ada[0],
        n1g=norm1_g[0].reshape(1, D_MODEL), n2g=norm2_g[0].reshape(1, D_MODEL),
        w_qkvu=w_in[0][:, :D_QKVU].astype(BF16), w_gate=w_in[0][:, D_QKVU:].astype(BF16),
        qg=q_norm_g[0].reshape(1, HEAD_DIM), kg=k_norm_g[0].reshape(1, HEAD_DIM),
        w_mix=w_pool_mix[0].astype(BF16), pool_scale=pool_scale[0].reshape(1, D_POOL),
        w_pp=w_pool_proj[0].astype(BF16), w_ap=w_attn_proj[0].astype(BF16), w_o=w_o[0].astype(BF16),
        w_r=jnp.concatenate([w_r_hi, (w_r - w_r_hi.astype(F32)).astype(BF16)], axis=1), b_r=b_r.reshape(1, LANES),
        w_egu=jnp.concatenate([w_exp_gate[0], w_exp_up[0]], axis=-1).astype(BF16), w_ed=w_exp_down[0].astype(BF16),
        final_g=final_g.reshape(1, D_MODEL),
    )
    return _trunk(x_prompt, c_prompt, p), _trunk(x_sample, c_sample, p)
```

```python
import functools

import numpy as np
import jax
import jax.numpy as jnp
from jax import lax
from jax.experimental import pallas as pl
from jax.experimental.pallas import tpu as pltpu
from jax.experimental.pallas import tpu_sc as plsc

F32 = jnp.float32
BF16 = jnp.bfloat16
I32 = jnp.int32

D_MODEL = 1024
GRID_W = 64
POOL_WINDOWS = (2, 4, 8, 16)
POOL_GC = 128
D_POOL = 512
HEAD_DIM = 128
N_Q_HEADS = 4
N_KV_HEADS = 2
D_ATTN = 512
D_KV = 256
ROPE_THETA = 10000.0
N_GROUPS = 4
EXPERTS_PER_GROUP = 8
N_EXPERTS = 32
D_EXPERT = 256
EPS = 1e-6
LOG2E = 1.4426950408889634
EXP2_SAFE_SPAN = 100.0
D_QKVU = D_POOL + D_ATTN + 2 * D_KV
POOL_HALO = 16
STAGE_ROWS = 256
MIXER_ROWS = 512
RING_SLOTS = 3
ATTN_TILES_PER_TRIP = 10
ROUTE_ROWS = 40
META_ROWS = 8
LANES = 128
D_PACK = D_MODEL // 2

V7X_VMEM_LIMIT = 56 * 1024 * 1024
SC_WINDOW = 64


def _const_spec(shape):
    nd = len(shape)
    return pl.BlockSpec(shape, lambda *_: (0,) * nd, pipeline_mode=pl.Buffered(1))


def _sigmoid(x):
    return 0.5 * jnp.tanh(0.5 * x) + 0.5


def _pack_bf16_pair(lo, hi):
    lo_bits = lax.bitcast_convert_type(lo.astype(BF16).astype(F32), jnp.uint32) >> 16
    hi_bits = lax.bitcast_convert_type(hi.astype(BF16).astype(F32), jnp.uint32) & jnp.uint32(0xFFFF0000)
    return lax.bitcast_convert_type(lo_bits | hi_bits, I32)


def _unpack_bf16_pair(packed):
    u = lax.bitcast_convert_type(packed, jnp.uint32)
    lo = lax.bitcast_convert_type(u << 16, F32)
    hi = lax.bitcast_convert_type(u & jnp.uint32(0xFFFF0000), F32)
    return lo, hi


def _input_ring(x_hbm, buf, sem, step, n_steps, rows):
    def fetch(s):
        slot = lax.rem(s, RING_SLOTS)
        src = x_hbm.at[pl.ds(pl.multiple_of(s * rows, rows), rows), :]
        return pltpu.make_async_copy(src, buf.at[slot], sem.at[slot])

    @pl.when((step == 0) & (n_steps > 0))
    def _():
        fetch(0).start()

    @pl.when((step == 0) & (n_steps > 1))
    def _():
        fetch(1).start()

    @pl.when(step + 2 < n_steps)
    def _():
        fetch(step + 2).start()

    @pl.when(step < n_steps)
    def _():
        fetch(step).wait()

    return lax.rem(step, RING_SLOTS)


def _mod_kernel(c_ref, w_ref, b_ref, o_ref):
    c = c_ref[...]
    s = c * _sigmoid(c)
    o_ref[...] = jnp.dot(s, w_ref[...], preferred_element_type=F32,
                         precision=lax.Precision.HIGHEST) + b_ref[...]


def _modulation(c, w_ada, b_ada):
    B = c.shape[0]
    n_out = w_ada.shape[1]
    bn = D_MODEL
    return pl.pallas_call(
        _mod_kernel,
        grid=(n_out // bn,),
        in_specs=[pl.BlockSpec((B, D_MODEL), lambda j: (0, 0)),
                  pl.BlockSpec((D_MODEL, bn), lambda j: (0, j)),
                  pl.BlockSpec((1, bn), lambda j: (0, j))],
        out_specs=pl.BlockSpec((B, bn), lambda j: (0, j)),
        out_shape=jax.ShapeDtypeStruct((B, n_out), F32),
        name="modulation",
    )(c, w_ada, b_ada.reshape(1, n_out))


def _scaled_norm(x, gain, shift):
    ms = jnp.mean(x * x, axis=-1, keepdims=True)
    return x * lax.rsqrt(ms + EPS) * gain + shift


def _stage_a_kernel(x_hbm, mod_ref, n1g_ref, w_ref, qg_ref, kg_ref, cos_ref, sin_ref,
                    u_ref, qt_ref, qb_ref, k_ref, vt_ref, xbuf, sem):
    tm = u_ref.shape[0]
    n_steps = pl.num_programs(0) * pl.num_programs(1)
    slot = _input_ring(x_hbm, xbuf, sem, pl.program_id(0) * pl.num_programs(1) + pl.program_id(1), n_steps, tm)
    gain1 = n1g_ref[...] * (1.0 + mod_ref[1:2, :])
    shift1 = mod_ref[0:1, :]
    lane = lax.broadcasted_iota(I32, (1, HEAD_DIM), 1)
    first_half = (lane % 64) < 32
    qg = qg_ref[...] * (LOG2E / np.sqrt(HEAD_DIM))
    k_norm_max = jnp.max(jnp.abs(kg_ref[...]), axis=-1, keepdims=True) * (np.sqrt(HEAD_DIM) * 1.01)

    for r0 in range(0, tm, STAGE_ROWS):
        rows = slice(r0, r0 + min(STAGE_ROWS, tm))
        h = _scaled_norm(xbuf[slot, rows, :], gain1, shift1)
        z = jnp.dot(h.astype(BF16), w_ref[...], preferred_element_type=F32)
        u_ref[rows, :] = z[:, :D_POOL].astype(BF16)
        cos = cos_ref[rows, :]
        sin_signed = sin_ref[rows, :]

        def norm_rope(xh, g):
            ms = jnp.mean(xh * xh, axis=-1, keepdims=True)
            xn = xh * lax.rsqrt(ms + EPS) * g
            rot = jnp.where(first_half, pltpu.roll(xn, 96, 1), pltpu.roll(xn, 32, 1))
            return xn * cos + rot * sin_signed

        for hq in range(N_Q_HEADS):
            o = D_POOL + hq * HEAD_DIM
            q_t = norm_rope(z[:, o:o + HEAD_DIM], qg).T.astype(BF16)
            qt_ref[hq, :, rows] = q_t
            q_f = q_t.astype(F32)
            qb_ref[hq, :, rows] = jnp.sqrt(jnp.sum(q_f * q_f, axis=0, keepdims=True)) * k_norm_max
        for hk in range(N_KV_HEADS):
            o = D_POOL + D_ATTN + hk * HEAD_DIM
            k_ref[rows, hk * HEAD_DIM:(hk + 1) * HEAD_DIM] = norm_rope(z[:, o:o + HEAD_DIM], kg_ref[...]).astype(BF16)
            o = D_POOL + D_ATTN + D_KV + hk * HEAD_DIM
            vt_ref[hk, :, rows] = z[:, o:o + HEAD_DIM].T.astype(BF16)


def _stage_a(x, mod, n1g, w_qkvu, qg, kg, cos, sin_signed, tm):
    B, T, _ = x.shape
    return pl.pallas_call(
        _stage_a_kernel,
        grid=(B, T // tm),
        in_specs=[
            pl.BlockSpec(memory_space=pl.ANY),
            pl.BlockSpec((None, 6, D_MODEL), lambda b, i: (b, 0, 0)),
            _const_spec((1, D_MODEL)),
            _const_spec((D_MODEL, D_QKVU)),
            _const_spec((1, HEAD_DIM)),
            _const_spec((1, HEAD_DIM)),
            pl.BlockSpec((tm, HEAD_DIM), lambda b, i: (i, 0)),
            pl.BlockSpec((tm, HEAD_DIM), lambda b, i: (i, 0)),
        ],
        out_specs=[
            pl.BlockSpec((None, tm, D_POOL), lambda b, i: (b, i, 0)),
            pl.BlockSpec((None, N_Q_HEADS, HEAD_DIM, tm), lambda b, i: (b, 0, 0, i)),
            pl.BlockSpec((None, N_Q_HEADS, 1, tm), lambda b, i: (b, 0, 0, i)),
            pl.BlockSpec((None, tm, D_KV), lambda b, i: (b, i, 0)),
            pl.BlockSpec((None, N_KV_HEADS, HEAD_DIM, tm), lambda b, i: (b, 0, 0, i)),
        ],
        out_shape=[
            jax.ShapeDtypeStruct((B, T, D_POOL), BF16),
            jax.ShapeDtypeStruct((B, N_Q_HEADS, HEAD_DIM, T), BF16),
            jax.ShapeDtypeStruct((B, N_Q_HEADS, 1, T), F32),
            jax.ShapeDtypeStruct((B, T, D_KV), BF16),
            jax.ShapeDtypeStruct((B, N_KV_HEADS, HEAD_DIM, T), BF16),
        ],
        scratch_shapes=[pltpu.VMEM((RING_SLOTS, tm, D_MODEL), F32), pltpu.SemaphoreType.DMA((RING_SLOTS,))],
        compiler_params=pltpu.CompilerParams(
            dimension_semantics=("arbitrary", "arbitrary"), vmem_limit_bytes=V7X_VMEM_LIMIT),
        name="stage_a",
    )(x.reshape(B * T, D_MODEL), mod, n1g, w_qkvu, qg, kg, cos, sin_signed)


def _write_attn_output(o_ref, acc, l, tq):
    o = acc * (1.0 / l)
    o_ref[:, :HEAD_DIM] = o[:, :tq].T.astype(BF16)
    o_ref[:, HEAD_DIM:] = o[:, tq:].T.astype(BF16)


def _attn_bounded_kernel(qt_ref, qb_ref, k_ref, vt_ref, o_ref, p_ref, acc_ref, l_ref, *, tk):
    T = k_ref.shape[0]
    tq = qt_ref.shape[2]
    nq = 2 * tq
    nk = T // tk
    qt = jnp.concatenate([qt_ref[0], qt_ref[1]], axis=1)
    qb = jnp.concatenate([qb_ref[0], qb_ref[1]], axis=1)

    def weights(j, slot):
        off = pl.multiple_of(j * tk, tk)
        st = jnp.dot(k_ref[pl.ds(off, tk), :], qt, preferred_element_type=F32)
        p = jnp.exp2(st - qb)
        l_ref[...] += jnp.sum(p.reshape(tk // 8, 8, nq), axis=0)
        p_ref[slot] = p.astype(BF16)

    def values(j, slot):
        off = pl.multiple_of(j * tk, tk)
        acc_ref[...] += jnp.dot(vt_ref[:, pl.ds(off, tk)], p_ref[slot], preferred_element_type=F32)

    def step(j, slot):
        values(j - 1, 1 - slot)
        weights(j, slot)

    acc_ref[...] = jnp.zeros_like(acc_ref)
    l_ref[...] = jnp.zeros_like(l_ref)
    weights(0, 0)

    unroll = max(1, min(ATTN_TILES_PER_TRIP, nk - 1))
    trips = (nk - 1) // unroll if unroll % 2 == 0 else 0

    if trips > 1:
        @pl.loop(0, trips)
        def _(g):
            for d in range(unroll):
                step(1 + unroll * g + d, (1 + d) % 2)
    else:
        trips = 0
    for j in range(1 + trips * unroll, nk):
        step(j, j % 2)
    values(nk - 1, (nk - 1) % 2)
    _write_attn_output(o_ref, acc_ref[...], jnp.sum(l_ref[...], axis=0, keepdims=True), tq)


def _attn_online_kernel(qt_ref, qb_ref, k_ref, vt_ref, o_ref, *, tk):
    del qb_ref
    T = k_ref.shape[0]
    tq = qt_ref.shape[2]
    nq = 2 * tq
    qt = jnp.concatenate([qt_ref[0], qt_ref[1]], axis=1)

    def body(s, carry):
        m, l, acc = carry
        off = pl.multiple_of(s * tk, tk)
        st = jnp.dot(k_ref[pl.ds(off, tk), :], qt, preferred_element_type=F32)
        m_new = jnp.maximum(m, jnp.max(st, axis=0, keepdims=True))
        alpha = jnp.exp2(m - m_new)
        p = jnp.exp2(st - m_new)
        l = alpha * l + jnp.sum(p, axis=0, keepdims=True)
        pv = jnp.dot(vt_ref[:, pl.ds(off, tk)], p.astype(BF16), preferred_element_type=F32)
        return m_new, l, alpha * acc + pv

    init = (jnp.full((1, nq), -jnp.inf, F32), jnp.zeros((1, nq), F32), jnp.zeros((HEAD_DIM, nq), F32))
    _, l, acc = lax.fori_loop(0, T // tk, body, init)
    _write_attn_output(o_ref, acc, l, tq)


def _attention(qt, qb, k, vt, tq, tk, running_max):
    B, _, _, T = qt.shape
    group = N_Q_HEADS // N_KV_HEADS
    nq = group * tq
    if running_max:
        body, scratch, name = functools.partial(_attn_online_kernel, tk=tk), [], "attention_online_max"
    else:
        body = functools.partial(_attn_bounded_kernel, tk=tk)
        scratch = [pltpu.VMEM((2, tk, nq), BF16), pltpu.VMEM((HEAD_DIM, nq), F32), pltpu.VMEM((8, nq), F32)]
        name = "attention_bounded"
    return pl.pallas_call(
        body,
        grid=(B, N_KV_HEADS, T // tq),
        in_specs=[
            pl.BlockSpec((None, group, HEAD_DIM, tq), lambda b, j, i: (b, j, 0, i)),
            pl.BlockSpec((None, group, 1, tq), lambda b, j, i: (b, j, 0, i)),
            pl.BlockSpec((None, T, HEAD_DIM), lambda b, j, i: (b, 0, j)),
            pl.BlockSpec((None, None, HEAD_DIM, T), lambda b, j, i: (b, j, 0, 0)),
        ],
        out_specs=pl.BlockSpec((None, tq, group * HEAD_DIM), lambda b, j, i: (b, i, j)),
        out_shape=jax.ShapeDtypeStruct((B, T, D_ATTN), BF16),
        scratch_shapes=scratch,
        compiler_params=pltpu.CompilerParams(
            dimension_semantics=("parallel", "parallel", "parallel"),
            vmem_limit_bytes=V7X_VMEM_LIMIT),
        name=name,
    )(qt, qb, k, vt)


def _route(logits_t):
    neg = jnp.float32(-jnp.inf)
    n = logits_t.shape[1]

    def top(vals, idx, far):
        v = jnp.max(vals, axis=0, keepdims=True)
        i = jnp.min(jnp.where(vals == v, idx, far), axis=0, keepdims=True)
        return v, i

    gidx = lax.broadcasted_iota(I32, (ROUTE_ROWS - N_EXPERTS, n), 0).astype(F32)
    grp = jnp.where(gidx < N_GROUPS, logits_t[N_EXPERTS:, :], neg)
    gmax, gsel = top(grp, gidx, jnp.float32(ROUTE_ROWS))
    p_group = 1.0 / jnp.sum(jnp.exp(grp - gmax), axis=0, keepdims=True)
    eidx = lax.broadcasted_iota(I32, (N_EXPERTS, n), 0).astype(F32)
    first = EXPERTS_PER_GROUP * gsel
    le = jnp.where((eidx >= first) & (eidx < first + EXPERTS_PER_GROUP), logits_t[:N_EXPERTS, :], neg)
    v1, e1 = top(le, eidx, jnp.float32(N_EXPERTS))
    v2, e2 = top(jnp.where(eidx == e1, neg, le), eidx, jnp.float32(N_EXPERTS))
    e21 = jnp.exp(v2 - v1)
    w1 = p_group / (1.0 + e21)
    return e1, e2, w1, w1 * e21


def _pool_deviation(ue_ref, lvl_ref, pool_ref, tile_start, tm, seq_len):
    n = tm + 2 * POOL_HALO
    t = tile_start + lax.broadcasted_iota(I32, (tm, 1), 0)
    for gi, w in enumerate(POOL_WINDOWS):
        half = w // 2
        cols = slice(gi * POOL_GC, (gi + 1) * POOL_GC)
        level = ue_ref[0:n, cols] + ue_ref[1:n + 1, cols]
        span, buf = 2, 0
        lvl_ref[buf, 0:n, :] = level
        while span < w:
            level = lvl_ref[buf, 0:n, :] + lvl_ref[buf, span:span + n, :]
            span, buf = 2 * span, 1 - buf
            lvl_ref[buf, 0:n, :] = level
        wsum = lvl_ref[buf, POOL_HALO - half:POOL_HALO - half + tm, :]
        cnt = (jnp.minimum(t + half, seq_len) - jnp.maximum(t - half, 0)).astype(F32)
        p = wsum * (1.0 / cnt) - ue_ref[POOL_HALO:POOL_HALO + tm, cols]
        pool_ref[:, cols] = p.astype(BF16)


def _stage_c_kernel(x_ref, mod_ref, n1g_ref, n2g_ref, u_ref, up_ref, un_ref, attn_ref,
                    wgate_ref, wmix_ref, ps_ref, wpp_ref, wap_ref, wo_ref, wr_ref, br_ref, utri_ref,
                    x1_ref, h2_ref, meta_ref, rw_ref, count_ref, ue_ref, lvl_ref, pool_ref, run_ref, *, seq_len):
    i = pl.program_id(1)
    tm = x_ref.shape[0]

    @pl.when((pl.program_id(0) == 0) & (i == 0))
    def _():
        run_ref[...] = jnp.zeros_like(run_ref)

    gain1 = n1g_ref[...] * (1.0 + mod_ref[1:2, :])
    shift1 = mod_ref[0:1, :]
    gain2 = n2g_ref[...] * (1.0 + mod_ref[4:5, :])
    shift2 = mod_ref[3:4, :]
    res_gate = mod_ref[2:3, :]

    ue_ref[0:POOL_HALO, :] = up_ref[...].astype(F32) * (i > 0).astype(F32)
    ue_ref[POOL_HALO:POOL_HALO + tm, :] = u_ref[...].astype(F32)
    ue_ref[POOL_HALO + tm:2 * POOL_HALO + tm, :] = un_ref[...].astype(F32) * (i < pl.num_programs(1) - 1).astype(F32)
    ue_ref[2 * POOL_HALO + tm:, :] = jnp.zeros((POOL_HALO, D_POOL), F32)
    lvl_ref[:, 2 * POOL_HALO + tm:, :] = jnp.zeros((2, POOL_HALO, POOL_GC), F32)
    _pool_deviation(ue_ref, lvl_ref, pool_ref, i * tm, tm, seq_len)

    for r0 in range(0, tm, MIXER_ROWS):
        rows = slice(r0, r0 + min(MIXER_ROWS, tm))
        x = x_ref[rows, :]
        h = _scaled_norm(x, gain1, shift1)
        gates = _sigmoid(jnp.dot(h.astype(BF16), wgate_ref[...], preferred_element_type=F32))
        mixed_groups = []
        for gi in range(len(POOL_WINDOWS)):
            cols = slice(gi * POOL_GC, (gi + 1) * POOL_GC)
            pm = jnp.dot(pool_ref[rows, cols], wmix_ref[gi], preferred_element_type=F32)
            mixed_groups.append(pm * ps_ref[:, cols])
        pool = jnp.concatenate(mixed_groups, axis=1)
        pool_out = jnp.dot(pool.astype(BF16), wpp_ref[...], preferred_element_type=F32)
        attn_out = jnp.dot(attn_ref[rows, :], wap_ref[...], preferred_element_type=F32)
        merged = gates[:, :D_MODEL] * pool_out + gates[:, D_MODEL:] * attn_out
        mixed = jnp.dot(merged.astype(BF16), wo_ref[...], preferred_element_type=F32)
        x1 = x + res_gate * mixed
        x1_ref[rows, :] = x1

        h2 = _scaled_norm(x1, gain2, shift2)
        h2_ref[rows, :] = _pack_bf16_pair(h2[:, :D_PACK], h2[:, D_PACK:])
        h_hi = h2.astype(BF16)
        h_lo = (h2 - h_hi.astype(F32)).astype(BF16)
        parts = (jnp.dot(h_hi, wr_ref[...], preferred_element_type=F32)
                 + jnp.dot(h_lo, wr_ref[...], preferred_element_type=F32))
        logits = parts[:, :LANES] + parts[:, LANES:] + br_ref[...]
        e1, e2, w1, w2 = _route(logits.T[0:ROUTE_ROWS, :])
        n = logits.shape[0]

        eidx = lax.broadcasted_iota(I32, (N_EXPERTS, n), 0).astype(F32)
        hit1 = eidx == e1
        hit2 = eidx == e2
        taken = jnp.where(hit1 | hit2, 1.0, 0.0)
        before = run_ref[...] + jnp.dot(taken.astype(BF16), utri_ref[...], preferred_element_type=F32)
        rank1 = jnp.sum(jnp.where(hit1, before, 0.0), axis=0, keepdims=True)
        rank2 = jnp.sum(jnp.where(hit2, before, 0.0), axis=0, keepdims=True)
        run_ref[...] = run_ref[...] + jnp.sum(taken, axis=1, keepdims=True)
        ridx = lax.broadcasted_iota(I32, (META_ROWS, n), 0)
        table = jnp.where(ridx == 0, e1, jnp.where(ridx == 1, e2, jnp.where(ridx == 2, rank1,
                          jnp.where(ridx == 3, rank2, 0.0))))
        meta_ref[:, rows] = table.astype(I32)
        weights_t = jnp.where(ridx == 0, w1, jnp.where(ridx == 1, w2, 0.0))
        rw_ref[rows, :] = jnp.concatenate([weights_t, jnp.zeros((LANES - META_ROWS, n), F32)], axis=0).T
    count_ref[...] = jnp.broadcast_to(run_ref[...], count_ref.shape)


def _stage_c(x, mod, n1g, n2g, u, attn, w_gate, w_mix, pool_scale, w_pp, w_ap, w_o, w_r, b_r, tm):
    B, T, _ = x.shape
    hb = tm // POOL_HALO
    n_halo_blocks = T // POOL_HALO
    tok = lambda b, i: (b, i, 0)
    flat = lambda b, i: (b * (T // tm) + i, 0)
    group_rows = min(MIXER_ROWS, tm)
    utri = jnp.triu(jnp.ones((group_rows, group_rows), BF16), k=1)
    return pl.pallas_call(
        functools.partial(_stage_c_kernel, seq_len=T),
        grid=(B, T // tm),
        in_specs=[
            pl.BlockSpec((None, tm, D_MODEL), tok),
            pl.BlockSpec((None, 6, D_MODEL), lambda b, i: (b, 0, 0)),
            _const_spec((1, D_MODEL)),
            _const_spec((1, D_MODEL)),
            pl.BlockSpec((None, tm, D_POOL), tok),
            pl.BlockSpec((None, POOL_HALO, D_POOL), lambda b, i: (b, jnp.maximum(i * hb - 1, 0), 0)),
            pl.BlockSpec((None, POOL_HALO, D_POOL),
                         lambda b, i: (b, jnp.minimum((i + 1) * hb, n_halo_blocks - 1), 0)),
            pl.BlockSpec((None, tm, D_ATTN), tok),
            _const_spec((D_MODEL, 2 * D_MODEL)),
            _const_spec((len(POOL_WINDOWS), POOL_GC, POOL_GC)),
            _const_spec((1, D_POOL)),
            _const_spec((D_POOL, D_MODEL)),
            _const_spec((D_ATTN, D_MODEL)),
            _const_spec((D_MODEL, D_MODEL)),
            _const_spec((D_MODEL, 2 * LANES)),
            _const_spec((1, LANES)),
            _const_spec((group_rows, group_rows)),
        ],
        out_specs=[
            pl.BlockSpec((None, tm, D_MODEL), tok),
            pl.BlockSpec((tm, D_PACK), flat),
            pl.BlockSpec((META_ROWS, tm), lambda b, i: (0, b * (T // tm) + i)),
            pl.BlockSpec((tm, LANES), flat),
            pl.BlockSpec((N_EXPERTS, LANES), lambda b, i: (0, 0)),
        ],
        out_shape=[
            jax.ShapeDtypeStruct((B, T, D_MODEL), F32),
            jax.ShapeDtypeStruct((B * T, D_PACK), I32),
            jax.ShapeDtypeStruct((META_ROWS, B * T), I32),
            jax.ShapeDtypeStruct((B * T, LANES), F32),
            jax.ShapeDtypeStruct((N_EXPERTS, LANES), F32),
        ],
        scratch_shapes=[pltpu.VMEM((tm + 3 * POOL_HALO, D_POOL), F32),
                        pltpu.VMEM((2, tm + 3 * POOL_HALO, POOL_GC), F32),
                        pltpu.VMEM((tm, D_POOL), BF16),
                        pltpu.VMEM((N_EXPERTS, 1), F32)],
        compiler_params=pltpu.CompilerParams(
            dimension_semantics=("arbitrary", "arbitrary"), vmem_limit_bytes=V7X_VMEM_LIMIT),
        name="stage_c",
    )(x, mod, n1g, n2g, u, u, u, attn, w_gate, w_mix, pool_scale, w_pp, w_ap, w_o, w_r, b_r, utri)


def _sc_workers():
    info = plsc.get_sparse_core_info()
    return info.num_cores, info.num_subcores


def _sc_scatter_rows(rows, idx0, idx1, n_out):
    n, d = rows.shape
    nc, ns = _sc_workers()
    per_worker = n // (nc * ns)
    n_win = per_worker // SC_WINDOW
    mesh = plsc.VectorSubcoreMesh(core_axis_name="c", subcore_axis_name="s")

    @functools.partial(
        pl.kernel, mesh=mesh,
        out_type=jax.ShapeDtypeStruct((n_out, d), rows.dtype),
        scratch_types=[pltpu.VMEM((SC_WINDOW,), I32), pltpu.VMEM((SC_WINDOW,), I32),
                       pltpu.VMEM((SC_WINDOW, d), rows.dtype)],
        name="sc_scatter_rows",
    )
    def scatter(rows_hbm, idx0_hbm, idx1_hbm, out_hbm, i0_v, i1_v, rows_v):
        wid = lax.axis_index("s") * nc + lax.axis_index("c")

        @pl.loop(0, n_win)
        def _(w):
            base = wid * per_worker + w * SC_WINDOW
            pltpu.sync_copy(rows_hbm.at[pl.ds(base, SC_WINDOW)], rows_v)
            pltpu.sync_copy(idx0_hbm.at[pl.ds(base, SC_WINDOW)], i0_v)
            pltpu.sync_copy(idx1_hbm.at[pl.ds(base, SC_WINDOW)], i1_v)
            pltpu.sync_copy(rows_v, out_hbm.at[i0_v])
            pltpu.sync_copy(rows_v, out_hbm.at[i1_v])

    return scatter(rows, idx0, idx1)


def _sc_gather_rows(table, idx):
    n = idx.shape[0]
    d = table.shape[1]
    nc, ns = _sc_workers()
    per_worker = n // (nc * ns)
    n_win = per_worker // SC_WINDOW
    mesh = plsc.VectorSubcoreMesh(core_axis_name="c", subcore_axis_name="s")

    @functools.partial(
        pl.kernel, mesh=mesh,
        out_type=jax.ShapeDtypeStruct((n, d), table.dtype),
        scratch_types=[pltpu.VMEM((SC_WINDOW,), I32), pltpu.VMEM((SC_WINDOW, d), table.dtype)],
        name="sc_gather_rows",
    )
    def gather(table_hbm, idx_hbm, out_hbm, i_v, rows_v):
        wid = lax.axis_index("s") * nc + lax.axis_index("c")

        @pl.loop(0, n_win)
        def _(w):
            base = wid * per_worker + w * SC_WINDOW
            pltpu.sync_copy(idx_hbm.at[pl.ds(base, SC_WINDOW)], i_v)
            pltpu.sync_copy(table_hbm.at[i_v], rows_v)
            pltpu.sync_copy(rows_v, out_hbm.at[pl.ds(base, SC_WINDOW)])

    return gather(table, idx)


def _moe_kernel(te_ref, tv_ref, nl_ref, x_hbm, wgu_ref, wd_ref, o_ref, xbuf, sem):
    del te_ref
    i = pl.program_id(0)
    n_live_tiles = nl_ref[0]
    valid = tv_ref[i]
    tm = o_ref.shape[0]
    slot = _input_ring(x_hbm, xbuf, sem, i, n_live_tiles, tm)

    group = min(STAGE_ROWS, tm)
    n_groups = tm // group

    def expert_rows(r0):
        rows = slice(r0, r0 + group)
        live = lax.broadcasted_iota(I32, (group, 1), 0) < valid - r0
        lo, hi = _unpack_bf16_pair(jnp.where(live, xbuf[slot, rows, :], 0))
        x = jnp.concatenate([lo, hi], axis=1).astype(BF16)
        gu = jnp.dot(x, wgu_ref[...], preferred_element_type=F32)
        a = gu[:, :D_EXPERT]
        hmid = (a * _sigmoid(a) * gu[:, D_EXPERT:]).astype(BF16)
        y = jnp.dot(hmid, wd_ref[...], preferred_element_type=F32)
        o_ref[rows, :] = _pack_bf16_pair(y[:, :D_PACK], y[:, D_PACK:])

    for n_live in range(n_groups + 1):
        @pl.when((valid > (n_live - 1) * group) & (valid <= n_live * group))
        def _():
            for g in range(n_live):
                expert_rows(g * group)
            if n_live < n_groups:
                o_ref[n_live * group:, :] = jnp.zeros((tm - n_live * group, D_PACK), I32)


def _moe(xs, tile_expert, tile_valid, n_live_tiles, w_gu, w_d, tm):
    n_tiles = xs.shape[0] // tm
    grid_spec = pltpu.PrefetchScalarGridSpec(
        num_scalar_prefetch=3,
        grid=(n_tiles,),
        in_specs=[
            pl.BlockSpec(memory_space=pl.ANY),
            pl.BlockSpec((None, D_MODEL, 2 * D_EXPERT), lambda i, te, tv, nl: (te[i], 0, 0)),
            pl.BlockSpec((None, D_EXPERT, D_MODEL), lambda i, te, tv, nl: (te[i], 0, 0)),
        ],
        out_specs=pl.BlockSpec((tm, D_PACK), lambda i, te, tv, nl: (jnp.minimum(i, nl[0]), 0)),
        scratch_shapes=[pltpu.VMEM((RING_SLOTS, tm, D_PACK), I32), pltpu.SemaphoreType.DMA((RING_SLOTS,))],
    )
    return pl.pallas_call(
        _moe_kernel,
        grid_spec=grid_spec,
        out_shape=jax.ShapeDtypeStruct(xs.shape, I32),
        compiler_params=pltpu.CompilerParams(
            dimension_semantics=("arbitrary",), vmem_limit_bytes=V7X_VMEM_LIMIT),
        name="experts",
    )(tile_expert, tile_valid, n_live_tiles, xs, w_gu, w_d)


def _final_kernel(x1_ref, y0_ref, y1_ref, rw_ref, mod_ref, fg_ref, o_ref):
    y0lo, y0hi = _unpack_bf16_pair(y0_ref[...])
    y1lo, y1hi = _unpack_bf16_pair(y1_ref[...])
    w0 = rw_ref[:, 0:1]
    w1 = rw_ref[:, 1:2]
    moe = jnp.concatenate([w0 * y0lo + w1 * y1lo, w0 * y0hi + w1 * y1hi], axis=1)
    x2 = x1_ref[...] + mod_ref[5:6, :] * moe
    ms = jnp.mean(x2 * x2, axis=-1, keepdims=True)
    o_ref[...] = x2 * lax.rsqrt(ms + EPS) * fg_ref[...]


def _final(x1, yb, rw, mod, final_g, tm):
    B, T, _ = x1.shape
    n_blocks = B * T // tm
    flat = lambda b, i: (b * (T // tm) + i, 0)
    return pl.pallas_call(
        _final_kernel,
        grid=(B, T // tm),
        in_specs=[
            pl.BlockSpec((None, tm, D_MODEL), lambda b, i: (b, i, 0)),
            pl.BlockSpec((tm, D_PACK), flat),
            pl.BlockSpec((tm, D_PACK), lambda b, i: (n_blocks + b * (T // tm) + i, 0)),
            pl.BlockSpec((tm, LANES), flat),
            pl.BlockSpec((None, 6, D_MODEL), lambda b, i: (b, 0, 0)),
            _const_spec((1, D_MODEL)),
        ],
        out_specs=pl.BlockSpec((None, tm, D_MODEL), lambda b, i: (b, i, 0)),
        out_shape=jax.ShapeDtypeStruct((B, T, D_MODEL), F32),
        compiler_params=pltpu.CompilerParams(
            dimension_semantics=("parallel", "parallel"), vmem_limit_bytes=V7X_VMEM_LIMIT),
        name="final_combine",
    )(x1, yb, yb, rw, mod, final_g)


def _rope_tables(T):
    rows = T // GRID_W
    row = np.repeat(np.arange(rows, dtype=np.float32), GRID_W)
    col = np.tile(np.arange(GRID_W, dtype=np.float32), rows)
    inv_freq = np.float32(ROPE_THETA) ** (-np.arange(0, 64, 2, dtype=np.float32) / np.float32(64))
    ang_r = row[:, None] * inv_freq[None, :]
    ang_c = col[:, None] * inv_freq[None, :]
    ang = np.concatenate([ang_r, ang_r, ang_c, ang_c], axis=-1).astype(np.float32)
    sign = np.where((np.arange(HEAD_DIM) % 64) < 32, -1.0, 1.0).astype(np.float32)
    return jnp.asarray(np.cos(ang)), jnp.asarray(np.sin(ang) * sign[None, :])


def _dispatch_plan(counts, tm, n_tiles):
    tiles_per_e = (counts + tm - 1) // tm
    tile_end = jnp.cumsum(tiles_per_e)
    tile_start = tile_end - tiles_per_e
    tile_id = jnp.arange(n_tiles, dtype=I32)
    te = jnp.minimum(jnp.sum((tile_id[:, None] >= tile_end[None, :]).astype(I32), axis=1), N_EXPERTS - 1)
    live = jnp.clip(counts[te] - (tile_id - tile_start[te]) * tm, 0, tm)
    tv = jnp.where(tile_id < tile_end[-1], live, 0).astype(I32)
    return tile_start.astype(I32), te, tv, tile_end[-1:].astype(I32)


def _slot_kernel(start_ref, meta_ref, pos_ref, *, tile_rows):
    e = meta_ref[0:2, :]
    first_tile = jnp.zeros_like(e)
    for ex in range(N_EXPERTS):
        first_tile = jnp.where(e == ex, start_ref[ex], first_tile)
    pos_ref[...] = first_tile * tile_rows + meta_ref[2:4, :]


def _slots(meta, tile_start, tile_rows):
    n = meta.shape[1]
    tn = min(n, 8192)
    grid_spec = pltpu.PrefetchScalarGridSpec(
        num_scalar_prefetch=1,
        grid=(n // tn,),
        in_specs=[pl.BlockSpec((META_ROWS, tn), lambda i, start: (0, i))],
        out_specs=pl.BlockSpec((2, tn), lambda i, start: (0, i)),
    )
    return pl.pallas_call(
        functools.partial(_slot_kernel, tile_rows=tile_rows),
        grid_spec=grid_spec,
        out_shape=jax.ShapeDtypeStruct((2, n), I32),
        name="dispatch_slots",
    )(tile_start, meta)


def _pick_tile(T, want):
    t = min(T, want)
    assert T % t == 0
    return t


def _trunk(x, c, p):
    B, T, _ = x.shape
    N = B * T
    tm = _pick_tile(T, 512)
    tq = _pick_tile(T, 512)
    tk = _pick_tile(T, 256)
    tme = _pick_tile(N, 512)
    cos, sin_signed = _rope_tables(T)
    mod = _modulation(c, p["w_ada"], p["b_ada"]).reshape(B, 6, D_MODEL)
    u, qt, qb, k, vt = _stage_a(x, mod, p["n1g"], p["w_qkvu"], p["qg"], p["kg"], cos, sin_signed, tm)
    attn = lax.cond(
        p["bounded_softmax_ok"],
        functools.partial(_attention, tq=tq, tk=_pick_tile(T, 2048), running_max=False),
        functools.partial(_attention, tq=tq, tk=tk, running_max=True),
        qt, qb, k, vt)
    x1, h2p, meta, rw, counts = _stage_c(x, mod, p["n1g"], p["n2g"], u, attn, p["w_gate"], p["w_mix"],
                                         p["pool_scale"], p["w_pp"], p["w_ap"], p["w_o"], p["w_r"], p["b_r"], tm)
    n_tiles = 2 * N // tme + N_EXPERTS
    tile_start, te, tv, n_live_tiles = _dispatch_plan(counts[:, 0].astype(I32), tme, n_tiles)
    pos = _slots(meta, tile_start, tme)
    xs = _sc_scatter_rows(h2p, pos[0], pos[1], n_tiles * tme)
    ys = _moe(xs, te, tv, n_live_tiles, p["w_egu"], p["w_ed"], tme)
    yb = _sc_gather_rows(ys, pos.reshape(-1))
    return _final(x1, yb, rw, mod, p["final_g"], _pick_tile(T, 1024))


def kernel(x_prompt, x_sample, c_prompt, c_sample, w_ada, b_ada, norm1_g, norm2_g, w_in, q_norm_g,
           k_norm_g, w_pool_mix, pool_scale, w_pool_proj, w_attn_proj, w_o, w_router_group,
           b_router_group, w_router_expert, b_router_expert, w_exp_gate, w_exp_up, w_exp_down, final_g):
    assert w_ada.shape[0] == 1, "single-layer block"
    n_r = N_GROUPS + N_EXPERTS
    w_r = jnp.concatenate([w_router_expert[0], w_router_group[0],
                           jnp.zeros((D_MODEL, LANES - n_r), F32)], axis=1)
    w_r_hi = w_r.astype(BF16)
    b_r = jnp.concatenate([b_router_expert[0], b_router_group[0], jnp.zeros((LANES - n_r,), F32)])
    score_bound = (1.01 * LOG2E * np.sqrt(HEAD_DIM)) * jnp.max(jnp.abs(q_norm_g[0])) * jnp.max(jnp.abs(k_norm_g[0]))
    p = dict(
        bounded_softmax_ok=2.0 * score_bound <= EXP2_SAFE_SPAN,
        w_ada=w_ada[0], b_ada=b_ada[0],
        n1g=norm1_g[0].reshape(1, D_MODEL), n2g=norm2_g[0].reshape(1, D_MODEL),
        w_qkvu=w_in[0][:, :D_QKVU].astype(BF16), w_gate=w_in[0][:, D_QKVU:].astype(BF16),
        qg=q_norm_g[0].reshape(1, HEAD_DIM), kg=k_norm_g[0].reshape(1, HEAD_DIM),
        w_mix=w_pool_mix[0].astype(BF16), pool_scale=pool_scale[0].reshape(1, D_POOL),
        w_pp=w_pool_proj[0].astype(BF16), w_ap=w_attn_proj[0].astype(BF16), w_o=w_o[0].astype(BF16),
        w_r=jnp.concatenate([w_r_hi, (w_r - w_r_hi.astype(F32)).astype(BF16)], axis=1), b_r=b_r.reshape(1, LANES),
        w_egu=jnp.concatenate([w_exp_gate[0], w_exp_up[0]], axis=-1).astype(BF16), w_ed=w_exp_down[0].astype(BF16),
        final_g=final_g.reshape(1, D_MODEL),
    )
    return _trunk(x_prompt, c_prompt, p), _trunk(x_sample, c_sample, p)
```

```python
import functools

import numpy as np
import jax
import jax.numpy as jnp
from jax import lax
from jax.experimental import pallas as pl
from jax.experimental.pallas import tpu as pltpu
from jax.experimental.pallas import tpu_sc as plsc

F32 = jnp.float32
BF16 = jnp.bfloat16
I32 = jnp.int32

D_MODEL = 1024
GRID_W = 64
POOL_WINDOWS = (2, 4, 8, 16)
POOL_GC = 128
D_POOL = 512
HEAD_DIM = 128
N_Q_HEADS = 4
N_KV_HEADS = 2
D_ATTN = 512
D_KV = 256
ROPE_THETA = 10000.0
N_GROUPS = 4
EXPERTS_PER_GROUP = 8
N_EXPERTS = 32
D_EXPERT = 256
EPS = 1e-6
LOG2E = 1.4426950408889634
EXP2_SAFE_SPAN = 100.0
D_QKVU = D_POOL + D_ATTN + 2 * D_KV
POOL_HALO = 16
STAGE_ROWS = 256
MERGE_COLS = 256
RING_SLOTS = 3
ATTN_TILES_PER_TRIP = 10
ROUTE_ROWS = 40
META_ROWS = 8
LANES = 128
D_PACK = D_MODEL // 2

V7X_VMEM_LIMIT = 56 * 1024 * 1024
SC_WINDOW = 64


def _const_spec(shape):
    nd = len(shape)
    return pl.BlockSpec(shape, lambda *_: (0,) * nd, pipeline_mode=pl.Buffered(1))


def _sigmoid(x):
    return 0.5 * jnp.tanh(0.5 * x) + 0.5


def _pack_bf16_pair(lo, hi):
    lo_bits = lax.bitcast_convert_type(lo.astype(BF16).astype(F32), jnp.uint32) >> 16
    hi_bits = lax.bitcast_convert_type(hi.astype(BF16).astype(F32), jnp.uint32) & jnp.uint32(0xFFFF0000)
    return lax.bitcast_convert_type(lo_bits | hi_bits, I32)


def _unpack_bf16_pair(packed):
    u = lax.bitcast_convert_type(packed, jnp.uint32)
    lo = lax.bitcast_convert_type(u << 16, F32)
    hi = lax.bitcast_convert_type(u & jnp.uint32(0xFFFF0000), F32)
    return lo, hi


def _input_ring(x_hbm, buf, sem, step, n_steps, rows):
    def fetch(s):
        slot = lax.rem(s, RING_SLOTS)
        src = x_hbm.at[pl.ds(pl.multiple_of(s * rows, rows), rows), :]
        return pltpu.make_async_copy(src, buf.at[slot], sem.at[slot])

    @pl.when((step == 0) & (n_steps > 0))
    def _():
        fetch(0).start()

    @pl.when((step == 0) & (n_steps > 1))
    def _():
        fetch(1).start()

    @pl.when(step + 2 < n_steps)
    def _():
        fetch(step + 2).start()

    @pl.when(step < n_steps)
    def _():
        fetch(step).wait()

    return lax.rem(step, RING_SLOTS)


def _mod_kernel(c_ref, w_ref, b_ref, o_ref):
    c = c_ref[...]
    s = c * _sigmoid(c)
    o_ref[...] = jnp.dot(s, w_ref[...], preferred_element_type=F32,
                         precision=lax.Precision.HIGHEST) + b_ref[...]


def _modulation(c, w_ada, b_ada):
    B = c.shape[0]
    n_out = w_ada.shape[1]
    bn = D_MODEL
    return pl.pallas_call(
        _mod_kernel,
        grid=(n_out // bn,),
        in_specs=[pl.BlockSpec((B, D_MODEL), lambda j: (0, 0)),
                  pl.BlockSpec((D_MODEL, bn), lambda j: (0, j)),
                  pl.BlockSpec((1, bn), lambda j: (0, j))],
        out_specs=pl.BlockSpec((B, bn), lambda j: (0, j)),
        out_shape=jax.ShapeDtypeStruct((B, n_out), F32),
        name="modulation",
    )(c, w_ada, b_ada.reshape(1, n_out))


def _scaled_norm(x, gain, shift):
    ms = jnp.mean(x * x, axis=-1, keepdims=True)
    return x * lax.rsqrt(ms + EPS) * gain + shift


def _stage_a_kernel(x_hbm, mod_ref, n1g_ref, w_ref, qg_ref, kg_ref, cos_ref, sin_ref,
                    u_ref, qt_ref, qb_ref, k_ref, vt_ref, xbuf, sem):
    tm = u_ref.shape[0]
    n_steps = pl.num_programs(0) * pl.num_programs(1)
    slot = _input_ring(x_hbm, xbuf, sem, pl.program_id(0) * pl.num_programs(1) + pl.program_id(1), n_steps, tm)
    gain1 = n1g_ref[...] * (1.0 + mod_ref[1:2, :])
    shift1 = mod_ref[0:1, :]
    lane = lax.broadcasted_iota(I32, (1, HEAD_DIM), 1)
    first_half = (lane % 64) < 32
    qg = qg_ref[...] * (LOG2E / np.sqrt(HEAD_DIM))
    k_norm_max = jnp.max(jnp.abs(kg_ref[...]), axis=-1, keepdims=True) * (np.sqrt(HEAD_DIM) * 1.01)

    for r0 in range(0, tm, STAGE_ROWS):
        rows = slice(r0, r0 + min(STAGE_ROWS, tm))
        h = _scaled_norm(xbuf[slot, rows, :], gain1, shift1)
        z = jnp.dot(h.astype(BF16), w_ref[...], preferred_element_type=F32)
        u_ref[rows, :] = z[:, :D_POOL].astype(BF16)
        cos = cos_ref[rows, :]
        sin_signed = sin_ref[rows, :]

        def norm_rope(xh, g):
            ms = jnp.mean(xh * xh, axis=-1, keepdims=True)
            xn = xh * lax.rsqrt(ms + EPS) * g
            rot = jnp.where(first_half, pltpu.roll(xn, 96, 1), pltpu.roll(xn, 32, 1))
            return xn * cos + rot * sin_signed

        for hq in range(N_Q_HEADS):
            o = D_POOL + hq * HEAD_DIM
            q_t = norm_rope(z[:, o:o + HEAD_DIM], qg).T.astype(BF16)
            qt_ref[hq, :, rows] = q_t
            q_f = q_t.astype(F32)
            qb_ref[hq, :, rows] = jnp.sqrt(jnp.sum(q_f * q_f, axis=0, keepdims=True)) * k_norm_max
        for hk in range(N_KV_HEADS):
            o = D_POOL + D_ATTN + hk * HEAD_DIM
            k_ref[rows, hk * HEAD_DIM:(hk + 1) * HEAD_DIM] = norm_rope(z[:, o:o + HEAD_DIM], kg_ref[...]).astype(BF16)
            o = D_POOL + D_ATTN + D_KV + hk * HEAD_DIM
            vt_ref[hk, :, rows] = z[:, o:o + HEAD_DIM].T.astype(BF16)


def _stage_a(x, mod, n1g, w_qkvu, qg, kg, cos, sin_signed, tm):
    B, T, _ = x.shape
    return pl.pallas_call(
        _stage_a_kernel,
        grid=(B, T // tm),
        in_specs=[
            pl.BlockSpec(memory_space=pl.ANY),
            pl.BlockSpec((None, 6, D_MODEL), lambda b, i: (b, 0, 0)),
            _const_spec((1, D_MODEL)),
            _const_spec((D_MODEL, D_QKVU)),
            _const_spec((1, HEAD_DIM)),
            _const_spec((1, HEAD_DIM)),
            pl.BlockSpec((tm, HEAD_DIM), lambda b, i: (i, 0)),
            pl.BlockSpec((tm, HEAD_DIM), lambda b, i: (i, 0)),
        ],
        out_specs=[
            pl.BlockSpec((None, tm, D_POOL), lambda b, i: (b, i, 0)),
            pl.BlockSpec((None, N_Q_HEADS, HEAD_DIM, tm), lambda b, i: (b, 0, 0, i)),
            pl.BlockSpec((None, N_Q_HEADS, 1, tm), lambda b, i: (b, 0, 0, i)),
            pl.BlockSpec((None, tm, D_KV), lambda b, i: (b, i, 0)),
            pl.BlockSpec((None, N_KV_HEADS, HEAD_DIM, tm), lambda b, i: (b, 0, 0, i)),
        ],
        out_shape=[
            jax.ShapeDtypeStruct((B, T, D_POOL), BF16),
            jax.ShapeDtypeStruct((B, N_Q_HEADS, HEAD_DIM, T), BF16),
            jax.ShapeDtypeStruct((B, N_Q_HEADS, 1, T), F32),
            jax.ShapeDtypeStruct((B, T, D_KV), BF16),
            jax.ShapeDtypeStruct((B, N_KV_HEADS, HEAD_DIM, T), BF16),
        ],
        scratch_shapes=[pltpu.VMEM((RING_SLOTS, tm, D_MODEL), F32), pltpu.SemaphoreType.DMA((RING_SLOTS,))],
        compiler_params=pltpu.CompilerParams(
            dimension_semantics=("arbitrary", "arbitrary"), vmem_limit_bytes=V7X_VMEM_LIMIT),
        name="stage_a",
    )(x.reshape(B * T, D_MODEL), mod, n1g, w_qkvu, qg, kg, cos, sin_signed)


def _write_attn_output(o_ref, acc, l, tq):
    o = acc * (1.0 / l)
    o_ref[:, :HEAD_DIM] = o[:, :tq].T.astype(BF16)
    o_ref[:, HEAD_DIM:] = o[:, tq:].T.astype(BF16)


def _attn_bounded_kernel(qt_ref, qb_ref, k_ref, vt_ref, o_ref, p_ref, acc_ref, l_ref, *, tk):
    T = k_ref.shape[0]
    tq = qt_ref.shape[2]
    nq = 2 * tq
    nk = T // tk
    qt = jnp.concatenate([qt_ref[0], qt_ref[1]], axis=1)
    qb = jnp.concatenate([qb_ref[0], qb_ref[1]], axis=1)

    def weights(j, slot):
        off = pl.multiple_of(j * tk, tk)
        st = jnp.dot(k_ref[pl.ds(off, tk), :], qt, preferred_element_type=F32)
        p = jnp.exp2(st - qb)
        l_ref[...] += jnp.sum(p.reshape(tk // 8, 8, nq), axis=0)
        p_ref[slot] = p.astype(BF16)

    def values(j, slot):
        off = pl.multiple_of(j * tk, tk)
        acc_ref[...] += jnp.dot(vt_ref[:, pl.ds(off, tk)], p_ref[slot], preferred_element_type=F32)

    def step(j, slot):
        values(j - 1, 1 - slot)
        weights(j, slot)

    acc_ref[...] = jnp.zeros_like(acc_ref)
    l_ref[...] = jnp.zeros_like(l_ref)
    weights(0, 0)

    unroll = max(1, min(ATTN_TILES_PER_TRIP, nk - 1))
    trips = (nk - 1) // unroll if unroll % 2 == 0 else 0

    if trips > 1:
        @pl.loop(0, trips)
        def _(g):
            for d in range(unroll):
                step(1 + unroll * g + d, (1 + d) % 2)
    else:
        trips = 0
    for j in range(1 + trips * unroll, nk):
        step(j, j % 2)
    values(nk - 1, (nk - 1) % 2)
    _write_attn_output(o_ref, acc_ref[...], jnp.sum(l_ref[...], axis=0, keepdims=True), tq)


def _attn_online_kernel(qt_ref, qb_ref, k_ref, vt_ref, o_ref, *, tk):
    del qb_ref
    T = k_ref.shape[0]
    tq = qt_ref.shape[2]
    nq = 2 * tq
    qt = jnp.concatenate([qt_ref[0], qt_ref[1]], axis=1)

    def body(s, carry):
        m, l, acc = carry
        off = pl.multiple_of(s * tk, tk)
        st = jnp.dot(k_ref[pl.ds(off, tk), :], qt, preferred_element_type=F32)
        m_new = jnp.maximum(m, jnp.max(st, axis=0, keepdims=True))
        alpha = jnp.exp2(m - m_new)
        p = jnp.exp2(st - m_new)
        l = alpha * l + jnp.sum(p, axis=0, keepdims=True)
        pv = jnp.dot(vt_ref[:, pl.ds(off, tk)], p.astype(BF16), preferred_element_type=F32)
        return m_new, l, alpha * acc + pv

    init = (jnp.full((1, nq), -jnp.inf, F32), jnp.zeros((1, nq), F32), jnp.zeros((HEAD_DIM, nq), F32))
    _, l, acc = lax.fori_loop(0, T // tk, body, init)
    _write_attn_output(o_ref, acc, l, tq)


def _attention(qt, qb, k, vt, tq, tk, running_max):
    B, _, _, T = qt.shape
    group = N_Q_HEADS // N_KV_HEADS
    nq = group * tq
    if running_max:
        body, scratch, name = functools.partial(_attn_online_kernel, tk=tk), [], "attention_online_max"
    else:
        body = functools.partial(_attn_bounded_kernel, tk=tk)
        scratch = [pltpu.VMEM((2, tk, nq), BF16), pltpu.VMEM((HEAD_DIM, nq), F32), pltpu.VMEM((8, nq), F32)]
        name = "attention_bounded"
    return pl.pallas_call(
        body,
        grid=(B, N_KV_HEADS, T // tq),
        in_specs=[
            pl.BlockSpec((None, group, HEAD_DIM, tq), lambda b, j, i: (b, j, 0, i)),
            pl.BlockSpec((None, group, 1, tq), lambda b, j, i: (b, j, 0, i)),
            pl.BlockSpec((None, T, HEAD_DIM), lambda b, j, i: (b, 0, j)),
            pl.BlockSpec((None, None, HEAD_DIM, T), lambda b, j, i: (b, j, 0, 0)),
        ],
        out_specs=pl.BlockSpec((None, tq, group * HEAD_DIM), lambda b, j, i: (b, i, j)),
        out_shape=jax.ShapeDtypeStruct((B, T, D_ATTN), BF16),
        scratch_shapes=scratch,
        compiler_params=pltpu.CompilerParams(
            dimension_semantics=("parallel", "parallel", "parallel"),
            vmem_limit_bytes=V7X_VMEM_LIMIT),
        name=name,
    )(qt, qb, k, vt)


def _route(logits_t):
    neg = jnp.float32(-jnp.inf)
    n = logits_t.shape[1]

    def top(vals, idx, far):
        v = jnp.max(vals, axis=0, keepdims=True)
        i = jnp.min(jnp.where(vals == v, idx, far), axis=0, keepdims=True)
        return v, i

    gidx = lax.broadcasted_iota(I32, (ROUTE_ROWS - N_EXPERTS, n), 0).astype(F32)
    grp = jnp.where(gidx < N_GROUPS, logits_t[N_EXPERTS:, :], neg)
    gmax, gsel = top(grp, gidx, jnp.float32(ROUTE_ROWS))
    p_group = 1.0 / jnp.sum(jnp.exp(grp - gmax), axis=0, keepdims=True)
    eidx = lax.broadcasted_iota(I32, (N_EXPERTS, n), 0).astype(F32)
    first = EXPERTS_PER_GROUP * gsel
    le = jnp.where((eidx >= first) & (eidx < first + EXPERTS_PER_GROUP), logits_t[:N_EXPERTS, :], neg)
    v1, e1 = top(le, eidx, jnp.float32(N_EXPERTS))
    v2, e2 = top(jnp.where(eidx == e1, neg, le), eidx, jnp.float32(N_EXPERTS))
    e21 = jnp.exp(v2 - v1)
    w1 = p_group / (1.0 + e21)
    return e1, e2, w1, w1 * e21


def _pool_deviation(ue_ref, lvl_ref, pool_ref, tile_start, tm, seq_len):
    n = tm + 2 * POOL_HALO
    t = tile_start + lax.broadcasted_iota(I32, (tm, 1), 0)
    for gi, w in enumerate(POOL_WINDOWS):
        half = w // 2
        cols = slice(gi * POOL_GC, (gi + 1) * POOL_GC)
        level = ue_ref[0:n, cols] + ue_ref[1:n + 1, cols]
        span, buf = 2, 0
        lvl_ref[buf, 0:n, :] = level
        while span < w:
            level = lvl_ref[buf, 0:n, :] + lvl_ref[buf, span:span + n, :]
            span, buf = 2 * span, 1 - buf
            lvl_ref[buf, 0:n, :] = level
        wsum = lvl_ref[buf, POOL_HALO - half:POOL_HALO - half + tm, :]
        cnt = (jnp.minimum(t + half, seq_len) - jnp.maximum(t - half, 0)).astype(F32)
        p = wsum * (1.0 / cnt) - ue_ref[POOL_HALO:POOL_HALO + tm, cols]
        pool_ref[:, cols] = p.astype(BF16)


def _norm2_split(x1, gain2, shift2, h2_ref):
    h2 = _scaled_norm(x1, gain2, shift2)
    h2_ref[...] = _pack_bf16_pair(h2[:, :D_PACK], h2[:, D_PACK:])
    h_hi = h2.astype(BF16)
    return h_hi, (h2 - h_hi.astype(F32)).astype(BF16)


def _router_logits(h_hi, h_lo, wr_ref, br_ref):
    parts = (jnp.dot(h_hi, wr_ref[...], preferred_element_type=F32)
             + jnp.dot(h_lo, wr_ref[...], preferred_element_type=F32))
    return parts[:, :LANES] + parts[:, LANES:] + br_ref[...]


def _dispatch_table(logits, is_real, utri_ref, run_ref, meta_ref, rw_ref):
    n = logits.shape[0]
    e1, e2, w1, w2 = _route(logits.T[0:ROUTE_ROWS, :])

    eidx = lax.broadcasted_iota(I32, (N_EXPERTS, n), 0).astype(F32)
    hit1 = eidx == e1
    hit2 = eidx == e2
    taken = jnp.where(hit1 | hit2, is_real, 0.0)
    before = run_ref[...] + jnp.dot(taken.astype(BF16), utri_ref[...], preferred_element_type=F32)
    rank1 = jnp.sum(jnp.where(hit1, before, 0.0), axis=0, keepdims=True)
    rank2 = jnp.sum(jnp.where(hit2, before, 0.0), axis=0, keepdims=True)
    run_ref[...] = run_ref[...] + jnp.sum(taken, axis=1, keepdims=True)
    ridx = lax.broadcasted_iota(I32, (META_ROWS, n), 0)
    table = jnp.where(ridx == 0, e1, jnp.where(ridx == 1, e2, jnp.where(ridx == 2, rank1,
                      jnp.where(ridx == 3, rank2, 0.0))))
    meta_ref[...] = table.astype(I32)
    weights_t = jnp.where(ridx == 0, w1, jnp.where(ridx == 1, w2, 0.0))
    rw_ref[...] = jnp.concatenate([weights_t, jnp.zeros((LANES - META_ROWS, n), F32)], axis=0).T


def _stage_c_kernel(x_ref, mod_ref, modp_ref, n1g_ref, n2g_ref, u_ref, up_ref, un_ref, attn_ref,
                    wgate_ref, wmix_ref, ps_ref, wpp_ref, wap_ref, wo_ref, wr_ref, br_ref, utri_ref,
                    x1_ref, h2_ref, meta_ref, rw_ref, count_ref,
                    ue_ref, lvl_ref, pool_ref, run_ref, merged_ref, x1_prev, *, seq_len):
    s = pl.program_id(0)
    tm = x_ref.shape[0]
    tiles_per_seq = seq_len // tm
    i = lax.rem(jnp.minimum(s, pl.num_programs(0) - 2), tiles_per_seq)

    @pl.when(s == 0)
    def _():
        run_ref[...] = jnp.zeros_like(run_ref)
        x1_prev[...] = jnp.zeros_like(x1_prev)

    gain1 = n1g_ref[...] * (1.0 + mod_ref[1:2, :])
    shift1 = mod_ref[0:1, :]
    res_gate = mod_ref[2:3, :]
    ue_ref[0:POOL_HALO, :] = up_ref[...].astype(F32) * (i > 0).astype(F32)
    ue_ref[POOL_HALO:POOL_HALO + tm, :] = u_ref[...].astype(F32)
    ue_ref[POOL_HALO + tm:2 * POOL_HALO + tm, :] = un_ref[...].astype(F32) * (i < tiles_per_seq - 1).astype(F32)
    ue_ref[2 * POOL_HALO + tm:, :] = jnp.zeros((POOL_HALO, D_POOL), F32)
    lvl_ref[:, 2 * POOL_HALO + tm:, :] = jnp.zeros((2, POOL_HALO, POOL_GC), F32)
    _pool_deviation(ue_ref, lvl_ref, pool_ref, i * tm, tm, seq_len)

    x = x_ref[...]
    h = _scaled_norm(x, gain1, shift1).astype(BF16)
    mixed_groups = []
    for gi in range(len(POOL_WINDOWS)):
        cols = slice(gi * POOL_GC, (gi + 1) * POOL_GC)
        pm = jnp.dot(pool_ref[:, cols], wmix_ref[gi], preferred_element_type=F32)
        mixed_groups.append(pm * ps_ref[:, cols])
    pool = jnp.concatenate(mixed_groups, axis=1).astype(BF16)
    attn = attn_ref[...]
    assert D_MODEL // MERGE_COLS >= 3
    for c in range(0, D_MODEL, MERGE_COLS):
        cols = slice(c, c + MERGE_COLS)
        gate_cols = slice(D_MODEL + c, D_MODEL + c + MERGE_COLS)
        gate_pool = _sigmoid(jnp.dot(h, wgate_ref[:, cols], preferred_element_type=F32))
        gate_attn = _sigmoid(jnp.dot(h, wgate_ref[:, gate_cols], preferred_element_type=F32))
        merged = (gate_pool * jnp.dot(pool, wpp_ref[:, cols], preferred_element_type=F32)
                  + gate_attn * jnp.dot(attn, wap_ref[:, cols], preferred_element_type=F32))
        merged_ref[:, cols] = merged.astype(BF16)
        if c == 0:
            h_hi, h_lo = _norm2_split(x1_prev[...], n2g_ref[...] * (1.0 + modp_ref[4:5, :]), modp_ref[3:4, :],
                                      h2_ref)
        elif c == MERGE_COLS:
            logits = _router_logits(h_hi, h_lo, wr_ref, br_ref)
        elif c == 2 * MERGE_COLS:
            _dispatch_table(logits, (s > 0).astype(F32), utri_ref, run_ref, meta_ref, rw_ref)
            count_ref[...] = jnp.broadcast_to(run_ref[...], count_ref.shape)
    for c in range(0, D_MODEL, MERGE_COLS):
        cols = slice(c, c + MERGE_COLS)
        mixed = jnp.dot(merged_ref[...], wo_ref[:, cols], preferred_element_type=F32)
        x1 = x[:, cols] + res_gate[:, cols] * mixed
        x1_ref[:, cols] = x1
        x1_prev[:, cols] = x1


def _stage_c(x, mod, n1g, n2g, u, attn, w_gate, w_mix, pool_scale, w_pp, w_ap, w_o, w_r, b_r, tm):
    B, T, _ = x.shape
    hb = tm // POOL_HALO
    n_halo_blocks = T // POOL_HALO
    nt = T // tm
    n_tiles = B * nt

    def merged_tile(s):
        m = jnp.minimum(s, n_tiles - 1)
        return m // nt, lax.rem(m, nt)

    def tok(s):
        b, i = merged_tile(s)
        return b, i, 0

    def halo_before(s):
        b, i = merged_tile(s)
        return b, jnp.maximum(i * hb - 1, 0), 0

    def halo_after(s):
        b, i = merged_tile(s)
        return b, jnp.minimum((i + 1) * hb, n_halo_blocks - 1), 0

    routed = lambda s: jnp.maximum(s - 1, 0)
    flat = lambda s: (routed(s), 0)
    utri = jnp.triu(jnp.ones((tm, tm), BF16), k=1)
    return pl.pallas_call(
        functools.partial(_stage_c_kernel, seq_len=T),
        grid=(n_tiles + 1,),
        in_specs=[
            pl.BlockSpec((None, tm, D_MODEL), tok),
            pl.BlockSpec((None, 6, D_MODEL), lambda s: (merged_tile(s)[0], 0, 0)),
            pl.BlockSpec((None, 6, D_MODEL), lambda s: (routed(s) // nt, 0, 0)),
            _const_spec((1, D_MODEL)),
            _const_spec((1, D_MODEL)),
            pl.BlockSpec((None, tm, D_POOL), tok),
            pl.BlockSpec((None, POOL_HALO, D_POOL), halo_before),
            pl.BlockSpec((None, POOL_HALO, D_POOL), halo_after),
            pl.BlockSpec((None, tm, D_ATTN), tok),
            _const_spec((D_MODEL, 2 * D_MODEL)),
            _const_spec((len(POOL_WINDOWS), POOL_GC, POOL_GC)),
            _const_spec((1, D_POOL)),
            _const_spec((D_POOL, D_MODEL)),
            _const_spec((D_ATTN, D_MODEL)),
            _const_spec((D_MODEL, D_MODEL)),
            _const_spec((D_MODEL, 2 * LANES)),
            _const_spec((1, LANES)),
            _const_spec((tm, tm)),
        ],
        out_specs=[
            pl.BlockSpec((None, tm, D_MODEL), tok),
            pl.BlockSpec((tm, D_PACK), flat),
            pl.BlockSpec((META_ROWS, tm), lambda s: (0, routed(s))),
            pl.BlockSpec((tm, LANES), flat),
            pl.BlockSpec((N_EXPERTS, LANES), lambda s: (0, 0)),
        ],
        out_shape=[
            jax.ShapeDtypeStruct((B, T, D_MODEL), F32),
            jax.ShapeDtypeStruct((B * T, D_PACK), I32),
            jax.ShapeDtypeStruct((META_ROWS, B * T), I32),
            jax.ShapeDtypeStruct((B * T, LANES), F32),
            jax.ShapeDtypeStruct((N_EXPERTS, LANES), F32),
        ],
        scratch_shapes=[pltpu.VMEM((tm + 3 * POOL_HALO, D_POOL), F32),
                        pltpu.VMEM((2, tm + 3 * POOL_HALO, POOL_GC), F32),
                        pltpu.VMEM((tm, D_POOL), BF16),
                        pltpu.VMEM((N_EXPERTS, 1), F32),
                        pltpu.VMEM((tm, D_MODEL), BF16),
                        pltpu.VMEM((tm, D_MODEL), F32)],
        compiler_params=pltpu.CompilerParams(
            dimension_semantics=("arbitrary",), vmem_limit_bytes=V7X_VMEM_LIMIT),
        name="stage_c",
    )(x, mod, mod, n1g, n2g, u, u, u, attn, w_gate, w_mix, pool_scale, w_pp, w_ap, w_o, w_r, b_r, utri)


def _sc_workers():
    info = plsc.get_sparse_core_info()
    return info.num_cores, info.num_subcores


def _sc_scatter_rows(rows, idx0, idx1, n_out):
    n, d = rows.shape
    nc, ns = _sc_workers()
    per_worker = n // (nc * ns)
    n_win = per_worker // SC_WINDOW
    mesh = plsc.VectorSubcoreMesh(core_axis_name="c", subcore_axis_name="s")

    @functools.partial(
        pl.kernel, mesh=mesh,
        out_type=jax.ShapeDtypeStruct((n_out, d), rows.dtype),
        scratch_types=[pltpu.VMEM((SC_WINDOW,), I32), pltpu.VMEM((SC_WINDOW,), I32),
                       pltpu.VMEM((SC_WINDOW, d), rows.dtype)],
        name="sc_scatter_rows",
    )
    def scatter(rows_hbm, idx0_hbm, idx1_hbm, out_hbm, i0_v, i1_v, rows_v):
        wid = lax.axis_index("s") * nc + lax.axis_index("c")

        @pl.loop(0, n_win)
        def _(w):
            base = wid * per_worker + w * SC_WINDOW
            pltpu.sync_copy(rows_hbm.at[pl.ds(base, SC_WINDOW)], rows_v)
            pltpu.sync_copy(idx0_hbm.at[pl.ds(base, SC_WINDOW)], i0_v)
            pltpu.sync_copy(idx1_hbm.at[pl.ds(base, SC_WINDOW)], i1_v)
            pltpu.sync_copy(rows_v, out_hbm.at[i0_v])
            pltpu.sync_copy(rows_v, out_hbm.at[i1_v])

    return scatter(rows, idx0, idx1)


def _sc_gather_rows(table, idx):
    n = idx.shape[0]
    d = table.shape[1]
    nc, ns = _sc_workers()
    per_worker = n // (nc * ns)
    n_win = per_worker // SC_WINDOW
    mesh = plsc.VectorSubcoreMesh(core_axis_name="c", subcore_axis_name="s")

    @functools.partial(
        pl.kernel, mesh=mesh,
        out_type=jax.ShapeDtypeStruct((n, d), table.dtype),
        scratch_types=[pltpu.VMEM((SC_WINDOW,), I32), pltpu.VMEM((SC_WINDOW, d), table.dtype)],
        name="sc_gather_rows",
    )
    def gather(table_hbm, idx_hbm, out_hbm, i_v, rows_v):
        wid = lax.axis_index("s") * nc + lax.axis_index("c")

        @pl.loop(0, n_win)
        def _(w):
            base = wid * per_worker + w * SC_WINDOW
            pltpu.sync_copy(idx_hbm.at[pl.ds(base, SC_WINDOW)], i_v)
            pltpu.sync_copy(table_hbm.at[i_v], rows_v)
            pltpu.sync_copy(rows_v, out_hbm.at[pl.ds(base, SC_WINDOW)])

    return gather(table, idx)


def _moe_kernel(te_ref, tv_ref, nl_ref, x_hbm, wgu_ref, wd_ref, o_ref, xbuf, sem):
    del te_ref
    i = pl.program_id(0)
    n_live_tiles = nl_ref[0]
    valid = tv_ref[i]
    tm = o_ref.shape[0]
    slot = _input_ring(x_hbm, xbuf, sem, i, n_live_tiles, tm)

    group = min(STAGE_ROWS, tm)
    n_groups = tm // group

    def expert_rows(r0):
        rows = slice(r0, r0 + group)
        live = lax.broadcasted_iota(I32, (group, 1), 0) < valid - r0
        lo, hi = _unpack_bf16_pair(jnp.where(live, xbuf[slot, rows, :], 0))
        x = jnp.concatenate([lo, hi], axis=1).astype(BF16)
        gu = jnp.dot(x, wgu_ref[...], preferred_element_type=F32)
        a = gu[:, :D_EXPERT]
        hmid = (a * _sigmoid(a) * gu[:, D_EXPERT:]).astype(BF16)
        y = jnp.dot(hmid, wd_ref[...], preferred_element_type=F32)
        o_ref[rows, :] = _pack_bf16_pair(y[:, :D_PACK], y[:, D_PACK:])

    for n_live in range(n_groups + 1):
        @pl.when((valid > (n_live - 1) * group) & (valid <= n_live * group))
        def _():
            for g in range(n_live):
                expert_rows(g * group)
            if n_live < n_groups:
                o_ref[n_live * group:, :] = jnp.zeros((tm - n_live * group, D_PACK), I32)


def _moe(xs, tile_expert, tile_valid, n_live_tiles, w_gu, w_d, tm):
    n_tiles = xs.shape[0] // tm
    grid_spec = pltpu.PrefetchScalarGridSpec(
        num_scalar_prefetch=3,
        grid=(n_tiles,),
        in_specs=[
            pl.BlockSpec(memory_space=pl.ANY),
            pl.BlockSpec((None, D_MODEL, 2 * D_EXPERT), lambda i, te, tv, nl: (te[i], 0, 0)),
            pl.BlockSpec((None, D_EXPERT, D_MODEL), lambda i, te, tv, nl: (te[i], 0, 0)),
        ],
        out_specs=pl.BlockSpec((tm, D_PACK), lambda i, te, tv, nl: (jnp.minimum(i, nl[0]), 0)),
        scratch_shapes=[pltpu.VMEM((RING_SLOTS, tm, D_PACK), I32), pltpu.SemaphoreType.DMA((RING_SLOTS,))],
    )
    return pl.pallas_call(
        _moe_kernel,
        grid_spec=grid_spec,
        out_shape=jax.ShapeDtypeStruct(xs.shape, I32),
        compiler_params=pltpu.CompilerParams(
            dimension_semantics=("arbitrary",), vmem_limit_bytes=V7X_VMEM_LIMIT),
        name="experts",
    )(tile_expert, tile_valid, n_live_tiles, xs, w_gu, w_d)


def _final_kernel(x1_ref, y0_ref, y1_ref, rw_ref, mod_ref, fg_ref, o_ref):
    y0lo, y0hi = _unpack_bf16_pair(y0_ref[...])
    y1lo, y1hi = _unpack_bf16_pair(y1_ref[...])
    w0 = rw_ref[:, 0:1]
    w1 = rw_ref[:, 1:2]
    moe = jnp.concatenate([w0 * y0lo + w1 * y1lo, w0 * y0hi + w1 * y1hi], axis=1)
    x2 = x1_ref[...] + mod_ref[5:6, :] * moe
    ms = jnp.mean(x2 * x2, axis=-1, keepdims=True)
    o_ref[...] = x2 * lax.rsqrt(ms + EPS) * fg_ref[...]


def _final(x1, yb, rw, mod, final_g, tm):
    B, T, _ = x1.shape
    n_blocks = B * T // tm
    flat = lambda b, i: (b * (T // tm) + i, 0)
    return pl.pallas_call(
        _final_kernel,
        grid=(B, T // tm),
        in_specs=[
            pl.BlockSpec((None, tm, D_MODEL), lambda b, i: (b, i, 0)),
            pl.BlockSpec((tm, D_PACK), flat),
            pl.BlockSpec((tm, D_PACK), lambda b, i: (n_blocks + b * (T // tm) + i, 0)),
            pl.BlockSpec((tm, LANES), flat),
            pl.BlockSpec((None, 6, D_MODEL), lambda b, i: (b, 0, 0)),
            _const_spec((1, D_MODEL)),
        ],
        out_specs=pl.BlockSpec((None, tm, D_MODEL), lambda b, i: (b, i, 0)),
        out_shape=jax.ShapeDtypeStruct((B, T, D_MODEL), F32),
        compiler_params=pltpu.CompilerParams(
            dimension_semantics=("parallel", "parallel"), vmem_limit_bytes=V7X_VMEM_LIMIT),
        name="final_combine",
    )(x1, yb, yb, rw, mod, final_g)


def _rope_tables(T):
    rows = T // GRID_W
    row = np.repeat(np.arange(rows, dtype=np.float32), GRID_W)
    col = np.tile(np.arange(GRID_W, dtype=np.float32), rows)
    inv_freq = np.float32(ROPE_THETA) ** (-np.arange(0, 64, 2, dtype=np.float32) / np.float32(64))
    ang_r = row[:, None] * inv_freq[None, :]
    ang_c = col[:, None] * inv_freq[None, :]
    ang = np.concatenate([ang_r, ang_r, ang_c, ang_c], axis=-1).astype(np.float32)
    sign = np.where((np.arange(HEAD_DIM) % 64) < 32, -1.0, 1.0).astype(np.float32)
    return jnp.asarray(np.cos(ang)), jnp.asarray(np.sin(ang) * sign[None, :])


def _dispatch_plan(counts, tm, n_tiles):
    tiles_per_e = (counts + tm - 1) // tm
    tile_end = jnp.cumsum(tiles_per_e)
    tile_start = tile_end - tiles_per_e
    tile_id = jnp.arange(n_tiles, dtype=I32)
    te = jnp.minimum(jnp.sum((tile_id[:, None] >= tile_end[None, :]).astype(I32), axis=1), N_EXPERTS - 1)
    live = jnp.clip(counts[te] - (tile_id - tile_start[te]) * tm, 0, tm)
    tv = jnp.where(tile_id < tile_end[-1], live, 0).astype(I32)
    return tile_start.astype(I32), te, tv, tile_end[-1:].astype(I32)


def _slot_kernel(start_ref, meta_ref, pos_ref, *, tile_rows):
    e = meta_ref[0:2, :]
    first_tile = jnp.zeros_like(e)
    for ex in range(N_EXPERTS):
        first_tile = jnp.where(e == ex, start_ref[ex], first_tile)
    pos_ref[...] = first_tile * tile_rows + meta_ref[2:4, :]


def _slots(meta, tile_start, tile_rows):
    n = meta.shape[1]
    tn = min(n, 8192)
    grid_spec = pltpu.PrefetchScalarGridSpec(
        num_scalar_prefetch=1,
        grid=(n // tn,),
        in_specs=[pl.BlockSpec((META_ROWS, tn), lambda i, start: (0, i))],
        out_specs=pl.BlockSpec((2, tn), lambda i, start: (0, i)),
    )
    return pl.pallas_call(
        functools.partial(_slot_kernel, tile_rows=tile_rows),
        grid_spec=grid_spec,
        out_shape=jax.ShapeDtypeStruct((2, n), I32),
        name="dispatch_slots",
    )(tile_start, meta)


def _pick_tile(T, want):
    t = min(T, want)
    assert T % t == 0
    return t


def _trunk(x, c, p):
    B, T, _ = x.shape
    N = B * T
    tm = _pick_tile(T, 512)
    tq = _pick_tile(T, 512)
    tk = _pick_tile(T, 256)
    tme = _pick_tile(N, 512)
    cos, sin_signed = _rope_tables(T)
    mod = _modulation(c, p["w_ada"], p["b_ada"]).reshape(B, 6, D_MODEL)
    u, qt, qb, k, vt = _stage_a(x, mod, p["n1g"], p["w_qkvu"], p["qg"], p["kg"], cos, sin_signed, tm)
    attn = lax.cond(
        p["bounded_softmax_ok"],
        functools.partial(_attention, tq=tq, tk=_pick_tile(T, 2048), running_max=False),
        functools.partial(_attention, tq=tq, tk=tk, running_max=True),
        qt, qb, k, vt)
    x1, h2p, meta, rw, counts = _stage_c(x, mod, p["n1g"], p["n2g"], u, attn, p["w_gate"], p["w_mix"],
                                         p["pool_scale"], p["w_pp"], p["w_ap"], p["w_o"], p["w_r"], p["b_r"], tm)
    n_tiles = 2 * N // tme + N_EXPERTS
    tile_start, te, tv, n_live_tiles = _dispatch_plan(counts[:, 0].astype(I32), tme, n_tiles)
    pos = _slots(meta, tile_start, tme)
    xs = _sc_scatter_rows(h2p, pos[0], pos[1], n_tiles * tme)
    ys = _moe(xs, te, tv, n_live_tiles, p["w_egu"], p["w_ed"], tme)
    yb = _sc_gather_rows(ys, pos.reshape(-1))
    return _final(x1, yb, rw, mod, p["final_g"], _pick_tile(T, 1024))


def kernel(x_prompt, x_sample, c_prompt, c_sample, w_ada, b_ada, norm1_g, norm2_g, w_in, q_norm_g,
           k_norm_g, w_pool_mix, pool_scale, w_pool_proj, w_attn_proj, w_o, w_router_group,
           b_router_group, w_router_expert, b_router_expert, w_exp_gate, w_exp_up, w_exp_down, final_g):
    assert w_ada.shape[0] == 1, "single-layer block"
    n_r = N_GROUPS + N_EXPERTS
    w_r = jnp.concatenate([w_router_expert[0], w_router_group[0],
                           jnp.zeros((D_MODEL, LANES - n_r), F32)], axis=1)
    w_r_hi = w_r.astype(BF16)
    b_r = jnp.concatenate([b_router_expert[0], b_router_group[0], jnp.zeros((LANES - n_r,), F32)])
    score_bound = (1.01 * LOG2E * np.sqrt(HEAD_DIM)) * jnp.max(jnp.abs(q_norm_g[0])) * jnp.max(jnp.abs(k_norm_g[0]))
    p = dict(
        bounded_softmax_ok=2.0 * score_bound <= EXP2_SAFE_SPAN,
        w_ada=w_ada[0], b_ada=b_ada[0],
        n1g=norm1_g[0].reshape(1, D_MODEL), n2g=norm2_g[0].reshape(1, D_MODEL),
        w_qkvu=w_in[0][:, :D_QKVU].astype(BF16), w_gate=w_in[0][:, D_QKVU:].astype(BF16),
        qg=q_norm_g[0].reshape(1, HEAD_DIM), kg=k_norm_g[0].reshape(1, HEAD_DIM),
        w_mix=w_pool_mix[0].astype(BF16), pool_scale=pool_scale[0].reshape(1, D_POOL),
        w_pp=w_pool_proj[0].astype(BF16), w_ap=w_attn_proj[0].astype(BF16), w_o=w_o[0].astype(BF16),
        w_r=jnp.concatenate([w_r_hi, (w_r - w_r_hi.astype(F32)).astype(BF16)], axis=1), b_r=b_r.reshape(1, LANES),
        w_egu=jnp.concatenate([w_exp_gate[0], w_exp_up[0]], axis=-1).astype(BF16), w_ed=w_exp_down[0].astype(BF16),
        final_g=final_g.reshape(1, D_MODEL),
    )
    return _trunk(x_prompt, c_prompt, p), _trunk(x_sample, c_sample, p)
```

```python
import functools

import numpy as np
import jax
import jax.numpy as jnp
from jax import lax
from jax.experimental import pallas as pl
from jax.experimental.pallas import tpu as pltpu
from jax.experimental.pallas import tpu_sc as plsc

F32 = jnp.float32
BF16 = jnp.bfloat16
I32 = jnp.int32

D_MODEL = 1024
GRID_W = 64
POOL_WINDOWS = (2, 4, 8, 16)
POOL_GC = 128
D_POOL = 512
HEAD_DIM = 128
N_Q_HEADS = 4
N_KV_HEADS = 2
D_ATTN = 512
D_KV = 256
ROPE_THETA = 10000.0
N_GROUPS = 4
EXPERTS_PER_GROUP = 8
N_EXPERTS = 32
D_EXPERT = 256
EPS = 1e-6
LOG2E = 1.4426950408889634
EXP2_SAFE_SPAN = 100.0
D_QKVU = D_POOL + D_ATTN + 2 * D_KV
POOL_HALO = 16
STAGE_ROWS = 256
PROJ_COLS = 256
MERGE_COLS = 256
RING_SLOTS = 3
ATTN_TILES_PER_TRIP = 10
ROUTE_ROWS = 40
META_ROWS = 8
LANES = 128
D_PACK = D_MODEL // 2

V7X_VMEM_LIMIT = 56 * 1024 * 1024
SC_WINDOW = 64


def _const_spec(shape):
    nd = len(shape)
    return pl.BlockSpec(shape, lambda *_: (0,) * nd, pipeline_mode=pl.Buffered(1))


def _sigmoid(x):
    return 0.5 * jnp.tanh(0.5 * x) + 0.5


def _pack_bf16_pair(lo, hi):
    lo_bits = lax.bitcast_convert_type(lo.astype(BF16).astype(F32), jnp.uint32) >> 16
    hi_bits = lax.bitcast_convert_type(hi.astype(BF16).astype(F32), jnp.uint32) & jnp.uint32(0xFFFF0000)
    return lax.bitcast_convert_type(lo_bits | hi_bits, I32)


def _unpack_bf16_pair(packed):
    u = lax.bitcast_convert_type(packed, jnp.uint32)
    lo = lax.bitcast_convert_type(u << 16, F32)
    hi = lax.bitcast_convert_type(u & jnp.uint32(0xFFFF0000), F32)
    return lo, hi


def _input_ring(x_hbm, buf, sem, step, n_steps, rows):
    def fetch(s):
        slot = lax.rem(s, RING_SLOTS)
        src = x_hbm.at[pl.ds(pl.multiple_of(s * rows, rows), rows), :]
        return pltpu.make_async_copy(src, buf.at[slot], sem.at[slot])

    @pl.when((step == 0) & (n_steps > 0))
    def _():
        fetch(0).start()

    @pl.when((step == 0) & (n_steps > 1))
    def _():
        fetch(1).start()

    @pl.when(step + 2 < n_steps)
    def _():
        fetch(step + 2).start()

    @pl.when(step < n_steps)
    def _():
        fetch(step).wait()

    return lax.rem(step, RING_SLOTS)


def _mod_kernel(c_ref, w_ref, b_ref, o_ref):
    c = c_ref[...]
    s = c * _sigmoid(c)
    o_ref[...] = jnp.dot(s, w_ref[...], preferred_element_type=F32,
                         precision=lax.Precision.HIGHEST) + b_ref[...]


def _modulation(c, w_ada, b_ada):
    B = c.shape[0]
    n_out = w_ada.shape[1]
    bn = D_MODEL
    return pl.pallas_call(
        _mod_kernel,
        grid=(n_out // bn,),
        in_specs=[pl.BlockSpec((B, D_MODEL), lambda j: (0, 0)),
                  pl.BlockSpec((D_MODEL, bn), lambda j: (0, j)),
                  pl.BlockSpec((1, bn), lambda j: (0, j))],
        out_specs=pl.BlockSpec((B, bn), lambda j: (0, j)),
        out_shape=jax.ShapeDtypeStruct((B, n_out), F32),
        name="modulation",
    )(c, w_ada, b_ada.reshape(1, n_out))


def _scaled_norm(x, gain, shift):
    ms = jnp.mean(x * x, axis=-1, keepdims=True)
    return x * lax.rsqrt(ms + EPS) * gain + shift


def _stage_a_kernel(x_hbm, mod_ref, n1g_ref, w_ref, qg_ref, kg_ref, cos_ref, sin_ref,
                    u_ref, qt_ref, qb_ref, k_ref, vt_ref, xbuf, sem, z_ref):
    s = pl.program_id(0)
    n_tiles = pl.num_programs(0) - 1
    tm = u_ref.shape[0]
    _input_ring(x_hbm, xbuf, sem, s, n_tiles, tm)
    slot = lax.rem(jnp.minimum(s, n_tiles - 1), RING_SLOTS)

    @pl.when(s == 0)
    def _():
        z_ref[...] = jnp.zeros_like(z_ref)

    gain1 = n1g_ref[...] * (1.0 + mod_ref[1:2, :])
    shift1 = mod_ref[0:1, :]
    lane = lax.broadcasted_iota(I32, (1, HEAD_DIM), 1)
    first_half = (lane % 64) < 32
    qg = qg_ref[...] * (LOG2E / np.sqrt(HEAD_DIM))
    k_norm_max = jnp.max(jnp.abs(kg_ref[...]), axis=-1, keepdims=True) * (np.sqrt(HEAD_DIM) * 1.01)
    cos = cos_ref[...]
    sin_signed = sin_ref[...]

    def norm_rope(xh, g):
        ms = jnp.mean(xh * xh, axis=-1, keepdims=True)
        xn = xh * lax.rsqrt(ms + EPS) * g
        rot = jnp.where(first_half, pltpu.roll(xn, 96, 1), pltpu.roll(xn, 32, 1))
        return xn * cos + rot * sin_signed

    def finish(o):
        zh = z_ref[:, o:o + HEAD_DIM]
        if o < D_POOL:
            u_ref[:, o:o + HEAD_DIM] = zh.astype(BF16)
        elif o < D_POOL + D_ATTN:
            hq = (o - D_POOL) // HEAD_DIM
            q_t = norm_rope(zh, qg).T.astype(BF16)
            qt_ref[hq] = q_t
            q_f = q_t.astype(F32)
            qb_ref[hq] = jnp.sqrt(jnp.sum(q_f * q_f, axis=0, keepdims=True)) * k_norm_max
        elif o < D_POOL + D_ATTN + D_KV:
            k_ref[:, o - D_POOL - D_ATTN:o - D_POOL - D_ATTN + HEAD_DIM] = norm_rope(zh, kg_ref[...]).astype(BF16)
        else:
            vt_ref[(o - D_POOL - D_ATTN - D_KV) // HEAD_DIM] = zh.T.astype(BF16)

    h = _scaled_norm(xbuf[slot], gain1, shift1).astype(BF16)
    for c in range(0, D_QKVU, PROJ_COLS):
        for o in range(c, c + PROJ_COLS, HEAD_DIM):
            finish(o)
        z_ref[:, c:c + PROJ_COLS] = jnp.dot(h, w_ref[:, c:c + PROJ_COLS], preferred_element_type=F32)


def _stage_a(x, mod, n1g, w_qkvu, qg, kg, cos, sin_signed, tm):
    B, T, _ = x.shape
    nt = T // tm
    n_tiles = B * nt
    projected = lambda s: jnp.minimum(s, n_tiles - 1)
    finished = lambda s: jnp.maximum(s - 1, 0)
    return pl.pallas_call(
        _stage_a_kernel,
        grid=(n_tiles + 1,),
        in_specs=[
            pl.BlockSpec(memory_space=pl.ANY),
            pl.BlockSpec((None, 6, D_MODEL), lambda s: (projected(s) // nt, 0, 0)),
            _const_spec((1, D_MODEL)),
            _const_spec((D_MODEL, D_QKVU)),
            _const_spec((1, HEAD_DIM)),
            _const_spec((1, HEAD_DIM)),
            pl.BlockSpec((tm, HEAD_DIM), lambda s: (lax.rem(finished(s), nt), 0)),
            pl.BlockSpec((tm, HEAD_DIM), lambda s: (lax.rem(finished(s), nt), 0)),
        ],
        out_specs=[
            pl.BlockSpec((None, tm, D_POOL), lambda s: (finished(s) // nt, lax.rem(finished(s), nt), 0)),
            pl.BlockSpec((None, N_Q_HEADS, HEAD_DIM, tm), lambda s: (finished(s) // nt, 0, 0, lax.rem(finished(s), nt))),
            pl.BlockSpec((None, N_Q_HEADS, 1, tm), lambda s: (finished(s) // nt, 0, 0, lax.rem(finished(s), nt))),
            pl.BlockSpec((None, tm, D_KV), lambda s: (finished(s) // nt, lax.rem(finished(s), nt), 0)),
            pl.BlockSpec((None, N_KV_HEADS, HEAD_DIM, tm), lambda s: (finished(s) // nt, 0, 0, lax.rem(finished(s), nt))),
        ],
        out_shape=[
            jax.ShapeDtypeStruct((B, T, D_POOL), BF16),
            jax.ShapeDtypeStruct((B, N_Q_HEADS, HEAD_DIM, T), BF16),
            jax.ShapeDtypeStruct((B, N_Q_HEADS, 1, T), F32),
            jax.ShapeDtypeStruct((B, T, D_KV), BF16),
            jax.ShapeDtypeStruct((B, N_KV_HEADS, HEAD_DIM, T), BF16),
        ],
        scratch_shapes=[pltpu.VMEM((RING_SLOTS, tm, D_MODEL), F32), pltpu.SemaphoreType.DMA((RING_SLOTS,)),
                        pltpu.VMEM((tm, D_QKVU), F32)],
        compiler_params=pltpu.CompilerParams(
            dimension_semantics=("arbitrary",), vmem_limit_bytes=V7X_VMEM_LIMIT),
        name="stage_a",
    )(x.reshape(B * T, D_MODEL), mod, n1g, w_qkvu, qg, kg, cos, sin_signed)


def _write_attn_output(o_ref, acc, l, tq):
    o = acc * (1.0 / l)
    o_ref[:, :HEAD_DIM] = o[:, :tq].T.astype(BF16)
    o_ref[:, HEAD_DIM:] = o[:, tq:].T.astype(BF16)


def _attn_bounded_kernel(qt_ref, qb_ref, k_ref, vt_ref, o_ref, p_ref, acc_ref, l_ref, *, tk):
    T = k_ref.shape[0]
    tq = qt_ref.shape[2]
    nq = 2 * tq
    nk = T // tk
    qt = jnp.concatenate([qt_ref[0], qt_ref[1]], axis=1)
    qb = jnp.concatenate([qb_ref[0], qb_ref[1]], axis=1)

    def weights(j, slot):
        off = pl.multiple_of(j * tk, tk)
        st = jnp.dot(k_ref[pl.ds(off, tk), :], qt, preferred_element_type=F32)
        p = jnp.exp2(st - qb)
        l_ref[...] += jnp.sum(p.reshape(tk // 8, 8, nq), axis=0)
        p_ref[slot] = p.astype(BF16)

    def values(j, slot):
        off = pl.multiple_of(j * tk, tk)
        acc_ref[...] += jnp.dot(vt_ref[:, pl.ds(off, tk)], p_ref[slot], preferred_element_type=F32)

    def step(j, slot):
        values(j - 1, 1 - slot)
        weights(j, slot)

    acc_ref[...] = jnp.zeros_like(acc_ref)
    l_ref[...] = jnp.zeros_like(l_ref)
    weights(0, 0)

    unroll = max(1, min(ATTN_TILES_PER_TRIP, nk - 1))
    trips = (nk - 1) // unroll if unroll % 2 == 0 else 0

    if trips > 1:
        @pl.loop(0, trips)
        def _(g):
            for d in range(unroll):
                step(1 + unroll * g + d, (1 + d) % 2)
    else:
        trips = 0
    for j in range(1 + trips * unroll, nk):
        step(j, j % 2)
    values(nk - 1, (nk - 1) % 2)
    _write_attn_output(o_ref, acc_ref[...], jnp.sum(l_ref[...], axis=0, keepdims=True), tq)


def _attn_online_kernel(qt_ref, qb_ref, k_ref, vt_ref, o_ref, *, tk):
    del qb_ref
    T = k_ref.shape[0]
    tq = qt_ref.shape[2]
    nq = 2 * tq
    qt = jnp.concatenate([qt_ref[0], qt_ref[1]], axis=1)

    def body(s, carry):
        m, l, acc = carry
        off = pl.multiple_of(s * tk, tk)
        st = jnp.dot(k_ref[pl.ds(off, tk), :], qt, preferred_element_type=F32)
        m_new = jnp.maximum(m, jnp.max(st, axis=0, keepdims=True))
        alpha = jnp.exp2(m - m_new)
        p = jnp.exp2(st - m_new)
        l = alpha * l + jnp.sum(p, axis=0, keepdims=True)
        pv = jnp.dot(vt_ref[:, pl.ds(off, tk)], p.astype(BF16), preferred_element_type=F32)
        return m_new, l, alpha * acc + pv

    init = (jnp.full((1, nq), -jnp.inf, F32), jnp.zeros((1, nq), F32), jnp.zeros((HEAD_DIM, nq), F32))
    _, l, acc = lax.fori_loop(0, T // tk, body, init)
    _write_attn_output(o_ref, acc, l, tq)


def _attention(qt, qb, k, vt, tq, tk, running_max):
    B, _, _, T = qt.shape
    group = N_Q_HEADS // N_KV_HEADS
    nq = group * tq
    if running_max:
        body, scratch, name = functools.partial(_attn_online_kernel, tk=tk), [], "attention_online_max"
    else:
        body = functools.partial(_attn_bounded_kernel, tk=tk)
        scratch = [pltpu.VMEM((2, tk, nq), BF16), pltpu.VMEM((HEAD_DIM, nq), F32), pltpu.VMEM((8, nq), F32)]
        name = "attention_bounded"
    return pl.pallas_call(
        body,
        grid=(B, N_KV_HEADS, T // tq),
        in_specs=[
            pl.BlockSpec((None, group, HEAD_DIM, tq), lambda b, j, i: (b, j, 0, i)),
            pl.BlockSpec((None, group, 1, tq), lambda b, j, i: (b, j, 0, i)),
            pl.BlockSpec((None, T, HEAD_DIM), lambda b, j, i: (b, 0, j)),
            pl.BlockSpec((None, None, HEAD_DIM, T), lambda b, j, i: (b, j, 0, 0)),
        ],
        out_specs=pl.BlockSpec((None, tq, group * HEAD_DIM), lambda b, j, i: (b, i, j)),
        out_shape=jax.ShapeDtypeStruct((B, T, D_ATTN), BF16),
        scratch_shapes=scratch,
        compiler_params=pltpu.CompilerParams(
            dimension_semantics=("parallel", "parallel", "parallel"),
            vmem_limit_bytes=V7X_VMEM_LIMIT),
        name=name,
    )(qt, qb, k, vt)


def _route(logits_t):
    neg = jnp.float32(-jnp.inf)
    n = logits_t.shape[1]

    def top(vals, idx, far):
        v = jnp.max(vals, axis=0, keepdims=True)
        i = jnp.min(jnp.where(vals == v, idx, far), axis=0, keepdims=True)
        return v, i

    gidx = lax.broadcasted_iota(I32, (ROUTE_ROWS - N_EXPERTS, n), 0).astype(F32)
    grp = jnp.where(gidx < N_GROUPS, logits_t[N_EXPERTS:, :], neg)
    gmax, gsel = top(grp, gidx, jnp.float32(ROUTE_ROWS))
    p_group = 1.0 / jnp.sum(jnp.exp(grp - gmax), axis=0, keepdims=True)
    eidx = lax.broadcasted_iota(I32, (N_EXPERTS, n), 0).astype(F32)
    first = EXPERTS_PER_GROUP * gsel
    le = jnp.where((eidx >= first) & (eidx < first + EXPERTS_PER_GROUP), logits_t[:N_EXPERTS, :], neg)
    v1, e1 = top(le, eidx, jnp.float32(N_EXPERTS))
    v2, e2 = top(jnp.where(eidx == e1, neg, le), eidx, jnp.float32(N_EXPERTS))
    e21 = jnp.exp(v2 - v1)
    w1 = p_group / (1.0 + e21)
    return e1, e2, w1, w1 * e21


def _pool_deviation(ue_ref, lvl_ref, pool_ref, tile_start, tm, seq_len):
    n = tm + 2 * POOL_HALO
    t = tile_start + lax.broadcasted_iota(I32, (tm, 1), 0)
    for gi, w in enumerate(POOL_WINDOWS):
        half = w // 2
        cols = slice(gi * POOL_GC, (gi + 1) * POOL_GC)
        level = ue_ref[0:n, cols] + ue_ref[1:n + 1, cols]
        span, buf = 2, 0
        lvl_ref[buf, 0:n, :] = level
        while span < w:
            level = lvl_ref[buf, 0:n, :] + lvl_ref[buf, span:span + n, :]
            span, buf = 2 * span, 1 - buf
            lvl_ref[buf, 0:n, :] = level
        wsum = lvl_ref[buf, POOL_HALO - half:POOL_HALO - half + tm, :]
        cnt = (jnp.minimum(t + half, seq_len) - jnp.maximum(t - half, 0)).astype(F32)
        p = wsum * (1.0 / cnt) - ue_ref[POOL_HALO:POOL_HALO + tm, cols]
        pool_ref[:, cols] = p.astype(BF16)


def _norm2_split(x1, gain2, shift2, h2_ref):
    h2 = _scaled_norm(x1, gain2, shift2)
    h2_ref[...] = _pack_bf16_pair(h2[:, :D_PACK], h2[:, D_PACK:])
    h_hi = h2.astype(BF16)
    return h_hi, (h2 - h_hi.astype(F32)).astype(BF16)


def _router_logits(h_hi, h_lo, wr_ref, br_ref):
    parts = (jnp.dot(h_hi, wr_ref[...], preferred_element_type=F32)
             + jnp.dot(h_lo, wr_ref[...], preferred_element_type=F32))
    return parts[:, :LANES] + parts[:, LANES:] + br_ref[...]


def _dispatch_table(logits, is_real, utri_ref, run_ref, meta_ref, rw_ref):
    n = logits.shape[0]
    e1, e2, w1, w2 = _route(logits.T[0:ROUTE_ROWS, :])

    eidx = lax.broadcasted_iota(I32, (N_EXPERTS, n), 0).astype(F32)
    hit1 = eidx == e1
    hit2 = eidx == e2
    taken = jnp.where(hit1 | hit2, is_real, 0.0)
    before = run_ref[...] + jnp.dot(taken.astype(BF16), utri_ref[...], preferred_element_type=F32)
    rank1 = jnp.sum(jnp.where(hit1, before, 0.0), axis=0, keepdims=True)
    rank2 = jnp.sum(jnp.where(hit2, before, 0.0), axis=0, keepdims=True)
    run_ref[...] = run_ref[...] + jnp.sum(taken, axis=1, keepdims=True)
    ridx = lax.broadcasted_iota(I32, (META_ROWS, n), 0)
    table = jnp.where(ridx == 0, e1, jnp.where(ridx == 1, e2, jnp.where(ridx == 2, rank1,
                      jnp.where(ridx == 3, rank2, 0.0))))
    meta_ref[...] = table.astype(I32)
    weights_t = jnp.where(ridx == 0, w1, jnp.where(ridx == 1, w2, 0.0))
    rw_ref[...] = jnp.concatenate([weights_t, jnp.zeros((LANES - META_ROWS, n), F32)], axis=0).T


def _stage_c_kernel(x_ref, mod_ref, modp_ref, n1g_ref, n2g_ref, u_ref, up_ref, un_ref, attn_ref,
                    wgate_ref, wmix_ref, ps_ref, wpp_ref, wap_ref, wo_ref, wr_ref, br_ref, utri_ref,
                    x1_ref, h2_ref, meta_ref, rw_ref, count_ref,
                    ue_ref, lvl_ref, pool_ref, run_ref, merged_ref, x1_prev, *, seq_len):
    s = pl.program_id(0)
    tm = x_ref.shape[0]
    tiles_per_seq = seq_len // tm
    i = lax.rem(jnp.minimum(s, pl.num_programs(0) - 2), tiles_per_seq)

    @pl.when(s == 0)
    def _():
        run_ref[...] = jnp.zeros_like(run_ref)
        x1_prev[...] = jnp.zeros_like(x1_prev)

    gain1 = n1g_ref[...] * (1.0 + mod_ref[1:2, :])
    shift1 = mod_ref[0:1, :]
    res_gate = mod_ref[2:3, :]
    ue_ref[0:POOL_HALO, :] = up_ref[...].astype(F32) * (i > 0).astype(F32)
    ue_ref[POOL_HALO:POOL_HALO + tm, :] = u_ref[...].astype(F32)
    ue_ref[POOL_HALO + tm:2 * POOL_HALO + tm, :] = un_ref[...].astype(F32) * (i < tiles_per_seq - 1).astype(F32)
    ue_ref[2 * POOL_HALO + tm:, :] = jnp.zeros((POOL_HALO, D_POOL), F32)
    lvl_ref[:, 2 * POOL_HALO + tm:, :] = jnp.zeros((2, POOL_HALO, POOL_GC), F32)
    _pool_deviation(ue_ref, lvl_ref, pool_ref, i * tm, tm, seq_len)

    x = x_ref[...]
    h = _scaled_norm(x, gain1, shift1).astype(BF16)
    mixed_groups = []
    for gi in range(len(POOL_WINDOWS)):
        cols = slice(gi * POOL_GC, (gi + 1) * POOL_GC)
        pm = jnp.dot(pool_ref[:, cols], wmix_ref[gi], preferred_element_type=F32)
        mixed_groups.append(pm * ps_ref[:, cols])
    pool = jnp.concatenate(mixed_groups, axis=1).astype(BF16)
    attn = attn_ref[...]
    assert D_MODEL // MERGE_COLS >= 3
    for c in range(0, D_MODEL, MERGE_COLS):
        cols = slice(c, c + MERGE_COLS)
        gate_cols = slice(D_MODEL + c, D_MODEL + c + MERGE_COLS)
        gate_pool = _sigmoid(jnp.dot(h, wgate_ref[:, cols], preferred_element_type=F32))
        gate_attn = _sigmoid(jnp.dot(h, wgate_ref[:, gate_cols], preferred_element_type=F32))
        merged = (gate_pool * jnp.dot(pool, wpp_ref[:, cols], preferred_element_type=F32)
                  + gate_attn * jnp.dot(attn, wap_ref[:, cols], preferred_element_type=F32))
        merged_ref[:, cols] = merged.astype(BF16)
        if c == 0:
            h_hi, h_lo = _norm2_split(x1_prev[...], n2g_ref[...] * (1.0 + modp_ref[4:5, :]), modp_ref[3:4, :],
                                      h2_ref)
        elif c == MERGE_COLS:
            logits = _router_logits(h_hi, h_lo, wr_ref, br_ref)
        elif c == 2 * MERGE_COLS:
            _dispatch_table(logits, (s > 0).astype(F32), utri_ref, run_ref, meta_ref, rw_ref)
            count_ref[...] = jnp.broadcast_to(run_ref[...], count_ref.shape)
    for c in range(0, D_MODEL, MERGE_COLS):
        cols = slice(c, c + MERGE_COLS)
        mixed = jnp.dot(merged_ref[...], wo_ref[:, cols], preferred_element_type=F32)
        x1 = x[:, cols] + res_gate[:, cols] * mixed
        x1_ref[:, cols] = x1
        x1_prev[:, cols] = x1


def _stage_c(x, mod, n1g, n2g, u, attn, w_gate, w_mix, pool_scale, w_pp, w_ap, w_o, w_r, b_r, tm):
    B, T, _ = x.shape
    hb = tm // POOL_HALO
    n_halo_blocks = T // POOL_HALO
    nt = T // tm
    n_tiles = B * nt

    def merged_tile(s):
        m = jnp.minimum(s, n_tiles - 1)
        return m // nt, lax.rem(m, nt)

    def tok(s):
        b, i = merged_tile(s)
        return b, i, 0

    def halo_before(s):
        b, i = merged_tile(s)
        return b, jnp.maximum(i * hb - 1, 0), 0

    def halo_after(s):
        b, i = merged_tile(s)
        return b, jnp.minimum((i + 1) * hb, n_halo_blocks - 1), 0

    routed = lambda s: jnp.maximum(s - 1, 0)
    flat = lambda s: (routed(s), 0)
    utri = jnp.triu(jnp.ones((tm, tm), BF16), k=1)
    return pl.pallas_call(
        functools.partial(_stage_c_kernel, seq_len=T),
        grid=(n_tiles + 1,),
        in_specs=[
            pl.BlockSpec((None, tm, D_MODEL), tok),
            pl.BlockSpec((None, 6, D_MODEL), lambda s: (merged_tile(s)[0], 0, 0)),
            pl.BlockSpec((None, 6, D_MODEL), lambda s: (routed(s) // nt, 0, 0)),
            _const_spec((1, D_MODEL)),
            _const_spec((1, D_MODEL)),
            pl.BlockSpec((None, tm, D_POOL), tok),
            pl.BlockSpec((None, POOL_HALO, D_POOL), halo_before),
            pl.BlockSpec((None, POOL_HALO, D_POOL), halo_after),
            pl.BlockSpec((None, tm, D_ATTN), tok),
            _const_spec((D_MODEL, 2 * D_MODEL)),
            _const_spec((len(POOL_WINDOWS), POOL_GC, POOL_GC)),
            _const_spec((1, D_POOL)),
            _const_spec((D_POOL, D_MODEL)),
            _const_spec((D_ATTN, D_MODEL)),
            _const_spec((D_MODEL, D_MODEL)),
            _const_spec((D_MODEL, 2 * LANES)),
            _const_spec((1, LANES)),
            _const_spec((tm, tm)),
        ],
        out_specs=[
            pl.BlockSpec((None, tm, D_MODEL), tok),
            pl.BlockSpec((tm, D_PACK), flat),
            pl.BlockSpec((META_ROWS, tm), lambda s: (0, routed(s))),
            pl.BlockSpec((tm, LANES), flat),
            pl.BlockSpec((N_EXPERTS, LANES), lambda s: (0, 0)),
        ],
        out_shape=[
            jax.ShapeDtypeStruct((B, T, D_MODEL), F32),
            jax.ShapeDtypeStruct((B * T, D_PACK), I32),
            jax.ShapeDtypeStruct((META_ROWS, B * T), I32),
            jax.ShapeDtypeStruct((B * T, LANES), F32),
            jax.ShapeDtypeStruct((N_EXPERTS, LANES), F32),
        ],
        scratch_shapes=[pltpu.VMEM((tm + 3 * POOL_HALO, D_POOL), F32),
                        pltpu.VMEM((2, tm + 3 * POOL_HALO, POOL_GC), F32),
                        pltpu.VMEM((tm, D_POOL), BF16),
                        pltpu.VMEM((N_EXPERTS, 1), F32),
                        pltpu.VMEM((tm, D_MODEL), BF16),
                        pltpu.VMEM((tm, D_MODEL), F32)],
        compiler_params=pltpu.CompilerParams(
            dimension_semantics=("arbitrary",), vmem_limit_bytes=V7X_VMEM_LIMIT),
        name="stage_c",
    )(x, mod, mod, n1g, n2g, u, u, u, attn, w_gate, w_mix, pool_scale, w_pp, w_ap, w_o, w_r, b_r, utri)


def _sc_workers():
    info = plsc.get_sparse_core_info()
    return info.num_cores, info.num_subcores


def _sc_scatter_rows(rows, idx0, idx1, n_out):
    n, d = rows.shape
    nc, ns = _sc_workers()
    per_worker = n // (nc * ns)
    n_win = per_worker // SC_WINDOW
    mesh = plsc.VectorSubcoreMesh(core_axis_name="c", subcore_axis_name="s")

    @functools.partial(
        pl.kernel, mesh=mesh,
        out_type=jax.ShapeDtypeStruct((n_out, d), rows.dtype),
        scratch_types=[pltpu.VMEM((SC_WINDOW,), I32), pltpu.VMEM((SC_WINDOW,), I32),
                       pltpu.VMEM((SC_WINDOW, d), rows.dtype)],
        name="sc_scatter_rows",
    )
    def scatter(rows_hbm, idx0_hbm, idx1_hbm, out_hbm, i0_v, i1_v, rows_v):
        wid = lax.axis_index("s") * nc + lax.axis_index("c")

        @pl.loop(0, n_win)
        def _(w):
            base = wid * per_worker + w * SC_WINDOW
            pltpu.sync_copy(rows_hbm.at[pl.ds(base, SC_WINDOW)], rows_v)
            pltpu.sync_copy(idx0_hbm.at[pl.ds(base, SC_WINDOW)], i0_v)
            pltpu.sync_copy(idx1_hbm.at[pl.ds(base, SC_WINDOW)], i1_v)
            pltpu.sync_copy(rows_v, out_hbm.at[i0_v])
            pltpu.sync_copy(rows_v, out_hbm.at[i1_v])

    return scatter(rows, idx0, idx1)


def _sc_gather_rows(table, idx):
    n = idx.shape[0]
    d = table.shape[1]
    nc, ns = _sc_workers()
    per_worker = n // (nc * ns)
    n_win = per_worker // SC_WINDOW
    mesh = plsc.VectorSubcoreMesh(core_axis_name="c", subcore_axis_name="s")

    @functools.partial(
        pl.kernel, mesh=mesh,
        out_type=jax.ShapeDtypeStruct((n, d), table.dtype),
        scratch_types=[pltpu.VMEM((SC_WINDOW,), I32), pltpu.VMEM((SC_WINDOW, d), table.dtype)],
        name="sc_gather_rows",
    )
    def gather(table_hbm, idx_hbm, out_hbm, i_v, rows_v):
        wid = lax.axis_index("s") * nc + lax.axis_index("c")

        @pl.loop(0, n_win)
        def _(w):
            base = wid * per_worker + w * SC_WINDOW
            pltpu.sync_copy(idx_hbm.at[pl.ds(base, SC_WINDOW)], i_v)
            pltpu.sync_copy(table_hbm.at[i_v], rows_v)
            pltpu.sync_copy(rows_v, out_hbm.at[pl.ds(base, SC_WINDOW)])

    return gather(table, idx)


def _moe_kernel(te_ref, tv_ref, nl_ref, x_hbm, wgu_ref, wd_ref, o_ref, xbuf, sem):
    del te_ref
    i = pl.program_id(0)
    n_live_tiles = nl_ref[0]
    valid = tv_ref[i]
    tm = o_ref.shape[0]
    slot = _input_ring(x_hbm, xbuf, sem, i, n_live_tiles, tm)

    group = min(STAGE_ROWS, tm)
    n_groups = tm // group

    def gate_up(r0):
        live = lax.broadcasted_iota(I32, (group, 1), 0) < valid - r0
        lo, hi = _unpack_bf16_pair(jnp.where(live, xbuf[slot, r0:r0 + group, :], 0))
        x = jnp.concatenate([lo, hi], axis=1).astype(BF16)
        return jnp.dot(x, wgu_ref[...], preferred_element_type=F32)

    def down(gu):
        a = gu[:, :D_EXPERT]
        hmid = (a * _sigmoid(a) * gu[:, D_EXPERT:]).astype(BF16)
        return jnp.dot(hmid, wd_ref[...], preferred_element_type=F32)

    def store(r0, y):
        o_ref[r0:r0 + group, :] = _pack_bf16_pair(y[:, :D_PACK], y[:, D_PACK:])

    for n_live in range(n_groups + 1):
        @pl.when((valid > (n_live - 1) * group) & (valid <= n_live * group))
        def _():
            gu, y = None, None
            for g in range(n_live + 2):
                nxt = gate_up(g * group) if g < n_live else None
                if y is not None:
                    store((g - 2) * group, y)
                y = down(gu) if gu is not None else None
                gu = nxt
            if n_live < n_groups:
                o_ref[n_live * group:, :] = jnp.zeros((tm - n_live * group, D_PACK), I32)


def _moe(xs, tile_expert, tile_valid, n_live_tiles, w_gu, w_d, tm):
    n_tiles = xs.shape[0] // tm
    grid_spec = pltpu.PrefetchScalarGridSpec(
        num_scalar_prefetch=3,
        grid=(n_tiles,),
        in_specs=[
            pl.BlockSpec(memory_space=pl.ANY),
            pl.BlockSpec((None, D_MODEL, 2 * D_EXPERT), lambda i, te, tv, nl: (te[i], 0, 0)),
            pl.BlockSpec((None, D_EXPERT, D_MODEL), lambda i, te, tv, nl: (te[i], 0, 0)),
        ],
        out_specs=pl.BlockSpec((tm, D_PACK), lambda i, te, tv, nl: (jnp.minimum(i, nl[0]), 0)),
        scratch_shapes=[pltpu.VMEM((RING_SLOTS, tm, D_PACK), I32), pltpu.SemaphoreType.DMA((RING_SLOTS,))],
    )
    return pl.pallas_call(
        _moe_kernel,
        grid_spec=grid_spec,
        out_shape=jax.ShapeDtypeStruct(xs.shape, I32),
        compiler_params=pltpu.CompilerParams(
            dimension_semantics=("arbitrary",), vmem_limit_bytes=V7X_VMEM_LIMIT),
        name="experts",
    )(tile_expert, tile_valid, n_live_tiles, xs, w_gu, w_d)


def _final_kernel(x1_ref, y0_ref, y1_ref, rw_ref, mod_ref, fg_ref, o_ref):
    y0lo, y0hi = _unpack_bf16_pair(y0_ref[...])
    y1lo, y1hi = _unpack_bf16_pair(y1_ref[...])
    w0 = rw_ref[:, 0:1]
    w1 = rw_ref[:, 1:2]
    moe = jnp.concatenate([w0 * y0lo + w1 * y1lo, w0 * y0hi + w1 * y1hi], axis=1)
    x2 = x1_ref[...] + mod_ref[5:6, :] * moe
    ms = jnp.mean(x2 * x2, axis=-1, keepdims=True)
    o_ref[...] = x2 * lax.rsqrt(ms + EPS) * fg_ref[...]


def _final(x1, yb, rw, mod, final_g, tm):
    B, T, _ = x1.shape
    n_blocks = B * T // tm
    flat = lambda b, i: (b * (T // tm) + i, 0)
    return pl.pallas_call(
        _final_kernel,
        grid=(B, T // tm),
        in_specs=[
            pl.BlockSpec((None, tm, D_MODEL), lambda b, i: (b, i, 0)),
            pl.BlockSpec((tm, D_PACK), flat),
            pl.BlockSpec((tm, D_PACK), lambda b, i: (n_blocks + b * (T // tm) + i, 0)),
            pl.BlockSpec((tm, LANES), flat),
            pl.BlockSpec((None, 6, D_MODEL), lambda b, i: (b, 0, 0)),
            _const_spec((1, D_MODEL)),
        ],
        out_specs=pl.BlockSpec((None, tm, D_MODEL), lambda b, i: (b, i, 0)),
        out_shape=jax.ShapeDtypeStruct((B, T, D_MODEL), F32),
        compiler_params=pltpu.CompilerParams(
            dimension_semantics=("parallel", "parallel"), vmem_limit_bytes=V7X_VMEM_LIMIT),
        name="final_combine",
    )(x1, yb, yb, rw, mod, final_g)


def _rope_tables(T):
    rows = T // GRID_W
    row = np.repeat(np.arange(rows, dtype=np.float32), GRID_W)
    col = np.tile(np.arange(GRID_W, dtype=np.float32), rows)
    inv_freq = np.float32(ROPE_THETA) ** (-np.arange(0, 64, 2, dtype=np.float32) / np.float32(64))
    ang_r = row[:, None] * inv_freq[None, :]
    ang_c = col[:, None] * inv_freq[None, :]
    ang = np.concatenate([ang_r, ang_r, ang_c, ang_c], axis=-1).astype(np.float32)
    sign = np.where((np.arange(HEAD_DIM) % 64) < 32, -1.0, 1.0).astype(np.float32)
    return jnp.asarray(np.cos(ang)), jnp.asarray(np.sin(ang) * sign[None, :])


def _dispatch_plan(counts, tm, n_tiles):
    tiles_per_e = (counts + tm - 1) // tm
    tile_end = jnp.cumsum(tiles_per_e)
    tile_start = tile_end - tiles_per_e
    tile_id = jnp.arange(n_tiles, dtype=I32)
    te = jnp.minimum(jnp.sum((tile_id[:, None] >= tile_end[None, :]).astype(I32), axis=1), N_EXPERTS - 1)
    live = jnp.clip(counts[te] - (tile_id - tile_start[te]) * tm, 0, tm)
    tv = jnp.where(tile_id < tile_end[-1], live, 0).astype(I32)
    return tile_start.astype(I32), te, tv, tile_end[-1:].astype(I32)


def _slot_kernel(start_ref, meta_ref, pos_ref, *, tile_rows):
    e = meta_ref[0:2, :]
    first_tile = jnp.zeros_like(e)
    for ex in range(N_EXPERTS):
        first_tile = jnp.where(e == ex, start_ref[ex], first_tile)
    pos_ref[...] = first_tile * tile_rows + meta_ref[2:4, :]


def _slots(meta, tile_start, tile_rows):
    n = meta.shape[1]
    tn = min(n, 8192)
    grid_spec = pltpu.PrefetchScalarGridSpec(
        num_scalar_prefetch=1,
        grid=(n // tn,),
        in_specs=[pl.BlockSpec((META_ROWS, tn), lambda i, start: (0, i))],
        out_specs=pl.BlockSpec((2, tn), lambda i, start: (0, i)),
    )
    return pl.pallas_call(
        functools.partial(_slot_kernel, tile_rows=tile_rows),
        grid_spec=grid_spec,
        out_shape=jax.ShapeDtypeStruct((2, n), I32),
        name="dispatch_slots",
    )(tile_start, meta)


def _pick_tile(T, want):
    t = min(T, want)
    assert T % t == 0
    return t


def _trunk(x, c, p):
    B, T, _ = x.shape
    N = B * T
    tm = _pick_tile(T, 512)
    tq = _pick_tile(T, 512)
    tk = _pick_tile(T, 256)
    tme = _pick_tile(N, 512)
    cos, sin_signed = _rope_tables(T)
    mod = _modulation(c, p["w_ada"], p["b_ada"]).reshape(B, 6, D_MODEL)
    u, qt, qb, k, vt = _stage_a(x, mod, p["n1g"], p["w_qkvu"], p["qg"], p["kg"], cos, sin_signed, tm)
    attn = lax.cond(
        p["bounded_softmax_ok"],
        functools.partial(_attention, tq=tq, tk=_pick_tile(T, 2048), running_max=False),
        functools.partial(_attention, tq=tq, tk=tk, running_max=True),
        qt, qb, k, vt)
    x1, h2p, meta, rw, counts = _stage_c(x, mod, p["n1g"], p["n2g"], u, attn, p["w_gate"], p["w_mix"],
                                         p["pool_scale"], p["w_pp"], p["w_ap"], p["w_o"], p["w_r"], p["b_r"], tm)
    n_tiles = 2 * N // tme + N_EXPERTS
    tile_start, te, tv, n_live_tiles = _dispatch_plan(counts[:, 0].astype(I32), tme, n_tiles)
    pos = _slots(meta, tile_start, tme)
    xs = _sc_scatter_rows(h2p, pos[0], pos[1], n_tiles * tme)
    ys = _moe(xs, te, tv, n_live_tiles, p["w_egu"], p["w_ed"], tme)
    yb = _sc_gather_rows(ys, pos.reshape(-1))
    return _final(x1, yb, rw, mod, p["final_g"], _pick_tile(T, 1024))


def kernel(x_prompt, x_sample, c_prompt, c_sample, w_ada, b_ada, norm1_g, norm2_g, w_in, q_norm_g,
           k_norm_g, w_pool_mix, pool_scale, w_pool_proj, w_attn_proj, w_o, w_router_group,
           b_router_group, w_router_expert, b_router_expert, w_exp_gate, w_exp_up, w_exp_down, final_g):
    assert w_ada.shape[0] == 1, "single-layer block"
    n_r = N_GROUPS + N_EXPERTS
    w_r = jnp.concatenate([w_router_expert[0], w_router_group[0],
                           jnp.zeros((D_MODEL, LANES - n_r), F32)], axis=1)
    w_r_hi = w_r.astype(BF16)
    b_r = jnp.concatenate([b_router_expert[0], b_router_group[0], jnp.zeros((LANES - n_r,), F32)])
    score_bound = (1.01 * LOG2E * np.sqrt(HEAD_DIM)) * jnp.max(jnp.abs(q_norm_g[0])) * jnp.max(jnp.abs(k_norm_g[0]))
    p = dict(
        bounded_softmax_ok=2.0 * score_bound <= EXP2_SAFE_SPAN,
        w_ada=w_ada[0], b_ada=b_ada[0],
        n1g=norm1_g[0].reshape(1, D_MODEL), n2g=norm2_g[0].reshape(1, D_MODEL),
        w_qkvu=w_in[0][:, :D_QKVU].astype(BF16), w_gate=w_in[0][:, D_QKVU:].astype(BF16),
        qg=q_norm_g[0].reshape(1, HEAD_DIM), kg=k_norm_g[0].reshape(1, HEAD_DIM),
        w_mix=w_pool_mix[0].astype(BF16), pool_scale=pool_scale[0].reshape(1, D_POOL),
        w_pp=w_pool_proj[0].astype(BF16), w_ap=w_attn_proj[0].astype(BF16), w_o=w_o[0].astype(BF16),
        w_r=jnp.concatenate([w_r_hi, (w_r - w_r_hi.astype(F32)).astype(BF16)], axis=1), b_r=b_r.reshape(1, LANES),
        w_egu=jnp.concatenate([w_exp_gate[0], w_exp_up[0]], axis=-1).astype(BF16), w_ed=w_exp_down[0].astype(BF16),
        final_g=final_g.reshape(1, D_MODEL),
    )
    return _trunk(x_prompt, c_prompt, p), _trunk(x_sample, c_sample, p)
```

```python
import functools

import numpy as np
import jax
import jax.numpy as jnp
from jax import lax
from jax.experimental import pallas as pl
from jax.experimental.pallas import tpu as pltpu
from jax.experimental.pallas import tpu_sc as plsc

F32 = jnp.float32
BF16 = jnp.bfloat16
I32 = jnp.int32

D_MODEL = 1024
GRID_W = 64
POOL_WINDOWS = (2, 4, 8, 16)
POOL_GC = 128
D_POOL = 512
HEAD_DIM = 128
N_Q_HEADS = 4
N_KV_HEADS = 2
D_ATTN = 512
D_KV = 256
ROPE_THETA = 10000.0
N_GROUPS = 4
EXPERTS_PER_GROUP = 8
N_EXPERTS = 32
D_EXPERT = 256
EPS = 1e-6
LOG2E = 1.4426950408889634
EXP2_SAFE_SPAN = 100.0
D_QKVU = D_POOL + D_ATTN + 2 * D_KV
POOL_HALO = 16
STAGE_ROWS = 256
PROJ_COLS = 256
ROUTED_PHASE_AFTER_BLOCK = (0, 1, 2)
MERGE_COLS = 256
RING_SLOTS = 3
ATTN_TILES_PER_TRIP = 10
ROUTE_ROWS = 40
META_ROWS = 8
LANES = 128
D_PACK = D_MODEL // 2

V7X_VMEM_LIMIT = 56 * 1024 * 1024
SC_WINDOW = 64


def _const_spec(shape):
    nd = len(shape)
    return pl.BlockSpec(shape, lambda *_: (0,) * nd, pipeline_mode=pl.Buffered(1))


def _sigmoid(x):
    return 0.5 * jnp.tanh(0.5 * x) + 0.5


def _pack_bf16_pair(lo, hi):
    lo_bits = lax.bitcast_convert_type(lo.astype(BF16).astype(F32), jnp.uint32) >> 16
    hi_bits = lax.bitcast_convert_type(hi.astype(BF16).astype(F32), jnp.uint32) & jnp.uint32(0xFFFF0000)
    return lax.bitcast_convert_type(lo_bits | hi_bits, I32)


def _unpack_bf16_pair(packed):
    u = lax.bitcast_convert_type(packed, jnp.uint32)
    lo = lax.bitcast_convert_type(u << 16, F32)
    hi = lax.bitcast_convert_type(u & jnp.uint32(0xFFFF0000), F32)
    return lo, hi


def _input_ring(x_hbm, buf, sem, step, n_steps, rows):
    def fetch(s):
        slot = lax.rem(s, RING_SLOTS)
        src = x_hbm.at[pl.ds(pl.multiple_of(s * rows, rows), rows), :]
        return pltpu.make_async_copy(src, buf.at[slot], sem.at[slot])

    @pl.when((step == 0) & (n_steps > 0))
    def _():
        fetch(0).start()

    @pl.when((step == 0) & (n_steps > 1))
    def _():
        fetch(1).start()

    @pl.when(step + 2 < n_steps)
    def _():
        fetch(step + 2).start()

    @pl.when(step < n_steps)
    def _():
        fetch(step).wait()

    return lax.rem(step, RING_SLOTS)


def _mod_kernel(c_ref, w_ref, b_ref, o_ref):
    c = c_ref[...]
    s = c * _sigmoid(c)
    o_ref[...] = jnp.dot(s, w_ref[...], preferred_element_type=F32,
                         precision=lax.Precision.HIGHEST) + b_ref[...]


def _modulation(c, w_ada, b_ada):
    B = c.shape[0]
    n_out = w_ada.shape[1]
    bn = D_MODEL
    return pl.pallas_call(
        _mod_kernel,
        grid=(n_out // bn,),
        in_specs=[pl.BlockSpec((B, D_MODEL), lambda j: (0, 0)),
                  pl.BlockSpec((D_MODEL, bn), lambda j: (0, j)),
                  pl.BlockSpec((1, bn), lambda j: (0, j))],
        out_specs=pl.BlockSpec((B, bn), lambda j: (0, j)),
        out_shape=jax.ShapeDtypeStruct((B, n_out), F32),
        name="modulation",
    )(c, w_ada, b_ada.reshape(1, n_out))


def _scaled_norm(x, gain, shift):
    ms = jnp.mean(x * x, axis=-1, keepdims=True)
    return x * lax.rsqrt(ms + EPS) * gain + shift


def _stage_a_kernel(x_hbm, mod_ref, n1g_ref, w_ref, qg_ref, kg_ref, cos_ref, sin_ref,
                    u_ref, qt_ref, qb_ref, k_ref, vt_ref, xbuf, sem, z_ref):
    s = pl.program_id(0)
    n_tiles = pl.num_programs(0) - 1
    tm = u_ref.shape[0]
    _input_ring(x_hbm, xbuf, sem, s, n_tiles, tm)
    slot = lax.rem(jnp.minimum(s, n_tiles - 1), RING_SLOTS)

    @pl.when(s == 0)
    def _():
        z_ref[...] = jnp.zeros_like(z_ref)

    gain1 = n1g_ref[...] * (1.0 + mod_ref[1:2, :])
    shift1 = mod_ref[0:1, :]
    lane = lax.broadcasted_iota(I32, (1, HEAD_DIM), 1)
    first_half = (lane % 64) < 32
    qg = qg_ref[...] * (LOG2E / np.sqrt(HEAD_DIM))
    k_norm_max = jnp.max(jnp.abs(kg_ref[...]), axis=-1, keepdims=True) * (np.sqrt(HEAD_DIM) * 1.01)
    cos = cos_ref[...]
    sin_signed = sin_ref[...]

    def norm_rope(xh, g):
        ms = jnp.mean(xh * xh, axis=-1, keepdims=True)
        xn = xh * lax.rsqrt(ms + EPS) * g
        rot = jnp.where(first_half, pltpu.roll(xn, 96, 1), pltpu.roll(xn, 32, 1))
        return xn * cos + rot * sin_signed

    def finish(o):
        zh = z_ref[:, o:o + HEAD_DIM]
        if o < D_POOL:
            u_ref[:, o:o + HEAD_DIM] = zh.astype(BF16)
        elif o < D_POOL + D_ATTN:
            hq = (o - D_POOL) // HEAD_DIM
            q_t = norm_rope(zh, qg).T.astype(BF16)
            qt_ref[hq] = q_t
            q_f = q_t.astype(F32)
            qb_ref[hq] = jnp.sqrt(jnp.sum(q_f * q_f, axis=0, keepdims=True)) * k_norm_max
        elif o < D_POOL + D_ATTN + D_KV:
            k_ref[:, o - D_POOL - D_ATTN:o - D_POOL - D_ATTN + HEAD_DIM] = norm_rope(zh, kg_ref[...]).astype(BF16)
        else:
            vt_ref[(o - D_POOL - D_ATTN - D_KV) // HEAD_DIM] = zh.T.astype(BF16)

    h = _scaled_norm(xbuf[slot], gain1, shift1).astype(BF16)
    for c in range(0, D_QKVU, PROJ_COLS):
        for o in range(c, c + PROJ_COLS, HEAD_DIM):
            finish(o)
        z_ref[:, c:c + PROJ_COLS] = jnp.dot(h, w_ref[:, c:c + PROJ_COLS], preferred_element_type=F32)


def _stage_a(x, mod, n1g, w_qkvu, qg, kg, cos, sin_signed, tm):
    B, T, _ = x.shape
    nt = T // tm
    n_tiles = B * nt
    projected = lambda s: jnp.minimum(s, n_tiles - 1)
    finished = lambda s: jnp.maximum(s - 1, 0)
    return pl.pallas_call(
        _stage_a_kernel,
        grid=(n_tiles + 1,),
        in_specs=[
            pl.BlockSpec(memory_space=pl.ANY),
            pl.BlockSpec((None, 6, D_MODEL), lambda s: (projected(s) // nt, 0, 0)),
            _const_spec((1, D_MODEL)),
            _const_spec((D_MODEL, D_QKVU)),
            _const_spec((1, HEAD_DIM)),
            _const_spec((1, HEAD_DIM)),
            pl.BlockSpec((tm, HEAD_DIM), lambda s: (lax.rem(finished(s), nt), 0)),
            pl.BlockSpec((tm, HEAD_DIM), lambda s: (lax.rem(finished(s), nt), 0)),
        ],
        out_specs=[
            pl.BlockSpec((None, tm, D_POOL), lambda s: (finished(s) // nt, lax.rem(finished(s), nt), 0)),
            pl.BlockSpec((None, N_Q_HEADS, HEAD_DIM, tm), lambda s: (finished(s) // nt, 0, 0, lax.rem(finished(s), nt))),
            pl.BlockSpec((None, N_Q_HEADS, 1, tm), lambda s: (finished(s) // nt, 0, 0, lax.rem(finished(s), nt))),
            pl.BlockSpec((None, tm, D_KV), lambda s: (finished(s) // nt, lax.rem(finished(s), nt), 0)),
            pl.BlockSpec((None, N_KV_HEADS, HEAD_DIM, tm), lambda s: (finished(s) // nt, 0, 0, lax.rem(finished(s), nt))),
        ],
        out_shape=[
            jax.ShapeDtypeStruct((B, T, D_POOL), BF16),
            jax.ShapeDtypeStruct((B, N_Q_HEADS, HEAD_DIM, T), BF16),
            jax.ShapeDtypeStruct((B, N_Q_HEADS, 1, T), F32),
            jax.ShapeDtypeStruct((B, T, D_KV), BF16),
            jax.ShapeDtypeStruct((B, N_KV_HEADS, HEAD_DIM, T), BF16),
        ],
        scratch_shapes=[pltpu.VMEM((RING_SLOTS, tm, D_MODEL), F32), pltpu.SemaphoreType.DMA((RING_SLOTS,)),
                        pltpu.VMEM((tm, D_QKVU), F32)],
        compiler_params=pltpu.CompilerParams(
            dimension_semantics=("arbitrary",), vmem_limit_bytes=V7X_VMEM_LIMIT),
        name="stage_a",
    )(x.reshape(B * T, D_MODEL), mod, n1g, w_qkvu, qg, kg, cos, sin_signed)


def _write_attn_output(o_ref, acc, l, tq):
    o = acc * (1.0 / l)
    o_ref[:, :HEAD_DIM] = o[:, :tq].T.astype(BF16)
    o_ref[:, HEAD_DIM:] = o[:, tq:].T.astype(BF16)


def _attn_bounded_kernel(qt_ref, qb_ref, k_ref, vt_ref, o_ref, p_ref, acc_ref, l_ref, *, tk):
    T = k_ref.shape[0]
    tq = qt_ref.shape[2]
    nq = 2 * tq
    nk = T // tk
    qt = jnp.concatenate([qt_ref[0], qt_ref[1]], axis=1)
    qb = jnp.concatenate([qb_ref[0], qb_ref[1]], axis=1)

    def weights(j, slot):
        off = pl.multiple_of(j * tk, tk)
        st = jnp.dot(k_ref[pl.ds(off, tk), :], qt, preferred_element_type=F32)
        p = jnp.exp2(st - qb)
        l_ref[...] += jnp.sum(p.reshape(tk // 8, 8, nq), axis=0)
        p_ref[slot] = p.astype(BF16)

    def values(j, slot):
        off = pl.multiple_of(j * tk, tk)
        acc_ref[...] += jnp.dot(vt_ref[:, pl.ds(off, tk)], p_ref[slot], preferred_element_type=F32)

    def step(j, slot):
        values(j - 1, 1 - slot)
        weights(j, slot)

    acc_ref[...] = jnp.zeros_like(acc_ref)
    l_ref[...] = jnp.zeros_like(l_ref)
    weights(0, 0)

    unroll = max(1, min(ATTN_TILES_PER_TRIP, nk - 1))
    trips = (nk - 1) // unroll if unroll % 2 == 0 else 0

    if trips > 1:
        @pl.loop(0, trips)
        def _(g):
            for d in range(unroll):
                step(1 + unroll * g + d, (1 + d) % 2)
    else:
        trips = 0
    for j in range(1 + trips * unroll, nk):
        step(j, j % 2)
    values(nk - 1, (nk - 1) % 2)
    _write_attn_output(o_ref, acc_ref[...], jnp.sum(l_ref[...], axis=0, keepdims=True), tq)


def _attn_online_kernel(qt_ref, qb_ref, k_ref, vt_ref, o_ref, *, tk):
    del qb_ref
    T = k_ref.shape[0]
    tq = qt_ref.shape[2]
    nq = 2 * tq
    qt = jnp.concatenate([qt_ref[0], qt_ref[1]], axis=1)

    def body(s, carry):
        m, l, acc = carry
        off = pl.multiple_of(s * tk, tk)
        st = jnp.dot(k_ref[pl.ds(off, tk), :], qt, preferred_element_type=F32)
        m_new = jnp.maximum(m, jnp.max(st, axis=0, keepdims=True))
        alpha = jnp.exp2(m - m_new)
        p = jnp.exp2(st - m_new)
        l = alpha * l + jnp.sum(p, axis=0, keepdims=True)
        pv = jnp.dot(vt_ref[:, pl.ds(off, tk)], p.astype(BF16), preferred_element_type=F32)
        return m_new, l, alpha * acc + pv

    init = (jnp.full((1, nq), -jnp.inf, F32), jnp.zeros((1, nq), F32), jnp.zeros((HEAD_DIM, nq), F32))
    _, l, acc = lax.fori_loop(0, T // tk, body, init)
    _write_attn_output(o_ref, acc, l, tq)


def _attention(qt, qb, k, vt, tq, tk, running_max):
    B, _, _, T = qt.shape
    group = N_Q_HEADS // N_KV_HEADS
    nq = group * tq
    if running_max:
        body, scratch, name = functools.partial(_attn_online_kernel, tk=tk), [], "attention_online_max"
    else:
        body = functools.partial(_attn_bounded_kernel, tk=tk)
        scratch = [pltpu.VMEM((2, tk, nq), BF16), pltpu.VMEM((HEAD_DIM, nq), F32), pltpu.VMEM((8, nq), F32)]
        name = "attention_bounded"
    return pl.pallas_call(
        body,
        grid=(B, N_KV_HEADS, T // tq),
        in_specs=[
            pl.BlockSpec((None, group, HEAD_DIM, tq), lambda b, j, i: (b, j, 0, i)),
            pl.BlockSpec((None, group, 1, tq), lambda b, j, i: (b, j, 0, i)),
            pl.BlockSpec((None, T, HEAD_DIM), lambda b, j, i: (b, 0, j)),
            pl.BlockSpec((None, None, HEAD_DIM, T), lambda b, j, i: (b, j, 0, 0)),
        ],
        out_specs=pl.BlockSpec((None, tq, group * HEAD_DIM), lambda b, j, i: (b, i, j)),
        out_shape=jax.ShapeDtypeStruct((B, T, D_ATTN), BF16),
        scratch_shapes=scratch,
        compiler_params=pltpu.CompilerParams(
            dimension_semantics=("parallel", "parallel", "parallel"),
            vmem_limit_bytes=V7X_VMEM_LIMIT),
        name=name,
    )(qt, qb, k, vt)


def _route(logits_t):
    neg = jnp.float32(-jnp.inf)
    n = logits_t.shape[1]

    def top(vals, idx, far):
        v = jnp.max(vals, axis=0, keepdims=True)
        i = jnp.min(jnp.where(vals == v, idx, far), axis=0, keepdims=True)
        return v, i

    gidx = lax.broadcasted_iota(I32, (ROUTE_ROWS - N_EXPERTS, n), 0).astype(F32)
    grp = jnp.where(gidx < N_GROUPS, logits_t[N_EXPERTS:, :], neg)
    gmax, gsel = top(grp, gidx, jnp.float32(ROUTE_ROWS))
    p_group = 1.0 / jnp.sum(jnp.exp(grp - gmax), axis=0, keepdims=True)
    eidx = lax.broadcasted_iota(I32, (N_EXPERTS, n), 0).astype(F32)
    first = EXPERTS_PER_GROUP * gsel
    le = jnp.where((eidx >= first) & (eidx < first + EXPERTS_PER_GROUP), logits_t[:N_EXPERTS, :], neg)
    v1, e1 = top(le, eidx, jnp.float32(N_EXPERTS))
    v2, e2 = top(jnp.where(eidx == e1, neg, le), eidx, jnp.float32(N_EXPERTS))
    e21 = jnp.exp(v2 - v1)
    w1 = p_group / (1.0 + e21)
    return e1, e2, w1, w1 * e21


def _pool_deviation(ue_ref, lvl_ref, pool_ref, tile_start, tm, seq_len):
    n = tm + 2 * POOL_HALO
    t = tile_start + lax.broadcasted_iota(I32, (tm, 1), 0)
    for gi, w in enumerate(POOL_WINDOWS):
        half = w // 2
        cols = slice(gi * POOL_GC, (gi + 1) * POOL_GC)
        level = ue_ref[0:n, cols] + ue_ref[1:n + 1, cols]
        span, buf = 2, 0
        lvl_ref[buf, 0:n, :] = level
        while span < w:
            level = lvl_ref[buf, 0:n, :] + lvl_ref[buf, span:span + n, :]
            span, buf = 2 * span, 1 - buf
            lvl_ref[buf, 0:n, :] = level
        wsum = lvl_ref[buf, POOL_HALO - half:POOL_HALO - half + tm, :]
        cnt = (jnp.minimum(t + half, seq_len) - jnp.maximum(t - half, 0)).astype(F32)
        p = wsum * (1.0 / cnt) - ue_ref[POOL_HALO:POOL_HALO + tm, cols]
        pool_ref[:, cols] = p.astype(BF16)


def _norm2_split(x1, gain2, shift2, h2_ref):
    h2 = _scaled_norm(x1, gain2, shift2)
    h2_ref[...] = _pack_bf16_pair(h2[:, :D_PACK], h2[:, D_PACK:])
    h_hi = h2.astype(BF16)
    return h_hi, (h2 - h_hi.astype(F32)).astype(BF16)


def _router_logits(h_hi, h_lo, wr_ref, br_ref):
    parts = (jnp.dot(h_hi, wr_ref[...], preferred_element_type=F32)
             + jnp.dot(h_lo, wr_ref[...], preferred_element_type=F32))
    return parts[:, :LANES] + parts[:, LANES:] + br_ref[...]


def _dispatch_table(logits, is_real, utri_ref, run_ref, meta_ref, rw_ref):
    n = logits.shape[0]
    e1, e2, w1, w2 = _route(logits.T[0:ROUTE_ROWS, :])

    eidx = lax.broadcasted_iota(I32, (N_EXPERTS, n), 0).astype(F32)
    hit1 = eidx == e1
    hit2 = eidx == e2
    taken = jnp.where(hit1 | hit2, is_real, 0.0)
    before = run_ref[...] + jnp.dot(taken.astype(BF16), utri_ref[...], preferred_element_type=F32)
    rank1 = jnp.sum(jnp.where(hit1, before, 0.0), axis=0, keepdims=True)
    rank2 = jnp.sum(jnp.where(hit2, before, 0.0), axis=0, keepdims=True)
    run_ref[...] = run_ref[...] + jnp.sum(taken, axis=1, keepdims=True)
    ridx = lax.broadcasted_iota(I32, (META_ROWS, n), 0)
    table = jnp.where(ridx == 0, e1, jnp.where(ridx == 1, e2, jnp.where(ridx == 2, rank1,
                      jnp.where(ridx == 3, rank2, 0.0))))
    meta_ref[...] = table.astype(I32)
    weights_t = jnp.where(ridx == 0, w1, jnp.where(ridx == 1, w2, 0.0))
    rw_ref[...] = jnp.concatenate([weights_t, jnp.zeros((LANES - META_ROWS, n), F32)], axis=0).T


def _stage_c_kernel(x_ref, mod_ref, modp_ref, n1g_ref, n2g_ref, u_ref, up_ref, un_ref, attn_ref,
                    wgate_ref, wmix_ref, ps_ref, wpp_ref, wap_ref, wo_ref, wr_ref, br_ref, utri_ref,
                    x1_ref, h2_ref, meta_ref, rw_ref, count_ref,
                    ue_ref, lvl_ref, pool_ref, run_ref, merged_ref, x1_prev, *, seq_len):
    s = pl.program_id(0)
    tm = x_ref.shape[0]
    tiles_per_seq = seq_len // tm
    i = lax.rem(jnp.minimum(s, pl.num_programs(0) - 2), tiles_per_seq)

    @pl.when(s == 0)
    def _():
        run_ref[...] = jnp.zeros_like(run_ref)
        x1_prev[...] = jnp.zeros_like(x1_prev)

    gain1 = n1g_ref[...] * (1.0 + mod_ref[1:2, :])
    shift1 = mod_ref[0:1, :]
    res_gate = mod_ref[2:3, :]
    ue_ref[0:POOL_HALO, :] = up_ref[...].astype(F32) * (i > 0).astype(F32)
    ue_ref[POOL_HALO:POOL_HALO + tm, :] = u_ref[...].astype(F32)
    ue_ref[POOL_HALO + tm:2 * POOL_HALO + tm, :] = un_ref[...].astype(F32) * (i < tiles_per_seq - 1).astype(F32)
    ue_ref[2 * POOL_HALO + tm:, :] = jnp.zeros((POOL_HALO, D_POOL), F32)
    lvl_ref[:, 2 * POOL_HALO + tm:, :] = jnp.zeros((2, POOL_HALO, POOL_GC), F32)
    _pool_deviation(ue_ref, lvl_ref, pool_ref, i * tm, tm, seq_len)

    x = x_ref[...]
    h = _scaled_norm(x, gain1, shift1).astype(BF16)
    mixed_groups = []
    for gi in range(len(POOL_WINDOWS)):
        cols = slice(gi * POOL_GC, (gi + 1) * POOL_GC)
        pm = jnp.dot(pool_ref[:, cols], wmix_ref[gi], preferred_element_type=F32)
        mixed_groups.append(pm * ps_ref[:, cols])
    pool = jnp.concatenate(mixed_groups, axis=1).astype(BF16)
    attn = attn_ref[...]
    assert max(ROUTED_PHASE_AFTER_BLOCK) < D_MODEL // MERGE_COLS
    for c in range(0, D_MODEL, MERGE_COLS):
        cols = slice(c, c + MERGE_COLS)
        gate_cols = slice(D_MODEL + c, D_MODEL + c + MERGE_COLS)
        gate_pool = _sigmoid(jnp.dot(h, wgate_ref[:, cols], preferred_element_type=F32))
        gate_attn = _sigmoid(jnp.dot(h, wgate_ref[:, gate_cols], preferred_element_type=F32))
        merged = (gate_pool * jnp.dot(pool, wpp_ref[:, cols], preferred_element_type=F32)
                  + gate_attn * jnp.dot(attn, wap_ref[:, cols], preferred_element_type=F32))
        merged_ref[:, cols] = merged.astype(BF16)
        block = c // MERGE_COLS
        if block == ROUTED_PHASE_AFTER_BLOCK[0]:
            h_hi, h_lo = _norm2_split(x1_prev[...], n2g_ref[...] * (1.0 + modp_ref[4:5, :]), modp_ref[3:4, :],
                                      h2_ref)
        if block == ROUTED_PHASE_AFTER_BLOCK[1]:
            logits = _router_logits(h_hi, h_lo, wr_ref, br_ref)
        if block == ROUTED_PHASE_AFTER_BLOCK[2]:
            _dispatch_table(logits, (s > 0).astype(F32), utri_ref, run_ref, meta_ref, rw_ref)
            count_ref[...] = jnp.broadcast_to(run_ref[...], count_ref.shape)
    for c in range(0, D_MODEL, MERGE_COLS):
        cols = slice(c, c + MERGE_COLS)
        mixed = jnp.dot(merged_ref[...], wo_ref[:, cols], preferred_element_type=F32)
        x1 = x[:, cols] + res_gate[:, cols] * mixed
        x1_ref[:, cols] = x1.astype(BF16)
        x1_prev[:, cols] = x1


def _stage_c(x, mod, n1g, n2g, u, attn, w_gate, w_mix, pool_scale, w_pp, w_ap, w_o, w_r, b_r, tm):
    B, T, _ = x.shape
    hb = tm // POOL_HALO
    n_halo_blocks = T // POOL_HALO
    nt = T // tm
    n_tiles = B * nt

    def merged_tile(s):
        m = jnp.minimum(s, n_tiles - 1)
        return m // nt, lax.rem(m, nt)

    def tok(s):
        b, i = merged_tile(s)
        return b, i, 0

    def halo_before(s):
        b, i = merged_tile(s)
        return b, jnp.maximum(i * hb - 1, 0), 0

    def halo_after(s):
        b, i = merged_tile(s)
        return b, jnp.minimum((i + 1) * hb, n_halo_blocks - 1), 0

    routed = lambda s: jnp.maximum(s - 1, 0)
    flat = lambda s: (routed(s), 0)
    utri = jnp.triu(jnp.ones((tm, tm), BF16), k=1)
    return pl.pallas_call(
        functools.partial(_stage_c_kernel, seq_len=T),
        grid=(n_tiles + 1,),
        in_specs=[
            pl.BlockSpec((None, tm, D_MODEL), tok),
            pl.BlockSpec((None, 6, D_MODEL), lambda s: (merged_tile(s)[0], 0, 0)),
            pl.BlockSpec((None, 6, D_MODEL), lambda s: (routed(s) // nt, 0, 0)),
            _const_spec((1, D_MODEL)),
            _const_spec((1, D_MODEL)),
            pl.BlockSpec((None, tm, D_POOL), tok),
            pl.BlockSpec((None, POOL_HALO, D_POOL), halo_before),
            pl.BlockSpec((None, POOL_HALO, D_POOL), halo_after),
            pl.BlockSpec((None, tm, D_ATTN), tok),
            _const_spec((D_MODEL, 2 * D_MODEL)),
            _const_spec((len(POOL_WINDOWS), POOL_GC, POOL_GC)),
            _const_spec((1, D_POOL)),
            _const_spec((D_POOL, D_MODEL)),
            _const_spec((D_ATTN, D_MODEL)),
            _const_spec((D_MODEL, D_MODEL)),
            _const_spec((D_MODEL, 2 * LANES)),
            _const_spec((1, LANES)),
            _const_spec((tm, tm)),
        ],
        out_specs=[
            pl.BlockSpec((None, tm, D_MODEL), tok),
            pl.BlockSpec((tm, D_PACK), flat),
            pl.BlockSpec((META_ROWS, tm), lambda s: (0, routed(s))),
            pl.BlockSpec((tm, LANES), flat),
            pl.BlockSpec((N_EXPERTS, LANES), lambda s: (0, 0)),
        ],
        out_shape=[
            jax.ShapeDtypeStruct((B, T, D_MODEL), BF16),
            jax.ShapeDtypeStruct((B * T, D_PACK), I32),
            jax.ShapeDtypeStruct((META_ROWS, B * T), I32),
            jax.ShapeDtypeStruct((B * T, LANES), F32),
            jax.ShapeDtypeStruct((N_EXPERTS, LANES), F32),
        ],
        scratch_shapes=[pltpu.VMEM((tm + 3 * POOL_HALO, D_POOL), F32),
                        pltpu.VMEM((2, tm + 3 * POOL_HALO, POOL_GC), F32),
                        pltpu.VMEM((tm, D_POOL), BF16),
                        pltpu.VMEM((N_EXPERTS, 1), F32),
                        pltpu.VMEM((tm, D_MODEL), BF16),
                        pltpu.VMEM((tm, D_MODEL), F32)],
        compiler_params=pltpu.CompilerParams(
            dimension_semantics=("arbitrary",), vmem_limit_bytes=V7X_VMEM_LIMIT),
        name="stage_c",
    )(x, mod, mod, n1g, n2g, u, u, u, attn, w_gate, w_mix, pool_scale, w_pp, w_ap, w_o, w_r, b_r, utri)


def _sc_workers():
    info = plsc.get_sparse_core_info()
    return info.num_cores, info.num_subcores


def _sc_scatter_rows(rows, idx0, idx1, n_out):
    n, d = rows.shape
    nc, ns = _sc_workers()
    per_worker = n // (nc * ns)
    n_win = per_worker // SC_WINDOW
    mesh = plsc.VectorSubcoreMesh(core_axis_name="c", subcore_axis_name="s")

    @functools.partial(
        pl.kernel, mesh=mesh,
        out_type=jax.ShapeDtypeStruct((n_out, d), rows.dtype),
        scratch_types=[pltpu.VMEM((SC_WINDOW,), I32), pltpu.VMEM((SC_WINDOW,), I32),
                       pltpu.VMEM((SC_WINDOW, d), rows.dtype)],
        name="sc_scatter_rows",
    )
    def scatter(rows_hbm, idx0_hbm, idx1_hbm, out_hbm, i0_v, i1_v, rows_v):
        wid = lax.axis_index("s") * nc + lax.axis_index("c")

        @pl.loop(0, n_win)
        def _(w):
            base = wid * per_worker + w * SC_WINDOW
            pltpu.sync_copy(rows_hbm.at[pl.ds(base, SC_WINDOW)], rows_v)
            pltpu.sync_copy(idx0_hbm.at[pl.ds(base, SC_WINDOW)], i0_v)
            pltpu.sync_copy(idx1_hbm.at[pl.ds(base, SC_WINDOW)], i1_v)
            pltpu.sync_copy(rows_v, out_hbm.at[i0_v])
            pltpu.sync_copy(rows_v, out_hbm.at[i1_v])

    return scatter(rows, idx0, idx1)


def _sc_gather_rows(table, idx):
    n = idx.shape[0]
    d = table.shape[1]
    nc, ns = _sc_workers()
    per_worker = n // (nc * ns)
    n_win = per_worker // SC_WINDOW
    mesh = plsc.VectorSubcoreMesh(core_axis_name="c", subcore_axis_name="s")

    @functools.partial(
        pl.kernel, mesh=mesh,
        out_type=jax.ShapeDtypeStruct((n, d), table.dtype),
        scratch_types=[pltpu.VMEM((SC_WINDOW,), I32), pltpu.VMEM((SC_WINDOW, d), table.dtype)],
        name="sc_gather_rows",
    )
    def gather(table_hbm, idx_hbm, out_hbm, i_v, rows_v):
        wid = lax.axis_index("s") * nc + lax.axis_index("c")

        @pl.loop(0, n_win)
        def _(w):
            base = wid * per_worker + w * SC_WINDOW
            pltpu.sync_copy(idx_hbm.at[pl.ds(base, SC_WINDOW)], i_v)
            pltpu.sync_copy(table_hbm.at[i_v], rows_v)
            pltpu.sync_copy(rows_v, out_hbm.at[pl.ds(base, SC_WINDOW)])

    return gather(table, idx)


def _moe_kernel(te_ref, tv_ref, nl_ref, x_hbm, wgu_ref, wd_ref, o_ref, xbuf, sem):
    del te_ref
    i = pl.program_id(0)
    n_live_tiles = nl_ref[0]
    valid = tv_ref[i]
    tm = o_ref.shape[0]
    slot = _input_ring(x_hbm, xbuf, sem, i, n_live_tiles, tm)

    group = min(STAGE_ROWS, tm)
    n_groups = tm // group

    def gate_up(r0):
        live = lax.broadcasted_iota(I32, (group, 1), 0) < valid - r0
        lo, hi = _unpack_bf16_pair(jnp.where(live, xbuf[slot, r0:r0 + group, :], 0))
        x = jnp.concatenate([lo, hi], axis=1).astype(BF16)
        return jnp.dot(x, wgu_ref[...], preferred_element_type=F32)

    def down(gu):
        a = gu[:, :D_EXPERT]
        hmid = (a * _sigmoid(a) * gu[:, D_EXPERT:]).astype(BF16)
        return jnp.dot(hmid, wd_ref[...], preferred_element_type=F32)

    def store(r0, y):
        o_ref[r0:r0 + group, :] = _pack_bf16_pair(y[:, :D_PACK], y[:, D_PACK:])

    for n_live in range(n_groups + 1):
        @pl.when((valid > (n_live - 1) * group) & (valid <= n_live * group))
        def _():
            gu, y = None, None
            for g in range(n_live + 2):
                nxt = gate_up(g * group) if g < n_live else None
                if y is not None:
                    store((g - 2) * group, y)
                y = down(gu) if gu is not None else None
                gu = nxt
            if n_live < n_groups:
                o_ref[n_live * group:, :] = jnp.zeros((tm - n_live * group, D_PACK), I32)


def _moe(xs, tile_expert, tile_valid, n_live_tiles, w_gu, w_d, tm):
    n_tiles = xs.shape[0] // tm
    grid_spec = pltpu.PrefetchScalarGridSpec(
        num_scalar_prefetch=3,
        grid=(n_tiles,),
        in_specs=[
            pl.BlockSpec(memory_space=pl.ANY),
            pl.BlockSpec((None, D_MODEL, 2 * D_EXPERT), lambda i, te, tv, nl: (te[i], 0, 0)),
            pl.BlockSpec((None, D_EXPERT, D_MODEL), lambda i, te, tv, nl: (te[i], 0, 0)),
        ],
        out_specs=pl.BlockSpec((tm, D_PACK), lambda i, te, tv, nl: (jnp.minimum(i, nl[0]), 0)),
        scratch_shapes=[pltpu.VMEM((RING_SLOTS, tm, D_PACK), I32), pltpu.SemaphoreType.DMA((RING_SLOTS,))],
    )
    return pl.pallas_call(
        _moe_kernel,
        grid_spec=grid_spec,
        out_shape=jax.ShapeDtypeStruct(xs.shape, I32),
        compiler_params=pltpu.CompilerParams(
            dimension_semantics=("arbitrary",), vmem_limit_bytes=V7X_VMEM_LIMIT),
        name="experts",
    )(tile_expert, tile_valid, n_live_tiles, xs, w_gu, w_d)


def _final_kernel(x1_ref, y0_ref, y1_ref, rw_ref, mod_ref, fg_ref, o_ref):
    y0lo, y0hi = _unpack_bf16_pair(y0_ref[...])
    y1lo, y1hi = _unpack_bf16_pair(y1_ref[...])
    w0 = rw_ref[:, 0:1]
    w1 = rw_ref[:, 1:2]
    moe = jnp.concatenate([w0 * y0lo + w1 * y1lo, w0 * y0hi + w1 * y1hi], axis=1)
    x2 = x1_ref[...].astype(F32) + mod_ref[5:6, :] * moe
    ms = jnp.mean(x2 * x2, axis=-1, keepdims=True)
    o_ref[...] = x2 * lax.rsqrt(ms + EPS) * fg_ref[...]


def _final(x1, yb, rw, mod, final_g, tm):
    B, T, _ = x1.shape
    n_blocks = B * T // tm
    flat = lambda b, i: (b * (T // tm) + i, 0)
    return pl.pallas_call(
        _final_kernel,
        grid=(B, T // tm),
        in_specs=[
            pl.BlockSpec((None, tm, D_MODEL), lambda b, i: (b, i, 0)),
            pl.BlockSpec((tm, D_PACK), flat),
            pl.BlockSpec((tm, D_PACK), lambda b, i: (n_blocks + b * (T // tm) + i, 0)),
            pl.BlockSpec((tm, LANES), flat),
            pl.BlockSpec((None, 6, D_MODEL), lambda b, i: (b, 0, 0)),
            _const_spec((1, D_MODEL)),
        ],
        out_specs=pl.BlockSpec((None, tm, D_MODEL), lambda b, i: (b, i, 0)),
        out_shape=jax.ShapeDtypeStruct((B, T, D_MODEL), F32),
        compiler_params=pltpu.CompilerParams(
            dimension_semantics=("parallel", "parallel"), vmem_limit_bytes=V7X_VMEM_LIMIT),
        name="final_combine",
    )(x1, yb, yb, rw, mod, final_g)


def _rope_tables(T):
    rows = T // GRID_W
    row = np.repeat(np.arange(rows, dtype=np.float32), GRID_W)
    col = np.tile(np.arange(GRID_W, dtype=np.float32), rows)
    inv_freq = np.float32(ROPE_THETA) ** (-np.arange(0, 64, 2, dtype=np.float32) / np.float32(64))
    ang_r = row[:, None] * inv_freq[None, :]
    ang_c = col[:, None] * inv_freq[None, :]
    ang = np.concatenate([ang_r, ang_r, ang_c, ang_c], axis=-1).astype(np.float32)
    sign = np.where((np.arange(HEAD_DIM) % 64) < 32, -1.0, 1.0).astype(np.float32)
    return jnp.asarray(np.cos(ang)), jnp.asarray(np.sin(ang) * sign[None, :])


def _dispatch_plan(counts, tm, n_tiles):
    tiles_per_e = (counts + tm - 1) // tm
    tile_end = jnp.cumsum(tiles_per_e)
    tile_start = tile_end - tiles_per_e
    tile_id = jnp.arange(n_tiles, dtype=I32)
    te = jnp.minimum(jnp.sum((tile_id[:, None] >= tile_end[None, :]).astype(I32), axis=1), N_EXPERTS - 1)
    live = jnp.clip(counts[te] - (tile_id - tile_start[te]) * tm, 0, tm)
    tv = jnp.where(tile_id < tile_end[-1], live, 0).astype(I32)
    return tile_start.astype(I32), te, tv, tile_end[-1:].astype(I32)


def _slot_kernel(start_ref, meta_ref, pos_ref, *, tile_rows):
    e = meta_ref[0:2, :]
    first_tile = jnp.zeros_like(e)
    for ex in range(N_EXPERTS):
        first_tile = jnp.where(e == ex, start_ref[ex], first_tile)
    pos_ref[...] = first_tile * tile_rows + meta_ref[2:4, :]


def _slots(meta, tile_start, tile_rows):
    n = meta.shape[1]
    tn = min(n, 8192)
    grid_spec = pltpu.PrefetchScalarGridSpec(
        num_scalar_prefetch=1,
        grid=(n // tn,),
        in_specs=[pl.BlockSpec((META_ROWS, tn), lambda i, start: (0, i))],
        out_specs=pl.BlockSpec((2, tn), lambda i, start: (0, i)),
    )
    return pl.pallas_call(
        functools.partial(_slot_kernel, tile_rows=tile_rows),
        grid_spec=grid_spec,
        out_shape=jax.ShapeDtypeStruct((2, n), I32),
        name="dispatch_slots",
    )(tile_start, meta)


def _pick_tile(T, want):
    t = min(T, want)
    assert T % t == 0
    return t


def _trunk(x, mod, p):
    B, T, _ = x.shape
    N = B * T
    tm = _pick_tile(T, 512)
    tq = _pick_tile(T, 512)
    tk = _pick_tile(T, 256)
    tme = _pick_tile(N, 512)
    cos, sin_signed = _rope_tables(T)
    u, qt, qb, k, vt = _stage_a(x, mod, p["n1g"], p["w_qkvu"], p["qg"], p["kg"], cos, sin_signed, tm)
    attn = lax.cond(
        p["bounded_softmax_ok"],
        functools.partial(_attention, tq=tq, tk=_pick_tile(T, 2048), running_max=False),
        functools.partial(_attention, tq=tq, tk=tk, running_max=True),
        qt, qb, k, vt)
    x1, h2p, meta, rw, counts = _stage_c(x, mod, p["n1g"], p["n2g"], u, attn, p["w_gate"], p["w_mix"],
                                         p["pool_scale"], p["w_pp"], p["w_ap"], p["w_o"], p["w_r"], p["b_r"], tm)
    n_tiles = 2 * N // tme + N_EXPERTS
    tile_start, te, tv, n_live_tiles = _dispatch_plan(counts[:, 0].astype(I32), tme, n_tiles)
    pos = _slots(meta, tile_start, tme)
    xs = _sc_scatter_rows(h2p, pos[0], pos[1], n_tiles * tme)
    ys = _moe(xs, te, tv, n_live_tiles, p["w_egu"], p["w_ed"], tme)
    yb = _sc_gather_rows(ys, pos.reshape(-1))
    return _final(x1, yb, rw, mod, p["final_g"], _pick_tile(T, 1024))


def kernel(x_prompt, x_sample, c_prompt, c_sample, w_ada, b_ada, norm1_g, norm2_g, w_in, q_norm_g,
           k_norm_g, w_pool_mix, pool_scale, w_pool_proj, w_attn_proj, w_o, w_router_group,
           b_router_group, w_router_expert, b_router_expert, w_exp_gate, w_exp_up, w_exp_down, final_g):
    assert w_ada.shape[0] == 1, "single-layer block"
    n_r = N_GROUPS + N_EXPERTS
    w_r = jnp.concatenate([w_router_expert[0], w_router_group[0],
                           jnp.zeros((D_MODEL, LANES - n_r), F32)], axis=1)
    w_r_hi = w_r.astype(BF16)
    b_r = jnp.concatenate([b_router_expert[0], b_router_group[0], jnp.zeros((LANES - n_r,), F32)])
    score_bound = (1.01 * LOG2E * np.sqrt(HEAD_DIM)) * jnp.max(jnp.abs(q_norm_g[0])) * jnp.max(jnp.abs(k_norm_g[0]))
    p = dict(
        bounded_softmax_ok=2.0 * score_bound <= EXP2_SAFE_SPAN,
        n1g=norm1_g[0].reshape(1, D_MODEL), n2g=norm2_g[0].reshape(1, D_MODEL),
        w_qkvu=w_in[0][:, :D_QKVU].astype(BF16), w_gate=w_in[0][:, D_QKVU:].astype(BF16),
        qg=q_norm_g[0].reshape(1, HEAD_DIM), kg=k_norm_g[0].reshape(1, HEAD_DIM),
        w_mix=w_pool_mix[0].astype(BF16), pool_scale=pool_scale[0].reshape(1, D_POOL),
        w_pp=w_pool_proj[0].astype(BF16), w_ap=w_attn_proj[0].astype(BF16), w_o=w_o[0].astype(BF16),
        w_r=jnp.concatenate([w_r_hi, (w_r - w_r_hi.astype(F32)).astype(BF16)], axis=1), b_r=b_r.reshape(1, LANES),
        w_egu=jnp.concatenate([w_exp_gate[0], w_exp_up[0]], axis=-1).astype(BF16), w_ed=w_exp_down[0].astype(BF16),
        final_g=final_g.reshape(1, D_MODEL),
    )
    n_prompt = c_prompt.shape[0]
    mod = _modulation(jnp.concatenate([c_prompt, c_sample], axis=0), w_ada[0], b_ada[0]).reshape(-1, 6, D_MODEL)
    return _trunk(x_prompt, mod[:n_prompt], p), _trunk(x_sample, mod[n_prompt:], p)
```

```python
import functools

import numpy as np
import jax
import jax.numpy as jnp
from jax import lax
from jax.experimental import pallas as pl
from jax.experimental.pallas import tpu as pltpu
from jax.experimental.pallas import tpu_sc as plsc

F32 = jnp.float32
BF16 = jnp.bfloat16
I32 = jnp.int32

D_MODEL = 1024
GRID_W = 64
POOL_WINDOWS = (2, 4, 8, 16)
POOL_GC = 128
D_POOL = 512
HEAD_DIM = 128
N_Q_HEADS = 4
N_KV_HEADS = 2
D_ATTN = 512
D_KV = 256
ROPE_THETA = 10000.0
N_GROUPS = 4
EXPERTS_PER_GROUP = 8
N_EXPERTS = 32
D_EXPERT = 256
EPS = 1e-6
LOG2E = 1.4426950408889634
EXP2_SAFE_SPAN = 100.0
D_QKVU = D_POOL + D_ATTN + 2 * D_KV
POOL_HALO = 16
STAGE_ROWS = 256
PROJ_COLS = 256
ROUTED_PHASE_AFTER_BLOCK = (0, 1, 2)
MERGE_COLS = 256
RING_SLOTS = 3
ATTN_TILES_PER_TRIP = 10
ROUTE_ROWS = 40
META_ROWS = 8
LANES = 128
D_PACK = D_MODEL // 2

V7X_VMEM_LIMIT = 56 * 1024 * 1024
SC_WINDOW = 64


def _const_spec(shape):
    nd = len(shape)
    return pl.BlockSpec(shape, lambda *_: (0,) * nd, pipeline_mode=pl.Buffered(1))


def _sigmoid(x):
    return 0.5 * jnp.tanh(0.5 * x) + 0.5


def _pack_bf16_pair(lo, hi):
    lo_bits = lax.bitcast_convert_type(lo.astype(BF16).astype(F32), jnp.uint32) >> 16
    hi_bits = lax.bitcast_convert_type(hi.astype(BF16).astype(F32), jnp.uint32) & jnp.uint32(0xFFFF0000)
    return lax.bitcast_convert_type(lo_bits | hi_bits, I32)


def _unpack_bf16_pair(packed):
    u = lax.bitcast_convert_type(packed, jnp.uint32)
    lo = lax.bitcast_convert_type(u << 16, F32)
    hi = lax.bitcast_convert_type(u & jnp.uint32(0xFFFF0000), F32)
    return lo, hi


def _input_ring(x_hbm, buf, sem, step, n_steps, rows):
    def fetch(s):
        slot = lax.rem(s, RING_SLOTS)
        src = x_hbm.at[pl.ds(pl.multiple_of(s * rows, rows), rows), :]
        return pltpu.make_async_copy(src, buf.at[slot], sem.at[slot])

    @pl.when((step == 0) & (n_steps > 0))
    def _():
        fetch(0).start()

    @pl.when((step == 0) & (n_steps > 1))
    def _():
        fetch(1).start()

    @pl.when(step + 2 < n_steps)
    def _():
        fetch(step + 2).start()

    @pl.when(step < n_steps)
    def _():
        fetch(step).wait()

    return lax.rem(step, RING_SLOTS)


def _mod_kernel(c_ref, w_ref, b_ref, o_ref):
    c = c_ref[...]
    s = c * _sigmoid(c)
    o_ref[...] = jnp.dot(s, w_ref[...], preferred_element_type=F32,
                         precision=lax.Precision.HIGHEST) + b_ref[...]


def _modulation(c, w_ada, b_ada):
    B = c.shape[0]
    n_out = w_ada.shape[1]
    bn = D_MODEL
    return pl.pallas_call(
        _mod_kernel,
        grid=(n_out // bn,),
        in_specs=[pl.BlockSpec((B, D_MODEL), lambda j: (0, 0)),
                  pl.BlockSpec((D_MODEL, bn), lambda j: (0, j)),
                  pl.BlockSpec((1, bn), lambda j: (0, j))],
        out_specs=pl.BlockSpec((B, bn), lambda j: (0, j)),
        out_shape=jax.ShapeDtypeStruct((B, n_out), F32),
        name="modulation",
    )(c, w_ada, b_ada.reshape(1, n_out))


def _scaled_norm(x, gain, shift):
    ms = jnp.mean(x * x, axis=-1, keepdims=True)
    return x * lax.rsqrt(ms + EPS) * gain + shift


def _stage_a_kernel(x_hbm, mod_ref, n1g_ref, w_ref, qg_ref, kg_ref, cos_ref, sin_ref,
                    u_ref, qt_ref, qb_ref, k_ref, vt_ref, xbuf, sem, z_ref):
    s = pl.program_id(0)
    n_tiles = pl.num_programs(0) - 1
    tm = u_ref.shape[0]
    _input_ring(x_hbm, xbuf, sem, s, n_tiles, tm)
    slot = lax.rem(jnp.minimum(s, n_tiles - 1), RING_SLOTS)

    @pl.when(s == 0)
    def _():
        z_ref[...] = jnp.zeros_like(z_ref)

    gain1 = n1g_ref[...] * (1.0 + mod_ref[1:2, :])
    shift1 = mod_ref[0:1, :]
    lane = lax.broadcasted_iota(I32, (1, HEAD_DIM), 1)
    first_half = (lane % 64) < 32
    qg = qg_ref[...] * (LOG2E / np.sqrt(HEAD_DIM))
    k_norm_max = jnp.max(jnp.abs(kg_ref[...]), axis=-1, keepdims=True) * (np.sqrt(HEAD_DIM) * 1.01)
    cos = cos_ref[...]
    sin_signed = sin_ref[...]

    def norm_rope(xh, g):
        ms = jnp.mean(xh * xh, axis=-1, keepdims=True)
        xn = xh * lax.rsqrt(ms + EPS) * g
        rot = jnp.where(first_half, pltpu.roll(xn, 96, 1), pltpu.roll(xn, 32, 1))
        return xn * cos + rot * sin_signed

    def finish(o):
        zh = z_ref[:, o:o + HEAD_DIM]
        if o < D_POOL:
            u_ref[:, o:o + HEAD_DIM] = zh.astype(BF16)
        elif o < D_POOL + D_ATTN:
            hq = (o - D_POOL) // HEAD_DIM
            q_t = norm_rope(zh, qg).T.astype(BF16)
            qt_ref[hq] = q_t
            q_f = q_t.astype(F32)
            qb_ref[hq] = jnp.sqrt(jnp.sum(q_f * q_f, axis=0, keepdims=True)) * k_norm_max
        elif o < D_POOL + D_ATTN + D_KV:
            k_ref[:, o - D_POOL - D_ATTN:o - D_POOL - D_ATTN + HEAD_DIM] = norm_rope(zh, kg_ref[...]).astype(BF16)
        else:
            vt_ref[(o - D_POOL - D_ATTN - D_KV) // HEAD_DIM] = zh.T.astype(BF16)

    h = _scaled_norm(xbuf[slot], gain1, shift1).astype(BF16)
    for c in range(0, D_QKVU, PROJ_COLS):
        for o in range(c, c + PROJ_COLS, HEAD_DIM):
            finish(o)
        z_ref[:, c:c + PROJ_COLS] = jnp.dot(h, w_ref[:, c:c + PROJ_COLS], preferred_element_type=F32)


def _stage_a(x, mod, n1g, w_qkvu, qg, kg, cos, sin_signed, tm):
    B, T, _ = x.shape
    nt = T // tm
    n_tiles = B * nt
    projected = lambda s: jnp.minimum(s, n_tiles - 1)
    finished = lambda s: jnp.maximum(s - 1, 0)
    return pl.pallas_call(
        _stage_a_kernel,
        grid=(n_tiles + 1,),
        in_specs=[
            pl.BlockSpec(memory_space=pl.ANY),
            pl.BlockSpec((None, 6, D_MODEL), lambda s: (projected(s) // nt, 0, 0)),
            _const_spec((1, D_MODEL)),
            _const_spec((D_MODEL, D_QKVU)),
            _const_spec((1, HEAD_DIM)),
            _const_spec((1, HEAD_DIM)),
            pl.BlockSpec((tm, HEAD_DIM), lambda s: (lax.rem(finished(s), nt), 0)),
            pl.BlockSpec((tm, HEAD_DIM), lambda s: (lax.rem(finished(s), nt), 0)),
        ],
        out_specs=[
            pl.BlockSpec((None, tm, D_POOL), lambda s: (finished(s) // nt, lax.rem(finished(s), nt), 0)),
            pl.BlockSpec((None, N_Q_HEADS, HEAD_DIM, tm), lambda s: (finished(s) // nt, 0, 0, lax.rem(finished(s), nt))),
            pl.BlockSpec((None, N_Q_HEADS, 1, tm), lambda s: (finished(s) // nt, 0, 0, lax.rem(finished(s), nt))),
            pl.BlockSpec((None, tm, D_KV), lambda s: (finished(s) // nt, lax.rem(finished(s), nt), 0)),
            pl.BlockSpec((None, N_KV_HEADS, HEAD_DIM, tm), lambda s: (finished(s) // nt, 0, 0, lax.rem(finished(s), nt))),
        ],
        out_shape=[
            jax.ShapeDtypeStruct((B, T, D_POOL), BF16),
            jax.ShapeDtypeStruct((B, N_Q_HEADS, HEAD_DIM, T), BF16),
            jax.ShapeDtypeStruct((B, N_Q_HEADS, 1, T), F32),
            jax.ShapeDtypeStruct((B, T, D_KV), BF16),
            jax.ShapeDtypeStruct((B, N_KV_HEADS, HEAD_DIM, T), BF16),
        ],
        scratch_shapes=[pltpu.VMEM((RING_SLOTS, tm, D_MODEL), F32), pltpu.SemaphoreType.DMA((RING_SLOTS,)),
                        pltpu.VMEM((tm, D_QKVU), F32)],
        compiler_params=pltpu.CompilerParams(
            dimension_semantics=("arbitrary",), vmem_limit_bytes=V7X_VMEM_LIMIT),
        name="stage_a",
    )(x.reshape(B * T, D_MODEL), mod, n1g, w_qkvu, qg, kg, cos, sin_signed)


def _write_attn_output(o_ref, acc, l, tq):
    o = acc * (1.0 / l)
    o_ref[:, :HEAD_DIM] = o[:, :tq].T.astype(BF16)
    o_ref[:, HEAD_DIM:] = o[:, tq:].T.astype(BF16)


def _attn_bounded_kernel(qt_ref, qb_ref, k_ref, vt_ref, o_ref, p_ref, acc_ref, l_ref, *, tk):
    T = k_ref.shape[0]
    tq = qt_ref.shape[2]
    nq = 2 * tq
    nk = T // tk
    qt = jnp.concatenate([qt_ref[0], qt_ref[1]], axis=1)
    qb = jnp.concatenate([qb_ref[0], qb_ref[1]], axis=1)

    def weights(j, slot):
        off = pl.multiple_of(j * tk, tk)
        st = jnp.dot(k_ref[pl.ds(off, tk), :], qt, preferred_element_type=F32)
        p = jnp.exp2(st - qb)
        l_ref[...] += jnp.sum(p.reshape(tk // 8, 8, nq), axis=0)
        p_ref[slot] = p.astype(BF16)

    def values(j, slot):
        off = pl.multiple_of(j * tk, tk)
        acc_ref[...] += jnp.dot(vt_ref[:, pl.ds(off, tk)], p_ref[slot], preferred_element_type=F32)

    def step(j, slot):
        values(j - 1, 1 - slot)
        weights(j, slot)

    acc_ref[...] = jnp.zeros_like(acc_ref)
    l_ref[...] = jnp.zeros_like(l_ref)
    weights(0, 0)

    unroll = max(1, min(ATTN_TILES_PER_TRIP, nk - 1))
    trips = (nk - 1) // unroll if unroll % 2 == 0 else 0

    if trips > 1:
        @pl.loop(0, trips)
        def _(g):
            for d in range(unroll):
                step(1 + unroll * g + d, (1 + d) % 2)
    else:
        trips = 0
    for j in range(1 + trips * unroll, nk):
        step(j, j % 2)
    values(nk - 1, (nk - 1) % 2)
    _write_attn_output(o_ref, acc_ref[...], jnp.sum(l_ref[...], axis=0, keepdims=True), tq)


def _attn_online_kernel(qt_ref, qb_ref, k_ref, vt_ref, o_ref, *, tk):
    del qb_ref
    T = k_ref.shape[0]
    tq = qt_ref.shape[2]
    nq = 2 * tq
    qt = jnp.concatenate([qt_ref[0], qt_ref[1]], axis=1)

    def body(s, carry):
        m, l, acc = carry
        off = pl.multiple_of(s * tk, tk)
        st = jnp.dot(k_ref[pl.ds(off, tk), :], qt, preferred_element_type=F32)
        m_new = jnp.maximum(m, jnp.max(st, axis=0, keepdims=True))
        alpha = jnp.exp2(m - m_new)
        p = jnp.exp2(st - m_new)
        l = alpha * l + jnp.sum(p, axis=0, keepdims=True)
        pv = jnp.dot(vt_ref[:, pl.ds(off, tk)], p.astype(BF16), preferred_element_type=F32)
        return m_new, l, alpha * acc + pv

    init = (jnp.full((1, nq), -jnp.inf, F32), jnp.zeros((1, nq), F32), jnp.zeros((HEAD_DIM, nq), F32))
    _, l, acc = lax.fori_loop(0, T // tk, body, init)
    _write_attn_output(o_ref, acc, l, tq)


def _attention(qt, qb, k, vt, tq, tk, running_max):
    B, _, _, T = qt.shape
    group = N_Q_HEADS // N_KV_HEADS
    nq = group * tq
    if running_max:
        body, scratch, name = functools.partial(_attn_online_kernel, tk=tk), [], "attention_online_max"
    else:
        body = functools.partial(_attn_bounded_kernel, tk=tk)
        scratch = [pltpu.VMEM((2, tk, nq), BF16), pltpu.VMEM((HEAD_DIM, nq), F32), pltpu.VMEM((8, nq), F32)]
        name = "attention_bounded"
    return pl.pallas_call(
        body,
        grid=(B, N_KV_HEADS, T // tq),
        in_specs=[
            pl.BlockSpec((None, group, HEAD_DIM, tq), lambda b, j, i: (b, j, 0, i)),
            pl.BlockSpec((None, group, 1, tq), lambda b, j, i: (b, j, 0, i)),
            pl.BlockSpec((None, T, HEAD_DIM), lambda b, j, i: (b, 0, j)),
            pl.BlockSpec((None, None, HEAD_DIM, T), lambda b, j, i: (b, j, 0, 0)),
        ],
        out_specs=pl.BlockSpec((None, tq, group * HEAD_DIM), lambda b, j, i: (b, i, j)),
        out_shape=jax.ShapeDtypeStruct((B, T, D_ATTN), BF16),
        scratch_shapes=scratch,
        compiler_params=pltpu.CompilerParams(
            dimension_semantics=("parallel", "parallel", "parallel"),
            vmem_limit_bytes=V7X_VMEM_LIMIT),
        name=name,
    )(qt, qb, k, vt)


def _route(logits_t):
    neg = jnp.float32(-jnp.inf)
    n = logits_t.shape[1]

    def top(vals, idx, far):
        v = jnp.max(vals, axis=0, keepdims=True)
        i = jnp.min(jnp.where(vals == v, idx, far), axis=0, keepdims=True)
        return v, i

    gidx = lax.broadcasted_iota(I32, (ROUTE_ROWS - N_EXPERTS, n), 0).astype(F32)
    grp = jnp.where(gidx < N_GROUPS, logits_t[N_EXPERTS:, :], neg)
    gmax, gsel = top(grp, gidx, jnp.float32(ROUTE_ROWS))
    p_group = 1.0 / jnp.sum(jnp.exp(grp - gmax), axis=0, keepdims=True)
    eidx = lax.broadcasted_iota(I32, (N_EXPERTS, n), 0).astype(F32)
    first = EXPERTS_PER_GROUP * gsel
    le = jnp.where((eidx >= first) & (eidx < first + EXPERTS_PER_GROUP), logits_t[:N_EXPERTS, :], neg)
    v1, e1 = top(le, eidx, jnp.float32(N_EXPERTS))
    v2, e2 = top(jnp.where(eidx == e1, neg, le), eidx, jnp.float32(N_EXPERTS))
    e21 = jnp.exp(v2 - v1)
    w1 = p_group / (1.0 + e21)
    return e1, e2, w1, w1 * e21


def _pool_deviation(ue_ref, lvl_ref, pool_ref, tile_start, tm, seq_len, groups):
    n = tm + 2 * POOL_HALO
    t = tile_start + lax.broadcasted_iota(I32, (tm, 1), 0)
    for gi in groups:
        w = POOL_WINDOWS[gi]
        half = w // 2
        cols = slice(gi * POOL_GC, (gi + 1) * POOL_GC)
        level = ue_ref[0:n, cols] + ue_ref[1:n + 1, cols]
        span, buf = 2, 0
        lvl_ref[buf, 0:n, :] = level
        while span < w:
            level = lvl_ref[buf, 0:n, :] + lvl_ref[buf, span:span + n, :]
            span, buf = 2 * span, 1 - buf
            lvl_ref[buf, 0:n, :] = level
        wsum = lvl_ref[buf, POOL_HALO - half:POOL_HALO - half + tm, :]
        cnt = (jnp.minimum(t + half, seq_len) - jnp.maximum(t - half, 0)).astype(F32)
        p = wsum * (1.0 / cnt) - ue_ref[POOL_HALO:POOL_HALO + tm, cols]
        pool_ref[:, cols] = p.astype(BF16)


def _norm2_split(x1, gain2, shift2, h2_ref):
    h2 = _scaled_norm(x1, gain2, shift2)
    h2_ref[...] = _pack_bf16_pair(h2[:, :D_PACK], h2[:, D_PACK:])
    h_hi = h2.astype(BF16)
    return h_hi, (h2 - h_hi.astype(F32)).astype(BF16)


def _router_logits(h_hi, h_lo, wr_ref, br_ref):
    parts = (jnp.dot(h_hi, wr_ref[...], preferred_element_type=F32)
             + jnp.dot(h_lo, wr_ref[...], preferred_element_type=F32))
    return parts[:, :LANES] + parts[:, LANES:] + br_ref[...]


def _dispatch_table(logits, is_real, utri_ref, run_ref, meta_ref, rw_ref):
    n = logits.shape[0]
    e1, e2, w1, w2 = _route(logits.T[0:ROUTE_ROWS, :])

    eidx = lax.broadcasted_iota(I32, (N_EXPERTS, n), 0).astype(F32)
    hit1 = eidx == e1
    hit2 = eidx == e2
    taken = jnp.where(hit1 | hit2, is_real, 0.0)
    before = run_ref[...] + jnp.dot(taken.astype(BF16), utri_ref[...], preferred_element_type=F32)
    rank1 = jnp.sum(jnp.where(hit1, before, 0.0), axis=0, keepdims=True)
    rank2 = jnp.sum(jnp.where(hit2, before, 0.0), axis=0, keepdims=True)
    run_ref[...] = run_ref[...] + jnp.sum(taken, axis=1, keepdims=True)
    ridx = lax.broadcasted_iota(I32, (META_ROWS, n), 0)
    table = jnp.where(ridx == 0, e1, jnp.where(ridx == 1, e2, jnp.where(ridx == 2, rank1,
                      jnp.where(ridx == 3, rank2, 0.0))))
    meta_ref[...] = table.astype(I32)
    weights_t = jnp.where(ridx == 0, w1, jnp.where(ridx == 1, w2, 0.0))
    rw_ref[...] = jnp.concatenate([weights_t, jnp.zeros((LANES - META_ROWS, n), F32)], axis=0).T


def _stage_c_kernel(x_ref, mod_ref, modp_ref, n1g_ref, n2g_ref, u_ref, up_ref, un_ref, attn_ref,
                    wgate_ref, wmix_ref, ps_ref, wpp_ref, wap_ref, wo_ref, wr_ref, br_ref, utri_ref,
                    x1_ref, h2_ref, meta_ref, rw_ref, count_ref,
                    ue_ref, lvl_ref, pool_ref, run_ref, merged_ref, x1_prev, *, seq_len):
    s = pl.program_id(0)
    tm = x_ref.shape[0]
    tiles_per_seq = seq_len // tm
    i = lax.rem(jnp.minimum(s, pl.num_programs(0) - 2), tiles_per_seq)

    @pl.when(s == 0)
    def _():
        run_ref[...] = jnp.zeros_like(run_ref)
        x1_prev[...] = jnp.zeros_like(x1_prev)

    gain1 = n1g_ref[...] * (1.0 + mod_ref[1:2, :])
    shift1 = mod_ref[0:1, :]
    res_gate = mod_ref[2:3, :]
    x = x_ref[...]
    h = _scaled_norm(x, gain1, shift1).astype(BF16)
    ue_ref[0:POOL_HALO, :] = up_ref[...].astype(F32) * (i > 0).astype(F32)
    ue_ref[POOL_HALO:POOL_HALO + tm, :] = u_ref[...].astype(F32)
    ue_ref[POOL_HALO + tm:2 * POOL_HALO + tm, :] = un_ref[...].astype(F32) * (i < tiles_per_seq - 1).astype(F32)
    ue_ref[2 * POOL_HALO + tm:, :] = jnp.zeros((POOL_HALO, D_POOL), F32)
    lvl_ref[:, 2 * POOL_HALO + tm:, :] = jnp.zeros((2, POOL_HALO, POOL_GC), F32)
    _pool_deviation(ue_ref, lvl_ref, pool_ref, i * tm, tm, seq_len, range(len(POOL_WINDOWS)))

    attn = attn_ref[...]
    mixed_groups = []
    for gi in range(len(POOL_WINDOWS)):
        cols = slice(gi * POOL_GC, (gi + 1) * POOL_GC)
        pm = jnp.dot(pool_ref[:, cols], wmix_ref[gi], preferred_element_type=F32)
        mixed_groups.append(pm * ps_ref[:, cols])
    pool = jnp.concatenate(mixed_groups, axis=1).astype(BF16)
    assert max(ROUTED_PHASE_AFTER_BLOCK) < D_MODEL // MERGE_COLS
    for c in range(0, D_MODEL, MERGE_COLS):
        cols = slice(c, c + MERGE_COLS)
        gate_cols = slice(D_MODEL + c, D_MODEL + c + MERGE_COLS)
        gate_pool = _sigmoid(jnp.dot(h, wgate_ref[:, cols], preferred_element_type=F32))
        gate_attn = _sigmoid(jnp.dot(h, wgate_ref[:, gate_cols], preferred_element_type=F32))
        merged = (gate_pool * jnp.dot(pool, wpp_ref[:, cols], preferred_element_type=F32)
                  + gate_attn * jnp.dot(attn, wap_ref[:, cols], preferred_element_type=F32))
        merged_ref[:, cols] = merged.astype(BF16)
        block = c // MERGE_COLS
        if block == ROUTED_PHASE_AFTER_BLOCK[0]:
            h_hi, h_lo = _norm2_split(x1_prev[...], n2g_ref[...] * (1.0 + modp_ref[4:5, :]), modp_ref[3:4, :],
                                      h2_ref)
        if block == ROUTED_PHASE_AFTER_BLOCK[1]:
            logits = _router_logits(h_hi, h_lo, wr_ref, br_ref)
        if block == ROUTED_PHASE_AFTER_BLOCK[2]:
            _dispatch_table(logits, (s > 0).astype(F32), utri_ref, run_ref, meta_ref, rw_ref)
            count_ref[...] = jnp.broadcast_to(run_ref[...], count_ref.shape)
    for c in range(0, D_MODEL, MERGE_COLS):
        cols = slice(c, c + MERGE_COLS)
        mixed = jnp.dot(merged_ref[...], wo_ref[:, cols], preferred_element_type=F32)
        x1 = x[:, cols] + res_gate[:, cols] * mixed
        x1_ref[:, cols] = x1.astype(BF16)
        x1_prev[:, cols] = x1


def _stage_c(x, mod, n1g, n2g, u, attn, w_gate, w_mix, pool_scale, w_pp, w_ap, w_o, w_r, b_r, tm):
    B, T, _ = x.shape
    hb = tm // POOL_HALO
    n_halo_blocks = T // POOL_HALO
    nt = T // tm
    n_tiles = B * nt

    def merged_tile(s):
        m = jnp.minimum(s, n_tiles - 1)
        return m // nt, lax.rem(m, nt)

    def tok(s):
        b, i = merged_tile(s)
        return b, i, 0

    def halo_before(s):
        b, i = merged_tile(s)
        return b, jnp.maximum(i * hb - 1, 0), 0

    def halo_after(s):
        b, i = merged_tile(s)
        return b, jnp.minimum((i + 1) * hb, n_halo_blocks - 1), 0

    routed = lambda s: jnp.maximum(s - 1, 0)
    flat = lambda s: (routed(s), 0)
    utri = jnp.triu(jnp.ones((tm, tm), BF16), k=1)
    return pl.pallas_call(
        functools.partial(_stage_c_kernel, seq_len=T),
        grid=(n_tiles + 1,),
        in_specs=[
            pl.BlockSpec((None, tm, D_MODEL), tok),
            pl.BlockSpec((None, 6, D_MODEL), lambda s: (merged_tile(s)[0], 0, 0)),
            pl.BlockSpec((None, 6, D_MODEL), lambda s: (routed(s) // nt, 0, 0)),
            _const_spec((1, D_MODEL)),
            _const_spec((1, D_MODEL)),
            pl.BlockSpec((None, tm, D_POOL), tok),
            pl.BlockSpec((None, POOL_HALO, D_POOL), halo_before),
            pl.BlockSpec((None, POOL_HALO, D_POOL), halo_after),
            pl.BlockSpec((None, tm, D_ATTN), tok),
            _const_spec((D_MODEL, 2 * D_MODEL)),
            _const_spec((len(POOL_WINDOWS), POOL_GC, POOL_GC)),
            _const_spec((1, D_POOL)),
            _const_spec((D_POOL, D_MODEL)),
            _const_spec((D_ATTN, D_MODEL)),
            _const_spec((D_MODEL, D_MODEL)),
            _const_spec((D_MODEL, 2 * LANES)),
            _const_spec((1, LANES)),
            _const_spec((tm, tm)),
        ],
        out_specs=[
            pl.BlockSpec((None, tm, D_MODEL), tok),
            pl.BlockSpec((tm, D_PACK), flat),
            pl.BlockSpec((META_ROWS, tm), lambda s: (0, routed(s))),
            pl.BlockSpec((tm, LANES), flat),
            pl.BlockSpec((N_EXPERTS, LANES), lambda s: (0, 0)),
        ],
        out_shape=[
            jax.ShapeDtypeStruct((B, T, D_MODEL), BF16),
            jax.ShapeDtypeStruct((B * T, D_PACK), I32),
            jax.ShapeDtypeStruct((META_ROWS, B * T), I32),
            jax.ShapeDtypeStruct((B * T, LANES), F32),
            jax.ShapeDtypeStruct((N_EXPERTS, LANES), F32),
        ],
        scratch_shapes=[pltpu.VMEM((tm + 3 * POOL_HALO, D_POOL), F32),
                        pltpu.VMEM((2, tm + 3 * POOL_HALO, POOL_GC), F32),
                        pltpu.VMEM((tm, D_POOL), BF16),
                        pltpu.VMEM((N_EXPERTS, 1), F32),
                        pltpu.VMEM((tm, D_MODEL), BF16),
                        pltpu.VMEM((tm, D_MODEL), F32)],
        compiler_params=pltpu.CompilerParams(
            dimension_semantics=("arbitrary",), vmem_limit_bytes=V7X_VMEM_LIMIT),
        name="stage_c",
    )(x, mod, mod, n1g, n2g, u, u, u, attn, w_gate, w_mix, pool_scale, w_pp, w_ap, w_o, w_r, b_r, utri)


def _sc_workers():
    info = plsc.get_sparse_core_info()
    return info.num_cores, info.num_subcores


def _sc_scatter_rows(rows, idx0, idx1, n_out):
    n, d = rows.shape
    nc, ns = _sc_workers()
    per_worker = n // (nc * ns)
    n_win = per_worker // SC_WINDOW
    mesh = plsc.VectorSubcoreMesh(core_axis_name="c", subcore_axis_name="s")

    @functools.partial(
        pl.kernel, mesh=mesh,
        out_type=jax.ShapeDtypeStruct((n_out, d), rows.dtype),
        scratch_types=[pltpu.VMEM((SC_WINDOW,), I32), pltpu.VMEM((SC_WINDOW,), I32),
                       pltpu.VMEM((SC_WINDOW, d), rows.dtype)],
        name="sc_scatter_rows",
    )
    def scatter(rows_hbm, idx0_hbm, idx1_hbm, out_hbm, i0_v, i1_v, rows_v):
        wid = lax.axis_index("s") * nc + lax.axis_index("c")

        @pl.loop(0, n_win)
        def _(w):
            base = wid * per_worker + w * SC_WINDOW
            pltpu.sync_copy(rows_hbm.at[pl.ds(base, SC_WINDOW)], rows_v)
            pltpu.sync_copy(idx0_hbm.at[pl.ds(base, SC_WINDOW)], i0_v)
            pltpu.sync_copy(idx1_hbm.at[pl.ds(base, SC_WINDOW)], i1_v)
            pltpu.sync_copy(rows_v, out_hbm.at[i0_v])
            pltpu.sync_copy(rows_v, out_hbm.at[i1_v])

    return scatter(rows, idx0, idx1)


def _sc_gather_rows(table, idx):
    n = idx.shape[0]
    d = table.shape[1]
    nc, ns = _sc_workers()
    per_worker = n // (nc * ns)
    n_win = per_worker // SC_WINDOW
    mesh = plsc.VectorSubcoreMesh(core_axis_name="c", subcore_axis_name="s")

    @functools.partial(
        pl.kernel, mesh=mesh,
        out_type=jax.ShapeDtypeStruct((n, d), table.dtype),
        scratch_types=[pltpu.VMEM((SC_WINDOW,), I32), pltpu.VMEM((SC_WINDOW, d), table.dtype)],
        name="sc_gather_rows",
    )
    def gather(table_hbm, idx_hbm, out_hbm, i_v, rows_v):
        wid = lax.axis_index("s") * nc + lax.axis_index("c")

        @pl.loop(0, n_win)
        def _(w):
            base = wid * per_worker + w * SC_WINDOW
            pltpu.sync_copy(idx_hbm.at[pl.ds(base, SC_WINDOW)], i_v)
            pltpu.sync_copy(table_hbm.at[i_v], rows_v)
            pltpu.sync_copy(rows_v, out_hbm.at[pl.ds(base, SC_WINDOW)])

    return gather(table, idx)


def _moe_kernel(te_ref, tv_ref, nl_ref, x_hbm, wg_ref, wu_ref, wd_ref, o_ref, xbuf, sem, wgu_ref):
    i = pl.program_id(0)
    n_live_tiles = nl_ref[0]
    valid = tv_ref[i]
    tm = o_ref.shape[0]
    slot = _input_ring(x_hbm, xbuf, sem, i, n_live_tiles, tm)

    @pl.when((i == 0) | (te_ref[i] != te_ref[jnp.maximum(i - 1, 0)]))
    def _():
        wgu_ref[:, :D_EXPERT] = wg_ref[...]
        wgu_ref[:, D_EXPERT:] = wu_ref[...]

    group = min(STAGE_ROWS, tm)
    n_groups = tm // group

    def gate_up(r0):
        live = lax.broadcasted_iota(I32, (group, 1), 0) < valid - r0
        lo, hi = _unpack_bf16_pair(jnp.where(live, xbuf[slot, r0:r0 + group, :], 0))
        x = jnp.concatenate([lo, hi], axis=1).astype(BF16)
        return jnp.dot(x, wgu_ref[...], preferred_element_type=F32)

    def down(gu):
        a = gu[:, :D_EXPERT]
        hmid = (a * _sigmoid(a) * gu[:, D_EXPERT:]).astype(BF16)
        return jnp.dot(hmid, wd_ref[...], preferred_element_type=F32)

    def store(r0, y):
        o_ref[r0:r0 + group, :] = _pack_bf16_pair(y[:, :D_PACK], y[:, D_PACK:])

    for n_live in range(n_groups + 1):
        @pl.when((valid > (n_live - 1) * group) & (valid <= n_live * group))
        def _():
            gu, y = None, None
            for g in range(n_live + 2):
                nxt = gate_up(g * group) if g < n_live else None
                if y is not None:
                    store((g - 2) * group, y)
                y = down(gu) if gu is not None else None
                gu = nxt
            if n_live < n_groups:
                o_ref[n_live * group:, :] = jnp.zeros((tm - n_live * group, D_PACK), I32)


def _moe(xs, tile_expert, tile_valid, n_live_tiles, w_g, w_u, w_d, tm):
    n_tiles = xs.shape[0] // tm
    grid_spec = pltpu.PrefetchScalarGridSpec(
        num_scalar_prefetch=3,
        grid=(n_tiles,),
        in_specs=[
            pl.BlockSpec(memory_space=pl.ANY),
            pl.BlockSpec((None, D_MODEL, D_EXPERT), lambda i, te, tv, nl: (te[i], 0, 0)),
            pl.BlockSpec((None, D_MODEL, D_EXPERT), lambda i, te, tv, nl: (te[i], 0, 0)),
            pl.BlockSpec((None, D_EXPERT, D_MODEL), lambda i, te, tv, nl: (te[i], 0, 0)),
        ],
        out_specs=pl.BlockSpec((tm, D_PACK), lambda i, te, tv, nl: (jnp.minimum(i, nl[0]), 0)),
        scratch_shapes=[pltpu.VMEM((RING_SLOTS, tm, D_PACK), I32), pltpu.SemaphoreType.DMA((RING_SLOTS,)),
                        pltpu.VMEM((D_MODEL, 2 * D_EXPERT), BF16)],
    )
    return pl.pallas_call(
        _moe_kernel,
        grid_spec=grid_spec,
        out_shape=jax.ShapeDtypeStruct(xs.shape, I32),
        compiler_params=pltpu.CompilerParams(
            dimension_semantics=("arbitrary",), vmem_limit_bytes=V7X_VMEM_LIMIT),
        name="experts",
    )(tile_expert, tile_valid, n_live_tiles, xs, w_g, w_u, w_d)


def _final_kernel(x1_ref, y0_ref, y1_ref, rw_ref, mod_ref, fg_ref, o_ref):
    y0lo, y0hi = _unpack_bf16_pair(y0_ref[...])
    y1lo, y1hi = _unpack_bf16_pair(y1_ref[...])
    w0 = rw_ref[:, 0:1]
    w1 = rw_ref[:, 1:2]
    moe = jnp.concatenate([w0 * y0lo + w1 * y1lo, w0 * y0hi + w1 * y1hi], axis=1)
    x2 = x1_ref[...].astype(F32) + mod_ref[5:6, :] * moe
    ms = jnp.mean(x2 * x2, axis=-1, keepdims=True)
    o_ref[...] = x2 * lax.rsqrt(ms + EPS) * fg_ref[...]


def _final(x1, yb, rw, mod, final_g, tm):
    B, T, _ = x1.shape
    n_blocks = B * T // tm
    flat = lambda b, i: (b * (T // tm) + i, 0)
    return pl.pallas_call(
        _final_kernel,
        grid=(B, T // tm),
        in_specs=[
            pl.BlockSpec((None, tm, D_MODEL), lambda b, i: (b, i, 0)),
            pl.BlockSpec((tm, D_PACK), flat),
            pl.BlockSpec((tm, D_PACK), lambda b, i: (n_blocks + b * (T // tm) + i, 0)),
            pl.BlockSpec((tm, LANES), flat),
            pl.BlockSpec((None, 6, D_MODEL), lambda b, i: (b, 0, 0)),
            _const_spec((1, D_MODEL)),
        ],
        out_specs=pl.BlockSpec((None, tm, D_MODEL), lambda b, i: (b, i, 0)),
        out_shape=jax.ShapeDtypeStruct((B, T, D_MODEL), F32),
        compiler_params=pltpu.CompilerParams(
            dimension_semantics=("parallel", "parallel"), vmem_limit_bytes=V7X_VMEM_LIMIT),
        name="final_combine",
    )(x1, yb, yb, rw, mod, final_g)


def _rope_tables(T):
    rows = T // GRID_W
    row = np.repeat(np.arange(rows, dtype=np.float32), GRID_W)
    col = np.tile(np.arange(GRID_W, dtype=np.float32), rows)
    inv_freq = np.float32(ROPE_THETA) ** (-np.arange(0, 64, 2, dtype=np.float32) / np.float32(64))
    ang_r = row[:, None] * inv_freq[None, :]
    ang_c = col[:, None] * inv_freq[None, :]
    ang = np.concatenate([ang_r, ang_r, ang_c, ang_c], axis=-1).astype(np.float32)
    sign = np.where((np.arange(HEAD_DIM) % 64) < 32, -1.0, 1.0).astype(np.float32)
    return jnp.asarray(np.cos(ang)), jnp.asarray(np.sin(ang) * sign[None, :])


def _dispatch_plan(counts, tm, n_tiles):
    tiles_per_e = (counts + tm - 1) // tm
    tile_end = jnp.cumsum(tiles_per_e)
    tile_start = tile_end - tiles_per_e
    tile_id = jnp.arange(n_tiles, dtype=I32)
    te = jnp.minimum(jnp.sum((tile_id[:, None] >= tile_end[None, :]).astype(I32), axis=1), N_EXPERTS - 1)
    live = jnp.clip(counts[te] - (tile_id - tile_start[te]) * tm, 0, tm)
    tv = jnp.where(tile_id < tile_end[-1], live, 0).astype(I32)
    return tile_start.astype(I32), te, tv, tile_end[-1:].astype(I32)


def _slot_kernel(start_ref, meta_ref, pos_ref, *, tile_rows):
    e = meta_ref[0:2, :]
    first_tile = jnp.zeros_like(e)
    for ex in range(N_EXPERTS):
        first_tile = jnp.where(e == ex, start_ref[ex], first_tile)
    pos_ref[...] = first_tile * tile_rows + meta_ref[2:4, :]


def _slots(meta, tile_start, tile_rows):
    n = meta.shape[1]
    tn = min(n, 8192)
    grid_spec = pltpu.PrefetchScalarGridSpec(
        num_scalar_prefetch=1,
        grid=(n // tn,),
        in_specs=[pl.BlockSpec((META_ROWS, tn), lambda i, start: (0, i))],
        out_specs=pl.BlockSpec((2, tn), lambda i, start: (0, i)),
    )
    return pl.pallas_call(
        functools.partial(_slot_kernel, tile_rows=tile_rows),
        grid_spec=grid_spec,
        out_shape=jax.ShapeDtypeStruct((2, n), I32),
        name="dispatch_slots",
    )(tile_start, meta)


def _pick_tile(T, want):
    t = min(T, want)
    assert T % t == 0
    return t


def _trunk(x, mod, p):
    B, T, _ = x.shape
    N = B * T
    tm = _pick_tile(T, 512)
    tq = _pick_tile(T, 512)
    tk = _pick_tile(T, 256)
    tme = _pick_tile(N, 512)
    cos, sin_signed = _rope_tables(T)
    u, qt, qb, k, vt = _stage_a(x, mod, p["n1g"], p["w_qkvu"], p["qg"], p["kg"], cos, sin_signed, tm)
    attn = lax.cond(
        p["bounded_softmax_ok"],
        functools.partial(_attention, tq=tq, tk=_pick_tile(T, 4096), running_max=False),
        functools.partial(_attention, tq=tq, tk=tk, running_max=True),
        qt, qb, k, vt)
    x1, h2p, meta, rw, counts = _stage_c(x, mod, p["n1g"], p["n2g"], u, attn, p["w_gate"], p["w_mix"],
                                         p["pool_scale"], p["w_pp"], p["w_ap"], p["w_o"], p["w_r"], p["b_r"], tm)
    n_tiles = 2 * N // tme + N_EXPERTS
    tile_start, te, tv, n_live_tiles = _dispatch_plan(counts[:, 0].astype(I32), tme, n_tiles)
    pos = _slots(meta, tile_start, tme)
    xs = _sc_scatter_rows(h2p, pos[0], pos[1], n_tiles * tme)
    ys = _moe(xs, te, tv, n_live_tiles, p["w_eg"], p["w_eu"], p["w_ed"], tme)
    yb = _sc_gather_rows(ys, pos.reshape(-1))
    return _final(x1, yb, rw, mod, p["final_g"], _pick_tile(T, 1024))


def kernel(x_prompt, x_sample, c_prompt, c_sample, w_ada, b_ada, norm1_g, norm2_g, w_in, q_norm_g,
           k_norm_g, w_pool_mix, pool_scale, w_pool_proj, w_attn_proj, w_o, w_router_group,
           b_router_group, w_router_expert, b_router_expert, w_exp_gate, w_exp_up, w_exp_down, final_g):
    assert w_ada.shape[0] == 1, "single-layer block"
    n_r = N_GROUPS + N_EXPERTS
    w_r = jnp.concatenate([w_router_expert[0], w_router_group[0],
                           jnp.zeros((D_MODEL, LANES - n_r), F32)], axis=1)
    w_r_hi = w_r.astype(BF16)
    b_r = jnp.concatenate([b_router_expert[0], b_router_group[0], jnp.zeros((LANES - n_r,), F32)])
    score_bound = (1.01 * LOG2E * np.sqrt(HEAD_DIM)) * jnp.max(jnp.abs(q_norm_g[0])) * jnp.max(jnp.abs(k_norm_g[0]))
    p = dict(
        bounded_softmax_ok=2.0 * score_bound <= EXP2_SAFE_SPAN,
        n1g=norm1_g[0].reshape(1, D_MODEL), n2g=norm2_g[0].reshape(1, D_MODEL),
        w_qkvu=w_in[0][:, :D_QKVU].astype(BF16), w_gate=w_in[0][:, D_QKVU:].astype(BF16),
        qg=q_norm_g[0].reshape(1, HEAD_DIM), kg=k_norm_g[0].reshape(1, HEAD_DIM),
        w_mix=w_pool_mix[0].astype(BF16), pool_scale=pool_scale[0].reshape(1, D_POOL),
        w_pp=w_pool_proj[0].astype(BF16), w_ap=w_attn_proj[0].astype(BF16), w_o=w_o[0].astype(BF16),
        w_r=jnp.concatenate([w_r_hi, (w_r - w_r_hi.astype(F32)).astype(BF16)], axis=1), b_r=b_r.reshape(1, LANES),
        w_eg=w_exp_gate[0].astype(BF16), w_eu=w_exp_up[0].astype(BF16), w_ed=w_exp_down[0].astype(BF16),
        final_g=final_g.reshape(1, D_MODEL),
    )
    n_prompt = c_prompt.shape[0]
    mod = _modulation(jnp.concatenate([c_prompt, c_sample], axis=0), w_ada[0], b_ada[0]).reshape(-1, 6, D_MODEL)
    return _trunk(x_prompt, mod[:n_prompt], p), _trunk(x_sample, mod[n_prompt:], p)
```

```python
import functools

import numpy as np
import jax
import jax.numpy as jnp
from jax import lax
from jax.experimental import pallas as pl
from jax.experimental.pallas import tpu as pltpu
from jax.experimental.pallas import tpu_sc as plsc

F32 = jnp.float32
BF16 = jnp.bfloat16
I32 = jnp.int32

D_MODEL = 1024
GRID_W = 64
POOL_WINDOWS = (2, 4, 8, 16)
POOL_GC = 128
D_POOL = 512
HEAD_DIM = 128
N_Q_HEADS = 4
N_KV_HEADS = 2
D_ATTN = 512
D_KV = 256
ROPE_THETA = 10000.0
N_GROUPS = 4
EXPERTS_PER_GROUP = 8
N_EXPERTS = 32
D_EXPERT = 256
EPS = 1e-6
LOG2E = 1.4426950408889634
EXP2_SAFE_SPAN = 100.0
D_QKVU = D_POOL + D_ATTN + 2 * D_KV
POOL_HALO = 16
STAGE_ROWS = 256
PROJ_COLS = 256
ROUTED_PHASE_AFTER_BLOCK = (0, 1, 2)
MERGE_COLS = 256
RING_SLOTS = 3
ATTN_TILES_PER_TRIP = 10
ROUTE_ROWS = 40
META_ROWS = 8
LANES = 128
D_PACK = D_MODEL // 2

V7X_VMEM_LIMIT = 56 * 1024 * 1024
SC_WINDOW = 64


def _const_spec(shape):
    nd = len(shape)
    return pl.BlockSpec(shape, lambda *_: (0,) * nd, pipeline_mode=pl.Buffered(1))


def _sigmoid(x):
    return 0.5 * jnp.tanh(0.5 * x) + 0.5


def _pack_bf16_pair(lo, hi):
    lo_bits = lax.bitcast_convert_type(lo.astype(BF16).astype(F32), jnp.uint32) >> 16
    hi_bits = lax.bitcast_convert_type(hi.astype(BF16).astype(F32), jnp.uint32) & jnp.uint32(0xFFFF0000)
    return lax.bitcast_convert_type(lo_bits | hi_bits, I32)


def _unpack_bf16_pair(packed):
    u = lax.bitcast_convert_type(packed, jnp.uint32)
    lo = lax.bitcast_convert_type(u << 16, F32)
    hi = lax.bitcast_convert_type(u & jnp.uint32(0xFFFF0000), F32)
    return lo, hi


def _input_ring(x_hbm, buf, sem, step, n_steps, rows):
    def fetch(s):
        slot = lax.rem(s, RING_SLOTS)
        src = x_hbm.at[pl.ds(pl.multiple_of(s * rows, rows), rows), :]
        return pltpu.make_async_copy(src, buf.at[slot], sem.at[slot])

    @pl.when((step == 0) & (n_steps > 0))
    def _():
        fetch(0).start()

    @pl.when((step == 0) & (n_steps > 1))
    def _():
        fetch(1).start()

    @pl.when(step + 2 < n_steps)
    def _():
        fetch(step + 2).start()

    @pl.when(step < n_steps)
    def _():
        fetch(step).wait()

    return lax.rem(step, RING_SLOTS)


def _mod_kernel(c_ref, w_ref, b_ref, o_ref):
    c = c_ref[...]
    s = c * _sigmoid(c)
    o_ref[...] = jnp.dot(s, w_ref[...], preferred_element_type=F32,
                         precision=lax.Precision.HIGHEST) + b_ref[...]


def _modulation(c, w_ada, b_ada):
    B = c.shape[0]
    n_out = w_ada.shape[1]
    bn = D_MODEL
    return pl.pallas_call(
        _mod_kernel,
        grid=(n_out // bn,),
        in_specs=[pl.BlockSpec((B, D_MODEL), lambda j: (0, 0)),
                  pl.BlockSpec((D_MODEL, bn), lambda j: (0, j)),
                  pl.BlockSpec((1, bn), lambda j: (0, j))],
        out_specs=pl.BlockSpec((B, bn), lambda j: (0, j)),
        out_shape=jax.ShapeDtypeStruct((B, n_out), F32),
        name="modulation",
    )(c, w_ada, b_ada.reshape(1, n_out))


def _scaled_norm(x, gain, shift):
    ms = jnp.mean(x * x, axis=-1, keepdims=True)
    return x * lax.rsqrt(ms + EPS) * gain + shift


def _stage_a_kernel(x_hbm, mod_ref, n1g_ref, w_ref, qg_ref, kg_ref, cos_ref, sin_ref,
                    u_ref, qt_ref, qb_ref, k_ref, vt_ref, xbuf, sem, z_ref):
    s = pl.program_id(0)
    n_tiles = pl.num_programs(0) - 1
    tm = u_ref.shape[0]
    _input_ring(x_hbm, xbuf, sem, s, n_tiles, tm)
    slot = lax.rem(jnp.minimum(s, n_tiles - 1), RING_SLOTS)

    @pl.when(s == 0)
    def _():
        z_ref[...] = jnp.zeros_like(z_ref)

    gain1 = n1g_ref[...] * (1.0 + mod_ref[1:2, :])
    shift1 = mod_ref[0:1, :]
    lane = lax.broadcasted_iota(I32, (1, HEAD_DIM), 1)
    first_half = (lane % 64) < 32
    qg = qg_ref[...] * (LOG2E / np.sqrt(HEAD_DIM))
    k_norm_max = jnp.max(jnp.abs(kg_ref[...]), axis=-1, keepdims=True) * (np.sqrt(HEAD_DIM) * 1.01)
    cos = cos_ref[...]
    sin_signed = sin_ref[...]

    def norm_rope(xh, g):
        ms = jnp.mean(xh * xh, axis=-1, keepdims=True)
        xn = xh * lax.rsqrt(ms + EPS) * g
        rot = jnp.where(first_half, pltpu.roll(xn, 96, 1), pltpu.roll(xn, 32, 1))
        return xn * cos + rot * sin_signed

    def finish(o):
        zh = z_ref[:, o:o + HEAD_DIM]
        if o < D_POOL:
            u_ref[:, o:o + HEAD_DIM] = zh.astype(BF16)
        elif o < D_POOL + D_ATTN:
            hq = (o - D_POOL) // HEAD_DIM
            q_t = norm_rope(zh, qg).T.astype(BF16)
            qt_ref[hq] = q_t
            q_f = q_t.astype(F32)
            qb_ref[hq] = jnp.sqrt(jnp.sum(q_f * q_f, axis=0, keepdims=True)) * k_norm_max
        elif o < D_POOL + D_ATTN + D_KV:
            k_ref[:, o - D_POOL - D_ATTN:o - D_POOL - D_ATTN + HEAD_DIM] = norm_rope(zh, kg_ref[...]).astype(BF16)
        else:
            vt_ref[(o - D_POOL - D_ATTN - D_KV) // HEAD_DIM] = zh.T.astype(BF16)

    h = _scaled_norm(xbuf[slot], gain1, shift1).astype(BF16)
    for c in range(0, D_QKVU, PROJ_COLS):
        for o in range(c, c + PROJ_COLS, HEAD_DIM):
            finish(o)
        z_ref[:, c:c + PROJ_COLS] = jnp.dot(h, w_ref[:, c:c + PROJ_COLS], preferred_element_type=F32)


def _stage_a(x, mod, n1g, w_qkvu, qg, kg, cos, sin_signed, tm):
    B, T, _ = x.shape
    nt = T // tm
    n_tiles = B * nt
    projected = lambda s: jnp.minimum(s, n_tiles - 1)
    finished = lambda s: jnp.maximum(s - 1, 0)
    return pl.pallas_call(
        _stage_a_kernel,
        grid=(n_tiles + 1,),
        in_specs=[
            pl.BlockSpec(memory_space=pl.ANY),
            pl.BlockSpec((None, 6, D_MODEL), lambda s: (projected(s) // nt, 0, 0)),
            _const_spec((1, D_MODEL)),
            _const_spec((D_MODEL, D_QKVU)),
            _const_spec((1, HEAD_DIM)),
            _const_spec((1, HEAD_DIM)),
            pl.BlockSpec((tm, HEAD_DIM), lambda s: (lax.rem(finished(s), nt), 0)),
            pl.BlockSpec((tm, HEAD_DIM), lambda s: (lax.rem(finished(s), nt), 0)),
        ],
        out_specs=[
            pl.BlockSpec((None, tm, D_POOL), lambda s: (finished(s) // nt, lax.rem(finished(s), nt), 0)),
            pl.BlockSpec((None, N_Q_HEADS, HEAD_DIM, tm), lambda s: (finished(s) // nt, 0, 0, lax.rem(finished(s), nt))),
            pl.BlockSpec((None, N_Q_HEADS, 1, tm), lambda s: (finished(s) // nt, 0, 0, lax.rem(finished(s), nt))),
            pl.BlockSpec((None, tm, D_KV), lambda s: (finished(s) // nt, lax.rem(finished(s), nt), 0)),
            pl.BlockSpec((None, N_KV_HEADS, HEAD_DIM, tm), lambda s: (finished(s) // nt, 0, 0, lax.rem(finished(s), nt))),
        ],
        out_shape=[
            jax.ShapeDtypeStruct((B, T, D_POOL), BF16),
            jax.ShapeDtypeStruct((B, N_Q_HEADS, HEAD_DIM, T), BF16),
            jax.ShapeDtypeStruct((B, N_Q_HEADS, 1, T), F32),
            jax.ShapeDtypeStruct((B, T, D_KV), BF16),
            jax.ShapeDtypeStruct((B, N_KV_HEADS, HEAD_DIM, T), BF16),
        ],
        scratch_shapes=[pltpu.VMEM((RING_SLOTS, tm, D_MODEL), F32), pltpu.SemaphoreType.DMA((RING_SLOTS,)),
                        pltpu.VMEM((tm, D_QKVU), F32)],
        compiler_params=pltpu.CompilerParams(
            dimension_semantics=("arbitrary",), vmem_limit_bytes=V7X_VMEM_LIMIT),
        name="stage_a",
    )(x.reshape(B * T, D_MODEL), mod, n1g, w_qkvu, qg, kg, cos, sin_signed)


def _write_attn_output(o_ref, acc, l, tq):
    o = acc * (1.0 / l)
    o_ref[:, :HEAD_DIM] = o[:, :tq].T.astype(BF16)
    o_ref[:, HEAD_DIM:] = o[:, tq:].T.astype(BF16)


def _attn_bounded_kernel(qt_ref, qb_ref, k_ref, vt_ref, o_ref, p_ref, acc_ref, l_ref, *, tk):
    T = k_ref.shape[0]
    tq = qt_ref.shape[2]
    nq = 2 * tq
    nk = T // tk
    qt = jnp.concatenate([qt_ref[0], qt_ref[1]], axis=1)
    qb = jnp.concatenate([qb_ref[0], qb_ref[1]], axis=1)

    def weights(j, slot):
        off = pl.multiple_of(j * tk, tk)
        st = jnp.dot(k_ref[pl.ds(off, tk), :], qt, preferred_element_type=F32)
        p = jnp.exp2(st - qb)
        l_ref[...] += jnp.sum(p.reshape(tk // 8, 8, nq), axis=0)
        p_ref[slot] = p.astype(BF16)

    def values(j, slot):
        off = pl.multiple_of(j * tk, tk)
        acc_ref[...] += jnp.dot(vt_ref[:, pl.ds(off, tk)], p_ref[slot], preferred_element_type=F32)

    def step(j, slot):
        values(j - 1, 1 - slot)
        weights(j, slot)

    acc_ref[...] = jnp.zeros_like(acc_ref)
    l_ref[...] = jnp.zeros_like(l_ref)
    weights(0, 0)

    unroll = max(1, min(ATTN_TILES_PER_TRIP, nk - 1))
    trips = (nk - 1) // unroll if unroll % 2 == 0 else 0

    if trips > 1:
        @pl.loop(0, trips)
        def _(g):
            for d in range(unroll):
                step(1 + unroll * g + d, (1 + d) % 2)
    else:
        trips = 0
    for j in range(1 + trips * unroll, nk):
        step(j, j % 2)
    values(nk - 1, (nk - 1) % 2)
    _write_attn_output(o_ref, acc_ref[...], jnp.sum(l_ref[...], axis=0, keepdims=True), tq)


def _attn_online_kernel(qt_ref, qb_ref, k_ref, vt_ref, o_ref, *, tk):
    del qb_ref
    T = k_ref.shape[0]
    tq = qt_ref.shape[2]
    nq = 2 * tq
    qt = jnp.concatenate([qt_ref[0], qt_ref[1]], axis=1)

    def body(s, carry):
        m, l, acc = carry
        off = pl.multiple_of(s * tk, tk)
        st = jnp.dot(k_ref[pl.ds(off, tk), :], qt, preferred_element_type=F32)
        m_new = jnp.maximum(m, jnp.max(st, axis=0, keepdims=True))
        alpha = jnp.exp2(m - m_new)
        p = jnp.exp2(st - m_new)
        l = alpha * l + jnp.sum(p, axis=0, keepdims=True)
        pv = jnp.dot(vt_ref[:, pl.ds(off, tk)], p.astype(BF16), preferred_element_type=F32)
        return m_new, l, alpha * acc + pv

    init = (jnp.full((1, nq), -jnp.inf, F32), jnp.zeros((1, nq), F32), jnp.zeros((HEAD_DIM, nq), F32))
    _, l, acc = lax.fori_loop(0, T // tk, body, init)
    _write_attn_output(o_ref, acc, l, tq)


def _attention(qt, qb, k, vt, tq, tk, running_max):
    B, _, _, T = qt.shape
    group = N_Q_HEADS // N_KV_HEADS
    nq = group * tq
    if running_max:
        body, scratch, name = functools.partial(_attn_online_kernel, tk=tk), [], "attention_online_max"
    else:
        body = functools.partial(_attn_bounded_kernel, tk=tk)
        scratch = [pltpu.VMEM((2, tk, nq), BF16), pltpu.VMEM((HEAD_DIM, nq), F32), pltpu.VMEM((8, nq), F32)]
        name = "attention_bounded"
    return pl.pallas_call(
        body,
        grid=(B, N_KV_HEADS, T // tq),
        in_specs=[
            pl.BlockSpec((None, group, HEAD_DIM, tq), lambda b, j, i: (b, j, 0, i)),
            pl.BlockSpec((None, group, 1, tq), lambda b, j, i: (b, j, 0, i)),
            pl.BlockSpec((None, T, HEAD_DIM), lambda b, j, i: (b, 0, j)),
            pl.BlockSpec((None, None, HEAD_DIM, T), lambda b, j, i: (b, j, 0, 0)),
        ],
        out_specs=pl.BlockSpec((None, tq, group * HEAD_DIM), lambda b, j, i: (b, i, j)),
        out_shape=jax.ShapeDtypeStruct((B, T, D_ATTN), BF16),
        scratch_shapes=scratch,
        compiler_params=pltpu.CompilerParams(
            dimension_semantics=("parallel", "parallel", "parallel"),
            vmem_limit_bytes=V7X_VMEM_LIMIT),
        name=name,
    )(qt, qb, k, vt)


def _route(logits_t):
    neg = jnp.float32(-jnp.inf)
    n = logits_t.shape[1]

    def top(vals, idx, far):
        v = jnp.max(vals, axis=0, keepdims=True)
        i = jnp.min(jnp.where(vals == v, idx, far), axis=0, keepdims=True)
        return v, i

    gidx = lax.broadcasted_iota(I32, (ROUTE_ROWS - N_EXPERTS, n), 0).astype(F32)
    grp = jnp.where(gidx < N_GROUPS, logits_t[N_EXPERTS:, :], neg)
    gmax, gsel = top(grp, gidx, jnp.float32(ROUTE_ROWS))
    p_group = 1.0 / jnp.sum(jnp.exp(grp - gmax), axis=0, keepdims=True)
    eidx = lax.broadcasted_iota(I32, (N_EXPERTS, n), 0).astype(F32)
    first = EXPERTS_PER_GROUP * gsel
    le = jnp.where((eidx >= first) & (eidx < first + EXPERTS_PER_GROUP), logits_t[:N_EXPERTS, :], neg)
    v1, e1 = top(le, eidx, jnp.float32(N_EXPERTS))
    v2, e2 = top(jnp.where(eidx == e1, neg, le), eidx, jnp.float32(N_EXPERTS))
    e21 = jnp.exp(v2 - v1)
    w1 = p_group / (1.0 + e21)
    return e1, e2, w1, w1 * e21


def _pool_deviation(ue_ref, lvl_ref, pool_ref, tile_start, tm, seq_len, groups):
    n = tm + 2 * POOL_HALO
    t = tile_start + lax.broadcasted_iota(I32, (tm, 1), 0)
    for gi in groups:
        w = POOL_WINDOWS[gi]
        half = w // 2
        cols = slice(gi * POOL_GC, (gi + 1) * POOL_GC)
        level = ue_ref[0:n, cols] + ue_ref[1:n + 1, cols]
        span, buf = 2, 0
        lvl_ref[buf, 0:n, :] = level
        while span < w:
            level = lvl_ref[buf, 0:n, :] + lvl_ref[buf, span:span + n, :]
            span, buf = 2 * span, 1 - buf
            lvl_ref[buf, 0:n, :] = level
        wsum = lvl_ref[buf, POOL_HALO - half:POOL_HALO - half + tm, :]
        cnt = (jnp.minimum(t + half, seq_len) - jnp.maximum(t - half, 0)).astype(F32)
        p = wsum * (1.0 / cnt) - ue_ref[POOL_HALO:POOL_HALO + tm, cols]
        pool_ref[:, cols] = p.astype(BF16)


def _norm2_split(x1, gain2, shift2, h2_ref):
    h2 = _scaled_norm(x1, gain2, shift2)
    h2_ref[...] = _pack_bf16_pair(h2[:, :D_PACK], h2[:, D_PACK:])
    h_hi = h2.astype(BF16)
    return h_hi, (h2 - h_hi.astype(F32)).astype(BF16)


def _router_logits(h_hi, h_lo, wr_ref, br_ref):
    parts = (jnp.dot(h_hi, wr_ref[...], preferred_element_type=F32)
             + jnp.dot(h_lo, wr_ref[...], preferred_element_type=F32))
    return parts[:, :LANES] + parts[:, LANES:] + br_ref[...]


def _dispatch_table(logits, is_real, utri_ref, run_ref, meta_ref, rw_ref):
    n = logits.shape[0]
    e1, e2, w1, w2 = _route(logits.T[0:ROUTE_ROWS, :])

    eidx = lax.broadcasted_iota(I32, (N_EXPERTS, n), 0).astype(F32)
    hit1 = eidx == e1
    hit2 = eidx == e2
    taken = jnp.where(hit1 | hit2, is_real, 0.0)
    before = run_ref[...] + jnp.dot(taken.astype(BF16), utri_ref[...], preferred_element_type=F32)
    rank1 = jnp.sum(jnp.where(hit1, before, 0.0), axis=0, keepdims=True)
    rank2 = jnp.sum(jnp.where(hit2, before, 0.0), axis=0, keepdims=True)
    run_ref[...] = run_ref[...] + jnp.sum(taken, axis=1, keepdims=True)
    ridx = lax.broadcasted_iota(I32, (META_ROWS, n), 0)
    table = jnp.where(ridx == 0, e1, jnp.where(ridx == 1, e2, jnp.where(ridx == 2, rank1,
                      jnp.where(ridx == 3, rank2, 0.0))))
    meta_ref[...] = table.astype(I32)
    weights_t = jnp.where(ridx == 0, w1, jnp.where(ridx == 1, w2, 0.0))
    rw_ref[...] = jnp.concatenate([weights_t, jnp.zeros((LANES - META_ROWS, n), F32)], axis=0).T


def _stage_c_kernel(x_ref, mod_ref, modp_ref, n1g_ref, n2g_ref, u_ref, up_ref, un_ref, attn_ref,
                    wgate_ref, wmix_ref, ps_ref, wpp_ref, wap_ref, wo_ref, wr_ref, br_ref, utri_ref,
                    x1_ref, h2_ref, meta_ref, rw_ref, count_ref,
                    ue_ref, lvl_ref, pool_ref, run_ref, merged_ref, x1_prev, *, seq_len):
    s = pl.program_id(0)
    tm = x_ref.shape[0]
    tiles_per_seq = seq_len // tm
    i = lax.rem(jnp.minimum(s, pl.num_programs(0) - 2), tiles_per_seq)

    @pl.when(s == 0)
    def _():
        run_ref[...] = jnp.zeros_like(run_ref)
        x1_prev[...] = jnp.zeros_like(x1_prev)

    gain1 = n1g_ref[...] * (1.0 + mod_ref[1:2, :])
    shift1 = mod_ref[0:1, :]
    res_gate = mod_ref[2:3, :]
    x = x_ref[...]
    h = _scaled_norm(x, gain1, shift1).astype(BF16)
    ue_ref[0:POOL_HALO, :] = up_ref[...].astype(F32) * (i > 0).astype(F32)
    ue_ref[POOL_HALO:POOL_HALO + tm, :] = u_ref[...].astype(F32)
    ue_ref[POOL_HALO + tm:2 * POOL_HALO + tm, :] = un_ref[...].astype(F32) * (i < tiles_per_seq - 1).astype(F32)
    ue_ref[2 * POOL_HALO + tm:, :] = jnp.zeros((POOL_HALO, D_POOL), F32)
    lvl_ref[:, 2 * POOL_HALO + tm:, :] = jnp.zeros((2, POOL_HALO, POOL_GC), F32)
    _pool_deviation(ue_ref, lvl_ref, pool_ref, i * tm, tm, seq_len, range(len(POOL_WINDOWS)))

    attn = attn_ref[...]
    mixed_groups = []
    for gi in range(len(POOL_WINDOWS)):
        cols = slice(gi * POOL_GC, (gi + 1) * POOL_GC)
        pm = jnp.dot(pool_ref[:, cols], wmix_ref[gi], preferred_element_type=F32)
        mixed_groups.append(pm * ps_ref[:, cols])
    pool = jnp.concatenate(mixed_groups, axis=1).astype(BF16)
    assert max(ROUTED_PHASE_AFTER_BLOCK) < D_MODEL // MERGE_COLS
    for c in range(0, D_MODEL, MERGE_COLS):
        cols = slice(c, c + MERGE_COLS)
        gate_cols = slice(D_MODEL + c, D_MODEL + c + MERGE_COLS)
        gate_pool = _sigmoid(jnp.dot(h, wgate_ref[:, cols], preferred_element_type=F32))
        gate_attn = _sigmoid(jnp.dot(h, wgate_ref[:, gate_cols], preferred_element_type=F32))
        merged = (gate_pool * jnp.dot(pool, wpp_ref[:, cols], preferred_element_type=F32)
                  + gate_attn * jnp.dot(attn, wap_ref[:, cols], preferred_element_type=F32))
        merged_ref[:, cols] = merged.astype(BF16)
        block = c // MERGE_COLS
        if block == ROUTED_PHASE_AFTER_BLOCK[0]:
            h_hi, h_lo = _norm2_split(x1_prev[...], n2g_ref[...] * (1.0 + modp_ref[4:5, :]), modp_ref[3:4, :],
                                      h2_ref)
        if block == ROUTED_PHASE_AFTER_BLOCK[1]:
            logits = _router_logits(h_hi, h_lo, wr_ref, br_ref)
        if block == ROUTED_PHASE_AFTER_BLOCK[2]:
            _dispatch_table(logits, (s > 0).astype(F32), utri_ref, run_ref, meta_ref, rw_ref)
            count_ref[...] = jnp.broadcast_to(run_ref[...], count_ref.shape)
    for c in range(0, D_MODEL, MERGE_COLS):
        cols = slice(c, c + MERGE_COLS)
        mixed = jnp.dot(merged_ref[...], wo_ref[:, cols], preferred_element_type=F32)
        x1 = x[:, cols] + res_gate[:, cols] * mixed
        x1_ref[:, cols] = x1.astype(BF16)
        x1_prev[:, cols] = x1


def _stage_c(x, mod, n1g, n2g, u, attn, w_gate, w_mix, pool_scale, w_pp, w_ap, w_o, w_r, b_r, tm):
    B, T, _ = x.shape
    hb = tm // POOL_HALO
    n_halo_blocks = T // POOL_HALO
    nt = T // tm
    n_tiles = B * nt

    def merged_tile(s):
        m = jnp.minimum(s, n_tiles - 1)
        return m // nt, lax.rem(m, nt)

    def tok(s):
        b, i = merged_tile(s)
        return b, i, 0

    def halo_before(s):
        b, i = merged_tile(s)
        return b, jnp.maximum(i * hb - 1, 0), 0

    def halo_after(s):
        b, i = merged_tile(s)
        return b, jnp.minimum((i + 1) * hb, n_halo_blocks - 1), 0

    routed = lambda s: jnp.maximum(s - 1, 0)
    flat = lambda s: (routed(s), 0)
    utri = jnp.triu(jnp.ones((tm, tm), BF16), k=1)
    return pl.pallas_call(
        functools.partial(_stage_c_kernel, seq_len=T),
        grid=(n_tiles + 1,),
        in_specs=[
            pl.BlockSpec((None, tm, D_MODEL), tok),
            pl.BlockSpec((None, 6, D_MODEL), lambda s: (merged_tile(s)[0], 0, 0)),
            pl.BlockSpec((None, 6, D_MODEL), lambda s: (routed(s) // nt, 0, 0)),
            _const_spec((1, D_MODEL)),
            _const_spec((1, D_MODEL)),
            pl.BlockSpec((None, tm, D_POOL), tok),
            pl.BlockSpec((None, POOL_HALO, D_POOL), halo_before),
            pl.BlockSpec((None, POOL_HALO, D_POOL), halo_after),
            pl.BlockSpec((None, tm, D_ATTN), tok),
            _const_spec((D_MODEL, 2 * D_MODEL)),
            _const_spec((len(POOL_WINDOWS), POOL_GC, POOL_GC)),
            _const_spec((1, D_POOL)),
            _const_spec((D_POOL, D_MODEL)),
            _const_spec((D_ATTN, D_MODEL)),
            _const_spec((D_MODEL, D_MODEL)),
            _const_spec((D_MODEL, 2 * LANES)),
            _const_spec((1, LANES)),
            _const_spec((tm, tm)),
        ],
        out_specs=[
            pl.BlockSpec((None, tm, D_MODEL), tok),
            pl.BlockSpec((tm, D_PACK), flat),
            pl.BlockSpec((META_ROWS, tm), lambda s: (0, routed(s))),
            pl.BlockSpec((tm, LANES), flat),
            pl.BlockSpec((N_EXPERTS, LANES), lambda s: (0, 0)),
        ],
        out_shape=[
            jax.ShapeDtypeStruct((B, T, D_MODEL), BF16),
            jax.ShapeDtypeStruct((B * T, D_PACK), I32),
            jax.ShapeDtypeStruct((META_ROWS, B * T), I32),
            jax.ShapeDtypeStruct((B * T, LANES), F32),
            jax.ShapeDtypeStruct((N_EXPERTS, LANES), F32),
        ],
        scratch_shapes=[pltpu.VMEM((tm + 3 * POOL_HALO, D_POOL), F32),
                        pltpu.VMEM((2, tm + 3 * POOL_HALO, POOL_GC), F32),
                        pltpu.VMEM((tm, D_POOL), BF16),
                        pltpu.VMEM((N_EXPERTS, 1), F32),
                        pltpu.VMEM((tm, D_MODEL), BF16),
                        pltpu.VMEM((tm, D_MODEL), F32)],
        compiler_params=pltpu.CompilerParams(
            dimension_semantics=("arbitrary",), vmem_limit_bytes=V7X_VMEM_LIMIT),
        name="stage_c",
    )(x, mod, mod, n1g, n2g, u, u, u, attn, w_gate, w_mix, pool_scale, w_pp, w_ap, w_o, w_r, b_r, utri)


def _sc_workers():
    info = plsc.get_sparse_core_info()
    return info.num_cores, info.num_subcores


def _sc_scatter_rows(rows, idx0, idx1, n_out):
    n, d = rows.shape
    nc, ns = _sc_workers()
    per_worker = n // (nc * ns)
    n_win = per_worker // SC_WINDOW
    mesh = plsc.VectorSubcoreMesh(core_axis_name="c", subcore_axis_name="s")

    @functools.partial(
        pl.kernel, mesh=mesh,
        out_type=jax.ShapeDtypeStruct((n_out, d), rows.dtype),
        scratch_types=[pltpu.VMEM((SC_WINDOW,), I32), pltpu.VMEM((SC_WINDOW,), I32),
                       pltpu.VMEM((SC_WINDOW, d), rows.dtype)],
        name="sc_scatter_rows",
    )
    def scatter(rows_hbm, idx0_hbm, idx1_hbm, out_hbm, i0_v, i1_v, rows_v):
        wid = lax.axis_index("s") * nc + lax.axis_index("c")

        @pl.loop(0, n_win)
        def _(w):
            base = wid * per_worker + w * SC_WINDOW
            pltpu.sync_copy(rows_hbm.at[pl.ds(base, SC_WINDOW)], rows_v)
            pltpu.sync_copy(idx0_hbm.at[pl.ds(base, SC_WINDOW)], i0_v)
            pltpu.sync_copy(idx1_hbm.at[pl.ds(base, SC_WINDOW)], i1_v)
            pltpu.sync_copy(rows_v, out_hbm.at[i0_v])
            pltpu.sync_copy(rows_v, out_hbm.at[i1_v])

    return scatter(rows, idx0, idx1)


def _sc_gather_rows(table, idx):
    n = idx.shape[0]
    d = table.shape[1]
    nc, ns = _sc_workers()
    per_worker = n // (nc * ns)
    n_win = per_worker // SC_WINDOW
    mesh = plsc.VectorSubcoreMesh(core_axis_name="c", subcore_axis_name="s")

    @functools.partial(
        pl.kernel, mesh=mesh,
        out_type=jax.ShapeDtypeStruct((n, d), table.dtype),
        scratch_types=[pltpu.VMEM((SC_WINDOW,), I32), pltpu.VMEM((SC_WINDOW, d), table.dtype)],
        name="sc_gather_rows",
    )
    def gather(table_hbm, idx_hbm, out_hbm, i_v, rows_v):
        wid = lax.axis_index("s") * nc + lax.axis_index("c")

        @pl.loop(0, n_win)
        def _(w):
            base = wid * per_worker + w * SC_WINDOW
            pltpu.sync_copy(idx_hbm.at[pl.ds(base, SC_WINDOW)], i_v)
            pltpu.sync_copy(table_hbm.at[i_v], rows_v)
            pltpu.sync_copy(rows_v, out_hbm.at[pl.ds(base, SC_WINDOW)])

    return gather(table, idx)


def _moe_kernel(te_ref, tv_ref, nl_ref, x_hbm, wg_ref, wu_ref, wd_ref, o_ref, xbuf, sem, wgu_ref):
    i = pl.program_id(0)
    n_live_tiles = nl_ref[0]
    valid = tv_ref[i]
    tm = o_ref.shape[0]
    slot = _input_ring(x_hbm, xbuf, sem, i, n_live_tiles, tm)

    @pl.when((i == 0) | (te_ref[i] != te_ref[jnp.maximum(i - 1, 0)]))
    def _():
        wgu_ref[:, :D_EXPERT] = wg_ref[...]
        wgu_ref[:, D_EXPERT:] = wu_ref[...]

    group = min(STAGE_ROWS, tm)
    n_groups = tm // group

    def gate_up(r0):
        live = lax.broadcasted_iota(I32, (group, 1), 0) < valid - r0
        lo, hi = _unpack_bf16_pair(jnp.where(live, xbuf[slot, r0:r0 + group, :], 0))
        x = jnp.concatenate([lo, hi], axis=1).astype(BF16)
        return jnp.dot(x, wgu_ref[...], preferred_element_type=F32)

    def down(gu):
        a = gu[:, :D_EXPERT]
        hmid = (a * _sigmoid(a) * gu[:, D_EXPERT:]).astype(BF16)
        return jnp.dot(hmid, wd_ref[...], preferred_element_type=F32)

    def store(r0, y):
        o_ref[r0:r0 + group, :] = _pack_bf16_pair(y[:, :D_PACK], y[:, D_PACK:])

    for n_live in range(n_groups + 1):
        @pl.when((valid > (n_live - 1) * group) & (valid <= n_live * group))
        def _():
            gu, y = None, None
            for g in range(n_live + 2):
                nxt = gate_up(g * group) if g < n_live else None
                if y is not None:
                    store((g - 2) * group, y)
                y = down(gu) if gu is not None else None
                gu = nxt
            if n_live < n_groups:
                o_ref[n_live * group:, :] = jnp.zeros((tm - n_live * group, D_PACK), I32)


def _moe(xs, tile_expert, tile_valid, n_live_tiles, w_g, w_u, w_d, tm):
    n_tiles = xs.shape[0] // tm
    grid_spec = pltpu.PrefetchScalarGridSpec(
        num_scalar_prefetch=3,
        grid=(n_tiles,),
        in_specs=[
            pl.BlockSpec(memory_space=pl.ANY),
            pl.BlockSpec((None, D_MODEL, D_EXPERT), lambda i, te, tv, nl: (te[i], 0, 0)),
            pl.BlockSpec((None, D_MODEL, D_EXPERT), lambda i, te, tv, nl: (te[i], 0, 0)),
            pl.BlockSpec((None, D_EXPERT, D_MODEL), lambda i, te, tv, nl: (te[i], 0, 0)),
        ],
        out_specs=pl.BlockSpec((tm, D_PACK), lambda i, te, tv, nl: (jnp.minimum(i, nl[0]), 0)),
        scratch_shapes=[pltpu.VMEM((RING_SLOTS, tm, D_PACK), I32), pltpu.SemaphoreType.DMA((RING_SLOTS,)),
                        pltpu.VMEM((D_MODEL, 2 * D_EXPERT), BF16)],
    )
    return pl.pallas_call(
        _moe_kernel,
        grid_spec=grid_spec,
        out_shape=jax.ShapeDtypeStruct(xs.shape, I32),
        compiler_params=pltpu.CompilerParams(
            dimension_semantics=("arbitrary",), vmem_limit_bytes=V7X_VMEM_LIMIT),
        name="experts",
    )(tile_expert, tile_valid, n_live_tiles, xs, w_g, w_u, w_d)


def _final_kernel(x1_ref, y0_ref, y1_ref, rw_ref, mod_ref, fg_ref, o_ref):
    y0lo, y0hi = _unpack_bf16_pair(y0_ref[...])
    y1lo, y1hi = _unpack_bf16_pair(y1_ref[...])
    w0 = rw_ref[:, 0:1]
    w1 = rw_ref[:, 1:2]
    moe = jnp.concatenate([w0 * y0lo + w1 * y1lo, w0 * y0hi + w1 * y1hi], axis=1)
    x2 = x1_ref[...].astype(F32) + mod_ref[5:6, :] * moe
    ms = jnp.mean(x2 * x2, axis=-1, keepdims=True)
    o_ref[...] = x2 * lax.rsqrt(ms + EPS) * fg_ref[...]


def _final(x1, yb, rw, mod, final_g, tm):
    B, T, _ = x1.shape
    n_blocks = B * T // tm
    flat = lambda b, i: (b * (T // tm) + i, 0)
    return pl.pallas_call(
        _final_kernel,
        grid=(B, T // tm),
        in_specs=[
            pl.BlockSpec((None, tm, D_MODEL), lambda b, i: (b, i, 0)),
            pl.BlockSpec((tm, D_PACK), flat),
            pl.BlockSpec((tm, D_PACK), lambda b, i: (n_blocks + b * (T // tm) + i, 0)),
            pl.BlockSpec((tm, LANES), flat),
            pl.BlockSpec((None, 6, D_MODEL), lambda b, i: (b, 0, 0)),
            _const_spec((1, D_MODEL)),
        ],
        out_specs=pl.BlockSpec((None, tm, D_MODEL), lambda b, i: (b, i, 0)),
        out_shape=jax.ShapeDtypeStruct((B, T, D_MODEL), F32),
        compiler_params=pltpu.CompilerParams(
            dimension_semantics=("parallel", "parallel"), vmem_limit_bytes=V7X_VMEM_LIMIT),
        name="final_combine",
    )(x1, yb, yb, rw, mod, final_g)


def _rope_tables(T):
    rows = T // GRID_W
    row = np.repeat(np.arange(rows, dtype=np.float32), GRID_W)
    col = np.tile(np.arange(GRID_W, dtype=np.float32), rows)
    inv_freq = np.float32(ROPE_THETA) ** (-np.arange(0, 64, 2, dtype=np.float32) / np.float32(64))
    ang_r = row[:, None] * inv_freq[None, :]
    ang_c = col[:, None] * inv_freq[None, :]
    ang = np.concatenate([ang_r, ang_r, ang_c, ang_c], axis=-1).astype(np.float32)
    sign = np.where((np.arange(HEAD_DIM) % 64) < 32, -1.0, 1.0).astype(np.float32)
    return jnp.asarray(np.cos(ang)), jnp.asarray(np.sin(ang) * sign[None, :])


def _dispatch_plan(counts, tm, n_tiles):
    tiles_per_e = (counts + tm - 1) // tm
    tile_end = jnp.cumsum(tiles_per_e)
    tile_start = tile_end - tiles_per_e
    tile_id = jnp.arange(n_tiles, dtype=I32)
    te = jnp.minimum(jnp.sum((tile_id[:, None] >= tile_end[None, :]).astype(I32), axis=1), N_EXPERTS - 1)
    live = jnp.clip(counts[te] - (tile_id - tile_start[te]) * tm, 0, tm)
    tv = jnp.where(tile_id < tile_end[-1], live, 0).astype(I32)
    return tile_start.astype(I32), te, tv, tile_end[-1:].astype(I32)


def _slot_kernel(start_ref, meta_ref, pos_ref, *, tile_rows):
    e = meta_ref[0:2, :]
    first_tile = jnp.zeros_like(e)
    for ex in range(N_EXPERTS):
        first_tile = jnp.where(e == ex, start_ref[ex], first_tile)
    pos_ref[...] = first_tile * tile_rows + meta_ref[2:4, :]


def _slots(meta, tile_start, tile_rows):
    n = meta.shape[1]
    tn = min(n, 8192)
    grid_spec = pltpu.PrefetchScalarGridSpec(
        num_scalar_prefetch=1,
        grid=(n // tn,),
        in_specs=[pl.BlockSpec((META_ROWS, tn), lambda i, start: (0, i))],
        out_specs=pl.BlockSpec((2, tn), lambda i, start: (0, i)),
    )
    return pl.pallas_call(
        functools.partial(_slot_kernel, tile_rows=tile_rows),
        grid_spec=grid_spec,
        out_shape=jax.ShapeDtypeStruct((2, n), I32),
        name="dispatch_slots",
    )(tile_start, meta)


def _pick_tile(T, want):
    t = min(T, want)
    assert T % t == 0
    return t


def _trunk(x, mod, p):
    B, T, _ = x.shape
    N = B * T
    tm = _pick_tile(T, 512)
    tq = _pick_tile(T, 512)
    tk = _pick_tile(T, 256)
    tme = _pick_tile(N, 1024)
    cos, sin_signed = _rope_tables(T)
    u, qt, qb, k, vt = _stage_a(x, mod, p["n1g"], p["w_qkvu"], p["qg"], p["kg"], cos, sin_signed, tm)
    attn = lax.cond(
        p["bounded_softmax_ok"],
        functools.partial(_attention, tq=tq, tk=_pick_tile(T, 4096), running_max=False),
        functools.partial(_attention, tq=tq, tk=tk, running_max=True),
        qt, qb, k, vt)
    x1, h2p, meta, rw, counts = _stage_c(x, mod, p["n1g"], p["n2g"], u, attn, p["w_gate"], p["w_mix"],
                                         p["pool_scale"], p["w_pp"], p["w_ap"], p["w_o"], p["w_r"], p["b_r"], tm)
    n_tiles = 2 * N // tme + N_EXPERTS
    tile_start, te, tv, n_live_tiles = _dispatch_plan(counts[:, 0].astype(I32), tme, n_tiles)
    pos = _slots(meta, tile_start, tme)
    xs = _sc_scatter_rows(h2p, pos[0], pos[1], n_tiles * tme)
    ys = _moe(xs, te, tv, n_live_tiles, p["w_eg"], p["w_eu"], p["w_ed"], tme)
    yb = _sc_gather_rows(ys, pos.reshape(-1))
    return _final(x1, yb, rw, mod, p["final_g"], _pick_tile(T, 1024))


def kernel(x_prompt, x_sample, c_prompt, c_sample, w_ada, b_ada, norm1_g, norm2_g, w_in, q_norm_g,
           k_norm_g, w_pool_mix, pool_scale, w_pool_proj, w_attn_proj, w_o, w_router_group,
           b_router_group, w_router_expert, b_router_expert, w_exp_gate, w_exp_up, w_exp_down, final_g):
    assert w_ada.shape[0] == 1, "single-layer block"
    n_r = N_GROUPS + N_EXPERTS
    w_r = jnp.concatenate([w_router_expert[0], w_router_group[0],
                           jnp.zeros((D_MODEL, LANES - n_r), F32)], axis=1)
    w_r_hi = w_r.astype(BF16)
    b_r = jnp.concatenate([b_router_expert[0], b_router_group[0], jnp.zeros((LANES - n_r,), F32)])
    score_bound = (1.01 * LOG2E * np.sqrt(HEAD_DIM)) * jnp.max(jnp.abs(q_norm_g[0])) * jnp.max(jnp.abs(k_norm_g[0]))
    p = dict(
        bounded_softmax_ok=2.0 * score_bound <= EXP2_SAFE_SPAN,
        n1g=norm1_g[0].reshape(1, D_MODEL), n2g=norm2_g[0].reshape(1, D_MODEL),
        w_qkvu=w_in[0][:, :D_QKVU].astype(BF16), w_gate=w_in[0][:, D_QKVU:].astype(BF16),
        qg=q_norm_g[0].reshape(1, HEAD_DIM), kg=k_norm_g[0].reshape(1, HEAD_DIM),
        w_mix=w_pool_mix[0].astype(BF16), pool_scale=pool_scale[0].reshape(1, D_POOL),
        w_pp=w_pool_proj[0].astype(BF16), w_ap=w_attn_proj[0].astype(BF16), w_o=w_o[0].astype(BF16),
        w_r=jnp.concatenate([w_r_hi, (w_r - w_r_hi.astype(F32)).astype(BF16)], axis=1), b_r=b_r.reshape(1, LANES),
        w_eg=w_exp_gate[0].astype(BF16), w_eu=w_exp_up[0].astype(BF16), w_ed=w_exp_down[0].astype(BF16),
        final_g=final_g.reshape(1, D_MODEL),
    )
    n_prompt = c_prompt.shape[0]
    mod = _modulation(jnp.concatenate([c_prompt, c_sample], axis=0), w_ada[0], b_ada[0]).reshape(-1, 6, D_MODEL)
    return _trunk(x_prompt, mod[:n_prompt], p), _trunk(x_sample, mod[n_prompt:], p)
```

```python
import functools

import numpy as np
import jax
import jax.numpy as jnp
from jax import lax
from jax.experimental import pallas as pl
from jax.experimental.pallas import tpu as pltpu
from jax.experimental.pallas import tpu_sc as plsc

F32 = jnp.float32
BF16 = jnp.bfloat16
I32 = jnp.int32

D_MODEL = 1024
GRID_W = 64
POOL_WINDOWS = (2, 4, 8, 16)
POOL_GC = 128
D_POOL = 512
HEAD_DIM = 128
N_Q_HEADS = 4
N_KV_HEADS = 2
D_ATTN = 512
D_KV = 256
ROPE_THETA = 10000.0
N_GROUPS = 4
EXPERTS_PER_GROUP = 8
N_EXPERTS = 32
D_EXPERT = 256
EPS = 1e-6
LOG2E = 1.4426950408889634
EXP2_SAFE_SPAN = 100.0
D_QKVU = D_POOL + D_ATTN + 2 * D_KV
POOL_HALO = 16
STAGE_ROWS = 256
PROJ_COLS = 256
ROUTED_PHASE_AFTER_BLOCK = (0, 1, 2)
MERGE_COLS = 256
RING_SLOTS = 3
ATTN_TILES_PER_TRIP = 10
ROUTE_ROWS = 40
META_ROWS = 8
LANES = 128
D_PACK = D_MODEL // 2

V7X_VMEM_LIMIT = 56 * 1024 * 1024
SC_WINDOW = 128


def _const_spec(shape):
    nd = len(shape)
    return pl.BlockSpec(shape, lambda *_: (0,) * nd, pipeline_mode=pl.Buffered(1))


def _sigmoid(x):
    return 0.5 * jnp.tanh(0.5 * x) + 0.5


def _pack_bf16_pair(lo, hi):
    lo_bits = lax.bitcast_convert_type(lo.astype(BF16).astype(F32), jnp.uint32) >> 16
    hi_bits = lax.bitcast_convert_type(hi.astype(BF16).astype(F32), jnp.uint32) & jnp.uint32(0xFFFF0000)
    return lax.bitcast_convert_type(lo_bits | hi_bits, I32)


def _unpack_bf16_pair(packed):
    u = lax.bitcast_convert_type(packed, jnp.uint32)
    lo = lax.bitcast_convert_type(u << 16, F32)
    hi = lax.bitcast_convert_type(u & jnp.uint32(0xFFFF0000), F32)
    return lo, hi


def _input_ring(x_hbm, buf, sem, step, n_steps, rows):
    def fetch(s):
        slot = lax.rem(s, RING_SLOTS)
        src = x_hbm.at[pl.ds(pl.multiple_of(s * rows, rows), rows), :]
        return pltpu.make_async_copy(src, buf.at[slot], sem.at[slot])

    @pl.when((step == 0) & (n_steps > 0))
    def _():
        fetch(0).start()

    @pl.when((step == 0) & (n_steps > 1))
    def _():
        fetch(1).start()

    @pl.when(step + 2 < n_steps)
    def _():
        fetch(step + 2).start()

    @pl.when(step < n_steps)
    def _():
        fetch(step).wait()

    return lax.rem(step, RING_SLOTS)


def _mod_kernel(c_ref, w_ref, b_ref, o_ref):
    c = c_ref[...]
    s = c * _sigmoid(c)
    o_ref[...] = jnp.dot(s, w_ref[...], preferred_element_type=F32,
                         precision=lax.Precision.HIGHEST) + b_ref[...]


def _modulation(c, w_ada, b_ada):
    B = c.shape[0]
    n_out = w_ada.shape[1]
    bn = D_MODEL
    return pl.pallas_call(
        _mod_kernel,
        grid=(n_out // bn,),
        in_specs=[pl.BlockSpec((B, D_MODEL), lambda j: (0, 0)),
                  pl.BlockSpec((D_MODEL, bn), lambda j: (0, j)),
                  pl.BlockSpec((1, bn), lambda j: (0, j))],
        out_specs=pl.BlockSpec((B, bn), lambda j: (0, j)),
        out_shape=jax.ShapeDtypeStruct((B, n_out), F32),
        name="modulation",
    )(c, w_ada, b_ada.reshape(1, n_out))


def _scaled_norm(x, gain, shift):
    ms = jnp.mean(x * x, axis=-1, keepdims=True)
    return x * lax.rsqrt(ms + EPS) * gain + shift


def _stage_a_kernel(x_hbm, mod_ref, n1g_ref, w_ref, qg_ref, kg_ref, cos_ref, sin_ref,
                    u_ref, qt_ref, qb_ref, k_ref, vt_ref, xbuf, sem, z_ref):
    s = pl.program_id(0)
    n_tiles = pl.num_programs(0) - 1
    tm = u_ref.shape[0]
    _input_ring(x_hbm, xbuf, sem, s, n_tiles, tm)
    slot = lax.rem(jnp.minimum(s, n_tiles - 1), RING_SLOTS)

    @pl.when(s == 0)
    def _():
        z_ref[...] = jnp.zeros_like(z_ref)

    gain1 = n1g_ref[...] * (1.0 + mod_ref[1:2, :])
    shift1 = mod_ref[0:1, :]
    lane = lax.broadcasted_iota(I32, (1, HEAD_DIM), 1)
    first_half = (lane % 64) < 32
    qg = qg_ref[...] * (LOG2E / np.sqrt(HEAD_DIM))
    k_norm_max = jnp.max(jnp.abs(kg_ref[...]), axis=-1, keepdims=True) * (np.sqrt(HEAD_DIM) * 1.01)
    cos = cos_ref[...]
    sin_signed = sin_ref[...]

    def norm_rope(xh, g):
        ms = jnp.mean(xh * xh, axis=-1, keepdims=True)
        xn = xh * lax.rsqrt(ms + EPS) * g
        rot = jnp.where(first_half, pltpu.roll(xn, 96, 1), pltpu.roll(xn, 32, 1))
        return xn * cos + rot * sin_signed

    def finish(o):
        zh = z_ref[:, o:o + HEAD_DIM]
        if o < D_POOL:
            u_ref[:, o:o + HEAD_DIM] = zh.astype(BF16)
        elif o < D_POOL + D_ATTN:
            hq = (o - D_POOL) // HEAD_DIM
            q_t = norm_rope(zh, qg).T.astype(BF16)
            qt_ref[hq] = q_t
            q_f = q_t.astype(F32)
            qb_ref[hq] = jnp.sqrt(jnp.sum(q_f * q_f, axis=0, keepdims=True)) * k_norm_max
        elif o < D_POOL + D_ATTN + D_KV:
            k_ref[:, o - D_POOL - D_ATTN:o - D_POOL - D_ATTN + HEAD_DIM] = norm_rope(zh, kg_ref[...]).astype(BF16)
        else:
            vt_ref[(o - D_POOL - D_ATTN - D_KV) // HEAD_DIM] = zh.T.astype(BF16)

    h = _scaled_norm(xbuf[slot], gain1, shift1).astype(BF16)
    for c in range(0, D_QKVU, PROJ_COLS):
        for o in range(c, c + PROJ_COLS, HEAD_DIM):
            finish(o)
        z_ref[:, c:c + PROJ_COLS] = jnp.dot(h, w_ref[:, c:c + PROJ_COLS], preferred_element_type=F32)


def _stage_a(x, mod, n1g, w_qkvu, qg, kg, cos, sin_signed, tm):
    B, T, _ = x.shape
    nt = T // tm
    n_tiles = B * nt
    projected = lambda s: jnp.minimum(s, n_tiles - 1)
    finished = lambda s: jnp.maximum(s - 1, 0)
    return pl.pallas_call(
        _stage_a_kernel,
        grid=(n_tiles + 1,),
        in_specs=[
            pl.BlockSpec(memory_space=pl.ANY),
            pl.BlockSpec((None, 6, D_MODEL), lambda s: (projected(s) // nt, 0, 0)),
            _const_spec((1, D_MODEL)),
            _const_spec((D_MODEL, D_QKVU)),
            _const_spec((1, HEAD_DIM)),
            _const_spec((1, HEAD_DIM)),
            pl.BlockSpec((tm, HEAD_DIM), lambda s: (lax.rem(finished(s), nt), 0)),
            pl.BlockSpec((tm, HEAD_DIM), lambda s: (lax.rem(finished(s), nt), 0)),
        ],
        out_specs=[
            pl.BlockSpec((None, tm, D_POOL), lambda s: (finished(s) // nt, lax.rem(finished(s), nt), 0)),
            pl.BlockSpec((None, N_Q_HEADS, HEAD_DIM, tm), lambda s: (finished(s) // nt, 0, 0, lax.rem(finished(s), nt))),
            pl.BlockSpec((None, N_Q_HEADS, 1, tm), lambda s: (finished(s) // nt, 0, 0, lax.rem(finished(s), nt))),
            pl.BlockSpec((None, tm, D_KV), lambda s: (finished(s) // nt, lax.rem(finished(s), nt), 0)),
            pl.BlockSpec((None, N_KV_HEADS, HEAD_DIM, tm), lambda s: (finished(s) // nt, 0, 0, lax.rem(finished(s), nt))),
        ],
        out_shape=[
            jax.ShapeDtypeStruct((B, T, D_POOL), BF16),
            jax.ShapeDtypeStruct((B, N_Q_HEADS, HEAD_DIM, T), BF16),
            jax.ShapeDtypeStruct((B, N_Q_HEADS, 1, T), F32),
            jax.ShapeDtypeStruct((B, T, D_KV), BF16),
            jax.ShapeDtypeStruct((B, N_KV_HEADS, HEAD_DIM, T), BF16),
        ],
        scratch_shapes=[pltpu.VMEM((RING_SLOTS, tm, D_MODEL), F32), pltpu.SemaphoreType.DMA((RING_SLOTS,)),
                        pltpu.VMEM((tm, D_QKVU), F32)],
        compiler_params=pltpu.CompilerParams(
            dimension_semantics=("arbitrary",), vmem_limit_bytes=V7X_VMEM_LIMIT),
        name="stage_a",
    )(x.reshape(B * T, D_MODEL), mod, n1g, w_qkvu, qg, kg, cos, sin_signed)


def _write_attn_output(o_ref, acc, l, tq):
    o = acc * (1.0 / l)
    o_ref[:, :HEAD_DIM] = o[:, :tq].T.astype(BF16)
    o_ref[:, HEAD_DIM:] = o[:, tq:].T.astype(BF16)


def _attn_bounded_kernel(qt_ref, qb_ref, k_ref, vt_ref, o_ref, p_ref, acc_ref, l_ref, *, tk):
    T = k_ref.shape[0]
    tq = qt_ref.shape[2]
    nq = 2 * tq
    nk = T // tk
    qt = jnp.concatenate([qt_ref[0], qt_ref[1]], axis=1)
    qb = jnp.concatenate([qb_ref[0], qb_ref[1]], axis=1)

    def weights(j, slot):
        off = pl.multiple_of(j * tk, tk)
        st = jnp.dot(k_ref[pl.ds(off, tk), :], qt, preferred_element_type=F32)
        p = jnp.exp2(st - qb)
        l_ref[...] += jnp.sum(p.reshape(tk // 8, 8, nq), axis=0)
        p_ref[slot] = p.astype(BF16)

    def values(j, slot):
        off = pl.multiple_of(j * tk, tk)
        acc_ref[...] += jnp.dot(vt_ref[:, pl.ds(off, tk)], p_ref[slot], preferred_element_type=F32)

    def step(j, slot):
        values(j - 1, 1 - slot)
        weights(j, slot)

    acc_ref[...] = jnp.zeros_like(acc_ref)
    l_ref[...] = jnp.zeros_like(l_ref)
    weights(0, 0)

    unroll = max(1, min(ATTN_TILES_PER_TRIP, nk - 1))
    trips = (nk - 1) // unroll if unroll % 2 == 0 else 0

    if trips > 1:
        @pl.loop(0, trips)
        def _(g):
            for d in range(unroll):
                step(1 + unroll * g + d, (1 + d) % 2)
    else:
        trips = 0
    for j in range(1 + trips * unroll, nk):
        step(j, j % 2)
    values(nk - 1, (nk - 1) % 2)
    _write_attn_output(o_ref, acc_ref[...], jnp.sum(l_ref[...], axis=0, keepdims=True), tq)


def _attn_online_kernel(qt_ref, qb_ref, k_ref, vt_ref, o_ref, *, tk):
    del qb_ref
    T = k_ref.shape[0]
    tq = qt_ref.shape[2]
    nq = 2 * tq
    qt = jnp.concatenate([qt_ref[0], qt_ref[1]], axis=1)

    def body(s, carry):
        m, l, acc = carry
        off = pl.multiple_of(s * tk, tk)
        st = jnp.dot(k_ref[pl.ds(off, tk), :], qt, preferred_element_type=F32)
        m_new = jnp.maximum(m, jnp.max(st, axis=0, keepdims=True))
        alpha = jnp.exp2(m - m_new)
        p = jnp.exp2(st - m_new)
        l = alpha * l + jnp.sum(p, axis=0, keepdims=True)
        pv = jnp.dot(vt_ref[:, pl.ds(off, tk)], p.astype(BF16), preferred_element_type=F32)
        return m_new, l, alpha * acc + pv

    init = (jnp.full((1, nq), -jnp.inf, F32), jnp.zeros((1, nq), F32), jnp.zeros((HEAD_DIM, nq), F32))
    _, l, acc = lax.fori_loop(0, T // tk, body, init)
    _write_attn_output(o_ref, acc, l, tq)


def _attention(qt, qb, k, vt, tq, tk, running_max):
    B, _, _, T = qt.shape
    group = N_Q_HEADS // N_KV_HEADS
    nq = group * tq
    if running_max:
        body, scratch, name = functools.partial(_attn_online_kernel, tk=tk), [], "attention_online_max"
    else:
        body = functools.partial(_attn_bounded_kernel, tk=tk)
        scratch = [pltpu.VMEM((2, tk, nq), BF16), pltpu.VMEM((HEAD_DIM, nq), F32), pltpu.VMEM((8, nq), F32)]
        name = "attention_bounded"
    return pl.pallas_call(
        body,
        grid=(B, N_KV_HEADS, T // tq),
        in_specs=[
            pl.BlockSpec((None, group, HEAD_DIM, tq), lambda b, j, i: (b, j, 0, i)),
            pl.BlockSpec((None, group, 1, tq), lambda b, j, i: (b, j, 0, i)),
            pl.BlockSpec((None, T, HEAD_DIM), lambda b, j, i: (b, 0, j)),
            pl.BlockSpec((None, None, HEAD_DIM, T), lambda b, j, i: (b, j, 0, 0)),
        ],
        out_specs=pl.BlockSpec((None, tq, group * HEAD_DIM), lambda b, j, i: (b, i, j)),
        out_shape=jax.ShapeDtypeStruct((B, T, D_ATTN), BF16),
        scratch_shapes=scratch,
        compiler_params=pltpu.CompilerParams(
            dimension_semantics=("parallel", "parallel", "parallel"),
            vmem_limit_bytes=V7X_VMEM_LIMIT),
        name=name,
    )(qt, qb, k, vt)


def _route(logits_t):
    neg = jnp.float32(-jnp.inf)
    n = logits_t.shape[1]

    def top(vals, idx, far):
        v = jnp.max(vals, axis=0, keepdims=True)
        i = jnp.min(jnp.where(vals == v, idx, far), axis=0, keepdims=True)
        return v, i

    gidx = lax.broadcasted_iota(I32, (ROUTE_ROWS - N_EXPERTS, n), 0).astype(F32)
    grp = jnp.where(gidx < N_GROUPS, logits_t[N_EXPERTS:, :], neg)
    gmax, gsel = top(grp, gidx, jnp.float32(ROUTE_ROWS))
    p_group = 1.0 / jnp.sum(jnp.exp(grp - gmax), axis=0, keepdims=True)
    eidx = lax.broadcasted_iota(I32, (N_EXPERTS, n), 0).astype(F32)
    first = EXPERTS_PER_GROUP * gsel
    le = jnp.where((eidx >= first) & (eidx < first + EXPERTS_PER_GROUP), logits_t[:N_EXPERTS, :], neg)
    v1, e1 = top(le, eidx, jnp.float32(N_EXPERTS))
    v2, e2 = top(jnp.where(eidx == e1, neg, le), eidx, jnp.float32(N_EXPERTS))
    e21 = jnp.exp(v2 - v1)
    w1 = p_group / (1.0 + e21)
    return e1, e2, w1, w1 * e21


def _pool_deviation(ue_ref, lvl_ref, pool_ref, tile_start, tm, seq_len, groups):
    n = tm + 2 * POOL_HALO
    t = tile_start + lax.broadcasted_iota(I32, (tm, 1), 0)
    for gi in groups:
        w = POOL_WINDOWS[gi]
        half = w // 2
        cols = slice(gi * POOL_GC, (gi + 1) * POOL_GC)
        level = ue_ref[0:n, cols] + ue_ref[1:n + 1, cols]
        span, buf = 2, 0
        lvl_ref[buf, 0:n, :] = level
        while span < w:
            level = lvl_ref[buf, 0:n, :] + lvl_ref[buf, span:span + n, :]
            span, buf = 2 * span, 1 - buf
            lvl_ref[buf, 0:n, :] = level
        wsum = lvl_ref[buf, POOL_HALO - half:POOL_HALO - half + tm, :]
        cnt = (jnp.minimum(t + half, seq_len) - jnp.maximum(t - half, 0)).astype(F32)
        p = wsum * (1.0 / cnt) - ue_ref[POOL_HALO:POOL_HALO + tm, cols]
        pool_ref[:, cols] = p.astype(BF16)


def _norm2_split(x1, gain2, shift2, h2_ref):
    h2 = _scaled_norm(x1, gain2, shift2)
    h2_ref[...] = _pack_bf16_pair(h2[:, :D_PACK], h2[:, D_PACK:])
    h_hi = h2.astype(BF16)
    return h_hi, (h2 - h_hi.astype(F32)).astype(BF16)


def _router_logits(h_hi, h_lo, wr_ref, br_ref):
    parts = (jnp.dot(h_hi, wr_ref[...], preferred_element_type=F32)
             + jnp.dot(h_lo, wr_ref[...], preferred_element_type=F32))
    return parts[:, :LANES] + parts[:, LANES:] + br_ref[...]


def _dispatch_table(logits, is_real, utri_ref, run_ref, meta_ref, rw_ref):
    n = logits.shape[0]
    e1, e2, w1, w2 = _route(logits.T[0:ROUTE_ROWS, :])

    eidx = lax.broadcasted_iota(I32, (N_EXPERTS, n), 0).astype(F32)
    hit1 = eidx == e1
    hit2 = eidx == e2
    taken = jnp.where(hit1 | hit2, is_real, 0.0)
    before = run_ref[...] + jnp.dot(taken.astype(BF16), utri_ref[...], preferred_element_type=F32)
    rank1 = jnp.sum(jnp.where(hit1, before, 0.0), axis=0, keepdims=True)
    rank2 = jnp.sum(jnp.where(hit2, before, 0.0), axis=0, keepdims=True)
    run_ref[...] = run_ref[...] + jnp.sum(taken, axis=1, keepdims=True)
    ridx = lax.broadcasted_iota(I32, (META_ROWS, n), 0)
    table = jnp.where(ridx == 0, e1, jnp.where(ridx == 1, e2, jnp.where(ridx == 2, rank1,
                      jnp.where(ridx == 3, rank2, 0.0))))
    meta_ref[...] = table.astype(I32)
    weights_t = jnp.where(ridx == 0, w1, jnp.where(ridx == 1, w2, 0.0))
    rw_ref[...] = jnp.concatenate([weights_t, jnp.zeros((LANES - META_ROWS, n), F32)], axis=0).T


def _stage_c_kernel(x_ref, mod_ref, modp_ref, n1g_ref, n2g_ref, u_ref, up_ref, un_ref, attn_ref,
                    wgate_ref, wmix_ref, ps_ref, wpp_ref, wap_ref, wo_ref, wr_ref, br_ref, utri_ref,
                    x1_ref, h2_ref, meta_ref, rw_ref, count_ref,
                    ue_ref, lvl_ref, pool_ref, run_ref, merged_ref, x1_prev, *, seq_len):
    s = pl.program_id(0)
    tm = x_ref.shape[0]
    tiles_per_seq = seq_len // tm
    i = lax.rem(jnp.minimum(s, pl.num_programs(0) - 2), tiles_per_seq)

    @pl.when(s == 0)
    def _():
        run_ref[...] = jnp.zeros_like(run_ref)
        x1_prev[...] = jnp.zeros_like(x1_prev)

    gain1 = n1g_ref[...] * (1.0 + mod_ref[1:2, :])
    shift1 = mod_ref[0:1, :]
    res_gate = mod_ref[2:3, :]
    x = x_ref[...]
    h = _scaled_norm(x, gain1, shift1).astype(BF16)
    ue_ref[0:POOL_HALO, :] = up_ref[...].astype(F32) * (i > 0).astype(F32)
    ue_ref[POOL_HALO:POOL_HALO + tm, :] = u_ref[...].astype(F32)
    ue_ref[POOL_HALO + tm:2 * POOL_HALO + tm, :] = un_ref[...].astype(F32) * (i < tiles_per_seq - 1).astype(F32)
    ue_ref[2 * POOL_HALO + tm:, :] = jnp.zeros((POOL_HALO, D_POOL), F32)
    lvl_ref[:, 2 * POOL_HALO + tm:, :] = jnp.zeros((2, POOL_HALO, POOL_GC), F32)
    _pool_deviation(ue_ref, lvl_ref, pool_ref, i * tm, tm, seq_len, range(len(POOL_WINDOWS)))

    attn = attn_ref[...]
    mixed_groups = []
    for gi in range(len(POOL_WINDOWS)):
        cols = slice(gi * POOL_GC, (gi + 1) * POOL_GC)
        pm = jnp.dot(pool_ref[:, cols], wmix_ref[gi], preferred_element_type=F32)
        mixed_groups.append(pm * ps_ref[:, cols])
    pool = jnp.concatenate(mixed_groups, axis=1).astype(BF16)
    assert max(ROUTED_PHASE_AFTER_BLOCK) < D_MODEL // MERGE_COLS
    for c in range(0, D_MODEL, MERGE_COLS):
        cols = slice(c, c + MERGE_COLS)
        gate_cols = slice(D_MODEL + c, D_MODEL + c + MERGE_COLS)
        gate_pool = _sigmoid(jnp.dot(h, wgate_ref[:, cols], preferred_element_type=F32))
        gate_attn = _sigmoid(jnp.dot(h, wgate_ref[:, gate_cols], preferred_element_type=F32))
        merged = (gate_pool * jnp.dot(pool, wpp_ref[:, cols], preferred_element_type=F32)
                  + gate_attn * jnp.dot(attn, wap_ref[:, cols], preferred_element_type=F32))
        merged_ref[:, cols] = merged.astype(BF16)
        block = c // MERGE_COLS
        if block == ROUTED_PHASE_AFTER_BLOCK[0]:
            h_hi, h_lo = _norm2_split(x1_prev[...], n2g_ref[...] * (1.0 + modp_ref[4:5, :]), modp_ref[3:4, :],
                                      h2_ref)
        if block == ROUTED_PHASE_AFTER_BLOCK[1]:
            logits = _router_logits(h_hi, h_lo, wr_ref, br_ref)
        if block == ROUTED_PHASE_AFTER_BLOCK[2]:
            _dispatch_table(logits, (s > 0).astype(F32), utri_ref, run_ref, meta_ref, rw_ref)
            count_ref[...] = jnp.broadcast_to(run_ref[...], count_ref.shape)
    for c in range(0, D_MODEL, MERGE_COLS):
        cols = slice(c, c + MERGE_COLS)
        mixed = jnp.dot(merged_ref[...], wo_ref[:, cols], preferred_element_type=F32)
        x1 = x[:, cols] + res_gate[:, cols] * mixed
        x1_ref[:, cols] = x1.astype(BF16)
        x1_prev[:, cols] = x1


def _stage_c(x, mod, n1g, n2g, u, attn, w_gate, w_mix, pool_scale, w_pp, w_ap, w_o, w_r, b_r, tm):
    B, T, _ = x.shape
    hb = tm // POOL_HALO
    n_halo_blocks = T // POOL_HALO
    nt = T // tm
    n_tiles = B * nt

    def merged_tile(s):
        m = jnp.minimum(s, n_tiles - 1)
        return m // nt, lax.rem(m, nt)

    def tok(s):
        b, i = merged_tile(s)
        return b, i, 0

    def halo_before(s):
        b, i = merged_tile(s)
        return b, jnp.maximum(i * hb - 1, 0), 0

    def halo_after(s):
        b, i = merged_tile(s)
        return b, jnp.minimum((i + 1) * hb, n_halo_blocks - 1), 0

    routed = lambda s: jnp.maximum(s - 1, 0)
    flat = lambda s: (routed(s), 0)
    utri = jnp.triu(jnp.ones((tm, tm), BF16), k=1)
    return pl.pallas_call(
        functools.partial(_stage_c_kernel, seq_len=T),
        grid=(n_tiles + 1,),
        in_specs=[
            pl.BlockSpec((None, tm, D_MODEL), tok),
            pl.BlockSpec((None, 6, D_MODEL), lambda s: (merged_tile(s)[0], 0, 0)),
            pl.BlockSpec((None, 6, D_MODEL), lambda s: (routed(s) // nt, 0, 0)),
            _const_spec((1, D_MODEL)),
            _const_spec((1, D_MODEL)),
            pl.BlockSpec((None, tm, D_POOL), tok),
            pl.BlockSpec((None, POOL_HALO, D_POOL), halo_before),
            pl.BlockSpec((None, POOL_HALO, D_POOL), halo_after),
            pl.BlockSpec((None, tm, D_ATTN), tok),
            _const_spec((D_MODEL, 2 * D_MODEL)),
            _const_spec((len(POOL_WINDOWS), POOL_GC, POOL_GC)),
            _const_spec((1, D_POOL)),
            _const_spec((D_POOL, D_MODEL)),
            _const_spec((D_ATTN, D_MODEL)),
            _const_spec((D_MODEL, D_MODEL)),
            _const_spec((D_MODEL, 2 * LANES)),
            _const_spec((1, LANES)),
            _const_spec((tm, tm)),
        ],
        out_specs=[
            pl.BlockSpec((None, tm, D_MODEL), tok),
            pl.BlockSpec((tm, D_PACK), flat),
            pl.BlockSpec((META_ROWS, tm), lambda s: (0, routed(s))),
            pl.BlockSpec((tm, LANES), flat),
            pl.BlockSpec((N_EXPERTS, LANES), lambda s: (0, 0)),
        ],
        out_shape=[
            jax.ShapeDtypeStruct((B, T, D_MODEL), BF16),
            jax.ShapeDtypeStruct((B * T, D_PACK), I32),
            jax.ShapeDtypeStruct((META_ROWS, B * T), I32),
            jax.ShapeDtypeStruct((B * T, LANES), F32),
            jax.ShapeDtypeStruct((N_EXPERTS, LANES), F32),
        ],
        scratch_shapes=[pltpu.VMEM((tm + 3 * POOL_HALO, D_POOL), F32),
                        pltpu.VMEM((2, tm + 3 * POOL_HALO, POOL_GC), F32),
                        pltpu.VMEM((tm, D_POOL), BF16),
                        pltpu.VMEM((N_EXPERTS, 1), F32),
                        pltpu.VMEM((tm, D_MODEL), BF16),
                        pltpu.VMEM((tm, D_MODEL), F32)],
        compiler_params=pltpu.CompilerParams(
            dimension_semantics=("arbitrary",), vmem_limit_bytes=V7X_VMEM_LIMIT),
        name="stage_c",
    )(x, mod, mod, n1g, n2g, u, u, u, attn, w_gate, w_mix, pool_scale, w_pp, w_ap, w_o, w_r, b_r, utri)


def _sc_workers():
    info = plsc.get_sparse_core_info()
    return info.num_cores, info.num_subcores


def _sc_scatter_rows(rows, idx0, idx1, n_out):
    n, d = rows.shape
    nc, ns = _sc_workers()
    per_worker = n // (nc * ns)
    n_win = per_worker // SC_WINDOW
    mesh = plsc.VectorSubcoreMesh(core_axis_name="c", subcore_axis_name="s")

    @functools.partial(
        pl.kernel, mesh=mesh,
        out_type=jax.ShapeDtypeStruct((n_out, d), rows.dtype),
        scratch_types=[pltpu.VMEM((SC_WINDOW,), I32), pltpu.VMEM((SC_WINDOW,), I32),
                       pltpu.VMEM((SC_WINDOW, d), rows.dtype)],
        name="sc_scatter_rows",
    )
    def scatter(rows_hbm, idx0_hbm, idx1_hbm, out_hbm, i0_v, i1_v, rows_v):
        wid = lax.axis_index("s") * nc + lax.axis_index("c")

        @pl.loop(0, n_win)
        def _(w):
            base = wid * per_worker + w * SC_WINDOW
            pltpu.sync_copy(rows_hbm.at[pl.ds(base, SC_WINDOW)], rows_v)
            pltpu.sync_copy(idx0_hbm.at[pl.ds(base, SC_WINDOW)], i0_v)
            pltpu.sync_copy(idx1_hbm.at[pl.ds(base, SC_WINDOW)], i1_v)
            pltpu.sync_copy(rows_v, out_hbm.at[i0_v])
            pltpu.sync_copy(rows_v, out_hbm.at[i1_v])

    return scatter(rows, idx0, idx1)


def _sc_gather_rows(table, idx):
    n = idx.shape[0]
    d = table.shape[1]
    nc, ns = _sc_workers()
    per_worker = n // (nc * ns)
    n_win = per_worker // SC_WINDOW
    mesh = plsc.VectorSubcoreMesh(core_axis_name="c", subcore_axis_name="s")

    @functools.partial(
        pl.kernel, mesh=mesh,
        out_type=jax.ShapeDtypeStruct((n, d), table.dtype),
        scratch_types=[pltpu.VMEM((SC_WINDOW,), I32), pltpu.VMEM((SC_WINDOW, d), table.dtype)],
        name="sc_gather_rows",
    )
    def gather(table_hbm, idx_hbm, out_hbm, i_v, rows_v):
        wid = lax.axis_index("s") * nc + lax.axis_index("c")

        @pl.loop(0, n_win)
        def _(w):
            base = wid * per_worker + w * SC_WINDOW
            pltpu.sync_copy(idx_hbm.at[pl.ds(base, SC_WINDOW)], i_v)
            pltpu.sync_copy(table_hbm.at[i_v], rows_v)
            pltpu.sync_copy(rows_v, out_hbm.at[pl.ds(base, SC_WINDOW)])

    return gather(table, idx)


def _moe_kernel(te_ref, tv_ref, nl_ref, x_hbm, wg_ref, wu_ref, wd_ref, o_ref, xbuf, sem, wgu_ref):
    i = pl.program_id(0)
    n_live_tiles = nl_ref[0]
    valid = tv_ref[i]
    tm = o_ref.shape[0]
    slot = _input_ring(x_hbm, xbuf, sem, i, n_live_tiles, tm)

    @pl.when((i == 0) | (te_ref[i] != te_ref[jnp.maximum(i - 1, 0)]))
    def _():
        wgu_ref[:, :D_EXPERT] = wg_ref[...]
        wgu_ref[:, D_EXPERT:] = wu_ref[...]

    group = min(STAGE_ROWS, tm)
    n_groups = tm // group

    def gate_up(r0):
        live = lax.broadcasted_iota(I32, (group, 1), 0) < valid - r0
        lo, hi = _unpack_bf16_pair(jnp.where(live, xbuf[slot, r0:r0 + group, :], 0))
        x = jnp.concatenate([lo, hi], axis=1).astype(BF16)
        return jnp.dot(x, wgu_ref[...], preferred_element_type=F32)

    def down(gu):
        a = gu[:, :D_EXPERT]
        hmid = (a * _sigmoid(a) * gu[:, D_EXPERT:]).astype(BF16)
        return jnp.dot(hmid, wd_ref[...], preferred_element_type=F32)

    def store(r0, y):
        o_ref[r0:r0 + group, :] = _pack_bf16_pair(y[:, :D_PACK], y[:, D_PACK:])

    for n_live in range(n_groups + 1):
        @pl.when((valid > (n_live - 1) * group) & (valid <= n_live * group))
        def _():
            gu, y = None, None
            for g in range(n_live + 2):
                nxt = gate_up(g * group) if g < n_live else None
                if y is not None:
                    store((g - 2) * group, y)
                y = down(gu) if gu is not None else None
                gu = nxt
            if n_live < n_groups:
                o_ref[n_live * group:, :] = jnp.zeros((tm - n_live * group, D_PACK), I32)


def _moe(xs, tile_expert, tile_valid, n_live_tiles, w_g, w_u, w_d, tm):
    n_tiles = xs.shape[0] // tm
    grid_spec = pltpu.PrefetchScalarGridSpec(
        num_scalar_prefetch=3,
        grid=(n_tiles,),
        in_specs=[
            pl.BlockSpec(memory_space=pl.ANY),
            pl.BlockSpec((None, D_MODEL, D_EXPERT), lambda i, te, tv, nl: (te[i], 0, 0)),
            pl.BlockSpec((None, D_MODEL, D_EXPERT), lambda i, te, tv, nl: (te[i], 0, 0)),
            pl.BlockSpec((None, D_EXPERT, D_MODEL), lambda i, te, tv, nl: (te[i], 0, 0)),
        ],
        out_specs=pl.BlockSpec((tm, D_PACK), lambda i, te, tv, nl: (jnp.minimum(i, nl[0]), 0)),
        scratch_shapes=[pltpu.VMEM((RING_SLOTS, tm, D_PACK), I32), pltpu.SemaphoreType.DMA((RING_SLOTS,)),
                        pltpu.VMEM((D_MODEL, 2 * D_EXPERT), BF16)],
    )
    return pl.pallas_call(
        _moe_kernel,
        grid_spec=grid_spec,
        out_shape=jax.ShapeDtypeStruct(xs.shape, I32),
        compiler_params=pltpu.CompilerParams(
            dimension_semantics=("arbitrary",), vmem_limit_bytes=V7X_VMEM_LIMIT),
        name="experts",
    )(tile_expert, tile_valid, n_live_tiles, xs, w_g, w_u, w_d)


def _final_kernel(x1_ref, y0_ref, y1_ref, rw_ref, mod_ref, fg_ref, o_ref):
    y0lo, y0hi = _unpack_bf16_pair(y0_ref[...])
    y1lo, y1hi = _unpack_bf16_pair(y1_ref[...])
    w0 = rw_ref[:, 0:1]
    w1 = rw_ref[:, 1:2]
    moe = jnp.concatenate([w0 * y0lo + w1 * y1lo, w0 * y0hi + w1 * y1hi], axis=1)
    x2 = x1_ref[...].astype(F32) + mod_ref[5:6, :] * moe
    ms = jnp.mean(x2 * x2, axis=-1, keepdims=True)
    o_ref[...] = x2 * lax.rsqrt(ms + EPS) * fg_ref[...]


def _final(x1, yb, rw, mod, final_g, tm):
    B, T, _ = x1.shape
    n_blocks = B * T // tm
    flat = lambda b, i: (b * (T // tm) + i, 0)
    return pl.pallas_call(
        _final_kernel,
        grid=(B, T // tm),
        in_specs=[
            pl.BlockSpec((None, tm, D_MODEL), lambda b, i: (b, i, 0)),
            pl.BlockSpec((tm, D_PACK), flat),
            pl.BlockSpec((tm, D_PACK), lambda b, i: (n_blocks + b * (T // tm) + i, 0)),
            pl.BlockSpec((tm, LANES), flat),
            pl.BlockSpec((None, 6, D_MODEL), lambda b, i: (b, 0, 0)),
            _const_spec((1, D_MODEL)),
        ],
        out_specs=pl.BlockSpec((None, tm, D_MODEL), lambda b, i: (b, i, 0)),
        out_shape=jax.ShapeDtypeStruct((B, T, D_MODEL), F32),
        compiler_params=pltpu.CompilerParams(
            dimension_semantics=("parallel", "parallel"), vmem_limit_bytes=V7X_VMEM_LIMIT),
        name="final_combine",
    )(x1, yb, yb, rw, mod, final_g)


def _rope_tables(T):
    rows = T // GRID_W
    row = np.repeat(np.arange(rows, dtype=np.float32), GRID_W)
    col = np.tile(np.arange(GRID_W, dtype=np.float32), rows)
    inv_freq = np.float32(ROPE_THETA) ** (-np.arange(0, 64, 2, dtype=np.float32) / np.float32(64))
    ang_r = row[:, None] * inv_freq[None, :]
    ang_c = col[:, None] * inv_freq[None, :]
    ang = np.concatenate([ang_r, ang_r, ang_c, ang_c], axis=-1).astype(np.float32)
    sign = np.where((np.arange(HEAD_DIM) % 64) < 32, -1.0, 1.0).astype(np.float32)
    return jnp.asarray(np.cos(ang)), jnp.asarray(np.sin(ang) * sign[None, :])


def _dispatch_plan(counts, tm, n_tiles):
    tiles_per_e = (counts + tm - 1) // tm
    tile_end = jnp.cumsum(tiles_per_e)
    tile_start = tile_end - tiles_per_e
    tile_id = jnp.arange(n_tiles, dtype=I32)
    te = jnp.minimum(jnp.sum((tile_id[:, None] >= tile_end[None, :]).astype(I32), axis=1), N_EXPERTS - 1)
    mine = te[:, None] == jnp.arange(N_EXPERTS, dtype=I32)[None, :]
    count_of = jnp.sum(jnp.where(mine, counts[None, :], 0), axis=1)
    start_of = jnp.sum(jnp.where(mine, tile_start[None, :], 0), axis=1)
    live = jnp.clip(count_of - (tile_id - start_of) * tm, 0, tm)
    tv = jnp.where(tile_id < tile_end[-1], live, 0).astype(I32)
    return tile_start.astype(I32), te, tv, tile_end[-1:].astype(I32)


def _slot_kernel(start_ref, meta_ref, pos_ref, *, tile_rows):
    e = meta_ref[0:2, :]
    first_tile = jnp.zeros_like(e)
    for ex in range(N_EXPERTS):
        first_tile = jnp.where(e == ex, start_ref[ex], first_tile)
    pos_ref[...] = first_tile * tile_rows + meta_ref[2:4, :]


def _slots(meta, tile_start, tile_rows):
    n = meta.shape[1]
    tn = min(n, 8192)
    grid_spec = pltpu.PrefetchScalarGridSpec(
        num_scalar_prefetch=1,
        grid=(n // tn,),
        in_specs=[pl.BlockSpec((META_ROWS, tn), lambda i, start: (0, i))],
        out_specs=pl.BlockSpec((2, tn), lambda i, start: (0, i)),
    )
    return pl.pallas_call(
        functools.partial(_slot_kernel, tile_rows=tile_rows),
        grid_spec=grid_spec,
        out_shape=jax.ShapeDtypeStruct((2, n), I32),
        name="dispatch_slots",
    )(tile_start, meta)


def _pick_tile(T, want):
    t = min(T, want)
    assert T % t == 0
    return t


def _trunk(x, mod, p):
    B, T, _ = x.shape
    N = B * T
    tm = _pick_tile(T, 512)
    tq = _pick_tile(T, 512)
    tk = _pick_tile(T, 256)
    tme = _pick_tile(N, 1024)
    cos, sin_signed = _rope_tables(T)
    u, qt, qb, k, vt = _stage_a(x, mod, p["n1g"], p["w_qkvu"], p["qg"], p["kg"], cos, sin_signed, tm)
    attn = lax.cond(
        p["bounded_softmax_ok"],
        functools.partial(_attention, tq=tq, tk=_pick_tile(T, 4096), running_max=False),
        functools.partial(_attention, tq=tq, tk=tk, running_max=True),
        qt, qb, k, vt)
    x1, h2p, meta, rw, counts = _stage_c(x, mod, p["n1g"], p["n2g"], u, attn, p["w_gate"], p["w_mix"],
                                         p["pool_scale"], p["w_pp"], p["w_ap"], p["w_o"], p["w_r"], p["b_r"], tm)
    n_tiles = 2 * N // tme + N_EXPERTS
    tile_start, te, tv, n_live_tiles = _dispatch_plan(counts[:, 0].astype(I32), tme, n_tiles)
    pos = _slots(meta, tile_start, tme)
    xs = _sc_scatter_rows(h2p, pos[0], pos[1], n_tiles * tme)
    ys = _moe(xs, te, tv, n_live_tiles, p["w_eg"], p["w_eu"], p["w_ed"], tme)
    yb = _sc_gather_rows(ys, pos.reshape(-1))
    return _final(x1, yb, rw, mod, p["final_g"], _pick_tile(T, 1024))


def kernel(x_prompt, x_sample, c_prompt, c_sample, w_ada, b_ada, norm1_g, norm2_g, w_in, q_norm_g,
           k_norm_g, w_pool_mix, pool_scale, w_pool_proj, w_attn_proj, w_o, w_router_group,
           b_router_group, w_router_expert, b_router_expert, w_exp_gate, w_exp_up, w_exp_down, final_g):
    assert w_ada.shape[0] == 1, "single-layer block"
    n_r = N_GROUPS + N_EXPERTS
    w_r = jnp.concatenate([w_router_expert[0], w_router_group[0],
                           jnp.zeros((D_MODEL, LANES - n_r), F32)], axis=1)
    w_r_hi = w_r.astype(BF16)
    b_r = jnp.concatenate([b_router_expert[0], b_router_group[0], jnp.zeros((LANES - n_r,), F32)])
    score_bound = (1.01 * LOG2E * np.sqrt(HEAD_DIM)) * jnp.max(jnp.abs(q_norm_g[0])) * jnp.max(jnp.abs(k_norm_g[0]))
    p = dict(
        bounded_softmax_ok=2.0 * score_bound <= EXP2_SAFE_SPAN,
        n1g=norm1_g[0].reshape(1, D_MODEL), n2g=norm2_g[0].reshape(1, D_MODEL),
        w_qkvu=w_in[0][:, :D_QKVU].astype(BF16), w_gate=w_in[0][:, D_QKVU:].astype(BF16),
        qg=q_norm_g[0].reshape(1, HEAD_DIM), kg=k_norm_g[0].reshape(1, HEAD_DIM),
        w_mix=w_pool_mix[0].astype(BF16), pool_scale=pool_scale[0].reshape(1, D_POOL),
        w_pp=w_pool_proj[0].astype(BF16), w_ap=w_attn_proj[0].astype(BF16), w_o=w_o[0].astype(BF16),
        w_r=jnp.concatenate([w_r_hi, (w_r - w_r_hi.astype(F32)).astype(BF16)], axis=1), b_r=b_r.reshape(1, LANES),
        w_eg=w_exp_gate[0].astype(BF16), w_eu=w_exp_up[0].astype(BF16), w_ed=w_exp_down[0].astype(BF16),
        final_g=final_g.reshape(1, D_MODEL),
    )
    n_prompt = c_prompt.shape[0]
    mod = _modulation(jnp.concatenate([c_prompt, c_sample], axis=0), w_ada[0], b_ada[0]).reshape(-1, 6, D_MODEL)
    return _trunk(x_prompt, mod[:n_prompt], p), _trunk(x_sample, mod[n_prompt:], p)
```

```python
import functools

import numpy as np
import jax
import jax.numpy as jnp
from jax import lax
from jax.experimental import pallas as pl
from jax.experimental.pallas import tpu as pltpu
from jax.experimental.pallas import tpu_sc as plsc

F32 = jnp.float32
BF16 = jnp.bfloat16
I32 = jnp.int32

D_MODEL = 1024
GRID_W = 64
POOL_WINDOWS = (2, 4, 8, 16)
POOL_GC = 128
D_POOL = 512
HEAD_DIM = 128
N_Q_HEADS = 4
N_KV_HEADS = 2
D_ATTN = 512
D_KV = 256
ROPE_THETA = 10000.0
N_GROUPS = 4
EXPERTS_PER_GROUP = 8
N_EXPERTS = 32
D_EXPERT = 256
EPS = 1e-6
LOG2E = 1.4426950408889634
EXP2_SAFE_SPAN = 100.0
D_QKVU = D_POOL + D_ATTN + 2 * D_KV
POOL_HALO = 16
STAGE_ROWS = 256
PROJ_COLS = 256
ROUTED_PHASE_AFTER_BLOCK = (0, 1, 2)
MERGE_COLS = 256
RING_SLOTS = 3
ATTN_TILES_PER_TRIP = 10
ROUTE_ROWS = 40
META_ROWS = 8
LANES = 128
SUBLANES = 8
KEY_NORM_MARGIN = 1.01
TOKEN_TILE = 512
QUERY_TILE = 512
KEY_TILE = 4096
KEY_TILE_ONLINE = 256
EXPERT_TILE = 1024
FINAL_TILE = 1024
SLOT_TILE = 8192
D_PACK = D_MODEL // 2

V7X_VMEM_LIMIT = 56 * 1024 * 1024
SC_WINDOW = 128


def _const_spec(shape):
    nd = len(shape)
    return pl.BlockSpec(shape, lambda *_: (0,) * nd, pipeline_mode=pl.Buffered(1))


def _sigmoid(x):
    return 0.5 * jnp.tanh(0.5 * x) + 0.5


def _pack_bf16_pair(lo, hi):
    lo_bits = lax.bitcast_convert_type(lo.astype(BF16).astype(F32), jnp.uint32) >> 16
    hi_bits = lax.bitcast_convert_type(hi.astype(BF16).astype(F32), jnp.uint32) & jnp.uint32(0xFFFF0000)
    return lax.bitcast_convert_type(lo_bits | hi_bits, I32)


def _unpack_bf16_pair(packed):
    u = lax.bitcast_convert_type(packed, jnp.uint32)
    lo = lax.bitcast_convert_type(u << 16, F32)
    hi = lax.bitcast_convert_type(u & jnp.uint32(0xFFFF0000), F32)
    return lo, hi


def _input_ring(x_hbm, buf, sem, step, n_steps, rows):
    def fetch(s):
        slot = lax.rem(s, RING_SLOTS)
        src = x_hbm.at[pl.ds(pl.multiple_of(s * rows, rows), rows), :]
        return pltpu.make_async_copy(src, buf.at[slot], sem.at[slot])

    @pl.when((step == 0) & (n_steps > 0))
    def _():
        fetch(0).start()

    @pl.when((step == 0) & (n_steps > 1))
    def _():
        fetch(1).start()

    @pl.when(step + 2 < n_steps)
    def _():
        fetch(step + 2).start()

    @pl.when(step < n_steps)
    def _():
        fetch(step).wait()

    return lax.rem(step, RING_SLOTS)


def _mod_kernel(c_ref, w_ref, b_ref, o_ref):
    c = c_ref[...]
    s = c * _sigmoid(c)
    o_ref[...] = jnp.dot(s, w_ref[...], preferred_element_type=F32,
                         precision=lax.Precision.HIGHEST) + b_ref[...]


def _modulation(c, w_ada, b_ada):
    B = c.shape[0]
    n_out = w_ada.shape[1]
    bn = D_MODEL
    return pl.pallas_call(
        _mod_kernel,
        grid=(n_out // bn,),
        in_specs=[pl.BlockSpec((B, D_MODEL), lambda j: (0, 0)),
                  pl.BlockSpec((D_MODEL, bn), lambda j: (0, j)),
                  pl.BlockSpec((1, bn), lambda j: (0, j))],
        out_specs=pl.BlockSpec((B, bn), lambda j: (0, j)),
        out_shape=jax.ShapeDtypeStruct((B, n_out), F32),
        name="modulation",
    )(c, w_ada, b_ada.reshape(1, n_out))


def _scaled_norm(x, gain, shift):
    ms = jnp.mean(x * x, axis=-1, keepdims=True)
    return x * lax.rsqrt(ms + EPS) * gain + shift


def _stage_a_kernel(x_hbm, mod_ref, n1g_ref, w_ref, qg_ref, kg_ref, cos_ref, sin_ref,
                    u_ref, qt_ref, qb_ref, k_ref, vt_ref, xbuf, sem, z_ref):
    s = pl.program_id(0)
    n_tiles = pl.num_programs(0) - 1
    tm = u_ref.shape[0]
    _input_ring(x_hbm, xbuf, sem, s, n_tiles, tm)
    slot = lax.rem(jnp.minimum(s, n_tiles - 1), RING_SLOTS)

    @pl.when(s == 0)
    def _():
        z_ref[...] = jnp.zeros_like(z_ref)

    gain1 = n1g_ref[...] * (1.0 + mod_ref[1:2, :])
    shift1 = mod_ref[0:1, :]
    lane = lax.broadcasted_iota(I32, (1, HEAD_DIM), 1)
    first_half = (lane % 64) < 32
    qg = qg_ref[...] * (LOG2E / np.sqrt(HEAD_DIM))
    k_norm_max = jnp.max(jnp.abs(kg_ref[...]), axis=-1, keepdims=True) * (np.sqrt(HEAD_DIM) * KEY_NORM_MARGIN)
    cos = cos_ref[...]
    sin_signed = sin_ref[...]

    def norm_rope(xh, g):
        ms = jnp.mean(xh * xh, axis=-1, keepdims=True)
        xn = xh * lax.rsqrt(ms + EPS) * g
        rot = jnp.where(first_half, pltpu.roll(xn, 96, 1), pltpu.roll(xn, 32, 1))
        return xn * cos + rot * sin_signed

    def finish(o):
        zh = z_ref[:, o:o + HEAD_DIM]
        if o < D_POOL:
            u_ref[:, o:o + HEAD_DIM] = zh.astype(BF16)
        elif o < D_POOL + D_ATTN:
            hq = (o - D_POOL) // HEAD_DIM
            q_t = norm_rope(zh, qg).T.astype(BF16)
            qt_ref[hq] = q_t
            q_f = q_t.astype(F32)
            qb_ref[hq] = jnp.sqrt(jnp.sum(q_f * q_f, axis=0, keepdims=True)) * k_norm_max
        elif o < D_POOL + D_ATTN + D_KV:
            k_ref[:, o - D_POOL - D_ATTN:o - D_POOL - D_ATTN + HEAD_DIM] = norm_rope(zh, kg_ref[...]).astype(BF16)
        else:
            vt_ref[(o - D_POOL - D_ATTN - D_KV) // HEAD_DIM] = zh.T.astype(BF16)

    h = _scaled_norm(xbuf[slot], gain1, shift1).astype(BF16)
    for c in range(0, D_QKVU, PROJ_COLS):
        for o in range(c, c + PROJ_COLS, HEAD_DIM):
            finish(o)
        z_ref[:, c:c + PROJ_COLS] = jnp.dot(h, w_ref[:, c:c + PROJ_COLS], preferred_element_type=F32)


def _stage_a(x, mod, n1g, w_qkvu, qg, kg, cos, sin_signed, tm):
    B, T, _ = x.shape
    nt = T // tm
    n_tiles = B * nt
    projected = lambda s: jnp.minimum(s, n_tiles - 1)
    finished = lambda s: jnp.maximum(s - 1, 0)
    return pl.pallas_call(
        _stage_a_kernel,
        grid=(n_tiles + 1,),
        in_specs=[
            pl.BlockSpec(memory_space=pl.ANY),
            pl.BlockSpec((None, 6, D_MODEL), lambda s: (projected(s) // nt, 0, 0)),
            _const_spec((1, D_MODEL)),
            _const_spec((D_MODEL, D_QKVU)),
            _const_spec((1, HEAD_DIM)),
            _const_spec((1, HEAD_DIM)),
            pl.BlockSpec((tm, HEAD_DIM), lambda s: (lax.rem(finished(s), nt), 0)),
            pl.BlockSpec((tm, HEAD_DIM), lambda s: (lax.rem(finished(s), nt), 0)),
        ],
        out_specs=[
            pl.BlockSpec((None, tm, D_POOL), lambda s: (finished(s) // nt, lax.rem(finished(s), nt), 0)),
            pl.BlockSpec((None, N_Q_HEADS, HEAD_DIM, tm), lambda s: (finished(s) // nt, 0, 0, lax.rem(finished(s), nt))),
            pl.BlockSpec((None, N_Q_HEADS, 1, tm), lambda s: (finished(s) // nt, 0, 0, lax.rem(finished(s), nt))),
            pl.BlockSpec((None, tm, D_KV), lambda s: (finished(s) // nt, lax.rem(finished(s), nt), 0)),
            pl.BlockSpec((None, N_KV_HEADS, HEAD_DIM, tm), lambda s: (finished(s) // nt, 0, 0, lax.rem(finished(s), nt))),
        ],
        out_shape=[
            jax.ShapeDtypeStruct((B, T, D_POOL), BF16),
            jax.ShapeDtypeStruct((B, N_Q_HEADS, HEAD_DIM, T), BF16),
            jax.ShapeDtypeStruct((B, N_Q_HEADS, 1, T), F32),
            jax.ShapeDtypeStruct((B, T, D_KV), BF16),
            jax.ShapeDtypeStruct((B, N_KV_HEADS, HEAD_DIM, T), BF16),
        ],
        scratch_shapes=[pltpu.VMEM((RING_SLOTS, tm, D_MODEL), F32), pltpu.SemaphoreType.DMA((RING_SLOTS,)),
                        pltpu.VMEM((tm, D_QKVU), F32)],
        compiler_params=pltpu.CompilerParams(
            dimension_semantics=("arbitrary",), vmem_limit_bytes=V7X_VMEM_LIMIT),
        name="stage_a",
    )(x.reshape(B * T, D_MODEL), mod, n1g, w_qkvu, qg, kg, cos, sin_signed)


def _write_attn_output(o_ref, acc, l, tq):
    o = acc * (1.0 / l)
    o_ref[:, :HEAD_DIM] = o[:, :tq].T.astype(BF16)
    o_ref[:, HEAD_DIM:] = o[:, tq:].T.astype(BF16)


def _attn_bounded_kernel(qt_ref, qb_ref, k_ref, vt_ref, o_ref, p_ref, acc_ref, l_ref, *, tk):
    T = k_ref.shape[0]
    tq = qt_ref.shape[2]
    nq = 2 * tq
    nk = T // tk
    qt = jnp.concatenate([qt_ref[0], qt_ref[1]], axis=1)
    qb = jnp.concatenate([qb_ref[0], qb_ref[1]], axis=1)

    def weights(j, slot):
        off = pl.multiple_of(j * tk, tk)
        st = jnp.dot(k_ref[pl.ds(off, tk), :], qt, preferred_element_type=F32)
        p = jnp.exp2(st - qb)
        l_ref[...] += jnp.sum(p.reshape(tk // SUBLANES, SUBLANES, nq), axis=0)
        p_ref[slot] = p.astype(BF16)

    def values(j, slot):
        off = pl.multiple_of(j * tk, tk)
        acc_ref[...] += jnp.dot(vt_ref[:, pl.ds(off, tk)], p_ref[slot], preferred_element_type=F32)

    def step(j, slot):
        values(j - 1, 1 - slot)
        weights(j, slot)

    acc_ref[...] = jnp.zeros_like(acc_ref)
    l_ref[...] = jnp.zeros_like(l_ref)
    weights(0, 0)

    unroll = max(1, min(ATTN_TILES_PER_TRIP, nk - 1))
    trips = (nk - 1) // unroll if unroll % 2 == 0 else 0

    if trips > 1:
        @pl.loop(0, trips)
        def _(g):
            for d in range(unroll):
                step(1 + unroll * g + d, (1 + d) % 2)
    else:
        trips = 0
    for j in range(1 + trips * unroll, nk):
        step(j, j % 2)
    values(nk - 1, (nk - 1) % 2)
    _write_attn_output(o_ref, acc_ref[...], jnp.sum(l_ref[...], axis=0, keepdims=True), tq)


def _attn_online_kernel(qt_ref, qb_ref, k_ref, vt_ref, o_ref, *, tk):
    del qb_ref
    T = k_ref.shape[0]
    tq = qt_ref.shape[2]
    nq = 2 * tq
    qt = jnp.concatenate([qt_ref[0], qt_ref[1]], axis=1)

    def body(s, carry):
        m, l, acc = carry
        off = pl.multiple_of(s * tk, tk)
        st = jnp.dot(k_ref[pl.ds(off, tk), :], qt, preferred_element_type=F32)
        m_new = jnp.maximum(m, jnp.max(st, axis=0, keepdims=True))
        alpha = jnp.exp2(m - m_new)
        p = jnp.exp2(st - m_new)
        l = alpha * l + jnp.sum(p, axis=0, keepdims=True)
        pv = jnp.dot(vt_ref[:, pl.ds(off, tk)], p.astype(BF16), preferred_element_type=F32)
        return m_new, l, alpha * acc + pv

    init = (jnp.full((1, nq), -jnp.inf, F32), jnp.zeros((1, nq), F32), jnp.zeros((HEAD_DIM, nq), F32))
    _, l, acc = lax.fori_loop(0, T // tk, body, init)
    _write_attn_output(o_ref, acc, l, tq)


def _attention(qt, qb, k, vt, tq, tk, running_max):
    B, _, _, T = qt.shape
    group = N_Q_HEADS // N_KV_HEADS
    nq = group * tq
    if running_max:
        body, scratch, name = functools.partial(_attn_online_kernel, tk=tk), [], "attention_online_max"
    else:
        body = functools.partial(_attn_bounded_kernel, tk=tk)
        scratch = [pltpu.VMEM((2, tk, nq), BF16), pltpu.VMEM((HEAD_DIM, nq), F32), pltpu.VMEM((SUBLANES, nq), F32)]
        name = "attention_bounded"
    return pl.pallas_call(
        body,
        grid=(B, N_KV_HEADS, T // tq),
        in_specs=[
            pl.BlockSpec((None, group, HEAD_DIM, tq), lambda b, j, i: (b, j, 0, i)),
            pl.BlockSpec((None, group, 1, tq), lambda b, j, i: (b, j, 0, i)),
            pl.BlockSpec((None, T, HEAD_DIM), lambda b, j, i: (b, 0, j)),
            pl.BlockSpec((None, None, HEAD_DIM, T), lambda b, j, i: (b, j, 0, 0)),
        ],
        out_specs=pl.BlockSpec((None, tq, group * HEAD_DIM), lambda b, j, i: (b, i, j)),
        out_shape=jax.ShapeDtypeStruct((B, T, D_ATTN), BF16),
        scratch_shapes=scratch,
        compiler_params=pltpu.CompilerParams(
            dimension_semantics=("parallel", "parallel", "parallel"),
            vmem_limit_bytes=V7X_VMEM_LIMIT),
        name=name,
    )(qt, qb, k, vt)


def _route(logits_t):
    neg = jnp.float32(-jnp.inf)
    n = logits_t.shape[1]

    def top(vals, idx, far):
        v = jnp.max(vals, axis=0, keepdims=True)
        i = jnp.min(jnp.where(vals == v, idx, far), axis=0, keepdims=True)
        return v, i

    gidx = lax.broadcasted_iota(I32, (ROUTE_ROWS - N_EXPERTS, n), 0).astype(F32)
    grp = jnp.where(gidx < N_GROUPS, logits_t[N_EXPERTS:, :], neg)
    gmax, gsel = top(grp, gidx, jnp.float32(ROUTE_ROWS))
    p_group = 1.0 / jnp.sum(jnp.exp(grp - gmax), axis=0, keepdims=True)
    eidx = lax.broadcasted_iota(I32, (N_EXPERTS, n), 0).astype(F32)
    first = EXPERTS_PER_GROUP * gsel
    le = jnp.where((eidx >= first) & (eidx < first + EXPERTS_PER_GROUP), logits_t[:N_EXPERTS, :], neg)
    v1, e1 = top(le, eidx, jnp.float32(N_EXPERTS))
    v2, e2 = top(jnp.where(eidx == e1, neg, le), eidx, jnp.float32(N_EXPERTS))
    e21 = jnp.exp(v2 - v1)
    w1 = p_group / (1.0 + e21)
    return e1, e2, w1, w1 * e21


def _pool_deviation(ue_ref, lvl_ref, pool_ref, tile_start, tm, seq_len, groups):
    n = tm + 2 * POOL_HALO
    t = tile_start + lax.broadcasted_iota(I32, (tm, 1), 0)
    for gi in groups:
        w = POOL_WINDOWS[gi]
        half = w // 2
        cols = slice(gi * POOL_GC, (gi + 1) * POOL_GC)
        level = ue_ref[0:n, cols] + ue_ref[1:n + 1, cols]
        span, buf = 2, 0
        lvl_ref[buf, 0:n, :] = level
        while span < w:
            level = lvl_ref[buf, 0:n, :] + lvl_ref[buf, span:span + n, :]
            span, buf = 2 * span, 1 - buf
            lvl_ref[buf, 0:n, :] = level
        wsum = lvl_ref[buf, POOL_HALO - half:POOL_HALO - half + tm, :]
        cnt = (jnp.minimum(t + half, seq_len) - jnp.maximum(t - half, 0)).astype(F32)
        p = wsum * (1.0 / cnt) - ue_ref[POOL_HALO:POOL_HALO + tm, cols]
        pool_ref[:, cols] = p.astype(BF16)


def _norm2_split(x1, gain2, shift2, h2_ref):
    h2 = _scaled_norm(x1, gain2, shift2)
    h2_ref[...] = _pack_bf16_pair(h2[:, :D_PACK], h2[:, D_PACK:])
    h_hi = h2.astype(BF16)
    return h_hi, (h2 - h_hi.astype(F32)).astype(BF16)


def _router_logits(h_hi, h_lo, wr_ref, br_ref):
    parts = (jnp.dot(h_hi, wr_ref[...], preferred_element_type=F32)
             + jnp.dot(h_lo, wr_ref[...], preferred_element_type=F32))
    return parts[:, :LANES] + parts[:, LANES:] + br_ref[...]


def _dispatch_table(logits, is_real, utri_ref, run_ref, meta_ref, rw_ref):
    n = logits.shape[0]
    e1, e2, w1, w2 = _route(logits.T[0:ROUTE_ROWS, :])

    eidx = lax.broadcasted_iota(I32, (N_EXPERTS, n), 0).astype(F32)
    hit1 = eidx == e1
    hit2 = eidx == e2
    taken = jnp.where(hit1 | hit2, is_real, 0.0)
    before = run_ref[...] + jnp.dot(taken.astype(BF16), utri_ref[...], preferred_element_type=F32)
    rank1 = jnp.sum(jnp.where(hit1, before, 0.0), axis=0, keepdims=True)
    rank2 = jnp.sum(jnp.where(hit2, before, 0.0), axis=0, keepdims=True)
    run_ref[...] = run_ref[...] + jnp.sum(taken, axis=1, keepdims=True)
    ridx = lax.broadcasted_iota(I32, (META_ROWS, n), 0)
    table = jnp.where(ridx == 0, e1, jnp.where(ridx == 1, e2, jnp.where(ridx == 2, rank1,
                      jnp.where(ridx == 3, rank2, 0.0))))
    meta_ref[...] = table.astype(I32)
    weights_t = jnp.where(ridx == 0, w1, jnp.where(ridx == 1, w2, 0.0))
    rw_ref[...] = jnp.concatenate([weights_t, jnp.zeros((LANES - META_ROWS, n), F32)], axis=0).T


def _stage_c_kernel(x_ref, mod_ref, modp_ref, n1g_ref, n2g_ref, u_ref, up_ref, un_ref, attn_ref,
                    wgate_ref, wmix_ref, ps_ref, wpp_ref, wap_ref, wo_ref, wr_ref, br_ref, utri_ref,
                    x1_ref, h2_ref, meta_ref, rw_ref, count_ref,
                    ue_ref, lvl_ref, pool_ref, run_ref, merged_ref, x1_prev, *, seq_len):
    s = pl.program_id(0)
    tm = x_ref.shape[0]
    tiles_per_seq = seq_len // tm
    i = lax.rem(jnp.minimum(s, pl.num_programs(0) - 2), tiles_per_seq)

    @pl.when(s == 0)
    def _():
        run_ref[...] = jnp.zeros_like(run_ref)
        x1_prev[...] = jnp.zeros_like(x1_prev)

    gain1 = n1g_ref[...] * (1.0 + mod_ref[1:2, :])
    shift1 = mod_ref[0:1, :]
    res_gate = mod_ref[2:3, :]
    x = x_ref[...]
    h = _scaled_norm(x, gain1, shift1).astype(BF16)
    ue_ref[0:POOL_HALO, :] = up_ref[...].astype(F32) * (i > 0).astype(F32)
    ue_ref[POOL_HALO:POOL_HALO + tm, :] = u_ref[...].astype(F32)
    ue_ref[POOL_HALO + tm:2 * POOL_HALO + tm, :] = un_ref[...].astype(F32) * (i < tiles_per_seq - 1).astype(F32)
    ue_ref[2 * POOL_HALO + tm:, :] = jnp.zeros((POOL_HALO, D_POOL), F32)
    lvl_ref[:, 2 * POOL_HALO + tm:, :] = jnp.zeros((2, POOL_HALO, POOL_GC), F32)
    _pool_deviation(ue_ref, lvl_ref, pool_ref, i * tm, tm, seq_len, range(len(POOL_WINDOWS)))

    attn = attn_ref[...]
    mixed_groups = []
    for gi in range(len(POOL_WINDOWS)):
        cols = slice(gi * POOL_GC, (gi + 1) * POOL_GC)
        pm = jnp.dot(pool_ref[:, cols], wmix_ref[gi], preferred_element_type=F32)
        mixed_groups.append(pm * ps_ref[:, cols])
    pool = jnp.concatenate(mixed_groups, axis=1).astype(BF16)
    assert max(ROUTED_PHASE_AFTER_BLOCK) < D_MODEL // MERGE_COLS
    for c in range(0, D_MODEL, MERGE_COLS):
        cols = slice(c, c + MERGE_COLS)
        gate_cols = slice(D_MODEL + c, D_MODEL + c + MERGE_COLS)
        gate_pool = _sigmoid(jnp.dot(h, wgate_ref[:, cols], preferred_element_type=F32))
        gate_attn = _sigmoid(jnp.dot(h, wgate_ref[:, gate_cols], preferred_element_type=F32))
        merged = (gate_pool * jnp.dot(pool, wpp_ref[:, cols], preferred_element_type=F32)
                  + gate_attn * jnp.dot(attn, wap_ref[:, cols], preferred_element_type=F32))
        merged_ref[:, cols] = merged.astype(BF16)
        block = c // MERGE_COLS
        if block == ROUTED_PHASE_AFTER_BLOCK[0]:
            h_hi, h_lo = _norm2_split(x1_prev[...], n2g_ref[...] * (1.0 + modp_ref[4:5, :]), modp_ref[3:4, :],
                                      h2_ref)
        if block == ROUTED_PHASE_AFTER_BLOCK[1]:
            logits = _router_logits(h_hi, h_lo, wr_ref, br_ref)
        if block == ROUTED_PHASE_AFTER_BLOCK[2]:
            _dispatch_table(logits, (s > 0).astype(F32), utri_ref, run_ref, meta_ref, rw_ref)
            count_ref[...] = jnp.broadcast_to(run_ref[...], count_ref.shape)
    for c in range(0, D_MODEL, MERGE_COLS):
        cols = slice(c, c + MERGE_COLS)
        mixed = jnp.dot(merged_ref[...], wo_ref[:, cols], preferred_element_type=F32)
        x1 = x[:, cols] + res_gate[:, cols] * mixed
        x1_ref[:, cols] = x1.astype(BF16)
        x1_prev[:, cols] = x1


def _stage_c(x, mod, n1g, n2g, u, attn, w_gate, w_mix, pool_scale, w_pp, w_ap, w_o, w_r, b_r, tm):
    B, T, _ = x.shape
    hb = tm // POOL_HALO
    n_halo_blocks = T // POOL_HALO
    nt = T // tm
    n_tiles = B * nt

    def merged_tile(s):
        m = jnp.minimum(s, n_tiles - 1)
        return m // nt, lax.rem(m, nt)

    def tok(s):
        b, i = merged_tile(s)
        return b, i, 0

    def halo_before(s):
        b, i = merged_tile(s)
        return b, jnp.maximum(i * hb - 1, 0), 0

    def halo_after(s):
        b, i = merged_tile(s)
        return b, jnp.minimum((i + 1) * hb, n_halo_blocks - 1), 0

    routed = lambda s: jnp.maximum(s - 1, 0)
    flat = lambda s: (routed(s), 0)
    utri = jnp.triu(jnp.ones((tm, tm), BF16), k=1)
    return pl.pallas_call(
        functools.partial(_stage_c_kernel, seq_len=T),
        grid=(n_tiles + 1,),
        in_specs=[
            pl.BlockSpec((None, tm, D_MODEL), tok),
            pl.BlockSpec((None, 6, D_MODEL), lambda s: (merged_tile(s)[0], 0, 0)),
            pl.BlockSpec((None, 6, D_MODEL), lambda s: (routed(s) // nt, 0, 0)),
            _const_spec((1, D_MODEL)),
            _const_spec((1, D_MODEL)),
            pl.BlockSpec((None, tm, D_POOL), tok),
            pl.BlockSpec((None, POOL_HALO, D_POOL), halo_before),
            pl.BlockSpec((None, POOL_HALO, D_POOL), halo_after),
            pl.BlockSpec((None, tm, D_ATTN), tok),
            _const_spec((D_MODEL, 2 * D_MODEL)),
            _const_spec((len(POOL_WINDOWS), POOL_GC, POOL_GC)),
            _const_spec((1, D_POOL)),
            _const_spec((D_POOL, D_MODEL)),
            _const_spec((D_ATTN, D_MODEL)),
            _const_spec((D_MODEL, D_MODEL)),
            _const_spec((D_MODEL, 2 * LANES)),
            _const_spec((1, LANES)),
            _const_spec((tm, tm)),
        ],
        out_specs=[
            pl.BlockSpec((None, tm, D_MODEL), tok),
            pl.BlockSpec((tm, D_PACK), flat),
            pl.BlockSpec((META_ROWS, tm), lambda s: (0, routed(s))),
            pl.BlockSpec((tm, LANES), flat),
            pl.BlockSpec((N_EXPERTS, LANES), lambda s: (0, 0)),
        ],
        out_shape=[
            jax.ShapeDtypeStruct((B, T, D_MODEL), BF16),
            jax.ShapeDtypeStruct((B * T, D_PACK), I32),
            jax.ShapeDtypeStruct((META_ROWS, B * T), I32),
            jax.ShapeDtypeStruct((B * T, LANES), F32),
            jax.ShapeDtypeStruct((N_EXPERTS, LANES), F32),
        ],
        scratch_shapes=[pltpu.VMEM((tm + 3 * POOL_HALO, D_POOL), F32),
                        pltpu.VMEM((2, tm + 3 * POOL_HALO, POOL_GC), F32),
                        pltpu.VMEM((tm, D_POOL), BF16),
                        pltpu.VMEM((N_EXPERTS, 1), F32),
                        pltpu.VMEM((tm, D_MODEL), BF16),
                        pltpu.VMEM((tm, D_MODEL), F32)],
        compiler_params=pltpu.CompilerParams(
            dimension_semantics=("arbitrary",), vmem_limit_bytes=V7X_VMEM_LIMIT),
        name="stage_c",
    )(x, mod, mod, n1g, n2g, u, u, u, attn, w_gate, w_mix, pool_scale, w_pp, w_ap, w_o, w_r, b_r, utri)


def _sc_workers():
    info = plsc.get_sparse_core_info()
    return info.num_cores, info.num_subcores


def _sc_scatter_rows(rows, idx0, idx1, n_out):
    n, d = rows.shape
    nc, ns = _sc_workers()
    per_worker = n // (nc * ns)
    n_win = per_worker // SC_WINDOW
    mesh = plsc.VectorSubcoreMesh(core_axis_name="c", subcore_axis_name="s")

    @functools.partial(
        pl.kernel, mesh=mesh,
        out_type=jax.ShapeDtypeStruct((n_out, d), rows.dtype),
        scratch_types=[pltpu.VMEM((SC_WINDOW,), I32), pltpu.VMEM((SC_WINDOW,), I32),
                       pltpu.VMEM((SC_WINDOW, d), rows.dtype)],
        name="sc_scatter_rows",
    )
    def scatter(rows_hbm, idx0_hbm, idx1_hbm, out_hbm, i0_v, i1_v, rows_v):
        wid = lax.axis_index("s") * nc + lax.axis_index("c")

        @pl.loop(0, n_win)
        def _(w):
            base = wid * per_worker + w * SC_WINDOW
            pltpu.sync_copy(rows_hbm.at[pl.ds(base, SC_WINDOW)], rows_v)
            pltpu.sync_copy(idx0_hbm.at[pl.ds(base, SC_WINDOW)], i0_v)
            pltpu.sync_copy(idx1_hbm.at[pl.ds(base, SC_WINDOW)], i1_v)
            pltpu.sync_copy(rows_v, out_hbm.at[i0_v])
            pltpu.sync_copy(rows_v, out_hbm.at[i1_v])

    return scatter(rows, idx0, idx1)


def _sc_gather_rows(table, idx):
    n = idx.shape[0]
    d = table.shape[1]
    nc, ns = _sc_workers()
    per_worker = n // (nc * ns)
    n_win = per_worker // SC_WINDOW
    mesh = plsc.VectorSubcoreMesh(core_axis_name="c", subcore_axis_name="s")

    @functools.partial(
        pl.kernel, mesh=mesh,
        out_type=jax.ShapeDtypeStruct((n, d), table.dtype),
        scratch_types=[pltpu.VMEM((SC_WINDOW,), I32), pltpu.VMEM((SC_WINDOW, d), table.dtype)],
        name="sc_gather_rows",
    )
    def gather(table_hbm, idx_hbm, out_hbm, i_v, rows_v):
        wid = lax.axis_index("s") * nc + lax.axis_index("c")

        @pl.loop(0, n_win)
        def _(w):
            base = wid * per_worker + w * SC_WINDOW
            pltpu.sync_copy(idx_hbm.at[pl.ds(base, SC_WINDOW)], i_v)
            pltpu.sync_copy(table_hbm.at[i_v], rows_v)
            pltpu.sync_copy(rows_v, out_hbm.at[pl.ds(base, SC_WINDOW)])

    return gather(table, idx)


def _moe_kernel(te_ref, tv_ref, nl_ref, x_hbm, wg_ref, wu_ref, wd_ref, o_ref, xbuf, sem, wgu_ref):
    i = pl.program_id(0)
    n_live_tiles = nl_ref[0]
    valid = tv_ref[i]
    tm = o_ref.shape[0]
    slot = _input_ring(x_hbm, xbuf, sem, i, n_live_tiles, tm)

    @pl.when((i == 0) | (te_ref[i] != te_ref[jnp.maximum(i - 1, 0)]))
    def _():
        wgu_ref[:, :D_EXPERT] = wg_ref[...]
        wgu_ref[:, D_EXPERT:] = wu_ref[...]

    group = min(STAGE_ROWS, tm)
    n_groups = tm // group

    def gate_up(r0):
        live = lax.broadcasted_iota(I32, (group, 1), 0) < valid - r0
        lo, hi = _unpack_bf16_pair(jnp.where(live, xbuf[slot, r0:r0 + group, :], 0))
        x = jnp.concatenate([lo, hi], axis=1).astype(BF16)
        return jnp.dot(x, wgu_ref[...], preferred_element_type=F32)

    def down(gu):
        a = gu[:, :D_EXPERT]
        hmid = (a * _sigmoid(a) * gu[:, D_EXPERT:]).astype(BF16)
        return jnp.dot(hmid, wd_ref[...], preferred_element_type=F32)

    def store(r0, y):
        o_ref[r0:r0 + group, :] = _pack_bf16_pair(y[:, :D_PACK], y[:, D_PACK:])

    for n_live in range(n_groups + 1):
        @pl.when((valid > (n_live - 1) * group) & (valid <= n_live * group))
        def _():
            gu, y = None, None
            for g in range(n_live + 2):
                nxt = gate_up(g * group) if g < n_live else None
                if y is not None:
                    store((g - 2) * group, y)
                y = down(gu) if gu is not None else None
                gu = nxt
            if n_live < n_groups:
                o_ref[n_live * group:, :] = jnp.zeros((tm - n_live * group, D_PACK), I32)


def _moe(xs, tile_expert, tile_valid, n_live_tiles, w_g, w_u, w_d, tm):
    n_tiles = xs.shape[0] // tm
    grid_spec = pltpu.PrefetchScalarGridSpec(
        num_scalar_prefetch=3,
        grid=(n_tiles,),
        in_specs=[
            pl.BlockSpec(memory_space=pl.ANY),
            pl.BlockSpec((None, D_MODEL, D_EXPERT), lambda i, te, tv, nl: (te[i], 0, 0)),
            pl.BlockSpec((None, D_MODEL, D_EXPERT), lambda i, te, tv, nl: (te[i], 0, 0)),
            pl.BlockSpec((None, D_EXPERT, D_MODEL), lambda i, te, tv, nl: (te[i], 0, 0)),
        ],
        out_specs=pl.BlockSpec((tm, D_PACK), lambda i, te, tv, nl: (jnp.minimum(i, nl[0]), 0)),
        scratch_shapes=[pltpu.VMEM((RING_SLOTS, tm, D_PACK), I32), pltpu.SemaphoreType.DMA((RING_SLOTS,)),
                        pltpu.VMEM((D_MODEL, 2 * D_EXPERT), BF16)],
    )
    return pl.pallas_call(
        _moe_kernel,
        grid_spec=grid_spec,
        out_shape=jax.ShapeDtypeStruct(xs.shape, I32),
        compiler_params=pltpu.CompilerParams(
            dimension_semantics=("arbitrary",), vmem_limit_bytes=V7X_VMEM_LIMIT),
        name="experts",
    )(tile_expert, tile_valid, n_live_tiles, xs, w_g, w_u, w_d)


def _final_kernel(x1_ref, y0_ref, y1_ref, rw_ref, mod_ref, fg_ref, o_ref):
    y0lo, y0hi = _unpack_bf16_pair(y0_ref[...])
    y1lo, y1hi = _unpack_bf16_pair(y1_ref[...])
    w0 = rw_ref[:, 0:1]
    w1 = rw_ref[:, 1:2]
    moe = jnp.concatenate([w0 * y0lo + w1 * y1lo, w0 * y0hi + w1 * y1hi], axis=1)
    x2 = x1_ref[...].astype(F32) + mod_ref[5:6, :] * moe
    ms = jnp.mean(x2 * x2, axis=-1, keepdims=True)
    o_ref[...] = x2 * lax.rsqrt(ms + EPS) * fg_ref[...]


def _final(x1, yb, rw, mod, final_g, tm):
    B, T, _ = x1.shape
    n_blocks = B * T // tm
    flat = lambda b, i: (b * (T // tm) + i, 0)
    return pl.pallas_call(
        _final_kernel,
        grid=(B, T // tm),
        in_specs=[
            pl.BlockSpec((None, tm, D_MODEL), lambda b, i: (b, i, 0)),
            pl.BlockSpec((tm, D_PACK), flat),
            pl.BlockSpec((tm, D_PACK), lambda b, i: (n_blocks + b * (T // tm) + i, 0)),
            pl.BlockSpec((tm, LANES), flat),
            pl.BlockSpec((None, 6, D_MODEL), lambda b, i: (b, 0, 0)),
            _const_spec((1, D_MODEL)),
        ],
        out_specs=pl.BlockSpec((None, tm, D_MODEL), lambda b, i: (b, i, 0)),
        out_shape=jax.ShapeDtypeStruct((B, T, D_MODEL), F32),
        compiler_params=pltpu.CompilerParams(
            dimension_semantics=("parallel", "parallel"), vmem_limit_bytes=V7X_VMEM_LIMIT),
        name="final_combine",
    )(x1, yb, yb, rw, mod, final_g)


def _rope_tables(T):
    rows = T // GRID_W
    row = np.repeat(np.arange(rows, dtype=np.float32), GRID_W)
    col = np.tile(np.arange(GRID_W, dtype=np.float32), rows)
    inv_freq = np.float32(ROPE_THETA) ** (-np.arange(0, 64, 2, dtype=np.float32) / np.float32(64))
    ang_r = row[:, None] * inv_freq[None, :]
    ang_c = col[:, None] * inv_freq[None, :]
    ang = np.concatenate([ang_r, ang_r, ang_c, ang_c], axis=-1).astype(np.float32)
    sign = np.where((np.arange(HEAD_DIM) % 64) < 32, -1.0, 1.0).astype(np.float32)
    return jnp.asarray(np.cos(ang)), jnp.asarray(np.sin(ang) * sign[None, :])


def _dispatch_plan(counts, tm, n_tiles):
    tiles_per_e = (counts + tm - 1) // tm
    tile_end = jnp.cumsum(tiles_per_e)
    tile_start = tile_end - tiles_per_e
    tile_id = jnp.arange(n_tiles, dtype=I32)
    te = jnp.minimum(jnp.sum((tile_id[:, None] >= tile_end[None, :]).astype(I32), axis=1), N_EXPERTS - 1)
    mine = te[:, None] == jnp.arange(N_EXPERTS, dtype=I32)[None, :]
    count_of = jnp.sum(jnp.where(mine, counts[None, :], 0), axis=1)
    start_of = jnp.sum(jnp.where(mine, tile_start[None, :], 0), axis=1)
    live = jnp.clip(count_of - (tile_id - start_of) * tm, 0, tm)
    tv = jnp.where(tile_id < tile_end[-1], live, 0).astype(I32)
    return tile_start.astype(I32), te, tv, tile_end[-1:].astype(I32)


def _slot_kernel(start_ref, meta_ref, pos_ref, *, tile_rows):
    e = meta_ref[0:2, :]
    first_tile = jnp.zeros_like(e)
    for ex in range(N_EXPERTS):
        first_tile = jnp.where(e == ex, start_ref[ex], first_tile)
    pos_ref[...] = first_tile * tile_rows + meta_ref[2:4, :]


def _slots(meta, tile_start, tile_rows):
    n = meta.shape[1]
    tn = min(n, SLOT_TILE)
    grid_spec = pltpu.PrefetchScalarGridSpec(
        num_scalar_prefetch=1,
        grid=(n // tn,),
        in_specs=[pl.BlockSpec((META_ROWS, tn), lambda i, start: (0, i))],
        out_specs=pl.BlockSpec((2, tn), lambda i, start: (0, i)),
    )
    return pl.pallas_call(
        functools.partial(_slot_kernel, tile_rows=tile_rows),
        grid_spec=grid_spec,
        out_shape=jax.ShapeDtypeStruct((2, n), I32),
        name="dispatch_slots",
    )(tile_start, meta)


def _pick_tile(T, want):
    t = min(T, want)
    assert T % t == 0
    return t


def _trunk(x, mod, p):
    B, T, _ = x.shape
    N = B * T
    tm = _pick_tile(T, TOKEN_TILE)
    tq = _pick_tile(T, QUERY_TILE)
    tme = _pick_tile(N, EXPERT_TILE)
    cos, sin_signed = _rope_tables(T)
    u, qt, qb, k, vt = _stage_a(x, mod, p["n1g"], p["w_qkvu"], p["qg"], p["kg"], cos, sin_signed, tm)
    attn = lax.cond(
        p["bounded_softmax_ok"],
        functools.partial(_attention, tq=tq, tk=_pick_tile(T, KEY_TILE), running_max=False),
        functools.partial(_attention, tq=tq, tk=_pick_tile(T, KEY_TILE_ONLINE), running_max=True),
        qt, qb, k, vt)
    x1, h2p, meta, rw, counts = _stage_c(x, mod, p["n1g"], p["n2g"], u, attn, p["w_gate"], p["w_mix"],
                                         p["pool_scale"], p["w_pp"], p["w_ap"], p["w_o"], p["w_r"], p["b_r"], tm)
    n_tiles = 2 * N // tme + N_EXPERTS
    tile_start, te, tv, n_live_tiles = _dispatch_plan(counts[:, 0].astype(I32), tme, n_tiles)
    pos = _slots(meta, tile_start, tme)
    xs = _sc_scatter_rows(h2p, pos[0], pos[1], n_tiles * tme)
    ys = _moe(xs, te, tv, n_live_tiles, p["w_eg"], p["w_eu"], p["w_ed"], tme)
    yb = _sc_gather_rows(ys, pos.reshape(-1))
    return _final(x1, yb, rw, mod, p["final_g"], _pick_tile(T, FINAL_TILE))


def kernel(x_prompt, x_sample, c_prompt, c_sample, w_ada, b_ada, norm1_g, norm2_g, w_in, q_norm_g,
           k_norm_g, w_pool_mix, pool_scale, w_pool_proj, w_attn_proj, w_o, w_router_group,
           b_router_group, w_router_expert, b_router_expert, w_exp_gate, w_exp_up, w_exp_down, final_g):
    assert w_ada.shape[0] == 1, "single-layer block"
    n_r = N_GROUPS + N_EXPERTS
    w_r = jnp.concatenate([w_router_expert[0], w_router_group[0],
                           jnp.zeros((D_MODEL, LANES - n_r), F32)], axis=1)
    w_r_hi = w_r.astype(BF16)
    b_r = jnp.concatenate([b_router_expert[0], b_router_group[0], jnp.zeros((LANES - n_r,), F32)])
    score_bound = ((KEY_NORM_MARGIN * LOG2E * np.sqrt(HEAD_DIM))
                   * jnp.max(jnp.abs(q_norm_g[0])) * jnp.max(jnp.abs(k_norm_g[0])))
    p = dict(
        bounded_softmax_ok=2.0 * score_bound <= EXP2_SAFE_SPAN,
        n1g=norm1_g[0].reshape(1, D_MODEL), n2g=norm2_g[0].reshape(1, D_MODEL),
        w_qkvu=w_in[0][:, :D_QKVU].astype(BF16), w_gate=w_in[0][:, D_QKVU:].astype(BF16),
        qg=q_norm_g[0].reshape(1, HEAD_DIM), kg=k_norm_g[0].reshape(1, HEAD_DIM),
        w_mix=w_pool_mix[0].astype(BF16), pool_scale=pool_scale[0].reshape(1, D_POOL),
        w_pp=w_pool_proj[0].astype(BF16), w_ap=w_attn_proj[0].astype(BF16), w_o=w_o[0].astype(BF16),
        w_r=jnp.concatenate([w_r_hi, (w_r - w_r_hi.astype(F32)).astype(BF16)], axis=1), b_r=b_r.reshape(1, LANES),
        w_eg=w_exp_gate[0].astype(BF16), w_eu=w_exp_up[0].astype(BF16), w_ed=w_exp_down[0].astype(BF16),
        final_g=final_g.reshape(1, D_MODEL),
    )
    n_prompt = c_prompt.shape[0]
    mod = _modulation(jnp.concatenate([c_prompt, c_sample], axis=0), w_ada[0], b_ada[0]).reshape(-1, 6, D_MODEL)
    return _trunk(x_prompt, mod[:n_prompt], p), _trunk(x_sample, mod[n_prompt:], p)
```

```python
import functools

import numpy as np
import jax
import jax.numpy as jnp
from jax import lax
from jax.experimental import pallas as pl
from jax.experimental.pallas import tpu as pltpu
from jax.experimental.pallas import tpu_sc as plsc

F32 = jnp.float32
BF16 = jnp.bfloat16
I32 = jnp.int32

D_MODEL = 1024
GRID_W = 64
POOL_WINDOWS = (2, 4, 8, 16)
POOL_GC = 128
D_POOL = 512
HEAD_DIM = 128
N_Q_HEADS = 4
N_KV_HEADS = 2
D_ATTN = 512
D_KV = 256
ROPE_THETA = 10000.0
N_GROUPS = 4
EXPERTS_PER_GROUP = 8
N_EXPERTS = 32
D_EXPERT = 256
EPS = 1e-6
LOG2E = 1.4426950408889634
EXP2_SAFE_SPAN = 100.0
D_QKVU = D_POOL + D_ATTN + 2 * D_KV
POOL_HALO = 16
STAGE_ROWS = 256
PROJ_COLS = 256
ROUTED_PHASE_AFTER_BLOCK = (0, 1, 2)
MERGE_COLS = 256
RING_SLOTS = 3
ATTN_TILES_PER_TRIP = 10
ROUTE_ROWS = 40
META_ROWS = 8
LANES = 128
SUBLANES = 8
KEY_NORM_MARGIN = 1.01
TOKEN_TILE = 512
QUERY_TILE = 512
KEY_TILE = 4096
KEY_TILE_ONLINE = 256
EXPERT_TILE = 1024
FINAL_TILE = 2048
SLOT_TILE = 8192
D_PACK = D_MODEL // 2

V7X_VMEM_LIMIT = 56 * 1024 * 1024
SC_WINDOW = 128


def _const_spec(shape):
    nd = len(shape)
    return pl.BlockSpec(shape, lambda *_: (0,) * nd, pipeline_mode=pl.Buffered(1))


def _sigmoid(x):
    return 0.5 * jnp.tanh(0.5 * x) + 0.5


def _pack_bf16_pair(lo, hi):
    lo_bits = lax.bitcast_convert_type(lo.astype(BF16).astype(F32), jnp.uint32) >> 16
    hi_bits = lax.bitcast_convert_type(hi.astype(BF16).astype(F32), jnp.uint32) & jnp.uint32(0xFFFF0000)
    return lax.bitcast_convert_type(lo_bits | hi_bits, I32)


def _unpack_bf16_pair(packed):
    u = lax.bitcast_convert_type(packed, jnp.uint32)
    lo = lax.bitcast_convert_type(u << 16, F32)
    hi = lax.bitcast_convert_type(u & jnp.uint32(0xFFFF0000), F32)
    return lo, hi


def _input_ring(x_hbm, buf, sem, step, n_steps, rows):
    def fetch(s):
        slot = lax.rem(s, RING_SLOTS)
        src = x_hbm.at[pl.ds(pl.multiple_of(s * rows, rows), rows), :]
        return pltpu.make_async_copy(src, buf.at[slot], sem.at[slot])

    @pl.when((step == 0) & (n_steps > 0))
    def _():
        fetch(0).start()

    @pl.when((step == 0) & (n_steps > 1))
    def _():
        fetch(1).start()

    @pl.when(step + 2 < n_steps)
    def _():
        fetch(step + 2).start()

    @pl.when(step < n_steps)
    def _():
        fetch(step).wait()

    return lax.rem(step, RING_SLOTS)


def _mod_kernel(c_ref, w_ref, b_ref, o_ref):
    c = c_ref[...]
    s = c * _sigmoid(c)
    o_ref[...] = jnp.dot(s, w_ref[...], preferred_element_type=F32,
                         precision=lax.Precision.HIGHEST) + b_ref[...]


def _modulation(c, w_ada, b_ada):
    B = c.shape[0]
    n_out = w_ada.shape[1]
    bn = D_MODEL
    return pl.pallas_call(
        _mod_kernel,
        grid=(n_out // bn,),
        in_specs=[pl.BlockSpec((B, D_MODEL), lambda j: (0, 0)),
                  pl.BlockSpec((D_MODEL, bn), lambda j: (0, j)),
                  pl.BlockSpec((1, bn), lambda j: (0, j))],
        out_specs=pl.BlockSpec((B, bn), lambda j: (0, j)),
        out_shape=jax.ShapeDtypeStruct((B, n_out), F32),
        name="modulation",
    )(c, w_ada, b_ada.reshape(1, n_out))


def _scaled_norm(x, gain, shift):
    ms = jnp.mean(x * x, axis=-1, keepdims=True)
    return x * lax.rsqrt(ms + EPS) * gain + shift


def _stage_a_kernel(x_hbm, mod_ref, n1g_ref, w_ref, qg_ref, kg_ref, cos_ref, sin_ref,
                    u_ref, qt_ref, qb_ref, k_ref, vt_ref, xbuf, sem, z_ref):
    s = pl.program_id(0)
    n_tiles = pl.num_programs(0) - 1
    tm = u_ref.shape[0]
    _input_ring(x_hbm, xbuf, sem, s, n_tiles, tm)
    slot = lax.rem(jnp.minimum(s, n_tiles - 1), RING_SLOTS)

    @pl.when(s == 0)
    def _():
        z_ref[...] = jnp.zeros_like(z_ref)

    gain1 = n1g_ref[...] * (1.0 + mod_ref[1:2, :])
    shift1 = mod_ref[0:1, :]
    lane = lax.broadcasted_iota(I32, (1, HEAD_DIM), 1)
    first_half = (lane % 64) < 32
    qg = qg_ref[...] * (LOG2E / np.sqrt(HEAD_DIM))
    k_norm_max = jnp.max(jnp.abs(kg_ref[...]), axis=-1, keepdims=True) * (np.sqrt(HEAD_DIM) * KEY_NORM_MARGIN)
    cos = cos_ref[...]
    sin_signed = sin_ref[...]

    def norm_rope(xh, g):
        ms = jnp.mean(xh * xh, axis=-1, keepdims=True)
        xn = xh * lax.rsqrt(ms + EPS) * g
        rot = jnp.where(first_half, pltpu.roll(xn, 96, 1), pltpu.roll(xn, 32, 1))
        return xn * cos + rot * sin_signed

    def finish(o):
        zh = z_ref[:, o:o + HEAD_DIM]
        if o < D_POOL:
            u_ref[:, o:o + HEAD_DIM] = zh.astype(BF16)
        elif o < D_POOL + D_ATTN:
            hq = (o - D_POOL) // HEAD_DIM
            q_t = norm_rope(zh, qg).T.astype(BF16)
            qt_ref[hq] = q_t
            q_f = q_t.astype(F32)
            qb_ref[hq] = jnp.sqrt(jnp.sum(q_f * q_f, axis=0, keepdims=True)) * k_norm_max
        elif o < D_POOL + D_ATTN + D_KV:
            k_ref[:, o - D_POOL - D_ATTN:o - D_POOL - D_ATTN + HEAD_DIM] = norm_rope(zh, kg_ref[...]).astype(BF16)
        else:
            vt_ref[(o - D_POOL - D_ATTN - D_KV) // HEAD_DIM] = zh.T.astype(BF16)

    h = _scaled_norm(xbuf[slot], gain1, shift1).astype(BF16)
    for c in range(0, D_QKVU, PROJ_COLS):
        for o in range(c, c + PROJ_COLS, HEAD_DIM):
            finish(o)
        z_ref[:, c:c + PROJ_COLS] = jnp.dot(h, w_ref[:, c:c + PROJ_COLS], preferred_element_type=F32)


def _stage_a(x, mod, n1g, w_qkvu, qg, kg, cos, sin_signed, tm):
    B, T, _ = x.shape
    nt = T // tm
    n_tiles = B * nt
    projected = lambda s: jnp.minimum(s, n_tiles - 1)
    finished = lambda s: jnp.maximum(s - 1, 0)
    return pl.pallas_call(
        _stage_a_kernel,
        grid=(n_tiles + 1,),
        in_specs=[
            pl.BlockSpec(memory_space=pl.ANY),
            pl.BlockSpec((None, 6, D_MODEL), lambda s: (projected(s) // nt, 0, 0)),
            _const_spec((1, D_MODEL)),
            _const_spec((D_MODEL, D_QKVU)),
            _const_spec((1, HEAD_DIM)),
            _const_spec((1, HEAD_DIM)),
            pl.BlockSpec((tm, HEAD_DIM), lambda s: (lax.rem(finished(s), nt), 0)),
            pl.BlockSpec((tm, HEAD_DIM), lambda s: (lax.rem(finished(s), nt), 0)),
        ],
        out_specs=[
            pl.BlockSpec((None, tm, D_POOL), lambda s: (finished(s) // nt, lax.rem(finished(s), nt), 0)),
            pl.BlockSpec((None, N_Q_HEADS, HEAD_DIM, tm), lambda s: (finished(s) // nt, 0, 0, lax.rem(finished(s), nt))),
            pl.BlockSpec((None, N_Q_HEADS, 1, tm), lambda s: (finished(s) // nt, 0, 0, lax.rem(finished(s), nt))),
            pl.BlockSpec((None, tm, D_KV), lambda s: (finished(s) // nt, lax.rem(finished(s), nt), 0)),
            pl.BlockSpec((None, N_KV_HEADS, HEAD_DIM, tm), lambda s: (finished(s) // nt, 0, 0, lax.rem(finished(s), nt))),
        ],
        out_shape=[
            jax.ShapeDtypeStruct((B, T, D_POOL), BF16),
            jax.ShapeDtypeStruct((B, N_Q_HEADS, HEAD_DIM, T), BF16),
            jax.ShapeDtypeStruct((B, N_Q_HEADS, 1, T), F32),
            jax.ShapeDtypeStruct((B, T, D_KV), BF16),
            jax.ShapeDtypeStruct((B, N_KV_HEADS, HEAD_DIM, T), BF16),
        ],
        scratch_shapes=[pltpu.VMEM((RING_SLOTS, tm, D_MODEL), F32), pltpu.SemaphoreType.DMA((RING_SLOTS,)),
                        pltpu.VMEM((tm, D_QKVU), F32)],
        compiler_params=pltpu.CompilerParams(
            dimension_semantics=("arbitrary",), vmem_limit_bytes=V7X_VMEM_LIMIT),
        name="stage_a",
    )(x.reshape(B * T, D_MODEL), mod, n1g, w_qkvu, qg, kg, cos, sin_signed)


def _write_attn_output(o_ref, acc, l, tq):
    o = acc * (1.0 / l)
    o_ref[:, :HEAD_DIM] = o[:, :tq].T.astype(BF16)
    o_ref[:, HEAD_DIM:] = o[:, tq:].T.astype(BF16)


def _attn_bounded_kernel(qt_ref, qb_ref, k_ref, vt_ref, o_ref, p_ref, acc_ref, l_ref, *, tk):
    T = k_ref.shape[0]
    tq = qt_ref.shape[2]
    nq = 2 * tq
    nk = T // tk
    qt = jnp.concatenate([qt_ref[0], qt_ref[1]], axis=1)
    qb = jnp.concatenate([qb_ref[0], qb_ref[1]], axis=1)

    def weights(j, slot):
        off = pl.multiple_of(j * tk, tk)
        st = jnp.dot(k_ref[pl.ds(off, tk), :], qt, preferred_element_type=F32)
        p = jnp.exp2(st - qb)
        l_ref[...] += jnp.sum(p.reshape(tk // SUBLANES, SUBLANES, nq), axis=0)
        p_ref[slot] = p.astype(BF16)

    def values(j, slot):
        off = pl.multiple_of(j * tk, tk)
        acc_ref[...] += jnp.dot(vt_ref[:, pl.ds(off, tk)], p_ref[slot], preferred_element_type=F32)

    def step(j, slot):
        values(j - 1, 1 - slot)
        weights(j, slot)

    acc_ref[...] = jnp.zeros_like(acc_ref)
    l_ref[...] = jnp.zeros_like(l_ref)
    weights(0, 0)

    unroll = max(1, min(ATTN_TILES_PER_TRIP, nk - 1))
    trips = (nk - 1) // unroll if unroll % 2 == 0 else 0

    if trips > 1:
        @pl.loop(0, trips)
        def _(g):
            for d in range(unroll):
                step(1 + unroll * g + d, (1 + d) % 2)
    else:
        trips = 0
    for j in range(1 + trips * unroll, nk):
        step(j, j % 2)
    values(nk - 1, (nk - 1) % 2)
    _write_attn_output(o_ref, acc_ref[...], jnp.sum(l_ref[...], axis=0, keepdims=True), tq)


def _attn_online_kernel(qt_ref, qb_ref, k_ref, vt_ref, o_ref, *, tk):
    del qb_ref
    T = k_ref.shape[0]
    tq = qt_ref.shape[2]
    nq = 2 * tq
    qt = jnp.concatenate([qt_ref[0], qt_ref[1]], axis=1)

    def body(s, carry):
        m, l, acc = carry
        off = pl.multiple_of(s * tk, tk)
        st = jnp.dot(k_ref[pl.ds(off, tk), :], qt, preferred_element_type=F32)
        m_new = jnp.maximum(m, jnp.max(st, axis=0, keepdims=True))
        alpha = jnp.exp2(m - m_new)
        p = jnp.exp2(st - m_new)
        l = alpha * l + jnp.sum(p, axis=0, keepdims=True)
        pv = jnp.dot(vt_ref[:, pl.ds(off, tk)], p.astype(BF16), preferred_element_type=F32)
        return m_new, l, alpha * acc + pv

    init = (jnp.full((1, nq), -jnp.inf, F32), jnp.zeros((1, nq), F32), jnp.zeros((HEAD_DIM, nq), F32))
    _, l, acc = lax.fori_loop(0, T // tk, body, init)
    _write_attn_output(o_ref, acc, l, tq)


def _attention(qt, qb, k, vt, tq, tk, running_max):
    B, _, _, T = qt.shape
    group = N_Q_HEADS // N_KV_HEADS
    nq = group * tq
    if running_max:
        body, scratch, name = functools.partial(_attn_online_kernel, tk=tk), [], "attention_online_max"
    else:
        body = functools.partial(_attn_bounded_kernel, tk=tk)
        scratch = [pltpu.VMEM((2, tk, nq), BF16), pltpu.VMEM((HEAD_DIM, nq), F32), pltpu.VMEM((SUBLANES, nq), F32)]
        name = "attention_bounded"
    return pl.pallas_call(
        body,
        grid=(B, N_KV_HEADS, T // tq),
        in_specs=[
            pl.BlockSpec((None, group, HEAD_DIM, tq), lambda b, j, i: (b, j, 0, i)),
            pl.BlockSpec((None, group, 1, tq), lambda b, j, i: (b, j, 0, i)),
            pl.BlockSpec((None, T, HEAD_DIM), lambda b, j, i: (b, 0, j)),
            pl.BlockSpec((None, None, HEAD_DIM, T), lambda b, j, i: (b, j, 0, 0)),
        ],
        out_specs=pl.BlockSpec((None, tq, group * HEAD_DIM), lambda b, j, i: (b, i, j)),
        out_shape=jax.ShapeDtypeStruct((B, T, D_ATTN), BF16),
        scratch_shapes=scratch,
        compiler_params=pltpu.CompilerParams(
            dimension_semantics=("parallel", "parallel", "parallel"),
            vmem_limit_bytes=V7X_VMEM_LIMIT),
        name=name,
    )(qt, qb, k, vt)


def _route(logits_t):
    neg = jnp.float32(-jnp.inf)
    n = logits_t.shape[1]

    def top(vals, idx, far):
        v = jnp.max(vals, axis=0, keepdims=True)
        i = jnp.min(jnp.where(vals == v, idx, far), axis=0, keepdims=True)
        return v, i

    gidx = lax.broadcasted_iota(I32, (ROUTE_ROWS - N_EXPERTS, n), 0).astype(F32)
    grp = jnp.where(gidx < N_GROUPS, logits_t[N_EXPERTS:, :], neg)
    gmax, gsel = top(grp, gidx, jnp.float32(ROUTE_ROWS))
    p_group = 1.0 / jnp.sum(jnp.exp(grp - gmax), axis=0, keepdims=True)
    eidx = lax.broadcasted_iota(I32, (N_EXPERTS, n), 0).astype(F32)
    first = EXPERTS_PER_GROUP * gsel
    le = jnp.where((eidx >= first) & (eidx < first + EXPERTS_PER_GROUP), logits_t[:N_EXPERTS, :], neg)
    v1, e1 = top(le, eidx, jnp.float32(N_EXPERTS))
    v2, e2 = top(jnp.where(eidx == e1, neg, le), eidx, jnp.float32(N_EXPERTS))
    e21 = jnp.exp(v2 - v1)
    w1 = p_group / (1.0 + e21)
    return e1, e2, w1, w1 * e21


def _pool_deviation(ue_ref, lvl_ref, pool_ref, tile_start, tm, seq_len, groups):
    n = tm + 2 * POOL_HALO
    t = tile_start + lax.broadcasted_iota(I32, (tm, 1), 0)
    for gi in groups:
        w = POOL_WINDOWS[gi]
        half = w // 2
        cols = slice(gi * POOL_GC, (gi + 1) * POOL_GC)
        level = ue_ref[0:n, cols] + ue_ref[1:n + 1, cols]
        span, buf = 2, 0
        lvl_ref[buf, 0:n, :] = level
        while span < w:
            level = lvl_ref[buf, 0:n, :] + lvl_ref[buf, span:span + n, :]
            span, buf = 2 * span, 1 - buf
            lvl_ref[buf, 0:n, :] = level
        wsum = lvl_ref[buf, POOL_HALO - half:POOL_HALO - half + tm, :]
        cnt = (jnp.minimum(t + half, seq_len) - jnp.maximum(t - half, 0)).astype(F32)
        p = wsum * (1.0 / cnt) - ue_ref[POOL_HALO:POOL_HALO + tm, cols]
        pool_ref[:, cols] = p.astype(BF16)


def _norm2_split(x1, gain2, shift2, h2_ref):
    h2 = _scaled_norm(x1, gain2, shift2)
    h2_ref[...] = _pack_bf16_pair(h2[:, :D_PACK], h2[:, D_PACK:])
    h_hi = h2.astype(BF16)
    return h_hi, (h2 - h_hi.astype(F32)).astype(BF16)


def _router_logits(h_hi, h_lo, wr_ref, br_ref):
    parts = (jnp.dot(h_hi, wr_ref[...], preferred_element_type=F32)
             + jnp.dot(h_lo, wr_ref[...], preferred_element_type=F32))
    return parts[:, :LANES] + parts[:, LANES:] + br_ref[...]


def _dispatch_table(logits, is_real, utri_ref, run_ref, meta_ref, rw_ref):
    n = logits.shape[0]
    e1, e2, w1, w2 = _route(logits.T[0:ROUTE_ROWS, :])

    eidx = lax.broadcasted_iota(I32, (N_EXPERTS, n), 0).astype(F32)
    hit1 = eidx == e1
    hit2 = eidx == e2
    taken = jnp.where(hit1 | hit2, is_real, 0.0)
    before = run_ref[...] + jnp.dot(taken.astype(BF16), utri_ref[...], preferred_element_type=F32)
    rank1 = jnp.sum(jnp.where(hit1, before, 0.0), axis=0, keepdims=True)
    rank2 = jnp.sum(jnp.where(hit2, before, 0.0), axis=0, keepdims=True)
    run_ref[...] = run_ref[...] + jnp.sum(taken, axis=1, keepdims=True)
    ridx = lax.broadcasted_iota(I32, (META_ROWS, n), 0)
    table = jnp.where(ridx == 0, e1, jnp.where(ridx == 1, e2, jnp.where(ridx == 2, rank1,
                      jnp.where(ridx == 3, rank2, 0.0))))
    meta_ref[...] = table.astype(I32)
    weights_t = jnp.where(ridx == 0, w1, jnp.where(ridx == 1, w2, 0.0))
    rw_ref[...] = jnp.concatenate([weights_t, jnp.zeros((LANES - META_ROWS, n), F32)], axis=0).T


def _stage_c_kernel(x_ref, mod_ref, modp_ref, n1g_ref, n2g_ref, u_ref, up_ref, un_ref, attn_ref,
                    wgate_ref, wmix_ref, ps_ref, wpp_ref, wap_ref, wo_ref, wr_ref, br_ref, utri_ref,
                    x1_ref, h2_ref, meta_ref, rw_ref, count_ref,
                    ue_ref, lvl_ref, pool_ref, run_ref, merged_ref, x1_prev, *, seq_len):
    s = pl.program_id(0)
    tm = x_ref.shape[0]
    tiles_per_seq = seq_len // tm
    i = lax.rem(jnp.minimum(s, pl.num_programs(0) - 2), tiles_per_seq)

    @pl.when(s == 0)
    def _():
        run_ref[...] = jnp.zeros_like(run_ref)
        x1_prev[...] = jnp.zeros_like(x1_prev)

    gain1 = n1g_ref[...] * (1.0 + mod_ref[1:2, :])
    shift1 = mod_ref[0:1, :]
    res_gate = mod_ref[2:3, :]
    x = x_ref[...]
    h = _scaled_norm(x, gain1, shift1).astype(BF16)
    ue_ref[0:POOL_HALO, :] = up_ref[...].astype(F32) * (i > 0).astype(F32)
    ue_ref[POOL_HALO:POOL_HALO + tm, :] = u_ref[...].astype(F32)
    ue_ref[POOL_HALO + tm:2 * POOL_HALO + tm, :] = un_ref[...].astype(F32) * (i < tiles_per_seq - 1).astype(F32)
    ue_ref[2 * POOL_HALO + tm:, :] = jnp.zeros((POOL_HALO, D_POOL), F32)
    lvl_ref[:, 2 * POOL_HALO + tm:, :] = jnp.zeros((2, POOL_HALO, POOL_GC), F32)
    _pool_deviation(ue_ref, lvl_ref, pool_ref, i * tm, tm, seq_len, range(len(POOL_WINDOWS)))

    attn = attn_ref[...]
    mixed_groups = []
    for gi in range(len(POOL_WINDOWS)):
        cols = slice(gi * POOL_GC, (gi + 1) * POOL_GC)
        pm = jnp.dot(pool_ref[:, cols], wmix_ref[gi], preferred_element_type=F32)
        mixed_groups.append(pm * ps_ref[:, cols])
    pool = jnp.concatenate(mixed_groups, axis=1).astype(BF16)
    assert max(ROUTED_PHASE_AFTER_BLOCK) < D_MODEL // MERGE_COLS
    for c in range(0, D_MODEL, MERGE_COLS):
        cols = slice(c, c + MERGE_COLS)
        gate_cols = slice(D_MODEL + c, D_MODEL + c + MERGE_COLS)
        gate_pool = _sigmoid(jnp.dot(h, wgate_ref[:, cols], preferred_element_type=F32))
        gate_attn = _sigmoid(jnp.dot(h, wgate_ref[:, gate_cols], preferred_element_type=F32))
        merged = (gate_pool * jnp.dot(pool, wpp_ref[:, cols], preferred_element_type=F32)
                  + gate_attn * jnp.dot(attn, wap_ref[:, cols], preferred_element_type=F32))
        merged_ref[:, cols] = merged.astype(BF16)
        block = c // MERGE_COLS
        if block == ROUTED_PHASE_AFTER_BLOCK[0]:
            h_hi, h_lo = _norm2_split(x1_prev[...], n2g_ref[...] * (1.0 + modp_ref[4:5, :]), modp_ref[3:4, :],
                                      h2_ref)
        if block == ROUTED_PHASE_AFTER_BLOCK[1]:
            logits = _router_logits(h_hi, h_lo, wr_ref, br_ref)
        if block == ROUTED_PHASE_AFTER_BLOCK[2]:
            _dispatch_table(logits, (s > 0).astype(F32), utri_ref, run_ref, meta_ref, rw_ref)
            count_ref[...] = jnp.broadcast_to(run_ref[...], count_ref.shape)
    for c in range(0, D_MODEL, MERGE_COLS):
        cols = slice(c, c + MERGE_COLS)
        mixed = jnp.dot(merged_ref[...], wo_ref[:, cols], preferred_element_type=F32)
        x1 = x[:, cols] + res_gate[:, cols] * mixed
        x1_ref[:, cols] = x1.astype(BF16)
        x1_prev[:, cols] = x1


def _stage_c(x, mod, n1g, n2g, u, attn, w_gate, w_mix, pool_scale, w_pp, w_ap, w_o, w_r, b_r, tm):
    B, T, _ = x.shape
    hb = tm // POOL_HALO
    n_halo_blocks = T // POOL_HALO
    nt = T // tm
    n_tiles = B * nt

    def merged_tile(s):
        m = jnp.minimum(s, n_tiles - 1)
        return m // nt, lax.rem(m, nt)

    def tok(s):
        b, i = merged_tile(s)
        return b, i, 0

    def halo_before(s):
        b, i = merged_tile(s)
        return b, jnp.maximum(i * hb - 1, 0), 0

    def halo_after(s):
        b, i = merged_tile(s)
        return b, jnp.minimum((i + 1) * hb, n_halo_blocks - 1), 0

    routed = lambda s: jnp.maximum(s - 1, 0)
    flat = lambda s: (routed(s), 0)
    utri = jnp.triu(jnp.ones((tm, tm), BF16), k=1)
    return pl.pallas_call(
        functools.partial(_stage_c_kernel, seq_len=T),
        grid=(n_tiles + 1,),
        in_specs=[
            pl.BlockSpec((None, tm, D_MODEL), tok),
            pl.BlockSpec((None, 6, D_MODEL), lambda s: (merged_tile(s)[0], 0, 0)),
            pl.BlockSpec((None, 6, D_MODEL), lambda s: (routed(s) // nt, 0, 0)),
            _const_spec((1, D_MODEL)),
            _const_spec((1, D_MODEL)),
            pl.BlockSpec((None, tm, D_POOL), tok),
            pl.BlockSpec((None, POOL_HALO, D_POOL), halo_before),
            pl.BlockSpec((None, POOL_HALO, D_POOL), halo_after),
            pl.BlockSpec((None, tm, D_ATTN), tok),
            _const_spec((D_MODEL, 2 * D_MODEL)),
            _const_spec((len(POOL_WINDOWS), POOL_GC, POOL_GC)),
            _const_spec((1, D_POOL)),
            _const_spec((D_POOL, D_MODEL)),
            _const_spec((D_ATTN, D_MODEL)),
            _const_spec((D_MODEL, D_MODEL)),
            _const_spec((D_MODEL, 2 * LANES)),
            _const_spec((1, LANES)),
            _const_spec((tm, tm)),
        ],
        out_specs=[
            pl.BlockSpec((None, tm, D_MODEL), tok),
            pl.BlockSpec((tm, D_PACK), flat),
            pl.BlockSpec((META_ROWS, tm), lambda s: (0, routed(s))),
            pl.BlockSpec((tm, LANES), flat),
            pl.BlockSpec((N_EXPERTS, LANES), lambda s: (0, 0)),
        ],
        out_shape=[
            jax.ShapeDtypeStruct((B, T, D_MODEL), BF16),
            jax.ShapeDtypeStruct((B * T, D_PACK), I32),
            jax.ShapeDtypeStruct((META_ROWS, B * T), I32),
            jax.ShapeDtypeStruct((B * T, LANES), F32),
            jax.ShapeDtypeStruct((N_EXPERTS, LANES), F32),
        ],
        scratch_shapes=[pltpu.VMEM((tm + 3 * POOL_HALO, D_POOL), F32),
                        pltpu.VMEM((2, tm + 3 * POOL_HALO, POOL_GC), F32),
                        pltpu.VMEM((tm, D_POOL), BF16),
                        pltpu.VMEM((N_EXPERTS, 1), F32),
                        pltpu.VMEM((tm, D_MODEL), BF16),
                        pltpu.VMEM((tm, D_MODEL), F32)],
        compiler_params=pltpu.CompilerParams(
            dimension_semantics=("arbitrary",), vmem_limit_bytes=V7X_VMEM_LIMIT),
        name="stage_c",
    )(x, mod, mod, n1g, n2g, u, u, u, attn, w_gate, w_mix, pool_scale, w_pp, w_ap, w_o, w_r, b_r, utri)


def _sc_workers():
    info = plsc.get_sparse_core_info()
    return info.num_cores, info.num_subcores


def _sc_scatter_rows(rows, idx0, idx1, n_out):
    n, d = rows.shape
    nc, ns = _sc_workers()
    per_worker = n // (nc * ns)
    n_win = per_worker // SC_WINDOW
    mesh = plsc.VectorSubcoreMesh(core_axis_name="c", subcore_axis_name="s")

    @functools.partial(
        pl.kernel, mesh=mesh,
        out_type=jax.ShapeDtypeStruct((n_out, d), rows.dtype),
        scratch_types=[pltpu.VMEM((SC_WINDOW,), I32), pltpu.VMEM((SC_WINDOW,), I32),
                       pltpu.VMEM((SC_WINDOW, d), rows.dtype)],
        name="sc_scatter_rows",
    )
    def scatter(rows_hbm, idx0_hbm, idx1_hbm, out_hbm, i0_v, i1_v, rows_v):
        wid = lax.axis_index("s") * nc + lax.axis_index("c")

        @pl.loop(0, n_win)
        def _(w):
            base = wid * per_worker + w * SC_WINDOW
            pltpu.sync_copy(rows_hbm.at[pl.ds(base, SC_WINDOW)], rows_v)
            pltpu.sync_copy(idx0_hbm.at[pl.ds(base, SC_WINDOW)], i0_v)
            pltpu.sync_copy(idx1_hbm.at[pl.ds(base, SC_WINDOW)], i1_v)
            pltpu.sync_copy(rows_v, out_hbm.at[i0_v])
            pltpu.sync_copy(rows_v, out_hbm.at[i1_v])

    return scatter(rows, idx0, idx1)


def _sc_gather_rows(table, idx):
    n = idx.shape[0]
    d = table.shape[1]
    nc, ns = _sc_workers()
    per_worker = n // (nc * ns)
    n_win = per_worker // SC_WINDOW
    mesh = plsc.VectorSubcoreMesh(core_axis_name="c", subcore_axis_name="s")

    @functools.partial(
        pl.kernel, mesh=mesh,
        out_type=jax.ShapeDtypeStruct((n, d), table.dtype),
        scratch_types=[pltpu.VMEM((SC_WINDOW,), I32), pltpu.VMEM((SC_WINDOW, d), table.dtype)],
        name="sc_gather_rows",
    )
    def gather(table_hbm, idx_hbm, out_hbm, i_v, rows_v):
        wid = lax.axis_index("s") * nc + lax.axis_index("c")

        @pl.loop(0, n_win)
        def _(w):
            base = wid * per_worker + w * SC_WINDOW
            pltpu.sync_copy(idx_hbm.at[pl.ds(base, SC_WINDOW)], i_v)
            pltpu.sync_copy(table_hbm.at[i_v], rows_v)
            pltpu.sync_copy(rows_v, out_hbm.at[pl.ds(base, SC_WINDOW)])

    return gather(table, idx)


def _moe_kernel(te_ref, tv_ref, nl_ref, x_hbm, wg_ref, wu_ref, wd_ref, o_ref, xbuf, sem, wgu_ref):
    i = pl.program_id(0)
    n_live_tiles = nl_ref[0]
    valid = tv_ref[i]
    tm = o_ref.shape[0]
    slot = _input_ring(x_hbm, xbuf, sem, i, n_live_tiles, tm)

    @pl.when((i == 0) | (te_ref[i] != te_ref[jnp.maximum(i - 1, 0)]))
    def _():
        wgu_ref[:, :D_EXPERT] = wg_ref[...]
        wgu_ref[:, D_EXPERT:] = wu_ref[...]

    group = min(STAGE_ROWS, tm)
    n_groups = tm // group

    def gate_up(r0):
        live = lax.broadcasted_iota(I32, (group, 1), 0) < valid - r0
        lo, hi = _unpack_bf16_pair(jnp.where(live, xbuf[slot, r0:r0 + group, :], 0))
        x = jnp.concatenate([lo, hi], axis=1).astype(BF16)
        return jnp.dot(x, wgu_ref[...], preferred_element_type=F32)

    def down(gu):
        a = gu[:, :D_EXPERT]
        hmid = (a * _sigmoid(a) * gu[:, D_EXPERT:]).astype(BF16)
        return jnp.dot(hmid, wd_ref[...], preferred_element_type=F32)

    def store(r0, y):
        o_ref[r0:r0 + group, :] = _pack_bf16_pair(y[:, :D_PACK], y[:, D_PACK:])

    for n_live in range(n_groups + 1):
        @pl.when((valid > (n_live - 1) * group) & (valid <= n_live * group))
        def _():
            gu, y = None, None
            for g in range(n_live + 2):
                nxt = gate_up(g * group) if g < n_live else None
                if y is not None:
                    store((g - 2) * group, y)
                y = down(gu) if gu is not None else None
                gu = nxt
            if n_live < n_groups:
                o_ref[n_live * group:, :] = jnp.zeros((tm - n_live * group, D_PACK), I32)


def _moe(xs, tile_expert, tile_valid, n_live_tiles, w_g, w_u, w_d, tm):
    n_tiles = xs.shape[0] // tm
    grid_spec = pltpu.PrefetchScalarGridSpec(
        num_scalar_prefetch=3,
        grid=(n_tiles,),
        in_specs=[
            pl.BlockSpec(memory_space=pl.ANY),
            pl.BlockSpec((None, D_MODEL, D_EXPERT), lambda i, te, tv, nl: (te[i], 0, 0)),
            pl.BlockSpec((None, D_MODEL, D_EXPERT), lambda i, te, tv, nl: (te[i], 0, 0)),
            pl.BlockSpec((None, D_EXPERT, D_MODEL), lambda i, te, tv, nl: (te[i], 0, 0)),
        ],
        out_specs=pl.BlockSpec((tm, D_PACK), lambda i, te, tv, nl: (jnp.minimum(i, nl[0]), 0)),
        scratch_shapes=[pltpu.VMEM((RING_SLOTS, tm, D_PACK), I32), pltpu.SemaphoreType.DMA((RING_SLOTS,)),
                        pltpu.VMEM((D_MODEL, 2 * D_EXPERT), BF16)],
    )
    return pl.pallas_call(
        _moe_kernel,
        grid_spec=grid_spec,
        out_shape=jax.ShapeDtypeStruct(xs.shape, I32),
        compiler_params=pltpu.CompilerParams(
            dimension_semantics=("arbitrary",), vmem_limit_bytes=V7X_VMEM_LIMIT),
        name="experts",
    )(tile_expert, tile_valid, n_live_tiles, xs, w_g, w_u, w_d)


def _final_kernel(x1_ref, y0_ref, y1_ref, rw_ref, mod_ref, fg_ref, o_ref):
    y0lo, y0hi = _unpack_bf16_pair(y0_ref[...])
    y1lo, y1hi = _unpack_bf16_pair(y1_ref[...])
    w0 = rw_ref[:, 0:1]
    w1 = rw_ref[:, 1:2]
    moe = jnp.concatenate([w0 * y0lo + w1 * y1lo, w0 * y0hi + w1 * y1hi], axis=1)
    x2 = x1_ref[...].astype(F32) + mod_ref[5:6, :] * moe
    ms = jnp.mean(x2 * x2, axis=-1, keepdims=True)
    o_ref[...] = x2 * lax.rsqrt(ms + EPS) * fg_ref[...]


def _final(x1, yb, rw, mod, final_g, tm):
    B, T, _ = x1.shape
    n_blocks = B * T // tm
    flat = lambda b, i: (b * (T // tm) + i, 0)
    return pl.pallas_call(
        _final_kernel,
        grid=(B, T // tm),
        in_specs=[
            pl.BlockSpec((None, tm, D_MODEL), lambda b, i: (b, i, 0)),
            pl.BlockSpec((tm, D_PACK), flat),
            pl.BlockSpec((tm, D_PACK), lambda b, i: (n_blocks + b * (T // tm) + i, 0)),
            pl.BlockSpec((tm, LANES), flat),
            pl.BlockSpec((None, 6, D_MODEL), lambda b, i: (b, 0, 0)),
            _const_spec((1, D_MODEL)),
        ],
        out_specs=pl.BlockSpec((None, tm, D_MODEL), lambda b, i: (b, i, 0)),
        out_shape=jax.ShapeDtypeStruct((B, T, D_MODEL), F32),
        compiler_params=pltpu.CompilerParams(
            dimension_semantics=("parallel", "parallel"), vmem_limit_bytes=V7X_VMEM_LIMIT),
        name="final_combine",
    )(x1, yb, yb, rw, mod, final_g)


def _rope_tables(T):
    rows = T // GRID_W
    row = np.repeat(np.arange(rows, dtype=np.float32), GRID_W)
    col = np.tile(np.arange(GRID_W, dtype=np.float32), rows)
    inv_freq = np.float32(ROPE_THETA) ** (-np.arange(0, 64, 2, dtype=np.float32) / np.float32(64))
    ang_r = row[:, None] * inv_freq[None, :]
    ang_c = col[:, None] * inv_freq[None, :]
    ang = np.concatenate([ang_r, ang_r, ang_c, ang_c], axis=-1).astype(np.float32)
    sign = np.where((np.arange(HEAD_DIM) % 64) < 32, -1.0, 1.0).astype(np.float32)
    return jnp.asarray(np.cos(ang)), jnp.asarray(np.sin(ang) * sign[None, :])


def _dispatch_plan(counts, tm, n_tiles):
    tiles_per_e = (counts + tm - 1) // tm
    tile_end = jnp.cumsum(tiles_per_e)
    tile_start = tile_end - tiles_per_e
    tile_id = jnp.arange(n_tiles, dtype=I32)
    te = jnp.minimum(jnp.sum((tile_id[:, None] >= tile_end[None, :]).astype(I32), axis=1), N_EXPERTS - 1)
    mine = te[:, None] == jnp.arange(N_EXPERTS, dtype=I32)[None, :]
    count_of = jnp.sum(jnp.where(mine, counts[None, :], 0), axis=1)
    start_of = jnp.sum(jnp.where(mine, tile_start[None, :], 0), axis=1)
    live = jnp.clip(count_of - (tile_id - start_of) * tm, 0, tm)
    tv = jnp.where(tile_id < tile_end[-1], live, 0).astype(I32)
    return tile_start.astype(I32), te, tv, tile_end[-1:].astype(I32)


def _slot_kernel(start_ref, meta_ref, pos_ref, *, tile_rows):
    e = meta_ref[0:2, :]
    first_tile = jnp.zeros_like(e)
    for ex in range(N_EXPERTS):
        first_tile = jnp.where(e == ex, start_ref[ex], first_tile)
    pos_ref[...] = first_tile * tile_rows + meta_ref[2:4, :]


def _slots(meta, tile_start, tile_rows):
    n = meta.shape[1]
    tn = min(n, SLOT_TILE)
    grid_spec = pltpu.PrefetchScalarGridSpec(
        num_scalar_prefetch=1,
        grid=(n // tn,),
        in_specs=[pl.BlockSpec((META_ROWS, tn), lambda i, start: (0, i))],
        out_specs=pl.BlockSpec((2, tn), lambda i, start: (0, i)),
    )
    return pl.pallas_call(
        functools.partial(_slot_kernel, tile_rows=tile_rows),
        grid_spec=grid_spec,
        out_shape=jax.ShapeDtypeStruct((2, n), I32),
        name="dispatch_slots",
    )(tile_start, meta)


def _pick_tile(T, want):
    t = min(T, want)
    assert T % t == 0
    return t


def _trunk(x, mod, p):
    B, T, _ = x.shape
    N = B * T
    tm = _pick_tile(T, TOKEN_TILE)
    tq = _pick_tile(T, QUERY_TILE)
    tme = _pick_tile(N, EXPERT_TILE)
    cos, sin_signed = _rope_tables(T)
    u, qt, qb, k, vt = _stage_a(x, mod, p["n1g"], p["w_qkvu"], p["qg"], p["kg"], cos, sin_signed, tm)
    attn = lax.cond(
        p["bounded_softmax_ok"],
        functools.partial(_attention, tq=tq, tk=_pick_tile(T, KEY_TILE), running_max=False),
        functools.partial(_attention, tq=tq, tk=_pick_tile(T, KEY_TILE_ONLINE), running_max=True),
        qt, qb, k, vt)
    x1, h2p, meta, rw, counts = _stage_c(x, mod, p["n1g"], p["n2g"], u, attn, p["w_gate"], p["w_mix"],
                                         p["pool_scale"], p["w_pp"], p["w_ap"], p["w_o"], p["w_r"], p["b_r"], tm)
    n_tiles = 2 * N // tme + N_EXPERTS
    tile_start, te, tv, n_live_tiles = _dispatch_plan(counts[:, 0].astype(I32), tme, n_tiles)
    pos = _slots(meta, tile_start, tme)
    xs = _sc_scatter_rows(h2p, pos[0], pos[1], n_tiles * tme)
    ys = _moe(xs, te, tv, n_live_tiles, p["w_eg"], p["w_eu"], p["w_ed"], tme)
    yb = _sc_gather_rows(ys, pos.reshape(-1))
    return _final(x1, yb, rw, mod, p["final_g"], _pick_tile(T, FINAL_TILE))


def kernel(x_prompt, x_sample, c_prompt, c_sample, w_ada, b_ada, norm1_g, norm2_g, w_in, q_norm_g,
           k_norm_g, w_pool_mix, pool_scale, w_pool_proj, w_attn_proj, w_o, w_router_group,
           b_router_group, w_router_expert, b_router_expert, w_exp_gate, w_exp_up, w_exp_down, final_g):
    assert w_ada.shape[0] == 1, "single-layer block"
    n_r = N_GROUPS + N_EXPERTS
    w_r = jnp.concatenate([w_router_expert[0], w_router_group[0],
                           jnp.zeros((D_MODEL, LANES - n_r), F32)], axis=1)
    w_r_hi = w_r.astype(BF16)
    b_r = jnp.concatenate([b_router_expert[0], b_router_group[0], jnp.zeros((LANES - n_r,), F32)])
    score_bound = ((KEY_NORM_MARGIN * LOG2E * np.sqrt(HEAD_DIM))
                   * jnp.max(jnp.abs(q_norm_g[0])) * jnp.max(jnp.abs(k_norm_g[0])))
    p = dict(
        bounded_softmax_ok=2.0 * score_bound <= EXP2_SAFE_SPAN,
        n1g=norm1_g[0].reshape(1, D_MODEL), n2g=norm2_g[0].reshape(1, D_MODEL),
        w_qkvu=w_in[0][:, :D_QKVU].astype(BF16), w_gate=w_in[0][:, D_QKVU:].astype(BF16),
        qg=q_norm_g[0].reshape(1, HEAD_DIM), kg=k_norm_g[0].reshape(1, HEAD_DIM),
        w_mix=w_pool_mix[0].astype(BF16), pool_scale=pool_scale[0].reshape(1, D_POOL),
        w_pp=w_pool_proj[0].astype(BF16), w_ap=w_attn_proj[0].astype(BF16), w_o=w_o[0].astype(BF16),
        w_r=jnp.concatenate([w_r_hi, (w_r - w_r_hi.astype(F32)).astype(BF16)], axis=1), b_r=b_r.reshape(1, LANES),
        w_eg=w_exp_gate[0].astype(BF16), w_eu=w_exp_up[0].astype(BF16), w_ed=w_exp_down[0].astype(BF16),
        final_g=final_g.reshape(1, D_MODEL),
    )
    n_prompt = c_prompt.shape[0]
    mod = _modulation(jnp.concatenate([c_prompt, c_sample], axis=0), w_ada[0], b_ada[0]).reshape(-1, 6, D_MODEL)
    return _trunk(x_prompt, mod[:n_prompt], p), _trunk(x_sample, mod[n_prompt:], p)
```
